```python
import jax
import jax.numpy as jnp
from jax import lax
import numpy as np


D_MODEL = 2048
BATCH = 8
SEQ = 4096
DEPTH = 2

N_MIXERS = 4
BRANCH_WIDTH = D_MODEL // N_MIXERS
D_FF = 4 * D_MODEL
ROPE_THETA = 500000.0
NORM_EPS = 1e-6
Q_BLOCK = 128
MASK_VALUE = -1e30

MLA_NOPE_DIM = 128
MLA_ROPE_DIM = 64
MLA_V_DIM = 128
MLA_HEADS = BRANCH_WIDTH // MLA_V_DIM
MLA_Q_RANK = 384
MLA_KV_RANK = 128

NSA_HEAD_DIM = 128
NSA_HEADS = BRANCH_WIDTH // NSA_HEAD_DIM
NSA_ROT_DIM = NSA_HEAD_DIM // 4
NSA_CMP_LEN = 32
NSA_CMP_STRIDE = 16
NSA_CMP_HIDDEN = 128
NSA_SEL_BLOCK = 64
NSA_TOP_N = 16
NSA_WINDOW = 512
NSA_SEL_Q_CHUNK = 64
NSA_FORCED_SCORE = 1000.0

RWKV_HEAD_DIM = 64
RWKV_HEADS = BRANCH_WIDTH // RWKV_HEAD_DIM
RWKV_DECAY_LORA = 96
RWKV_A_LORA = 96
RWKV_GATE_LORA = 256
RWKV_GN_EPS = 64e-5
RWKV_IN = 3 * BRANCH_WIDTH + RWKV_DECAY_LORA + RWKV_A_LORA + RWKV_GATE_LORA

RET_HEADS = 4
RET_V_DIM = BRANCH_WIDTH // RET_HEADS
RET_K_DIM = RET_V_DIM // 2
RET_CHUNK = 128
RET_THETA = 10000.0

IN_SPLITS = (
    MLA_Q_RANK, MLA_KV_RANK, MLA_ROPE_DIM,
    NSA_HEADS * NSA_HEAD_DIM,
    NSA_HEAD_DIM, NSA_HEAD_DIM,
    NSA_HEAD_DIM, NSA_HEAD_DIM,
    NSA_HEAD_DIM, NSA_HEAD_DIM,
    NSA_HEADS * 3,
    RWKV_IN,
    RET_HEADS * RET_K_DIM, RET_HEADS * RET_K_DIM, BRANCH_WIDTH, BRANCH_WIDTH,
    N_MIXERS * D_MODEL,
)
IN_WIDTH = sum(IN_SPLITS)

kernel_name = 'hybrid_mla_nsa_rwkv7_retention_block'


def split_cols(h, sizes):
    offs = np.cumsum(np.array(sizes))[:-1].tolist()
    return jnp.split(h, offs, axis=-1)


def rms_norm(x, g, eps=NORM_EPS):
    xf = x.astype(jnp.float32)
    y = xf * lax.rsqrt(jnp.mean(xf * xf, axis=-1, keepdims=True) + eps)
    return (y * g.astype(jnp.float32)).astype(x.dtype)


def head_norm(o, eps):
    o = o.astype(jnp.float32)
    c = o - jnp.mean(o, axis=-1, keepdims=True)
    return c * lax.rsqrt(jnp.mean(c * c, axis=-1, keepdims=True) + eps)


def rope_inv_freq(rot_dim, theta):
    return jnp.float32(theta) ** (-jnp.arange(0, rot_dim, 2, dtype=jnp.float32) / rot_dim)


def rotary_tables(seq_len, inv_freq):
    ang = jnp.arange(seq_len, dtype=jnp.float32)[:, None] * inv_freq[None, :]
    return jnp.cos(ang), jnp.sin(ang)


def apply_rotary(x, cos, sin):
    half = cos.shape[-1]
    x1, x2, rest = x[..., :half], x[..., half:2 * half], x[..., 2 * half:]
    c = cos[:, None, :].astype(x.dtype)
    s = sin[:, None, :].astype(x.dtype)
    return jnp.concatenate([x1 * c - x2 * s, x2 * c + x1 * s, rest], axis=-1)


def masked_softmax(s, mask):
    s = jnp.where(mask, s.astype(jnp.float32), MASK_VALUE)
    p = jax.nn.softmax(s, axis=-1)
    return jnp.where(mask, p, 0.0)


def mla_mixer(q_lat, kv_lat, k_rope, g_q, g_kv, w_uq, w_ukv, cos, sin):
    B, S, _ = q_lat.shape
    H, Dn, Dr, Dv = MLA_HEADS, MLA_NOPE_DIM, MLA_ROPE_DIM, MLA_V_DIM
    q = (rms_norm(q_lat, g_q) @ w_uq).reshape(B, S, H, Dn + Dr)
    q_nope, q_rope = q[..., :Dn], apply_rotary(q[..., Dn:], cos, sin)
    kv = (rms_norm(kv_lat, g_kv) @ w_ukv).reshape(B, S, H, Dn + Dv)
    k_nope, v = kv[..., :Dn], kv[..., Dn:]
    k_rope = apply_rotary(k_rope[:, :, None, :], cos, sin)[:, :, 0]
    scale = (Dn + Dr) ** -0.5
    nq = S // Q_BLOCK
    to_blocks = lambda t: t.reshape(B, nq, Q_BLOCK, H, t.shape[-1]).transpose(1, 0, 2, 3, 4)
    kpos = jnp.arange(S)

    def attend(args):
        qn, qr, q0 = args
        s = jnp.einsum('bqhd,bkhd->bhqk', qn, k_nope) + jnp.einsum('bqhd,bkd->bhqk', qr, k_rope)
        qpos = q0 + jnp.arange(Q_BLOCK)
        p = masked_softmax(s * scale, kpos[None, :] <= qpos[:, None])
        return jnp.einsum('bhqk,bkhd->bqhd', p.astype(v.dtype), v)

    o = lax.map(attend, (to_blocks(q_nope), to_blocks(q_rope), jnp.arange(nq) * Q_BLOCK))
    return o.transpose(1, 0, 2, 3, 4).reshape(B, S, H * Dv)


def nsa_mixer(q, k_cmp, v_cmp, k_slc, v_slc, k_win, v_win, gate_logits, cmp_pos, cmp_w1, cmp_w2, cos, sin):
    B, S, _ = q.shape
    H, Dh = NSA_HEADS, NSA_HEAD_DIM
    scale = Dh ** -0.5
    pos = jnp.arange(S)
    q = apply_rotary(q.reshape(B, S, H, Dh), cos, sin)
    rot_k = lambda k: apply_rotary(k[:, :, None, :], cos, sin)[:, :, 0]
    k_cmp, k_slc, k_win = rot_k(k_cmp), rot_k(k_slc), rot_k(k_win)

    n_cmp = (S - NSA_CMP_LEN) // NSA_CMP_STRIDE + 1
    cmp_start = jnp.arange(n_cmp) * NSA_CMP_STRIDE
    cmp_end = cmp_start + NSA_CMP_LEN - 1
    cmp_idx = cmp_start[:, None] + jnp.arange(NSA_CMP_LEN)[None, :]
    raw = jnp.stack([k_cmp, v_cmp])[:, :, cmp_idx] + cmp_pos[:, None, None]
    raw = raw.reshape(2, B, n_cmp, NSA_CMP_LEN * Dh)
    hid = jax.nn.gelu(jnp.einsum('zbnf,zfe->zbne', raw, cmp_w1))
    kvc = jnp.einsum('zbne,zed->zbnd', hid, cmp_w2)
    kc, vc = kvc[0], kvc[1]
    s_cmp = jnp.einsum('bshd,bnd->bhsn', q, kc) * scale
    p_cmp = masked_softmax(s_cmp, cmp_end[None, :] <= pos[:, None])
    o_cmp = jnp.einsum('bhsn,bnd->bshd', p_cmp.astype(vc.dtype), vc)

    n_blk = S // NSA_SEL_BLOCK
    jb = jnp.arange(n_blk)
    blk_start = jb * NSA_SEL_BLOCK
    cover = ((cmp_start[:, None] <= blk_start[None, :] + NSA_SEL_BLOCK - 1)
             & (cmp_end[:, None] >= blk_start[None, :])).astype(jnp.float32)
    imp = jnp.einsum('bhsn,nj->bsj', p_cmp, cover)
    cur = pos // NSA_SEL_BLOCK
    forced = (jb[None, :] == 0) | (jb[None, :] == cur[:, None]) | (jb[None, :] == cur[:, None] - 1)
    visible = jb[None, :] <= cur[:, None]
    imp = jnp.where(visible, imp + NSA_FORCED_SCORE * forced, -jnp.inf)
    n_sel = min(NSA_TOP_N, n_blk)
    _, sel = lax.top_k(imp, n_sel)
    kb = k_slc.reshape(B, n_blk, NSA_SEL_BLOCK, Dh)
    vb = v_slc.reshape(B, n_blk, NSA_SEL_BLOCK, Dh)
    gather = jax.vmap(lambda blocks, idx: blocks[idx])
    qc_len = NSA_SEL_Q_CHUNK
    nqc = S // qc_len
    offs = jnp.arange(NSA_SEL_BLOCK)

    def attend_sel(args):
        qc, ic, q0 = args
        ks = gather(kb, ic).reshape(B, qc_len, n_sel * NSA_SEL_BLOCK, Dh)
        vs = gather(vb, ic).reshape(B, qc_len, n_sel * NSA_SEL_BLOCK, Dh)
        kpos = (ic[..., None] * NSA_SEL_BLOCK + offs).reshape(B, qc_len, n_sel * NSA_SEL_BLOCK)
        qpos = q0 + jnp.arange(qc_len)
        s = jnp.einsum('bqhd,bqkd->bhqk', qc, ks) * scale
        p = masked_softmax(s, (kpos <= qpos[None, :, None])[:, None])
        return jnp.einsum('bhqk,bqkd->bqhd', p.astype(vs.dtype), vs)

    o_slc = lax.map(attend_sel, (q.reshape(B, nqc, qc_len, H, Dh).transpose(1, 0, 2, 3, 4),
                                 sel.reshape(B, nqc, qc_len, n_sel).transpose(1, 0, 2, 3),
                                 jnp.arange(nqc) * qc_len))
    o_slc = o_slc.transpose(1, 0, 2, 3, 4).reshape(B, S, H, Dh)

    nqb = S // Q_BLOCK
    span = NSA_WINDOW + Q_BLOCK
    band = jnp.arange(nqb)[:, None] * Q_BLOCK + jnp.arange(span)[None, :]
    pad = ((0, 0), (NSA_WINDOW, 0), (0, 0))
    kw = jnp.pad(k_win, pad)[:, band]
    vw = jnp.pad(v_win, pad)[:, band]
    kpos_w = band - NSA_WINDOW
    qpos_w = pos.reshape(nqb, Q_BLOCK)
    dist = qpos_w[:, :, None] - kpos_w[:, None, :]
    wmask = (dist >= 0) & (dist < NSA_WINDOW) & (kpos_w[:, None, :] >= 0)
    s_win = jnp.einsum('bnqhd,bnkd->bhnqk', q.reshape(B, nqb, Q_BLOCK, H, Dh), kw) * scale
    p_win = masked_softmax(s_win, wmask)
    o_win = jnp.einsum('bhnqk,bnkd->bnqhd', p_win.astype(vw.dtype), vw).reshape(B, S, H, Dh)

    g = jax.nn.sigmoid(gate_logits.astype(jnp.float32)).reshape(B, S, H, 3).astype(q.dtype)
    o = g[..., 0:1] * o_cmp + g[..., 1:2] * o_slc + g[..., 2:3] * o_win
    return o.reshape(B, S, H * Dh)


def rwkv7_mixer(z, mu, w0, w2, a0, a2, g2, k_k, k_a, r_k, gn_w, gn_b):
    B, S, _ = z.shape
    H, N, C = RWKV_HEADS, RWKV_HEAD_DIM, BRANCH_WIDTH
    z = z.astype(jnp.float32)
    z_prev = jnp.pad(z, ((0, 0), (1, 0), (0, 0)))[:, :-1]
    z = z + (z_prev - z) * mu
    r, k, v, wd, ad, gd = split_cols(z, (C, C, C, RWKV_DECAY_LORA, RWKV_A_LORA, RWKV_GATE_LORA))
    w_log = -jax.nn.softplus(-(w0 + jnp.tanh(wd) @ w2)) - 0.5
    decay = jnp.exp(-jnp.exp(w_log))
    a = jax.nn.sigmoid(a0 + ad @ a2)
    g = jax.nn.sigmoid(gd) @ g2
    heads = lambda t: t.reshape(B, S, H, N)
    kk = heads(k * k_k)
    kk = kk * lax.rsqrt(jnp.maximum(jnp.sum(kk * kk, axis=-1, keepdims=True), 1e-24))
    k = k * (1.0 + (a - 1.0) * k_a)
    r, k, v, a, decay = heads(r), heads(k), heads(v), heads(a), heads(decay)

    def step(state, inp):
        r_t, w_t, k_t, v_t, kk_t, a_t = inp
        sa = jnp.einsum('bhvk,bhk->bhv', state, -kk_t)
        state = (state * w_t[:, :, None, :] + sa[..., None] * (kk_t * a_t)[:, :, None, :]
                 + v_t[..., None] * k_t[:, :, None, :])
        return state, jnp.einsum('bhvk,bhk->bhv', state, r_t)

    tmaj = lambda t: jnp.moveaxis(t, 1, 0)
    _, o = lax.scan(step, jnp.zeros((B, H, N, N), jnp.float32),
                    (tmaj(r), tmaj(decay), tmaj(k), tmaj(v), tmaj(kk), tmaj(a)))
    o = head_norm(jnp.moveaxis(o, 0, 1), RWKV_GN_EPS).reshape(B, S, C) * gn_w + gn_b
    bonus = jnp.sum(r * k * r_k, axis=-1, keepdims=True) * v
    return (o + bonus.reshape(B, S, C)) * g


def retention_mixer(q, k, v, gate, cos, sin):
    B, S, _ = q.shape
    H, Dk, Dv, C = RET_HEADS, RET_K_DIM, RET_V_DIM, RET_CHUNK
    f32 = jnp.float32
    q = apply_rotary(q.astype(f32).reshape(B, S, H, Dk), cos, sin)
    k = apply_rotary(k.astype(f32).reshape(B, S, H, Dk), cos, sin) * Dk ** -0.5
    v = v.astype(f32).reshape(B, S, H, Dv)
    nc = S // C
    to_chunks = lambda t: t.reshape(B, nc, C, H, t.shape[-1]).transpose(0, 3, 1, 2, 4)
    qc, kc, vc = to_chunks(q), to_chunks(k), to_chunks(v)
    log_gamma = jnp.log1p(-jnp.exp2(-5.0 - jnp.arange(H, dtype=f32)))
    n = jnp.arange(C, dtype=f32)
    dist = n[:, None] - n[None, :]
    inner_decay = jnp.where(dist >= 0, jnp.exp(jnp.maximum(dist, 0.0) * log_gamma[:, None, None]), 0.0)
    q_decay = jnp.exp((n + 1.0) * log_gamma[:, None])
    k_decay = jnp.exp((C - 1.0 - n) * log_gamma[:, None])
    chunk_decay = jnp.exp(C * log_gamma)
    scores = jnp.einsum('bhcnd,bhcmd->bhcnm', qc, kc) * inner_decay[:, None]
    o_inner = jnp.einsum('bhcnm,bhcme->bhcne', scores, vc)
    u = jnp.einsum('bhcmd,bhcme->cbhde', kc * k_decay[:, None, :, None], vc)

    def step(state, u_c):
        return state * chunk_decay[:, None, None] + u_c, state

    _, prev = lax.scan(step, jnp.zeros((B, H, Dk, Dv), f32), u)
    o_cross = jnp.einsum('bhcnd,cbhde->bhcne', qc * q_decay[:, None, :, None], prev)
    o = (o_inner + o_cross).transpose(0, 2, 3, 1, 4).reshape(B, S, H, Dv)
    return head_norm(o, NORM_EPS).reshape(B, S, H * Dv) * jax.nn.silu(gate.astype(f32))


def setup_inputs(seed: int = 0) -> dict:
    key = jax.random.key(seed)
    ks = jax.random.split(key, 25)
    f32 = jnp.float32
    nrm = lambda k, shape, scale: scale * jax.random.normal(k, shape, f32)
    L, C = DEPTH, BRANCH_WIDTH
    return {
        'x': nrm(ks[0], (BATCH, SEQ, D_MODEL), 1.0),
        'w_in': nrm(ks[1], (L, D_MODEL, IN_WIDTH), D_MODEL ** -0.5),
        'w_branch': nrm(ks[2], (L, N_MIXERS, C, D_MODEL), C ** -0.5),
        'w_out': nrm(ks[3], (L, D_MODEL, D_MODEL), D_MODEL ** -0.5),
        'w_up': nrm(ks[4], (L, D_MODEL, D_FF), D_MODEL ** -0.5),
        'w_down': nrm(ks[5], (L, D_FF, D_MODEL), D_FF ** -0.5),
        'norm_gains': 1.0 + nrm(ks[6], (L, 4, D_MODEL), 0.02),
        'mla_g_q': 1.0 + nrm(ks[7], (L, MLA_Q_RANK), 0.02),
        'mla_g_kv': 1.0 + nrm(ks[8], (L, MLA_KV_RANK), 0.02),
        'mla_w_uq': nrm(ks[9], (L, MLA_Q_RANK, MLA_HEADS * (MLA_NOPE_DIM + MLA_ROPE_DIM)), MLA_Q_RANK ** -0.5),
        'mla_w_ukv': nrm(ks[10], (L, MLA_KV_RANK, MLA_HEADS * (MLA_NOPE_DIM + MLA_V_DIM)), MLA_KV_RANK ** -0.5),
        'nsa_cmp_pos': nrm(ks[11], (L, 2, NSA_CMP_LEN, NSA_HEAD_DIM), 0.1),
        'nsa_cmp_w1': nrm(ks[12], (L, 2, NSA_CMP_LEN * NSA_HEAD_DIM, NSA_CMP_HIDDEN), (NSA_CMP_LEN * NSA_HEAD_DIM) ** -0.5),
        'nsa_cmp_w2': nrm(ks[13], (L, 2, NSA_CMP_HIDDEN, NSA_HEAD_DIM), NSA_CMP_HIDDEN ** -0.5),
        'rwkv_mu': jax.random.uniform(ks[14], (L, RWKV_IN), f32),
        'rwkv_w0': jax.random.uniform(ks[15], (L, C), f32, -6.0, -1.0),
        'rwkv_w2': nrm(ks[16], (L, RWKV_DECAY_LORA, C), 0.1 * RWKV_DECAY_LORA ** -0.5),
        'rwkv_a0': nrm(ks[17], (L, C), 0.1),
        'rwkv_a2': nrm(ks[18], (L, RWKV_A_LORA, C), RWKV_A_LORA ** -0.5),
        'rwkv_g2': nrm(ks[19], (L, RWKV_GATE_LORA, C), RWKV_GATE_LORA ** -0.5),
        'rwkv_k_k': 0.85 + nrm(ks[20], (L, C), 0.05),
        'rwkv_k_a': 1.0 + nrm(ks[21], (L, C), 0.05),
        'rwkv_r_k': nrm(ks[22], (L, RWKV_HEADS, RWKV_HEAD_DIM), 0.1),
        'rwkv_gn_w': 1.0 + nrm(ks[23], (L, C), 0.02),
        'rwkv_gn_b': nrm(ks[24], (L, C), 0.02),
    }


def reference(x, w_in, w_branch, w_out, w_up, w_down, norm_gains, mla_g_q, mla_g_kv, mla_w_uq, mla_w_ukv,
              nsa_cmp_pos, nsa_cmp_w1, nsa_cmp_w2, rwkv_mu, rwkv_w0, rwkv_w2, rwkv_a0, rwkv_a2, rwkv_g2,
              rwkv_k_k, rwkv_k_a, rwkv_r_k, rwkv_gn_w, rwkv_gn_b):
    B, S, _ = x.shape
    dt = x.dtype
    mla_cos, mla_sin = rotary_tables(S, rope_inv_freq(MLA_ROPE_DIM, ROPE_THETA))
    nsa_cos, nsa_sin = rotary_tables(S, rope_inv_freq(NSA_ROT_DIM, ROPE_THETA))
    ret_cos, ret_sin = rotary_tables(S, jnp.float32(RET_THETA) ** (-jnp.linspace(0.0, 1.0, RET_K_DIM // 2, dtype=jnp.float32)))
    for l in range(DEPTH):
        h = rms_norm(x, norm_gains[l, 0])
        (mla_ql, mla_kvl, mla_kr, nsa_q, nsa_kc, nsa_vc, nsa_ks, nsa_vs, nsa_kw, nsa_vw, nsa_g,
         rwkv_z, ret_q, ret_k, ret_v, ret_g, merge_logits) = split_cols(h @ w_in[l], IN_SPLITS)
        branches = (
            mla_mixer(mla_ql, mla_kvl, mla_kr, mla_g_q[l], mla_g_kv[l], mla_w_uq[l], mla_w_ukv[l], mla_cos, mla_sin),
            nsa_mixer(nsa_q, nsa_kc, nsa_vc, nsa_ks, nsa_vs, nsa_kw, nsa_vw, nsa_g,
                      nsa_cmp_pos[l], nsa_cmp_w1[l], nsa_cmp_w2[l], nsa_cos, nsa_sin),
            rwkv7_mixer(rwkv_z, rwkv_mu[l], rwkv_w0[l], rwkv_w2[l], rwkv_a0[l], rwkv_a2[l], rwkv_g2[l],
                        rwkv_k_k[l], rwkv_k_a[l], rwkv_r_k[l], rwkv_gn_w[l], rwkv_gn_b[l]).astype(dt),
            retention_mixer(ret_q, ret_k, ret_v, ret_g, ret_cos, ret_sin).astype(dt),
        )
        merged = None
        for m, br in enumerate(branches):
            gate = jax.nn.sigmoid(merge_logits[..., m * D_MODEL:(m + 1) * D_MODEL])
            term = gate * (br @ w_branch[l, m])
            merged = term if merged is None else merged + term
        x = x + rms_norm(merged @ w_out[l], norm_gains[l, 1])
        h = rms_norm(x, norm_gains[l, 2])
        ff = jnp.square(jax.nn.relu(h @ w_up[l])) @ w_down[l]
        x = x + rms_norm(ff, norm_gains[l, 3])
    return x
```

```python
import functools

import numpy as np
import jax
import jax.numpy as jnp
from jax import lax
from jax.experimental import pallas as pl
from jax.experimental.pallas import tpu as pltpu

F32 = jnp.float32
BF16 = jnp.bfloat16
HIGHEST = lax.Precision.HIGHEST

V7X_LANES = 128
V7X_VMEM_LIMIT_BYTES = 56 * 1024 * 1024

D_MODEL = 2048
N_MIXERS = 4
BRANCH_WIDTH = D_MODEL // N_MIXERS
D_FF = 4 * D_MODEL
ROPE_THETA = 500000.0
NORM_EPS = 1e-6
MASK_VALUE = -1e30

MLA_NOPE_DIM = 128
MLA_ROPE_DIM = 64
MLA_V_DIM = 128
MLA_HEADS = 4
MLA_Q_RANK = 384
MLA_KV_RANK = 128

NSA_HEAD_DIM = 128
NSA_HEADS = 4
NSA_ROT_DIM = 32
NSA_CMP_LEN = 32
NSA_CMP_STRIDE = 16
NSA_SEL_BLOCK = 64
NSA_TOP_N = 16
NSA_WINDOW = 512
NSA_FORCED_SCORE = 1000.0

RWKV_HEAD_DIM = 64
RWKV_HEADS = 8
RWKV_DECAY_LORA = 96
RWKV_A_LORA = 96
RWKV_GATE_LORA = 256
RWKV_GN_EPS = 64e-5
RWKV_CHUNK = 64

RET_HEADS = 4
RET_V_DIM = 128
RET_K_DIM = 64
RET_CHUNK = 128
RET_THETA = 10000.0


def _params(*sem):
    return pltpu.CompilerParams(dimension_semantics=sem, vmem_limit_bytes=V7X_VMEM_LIMIT_BYTES)


def _dot(a, b, precision=None):
    return jnp.dot(a, b, preferred_element_type=F32, precision=precision)


def _dot_nt(a, b, precision=None):
    return lax.dot_general(a, b, (((1,), (1,)), ((), ())), preferred_element_type=F32, precision=precision)


def _dot_tn(a, b, precision=None):
    return lax.dot_general(a, b, (((0,), (0,)), ((), ())), preferred_element_type=F32, precision=precision)


def _rms(x, g, eps=NORM_EPS):
    return x * lax.rsqrt(jnp.mean(x * x, axis=-1, keepdims=True) + eps) * g


def _norm_matmul_kernel(x_ref, g_ref, w_ref, o_ref, h_ref):
    @pl.when(pl.program_id(1) == 0)
    def _():
        h_ref[...] = _rms(x_ref[...], g_ref[...]).astype(BF16)

    o_ref[...] = _dot(h_ref[...], w_ref[...]).astype(o_ref.dtype)


def norm_matmul(x, g, w, out_dtype, tm, tn):
    m, k = x.shape
    n = w.shape[1]
    assert m % tm == 0 and n % tn == 0
    return pl.pallas_call(
        _norm_matmul_kernel,
        grid=(m // tm, n // tn),
        in_specs=[pl.BlockSpec((tm, k), lambda i, j: (i, 0)),
                  pl.BlockSpec((1, k), lambda i, j: (0, 0)),
                  pl.BlockSpec((k, tn), lambda i, j: (0, j))],
        out_specs=pl.BlockSpec((tm, tn), lambda i, j: (i, j)),
        out_shape=jax.ShapeDtypeStruct((m, n), out_dtype),
        scratch_shapes=[pltpu.VMEM((tm, k), BF16)],
        compiler_params=_params("parallel", "arbitrary"),
        name="norm_matmul",
    )(x, g.reshape(1, k), w)


def _merge_kernel(y0_ref, y1_ref, y2_ref, y3_ref, g0_ref, g1_ref, g2_ref, g3_ref, wb_ref, o_ref):
    acc = None
    for m, (y_ref, g_ref) in enumerate(((y0_ref, g0_ref), (y1_ref, g1_ref), (y2_ref, g2_ref), (y3_ref, g3_ref))):
        term = jax.nn.sigmoid(g_ref[...].astype(F32)) * _dot(y_ref[...], wb_ref[m])
        acc = term if acc is None else acc + term
    o_ref[...] = acc.astype(o_ref.dtype)


def merge_branches(ys, gates, wb, tm, tn):
    m, c = ys[0].shape
    d = wb.shape[2]
    nj = d // tn
    y_spec = pl.BlockSpec((tm, c), lambda i, j: (i, 0))
    gate_specs = [pl.BlockSpec((tm, tn), functools.partial(lambda i, j, mm: (i, mm * nj + j), mm=mm))
                  for mm in range(N_MIXERS)]
    return pl.pallas_call(
        _merge_kernel,
        grid=(m // tm, nj),
        in_specs=[y_spec] * N_MIXERS + gate_specs + [pl.BlockSpec((N_MIXERS, c, tn), lambda i, j: (0, 0, j))],
        out_specs=pl.BlockSpec((tm, tn), lambda i, j: (i, j)),
        out_shape=jax.ShapeDtypeStruct((m, d), BF16),
        compiler_params=_params("parallel", "arbitrary"),
        name="merge_branches",
    )(*ys, gates, gates, gates, gates, wb)


def _matmul_norm_res_kernel(a_ref, w_ref, g_ref, x_ref, o_ref):
    y = _dot(a_ref[...], w_ref[...])
    o_ref[...] = x_ref[...] + _rms(y, g_ref[...])


def matmul_norm_residual(a, w, g, x, tm):
    m, k = a.shape
    d = w.shape[1]
    return pl.pallas_call(
        _matmul_norm_res_kernel,
        grid=(m // tm,),
        in_specs=[pl.BlockSpec((tm, k), lambda i: (i, 0)),
                  pl.BlockSpec((k, d), lambda i: (0, 0)),
                  pl.BlockSpec((1, d), lambda i: (0, 0)),
                  pl.BlockSpec((tm, d), lambda i: (i, 0))],
        out_specs=pl.BlockSpec((tm, d), lambda i: (i, 0)),
        out_shape=jax.ShapeDtypeStruct((m, d), F32),
        compiler_params=_params("parallel"),
        name="matmul_norm_residual",
    )(a, w, g.reshape(1, d), x)


def _ffn_kernel(x_ref, g1_ref, wu_ref, wd_ref, g2_ref, o_ref, h_ref, acc_ref):
    f = pl.program_id(1)

    @pl.when(f == 0)
    def _():
        h_ref[...] = _rms(x_ref[...], g1_ref[...]).astype(BF16)
        acc_ref[...] = jnp.zeros_like(acc_ref)

    u = jnp.maximum(_dot(h_ref[...], wu_ref[...]), 0.0)
    acc_ref[...] += _dot((u * u).astype(BF16), wd_ref[...])

    @pl.when(f == pl.num_programs(1) - 1)
    def _():
        o_ref[...] = x_ref[...] + _rms(acc_ref[...], g2_ref[...])


def ffn_block(x, g1, wu, wd, g2, tm, tf):
    m, d = x.shape
    ff = wu.shape[1]
    return pl.pallas_call(
        _ffn_kernel,
        grid=(m // tm, ff // tf),
        in_specs=[pl.BlockSpec((tm, d), lambda i, f: (i, 0)),
                  pl.BlockSpec((1, d), lambda i, f: (0, 0)),
                  pl.BlockSpec((d, tf), lambda i, f: (0, f)),
                  pl.BlockSpec((tf, d), lambda i, f: (f, 0)),
                  pl.BlockSpec((1, d), lambda i, f: (0, 0))],
        out_specs=pl.BlockSpec((tm, d), lambda i, f: (i, 0)),
        out_shape=jax.ShapeDtypeStruct((m, d), F32),
        scratch_shapes=[pltpu.VMEM((tm, d), BF16), pltpu.VMEM((tm, d), F32)],
        compiler_params=_params("parallel", "arbitrary"),
        name="ffn_block",
    )(x, g1.reshape(1, d), wu, wd, g2.reshape(1, d))


def _online_softmax_step(carry, s, mask, v_tile):
    m_prev, l_prev, acc_prev = carry
    s = jnp.where(mask, s, MASK_VALUE)
    m_new = jnp.maximum(m_prev, jnp.max(s, axis=-1, keepdims=True))
    alpha = jnp.exp(m_prev - m_new)
    p = jnp.where(mask, jnp.exp(s - m_new), 0.0)
    l_new = alpha * l_prev + jnp.sum(p, axis=-1, keepdims=True)
    acc_new = alpha * acc_prev + _dot(p.astype(BF16), v_tile)
    return m_new, l_new, acc_new


def _softmax_init(rows, dv):
    return (jnp.full((rows, 1), MASK_VALUE, F32), jnp.zeros((rows, 1), F32), jnp.zeros((rows, dv), F32))


def _softmax_finish(carry):
    _, l, acc = carry
    return acc / jnp.where(l > 0.0, l, 1.0)


def _mla_prep_kernel(p_ref, gq_ref, gkv_ref, wq_ref, wkv_ref, cos_ref, sin_ref,
                     qn_ref, qr_ref, kn_ref, v_ref, kr_ref, *, scale):
    p = p_ref[0].astype(F32)
    nq = _rms(p[:, :MLA_Q_RANK], gq_ref[...]).astype(BF16)
    nkv = _rms(p[:, MLA_Q_RANK:MLA_Q_RANK + MLA_KV_RANK], gkv_ref[...]).astype(BF16)
    q = _dot(nq, wq_ref[...])
    kv = _dot(nkv, wkv_ref[...])
    cos, sin = cos_ref[...], sin_ref[...]
    hw = MLA_HEADS * MLA_NOPE_DIM
    qn_ref[0] = (q[:, :hw] * scale).astype(BF16)
    for h in range(MLA_HEADS):
        a = q[:, hw + h * 128: hw + (h + 1) * 128]
        b = q[:, 2 * hw + h * 128: 2 * hw + (h + 1) * 128]
        qr_ref[0, :, h * 128:(h + 1) * 128] = ((a * cos + b * sin) * scale).astype(BF16)
    kn_ref[0] = kv[:, :hw].astype(BF16)
    v_ref[0] = kv[:, hw:].astype(BF16)
    kr_ref[0] = (p[:, 512:640] * cos + p[:, 640:768] * sin).astype(BF16)


def _mla_attn_kernel(qn_ref, qr_ref, kn_ref, kr_ref, v_ref, o_ref, *, tq, tk):
    i = pl.program_id(2)
    q0 = i * tq
    qn, qr = qn_ref[0], qr_ref[0]
    qpos = q0 + lax.broadcasted_iota(jnp.int32, (tq, tk), 0)
    kiota = lax.broadcasted_iota(jnp.int32, (tq, tk), 1)

    def body(t, carry):
        k0 = pl.multiple_of(t * tk, tk)
        s = _dot_nt(qn, kn_ref[0, pl.ds(k0, tk), :]) + _dot_nt(qr, kr_ref[0, pl.ds(k0, tk), :])
        return _online_softmax_step(carry, s, k0 + kiota <= qpos, v_ref[0, pl.ds(k0, tk), :])

    n_tiles = (q0 + tq + tk - 1) // tk
    carry = lax.fori_loop(0, n_tiles, body, _softmax_init(tq, MLA_V_DIM))
    o_ref[0] = _softmax_finish(carry).astype(o_ref.dtype)


def mla_mixer(p, g_q, g_kv, wq, wkv, cos, sin, *, tm=512, tq=256, tk=512):
    b, s, _ = p.shape
    tm, tq, tk = min(tm, s), min(tq, s), min(tk, s)
    hw = MLA_HEADS * 128
    scale = (MLA_NOPE_DIM + MLA_ROPE_DIM) ** -0.5
    full = lambda shape: pl.BlockSpec(shape, lambda bi, i: (0,) * len(shape))
    row = lambda w: pl.BlockSpec((1, tm, w), lambda bi, i: (bi, i, 0))
    outs = pl.pallas_call(
        functools.partial(_mla_prep_kernel, scale=scale),
        grid=(b, s // tm),
        in_specs=[row(768), full((1, MLA_Q_RANK)), full((1, MLA_KV_RANK)), full(wq.shape), full(wkv.shape),
                  pl.BlockSpec((tm, 128), lambda bi, i: (i, 0)), pl.BlockSpec((tm, 128), lambda bi, i: (i, 0))],
        out_specs=[row(hw), row(hw), row(hw), row(hw), row(128)],
        out_shape=[jax.ShapeDtypeStruct((b, s, hw), BF16)] * 4 + [jax.ShapeDtypeStruct((b, s, 128), BF16)],
        compiler_params=_params("parallel", "parallel"),
        name="mla_prep",
    )(p, g_q.reshape(1, -1), g_kv.reshape(1, -1), wq, wkv, cos, sin)
    qn, qr, kn, v, kr = outs
    q_spec = pl.BlockSpec((1, tq, 128), lambda bi, h, i: (bi, i, h))
    kv_spec = pl.BlockSpec((1, s, 128), lambda bi, h, i: (bi, 0, h))
    return pl.pallas_call(
        functools.partial(_mla_attn_kernel, tq=tq, tk=tk),
        grid=(b, MLA_HEADS, s // tq),
        in_specs=[q_spec, q_spec, kv_spec, pl.BlockSpec((1, s, 128), lambda bi, h, i: (bi, 0, 0)), kv_spec],
        out_specs=q_spec,
        out_shape=jax.ShapeDtypeStruct((b, s, hw), BF16),
        compiler_params=_params("parallel", "parallel", "arbitrary"),
        name="mla_attn",
    )(qn, qr, kn, kr, v)


def _nsa_prep_kernel(p_ref, cos_ref, sin_ref, q_ref, kc_ref, vc_ref, ks_ref, kw_ref, *, scale):
    cos, sin = cos_ref[...], sin_ref[...]
    hw = NSA_HEADS * NSA_HEAD_DIM
    for h in range(NSA_HEADS):
        a = p_ref[0, :, h * 128:(h + 1) * 128].astype(F32)
        b = p_ref[0, :, hw + h * 128: hw + (h + 1) * 128].astype(F32)
        q_ref[0, :, h * 128:(h + 1) * 128] = ((a * cos + b * sin) * scale).astype(BF16)
    base = 2 * hw

    def rot(col):
        a = p_ref[0, :, base + col * 128: base + (col + 1) * 128].astype(F32)
        b = p_ref[0, :, base + (col + 1) * 128: base + (col + 2) * 128].astype(F32)
        return a * cos + b * sin

    kc_ref[0] = rot(0)
    vc_ref[0] = p_ref[0, :, base + 256: base + 384].astype(F32)
    ks_ref[0] = rot(3).astype(BF16)
    kw_ref[0] = rot(6).astype(BF16)


def _gelu_tanh(x):
    return 0.5 * x * (1.0 + jnp.tanh(np.sqrt(2.0 / np.pi).astype(np.float32) * (x + 0.044715 * (x * x * x))))


def _nsa_compress_kernel(k_ref, v_ref, pos_ref, w1_ref, w2_ref, kc_ref, vc_ref):
    nrow = k_ref.shape[1]
    row = lax.broadcasted_iota(jnp.int32, (nrow, NSA_HEAD_DIM), 0)
    for z, (x_ref, o_ref) in enumerate(((k_ref, kc_ref), (v_ref, vc_ref))):
        res = _dot(x_ref[0], w1_ref[z], HIGHEST)
        pb = _dot(pos_ref[z], w1_ref[z], HIGHEST)
        bias = pb[0:1, :NSA_HEAD_DIM] + pb[1:2, NSA_HEAD_DIM:]
        nxt = pltpu.roll(res[:, NSA_HEAD_DIM:], nrow - 1, axis=0)
        hid = _gelu_tanh(res[:, :NSA_HEAD_DIM] + nxt + bias)
        out = _dot(hid, w2_ref[z], HIGHEST)
        o_ref[0] = jnp.where(row < nrow - 1, out, 0.0)


def _nsa_cmp_attn_kernel(q_ref, kc_ref, vc_ref, cover_ref, o_ref, sel_ref, *, tq, n_blk):
    i = pl.program_id(1)
    ncmp = kc_ref.shape[1]
    pos = i * tq + lax.broadcasted_iota(jnp.int32, (tq, ncmp), 0)
    n = lax.broadcasted_iota(jnp.int32, (tq, ncmp), 1)
    mask = (n * NSA_CMP_STRIDE + (NSA_CMP_LEN - 1) <= pos) & (n < ncmp - 1)
    kc, vc = kc_ref[0], vc_ref[0]
    imp = jnp.zeros((tq, 128), F32)
    for h in range(NSA_HEADS):
        q = q_ref[0, :, h * 128:(h + 1) * 128].astype(F32)
        s = jnp.where(mask, _dot_nt(q, kc, HIGHEST), MASK_VALUE)
        e = jnp.where(mask, jnp.exp(s - jnp.max(s, axis=-1, keepdims=True)), 0.0)
        l = jnp.sum(e, axis=-1, keepdims=True)
        p = e / jnp.where(l > 0.0, l, 1.0)
        o_ref[0, :, h * 128:(h + 1) * 128] = _dot(p, vc, HIGHEST)
        imp = imp + _dot(p, cover_ref[...], HIGHEST)
    jb = lax.broadcasted_iota(jnp.int32, (tq, 128), 1)
    cur = (i * tq + lax.broadcasted_iota(jnp.int32, (tq, 128), 0)) // NSA_SEL_BLOCK
    forced = (jb == 0) | (jb == cur) | (jb == cur - 1)
    visible = jb <= cur
    score = jnp.where(visible, imp + jnp.where(forced, NSA_FORCED_SCORE, 0.0), -jnp.inf)
    beaten = jnp.zeros((tq, 128), F32)
    for j in range(n_blk):
        col = jnp.broadcast_to(score[:, j:j + 1], (tq, 128))
        beaten = beaten + jnp.where((col > score) | ((col == score) & (jb > j)), 1.0, 0.0)
    keep = (beaten < float(NSA_TOP_N)) & visible
    sel_ref[0] = jnp.where(keep, 1.0, 0.0).astype(BF16)


def _nsa_attn_kernel(q_ref, ks_ref, vs_ref, kw_ref, vw_ref, sel_ref, e_ref, oc_ref, g_ref, o_ref, *, tq, tk):
    i = pl.program_id(1)
    q0 = i * tq
    rows = NSA_HEADS * tq
    q4 = jnp.concatenate([q_ref[0, :, h * 128:(h + 1) * 128] for h in range(NSA_HEADS)], axis=0)
    sel4 = jnp.concatenate([sel_ref[0]] * NSA_HEADS, axis=0)
    qpos = q0 + (lax.broadcasted_iota(jnp.int32, (rows, tk), 0) & (tq - 1))
    kiota = lax.broadcasted_iota(jnp.int32, (rows, tk), 1)

    def slc_body(t, carry):
        k0 = pl.multiple_of(t * tk, tk)
        s = _dot_nt(q4, ks_ref[0, pl.ds(k0, tk), :])
        kept = _dot(sel4, e_ref[:, pl.ds(k0, tk)])
        mask = (k0 + kiota <= qpos) & (kept > 0.5)
        return _online_softmax_step(carry, s, mask, vs_ref[0, pl.ds(k0, tk), :])

    o_slc = _softmax_finish(lax.fori_loop(0, (q0 + tq + tk - 1) // tk, slc_body, _softmax_init(rows, NSA_HEAD_DIM)))

    def win_body(t, carry):
        k0 = pl.multiple_of(t * tk, tk)
        s = _dot_nt(q4, kw_ref[0, pl.ds(k0, tk), :])
        dist = qpos - (k0 + kiota)
        return _online_softmax_step(carry, s, (dist >= 0) & (dist < NSA_WINDOW), vw_ref[0, pl.ds(k0, tk), :])

    t_lo = jnp.maximum(q0 - NSA_WINDOW + 1, 0) // tk
    o_win = _softmax_finish(lax.fori_loop(t_lo, (q0 + tq + tk - 1) // tk, win_body, _softmax_init(rows, NSA_HEAD_DIM)))

    g = jax.nn.sigmoid(g_ref[0].astype(F32))
    for h in range(NSA_HEADS):
        sl = slice(h * tq, (h + 1) * tq)
        o = (g[:, 3 * h:3 * h + 1] * oc_ref[0, :, h * 128:(h + 1) * 128]
             + g[:, 3 * h + 1:3 * h + 2] * o_slc[sl] + g[:, 3 * h + 2:3 * h + 3] * o_win[sl])
        o_ref[0, :, h * 128:(h + 1) * 128] = o.astype(o_ref.dtype)


def nsa_mixer(p, pos_emb, w1, w2, cos, sin, *, tm=512, tq=128, tk=256):
    b, s, _ = p.shape
    tm, tk = min(tm, s), min(tk, s)
    hw = NSA_HEADS * NSA_HEAD_DIM
    scale = NSA_HEAD_DIM ** -0.5
    row = lambda w: pl.BlockSpec((1, tm, w), lambda bi, i: (bi, i, 0))
    tab = pl.BlockSpec((tm, 128), lambda bi, i: (i, 0))
    q, kc_in, vc_in, ks, kw = pl.pallas_call(
        functools.partial(_nsa_prep_kernel, scale=scale),
        grid=(b, s // tm),
        in_specs=[row(p.shape[2]), tab, tab],
        out_specs=[row(hw), row(128), row(128), row(128), row(128)],
        out_shape=[jax.ShapeDtypeStruct((b, s, hw), BF16), jax.ShapeDtypeStruct((b, s, 128), F32),
                   jax.ShapeDtypeStruct((b, s, 128), F32), jax.ShapeDtypeStruct((b, s, 128), BF16),
                   jax.ShapeDtypeStruct((b, s, 128), BF16)],
        compiler_params=_params("parallel", "parallel"),
        name="nsa_prep",
    )(p, cos, sin)

    nrow = s // NSA_CMP_STRIDE
    fw = NSA_CMP_STRIDE * NSA_HEAD_DIM
    half = w1.shape[1] // 2
    w1r = jnp.concatenate([w1[:, :half], w1[:, half:]], axis=2)
    pos2 = jnp.pad(pos_emb.reshape(2, 2, fw), ((0, 0), (0, 6), (0, 0)))
    per_b = lambda shape: pl.BlockSpec((1,) + shape, lambda bi: (bi, 0, 0))
    const3 = lambda shape: pl.BlockSpec(shape, lambda bi: (0, 0, 0))
    kc, vc = pl.pallas_call(
        _nsa_compress_kernel,
        grid=(b,),
        in_specs=[per_b((nrow, fw)), per_b((nrow, fw)), const3(pos2.shape), const3(w1r.shape), const3(w2.shape)],
        out_specs=[per_b((nrow, NSA_HEAD_DIM))] * 2,
        out_shape=[jax.ShapeDtypeStruct((b, nrow, NSA_HEAD_DIM), F32)] * 2,
        compiler_params=_params("parallel"),
        name="nsa_compress",
    )(kc_in.reshape(b, nrow, fw), vc_in.reshape(b, nrow, fw), pos2, w1r, w2)

    n_blk = s // NSA_SEL_BLOCK
    assert n_blk <= 128 and tq & (tq - 1) == 0
    cmp_start = np.arange(nrow) * NSA_CMP_STRIDE
    cmp_end = cmp_start + NSA_CMP_LEN - 1
    blk_start = np.arange(128) * NSA_SEL_BLOCK
    cover = ((cmp_start[:, None] <= blk_start[None, :] + NSA_SEL_BLOCK - 1)
             & (cmp_end[:, None] >= blk_start[None, :]) & (np.arange(128)[None, :] < n_blk)
             & (np.arange(nrow)[:, None] < nrow - 1)).astype(np.float32)
    qrow = lambda w: pl.BlockSpec((1, tq, w), lambda bi, i: (bi, i, 0))
    cmp_spec = pl.BlockSpec((1, nrow, NSA_HEAD_DIM), lambda bi, i: (bi, 0, 0))
    o_cmp, sel = pl.pallas_call(
        functools.partial(_nsa_cmp_attn_kernel, tq=tq, n_blk=n_blk),
        grid=(b, s // tq),
        in_specs=[qrow(hw), cmp_spec, cmp_spec, pl.BlockSpec((nrow, 128), lambda bi, i: (0, 0))],
        out_specs=[qrow(hw), qrow(128)],
        out_shape=[jax.ShapeDtypeStruct((b, s, hw), F32), jax.ShapeDtypeStruct((b, s, 128), BF16)],
        compiler_params=_params("parallel", "parallel"),
        name="nsa_cmp_attn",
    )(q, kc, vc, jnp.asarray(cover))

    expand = (np.arange(128)[:, None] == (np.arange(s)[None, :] // NSA_SEL_BLOCK)).astype(np.float32)
    kv_spec = pl.BlockSpec((1, s, 128), lambda bi, i: (bi, 0, 0))
    base = 2 * hw
    col = lambda c: pl.BlockSpec((1, s, 128), functools.partial(lambda bi, i, cc: (bi, 0, cc), cc=(base // 128) + c))
    return pl.pallas_call(
        functools.partial(_nsa_attn_kernel, tq=tq, tk=tk),
        grid=(b, s // tq),
        in_specs=[qrow(hw), kv_spec, col(5), kv_spec, col(8), qrow(128),
                  pl.BlockSpec((128, s), lambda bi, i: (0, 0)), qrow(hw),
                  pl.BlockSpec((1, tq, 128), lambda bi, i: (bi, i, (base // 128) + 9))],
        out_specs=qrow(hw),
        out_shape=jax.ShapeDtypeStruct((b, s, hw), BF16),
        compiler_params=_params("parallel", "arbitrary"),
        name="nsa_attn",
    )(q, ks, p, kw, p, sel, jnp.asarray(expand, dtype=BF16), o_cmp, p)


def _rwkv_prep_kernel(z_ref, zp_ref, mu_ref, w0_ref, a0_ref, kk_ref, ka_ref, rk_ref, w2_ref, a2_ref, g2_ref, bd_ref,
                      r_ref, lw_ref, k_ref, v_ref, kkn_ref, b_ref, g_ref, bonus_ref):
    i = pl.program_id(1)
    z = z_ref[0]
    tm = z.shape[0]
    prev_row = jnp.where(i > 0, zp_ref[0, 7:8, :], 0.0)
    rowid = lax.broadcasted_iota(jnp.int32, z.shape, 0)
    z_prev = jnp.where(rowid == 0, prev_row, pltpu.roll(z, 1, axis=0))
    z = z + (z_prev - z) * mu_ref[...]
    c = BRANCH_WIDTH
    r, k, v = z[:, :c], z[:, c:2 * c], z[:, 2 * c:3 * c]
    wd, ad, gd = z[:, 3 * c:3 * c + 128], z[:, 3 * c + 128:3 * c + 256], z[:, 3 * c + 256:3 * c + 512]
    wpre = -(w0_ref[...] + _dot(jnp.tanh(wd), w2_ref[...], HIGHEST))
    w_log = -(jnp.maximum(wpre, 0.0) + jnp.log1p(jnp.exp(-jnp.abs(wpre)))) - 0.5
    lw_ref[0] = -jnp.exp(w_log)
    a = jax.nn.sigmoid(a0_ref[...] + _dot(ad, a2_ref[...], HIGHEST))
    g_ref[0] = _dot(jax.nn.sigmoid(gd), g2_ref[...], HIGHEST)
    kk = k * kk_ref[...]
    kk = kk * lax.rsqrt(jnp.maximum(_dot(kk * kk, bd_ref[...], HIGHEST), 1e-24))
    k = k * (1.0 + (a - 1.0) * ka_ref[...])
    r_ref[0] = r
    k_ref[0] = k
    v_ref[0] = v
    kkn_ref[0] = kk
    b_ref[0] = kk * a
    bonus_ref[0] = _dot(r * k * rk_ref[...], bd_ref[...], HIGHEST) * v


def _rwkv_scan_kernel(r_ref, lw_ref, k_ref, v_ref, kk_ref, b_ref, o_ref, state_ref, *, precision):
    L = r_ref.shape[2]

    @pl.when(pl.program_id(1) == 0)
    def _():
        state_ref[...] = jnp.zeros_like(state_ref)

    ri = lax.broadcasted_iota(jnp.int32, (L, L), 0)
    ci = lax.broadcasted_iota(jnp.int32, (L, L), 1)
    tril_incl = (ci <= ri)
    tril_strict = (ci < ri)
    ones_tril = jnp.where(tril_incl, 1.0, 0.0)
    eye = jnp.where(ci == ri, 1.0, 0.0)
    cast = (lambda t: t) if precision is not None else (lambda t: t.astype(BF16))
    mm = lambda a, b: _dot(cast(a), cast(b), precision)
    mm_nt = lambda a, b: _dot_nt(cast(a), cast(b), precision)
    mm_tn = lambda a, b: _dot_tn(cast(a), cast(b), precision)

    for h in range(r_ref.shape[1]):
        r, lw, k, v, kk, b = r_ref[0, h], lw_ref[0, h], k_ref[0, h], v_ref[0, h], kk_ref[0, h], b_ref[0, h]
        cum = _dot(ones_tril, lw, HIGHEST)
        total = cum[L - 1:L, :]
        dec_in, dec_out = jnp.exp(cum), jnp.exp(-cum)
        x = jnp.concatenate([kk * jnp.exp(cum - lw), r * dec_in], axis=0)
        y = jnp.concatenate([k * dec_out, b * dec_out], axis=0)
        g = mm_nt(x, y)
        a_kk = jnp.where(tril_strict, g[:L, :L], 0.0)
        a_kb = jnp.where(tril_strict, g[:L, L:], 0.0)
        a_rk = jnp.where(tril_incl, g[L:, :L], 0.0)
        a_rb = jnp.where(tril_incl, g[L:, L:], 0.0)
        tinv = eye - a_kb
        pw = a_kb
        for _ in range(int(np.log2(L)) - 1):
            pw = mm(pw, pw)
            tinv = tinv + mm(tinv, pw)
        st = state_ref[h]
        xs = mm_nt(x, st)
        u = mm(tinv, -(xs[:L] + mm(a_kk, v)))
        o = xs[L:] + mm(a_rk, v) + mm(a_rb, u)
        dec_end = jnp.exp(total - cum)
        state_ref[h] = st * jnp.exp(total) + mm_tn(v, k * dec_end) + mm_tn(u, b * dec_end)
        c = o - jnp.mean(o, axis=-1, keepdims=True)
        o_ref[0, h] = c * lax.rsqrt(jnp.mean(c * c, axis=-1, keepdims=True) + RWKV_GN_EPS)


def _rwkv_post_kernel(o_ref, bonus_ref, g_ref, w_ref, b_ref, y_ref):
    y_ref[0] = ((o_ref[0] * w_ref[...] + b_ref[...] + bonus_ref[0]) * g_ref[0]).astype(y_ref.dtype)


def rwkv7_mixer(z, mu, w0, w2, a0, a2, g2, k_k, k_a, r_k, gn_w, gn_b, *, tm=256, precision=None):
    b, s, zw = z.shape
    tm = min(tm, s)
    c, hd, nh = BRANCH_WIDTH, RWKV_HEAD_DIM, RWKV_HEADS
    L = min(RWKV_CHUNK, s)
    bd = jnp.asarray(np.kron(np.eye(nh, dtype=np.float32), np.ones((hd, hd), np.float32)))
    row = lambda w: pl.BlockSpec((1, tm, w), lambda bi, i: (bi, i, 0))
    vec = lambda w: pl.BlockSpec((1, w), lambda bi, i: (0, 0))
    mat = lambda shape: pl.BlockSpec(shape, lambda bi, i: (0, 0))
    outs = pl.pallas_call(
        _rwkv_prep_kernel,
        grid=(b, s // tm),
        in_specs=[row(zw), pl.BlockSpec((1, 8, zw), lambda bi, i: (bi, jnp.maximum(i * (tm // 8) - 1, 0), 0)),
                  vec(zw), vec(c), vec(c), vec(c), vec(c), vec(c), mat(w2.shape), mat(a2.shape), mat(g2.shape), mat(bd.shape)],
        out_specs=[row(c)] * 8,
        out_shape=[jax.ShapeDtypeStruct((b, s, c), F32)] * 8,
        compiler_params=_params("parallel", "parallel"),
        name="rwkv_prep",
    )(z, z, mu.reshape(1, zw), w0.reshape(1, c), a0.reshape(1, c), k_k.reshape(1, c), k_a.reshape(1, c),
      r_k.reshape(1, c), w2, a2, g2, bd)
    r, lw, k, v, kkn, bb, g, bonus = outs
    heads = lambda t: t.reshape(b, s, nh, hd).transpose(0, 2, 1, 3)
    blk = pl.BlockSpec((1, nh, L, hd), lambda bi, ci: (bi, 0, ci, 0))
    o = pl.pallas_call(
        functools.partial(_rwkv_scan_kernel, precision=precision),
        grid=(b, s // L),
        in_specs=[blk] * 6,
        out_specs=blk,
        out_shape=jax.ShapeDtypeStruct((b, nh, s, hd), F32),
        scratch_shapes=[pltpu.VMEM((nh, hd, hd), F32)],
        compiler_params=_params("parallel", "arbitrary"),
        name="rwkv_scan",
    )(heads(r), heads(lw), heads(k), heads(v), heads(kkn), heads(bb))
    o = o.transpose(0, 2, 1, 3).reshape(b, s, c)
    return pl.pallas_call(
        _rwkv_post_kernel,
        grid=(b, s // tm),
        in_specs=[row(c), row(c), row(c), vec(c), vec(c)],
        out_specs=row(c),
        out_shape=jax.ShapeDtypeStruct((b, s, c), BF16),
        compiler_params=_params("parallel", "parallel"),
        name="rwkv_post",
    )(o, bonus, g, gn_w.reshape(1, c), gn_b.reshape(1, c))


def _retention_kernel(p_ref, cos_ref, sin_ref, inner_ref, qd_ref, kd_ref, cd_ref, o_ref, state_ref):
    @pl.when(pl.program_id(1) == 0)
    def _():
        state_ref[...] = jnp.zeros_like(state_ref)

    C = p_ref.shape[1]
    kw = RET_HEADS * RET_K_DIM
    cos, sin = cos_ref[...], sin_ref[...]
    q = p_ref[0, :, 0:kw].astype(F32) * cos + p_ref[0, :, kw:2 * kw].astype(F32) * sin
    k = (p_ref[0, :, 2 * kw:3 * kw].astype(F32) * cos + p_ref[0, :, 3 * kw:4 * kw].astype(F32) * sin) * RET_K_DIM ** -0.5
    lane = lax.broadcasted_iota(jnp.int32, (C, 128), 1)
    for h in range(RET_HEADS):
        pair, sub = h // 2, h % 2
        in_head = (lane // RET_K_DIM) == sub
        qh = jnp.where(in_head, q[:, pair * 128:(pair + 1) * 128], 0.0)
        kh = jnp.where(in_head, k[:, pair * 128:(pair + 1) * 128], 0.0)
        vh = p_ref[0, :, 4 * kw + h * RET_V_DIM: 4 * kw + (h + 1) * RET_V_DIM]
        gate = p_ref[0, :, 4 * kw + RET_HEADS * RET_V_DIM + h * RET_V_DIM:
                     4 * kw + RET_HEADS * RET_V_DIM + (h + 1) * RET_V_DIM].astype(F32)
        scores = _dot_nt(qh.astype(BF16), kh.astype(BF16)) * inner_ref[h]
        st = state_ref[h]
        o = _dot(scores.astype(BF16), vh) + _dot((qh * qd_ref[h]).astype(BF16), st.astype(BF16))
        state_ref[h] = st * cd_ref[h, 0:1, :] + _dot_tn((kh * kd_ref[h]).astype(BF16), vh)
        c = o - jnp.mean(o, axis=-1, keepdims=True)
        y = c * lax.rsqrt(jnp.mean(c * c, axis=-1, keepdims=True) + NORM_EPS)
        o_ref[0, :, h * RET_V_DIM:(h + 1) * RET_V_DIM] = (y * (gate * jax.nn.sigmoid(gate))).astype(o_ref.dtype)


def retention_mixer(p, cos, sin):
    b, s, pw = p.shape
    H, C = RET_HEADS, min(RET_CHUNK, s)
    log_gamma = jnp.log1p(-jnp.exp2(-5.0 - jnp.arange(H, dtype=F32)))
    n = jnp.arange(C, dtype=F32)
    dist = n[:, None] - n[None, :]
    inner = jnp.where(dist >= 0, jnp.exp(jnp.maximum(dist, 0.0) * log_gamma[:, None, None]), 0.0)
    q_decay = jnp.exp((n + 1.0) * log_gamma[:, None])
    k_decay = jnp.exp((C - 1.0 - n) * log_gamma[:, None])
    chunk_decay = jnp.exp(C * log_gamma)
    lanes = lambda t: jnp.broadcast_to(t[:, :, None], (H, C, 128))
    const = lambda shape: pl.BlockSpec(shape, lambda bi, ci: (0,) * len(shape))
    kw = RET_HEADS * RET_K_DIM
    return pl.pallas_call(
        _retention_kernel,
        grid=(b, s // C),
        in_specs=[pl.BlockSpec((1, C, pw), lambda bi, ci: (bi, ci, 0)),
                  pl.BlockSpec((C, kw), lambda bi, ci: (ci, 0)), pl.BlockSpec((C, kw), lambda bi, ci: (ci, 0)),
                  const((H, C, C)), const((H, C, 128)), const((H, C, 128)), const((H, 8, 128))],
        out_specs=pl.BlockSpec((1, C, H * RET_V_DIM), lambda bi, ci: (bi, ci, 0)),
        out_shape=jax.ShapeDtypeStruct((b, s, H * RET_V_DIM), BF16),
        scratch_shapes=[pltpu.VMEM((H, 128, 128), F32)],
        compiler_params=_params("parallel", "arbitrary"),
        name="retention",
    )(p, cos, sin, inner, lanes(q_decay), lanes(k_decay), jnp.broadcast_to(chunk_decay[:, None, None], (H, 8, 128)))


def _rot_half_cols(w, half):
    return jnp.concatenate([-w[..., half:2 * half], w[..., :half]], axis=-1)


def _pad_cols(w, width):
    return jnp.pad(w, [(0, 0)] * (w.ndim - 1) + [(0, width - w.shape[-1])])


def _rope_tables(s, inv_freq, width, fill_cos):
    ang = jnp.arange(s, dtype=F32)[:, None] * inv_freq[None, :]
    cos, sin = jnp.cos(ang), jnp.sin(ang)
    rot = 2 * cos.shape[1]
    cos2 = jnp.concatenate([cos, cos, jnp.full((s, width - rot), fill_cos, F32)], axis=1)
    sin2 = jnp.concatenate([sin, sin, jnp.zeros((s, width - rot), F32)], axis=1)
    return cos2, sin2


def _split_offsets():
    sizes = (MLA_Q_RANK, MLA_KV_RANK, MLA_ROPE_DIM, 512, 128, 128, 128, 128, 128, 128, 12,
             3 * BRANCH_WIDTH + RWKV_DECAY_LORA + RWKV_A_LORA + RWKV_GATE_LORA, 256, 256, 512, 512, N_MIXERS * D_MODEL)
    offs = np.concatenate([[0], np.cumsum(sizes)])
    return [(int(offs[i]), int(offs[i + 1])) for i in range(len(sizes))]


def _pack_in_proj(w):
    sl = [w[:, a:b] for a, b in _split_offsets()]
    (ql, kvl, kr, nq, nkc, nvc, nks, nvs, nkw, nvw, ng, rz, rq, rk, rv, rg, mg) = sl
    w_mla = jnp.concatenate([ql, kvl, _pad_cols(kr, 128), _pad_cols(_rot_half_cols(kr, 32), 128)], axis=1)
    nh = NSA_ROT_DIM // 2
    rot128 = lambda t: _pad_cols(_rot_half_cols(t[:, :NSA_ROT_DIM], nh), 128)
    nq_rot = jnp.concatenate([rot128(nq[:, h * 128:(h + 1) * 128]) for h in range(NSA_HEADS)], axis=1)
    w_nsa = jnp.concatenate([nq, nq_rot, nkc, rot128(nkc), nvc, nks, rot128(nks), nvs, nkw, rot128(nkw), nvw,
                             _pad_cols(ng, 128)], axis=1)
    rh = RET_K_DIM // 2
    ret_rot = lambda t: jnp.concatenate([_rot_half_cols(t[:, h * 64:(h + 1) * 64], rh) for h in range(RET_HEADS)], axis=1)
    w_ret = jnp.concatenate([rq, ret_rot(rq), rk, ret_rot(rk), rv, rg], axis=1)
    c = BRANCH_WIDTH
    w_rwkv = jnp.concatenate([rz[:, :3 * c], _pad_cols(rz[:, 3 * c:3 * c + 96], 128),
                              _pad_cols(rz[:, 3 * c + 96:3 * c + 192], 128), rz[:, 3 * c + 192:]], axis=1)
    bf = lambda t: t.astype(BF16)
    return bf(w_mla), bf(w_nsa), bf(w_ret), bf(w_rwkv), bf(mg)


def _pack_mla_weights(w_uq, w_ukv):
    H, dn, dr, dv = MLA_HEADS, MLA_NOPE_DIM, MLA_ROPE_DIM, MLA_V_DIM
    uq = w_uq.reshape(-1, H, dn + dr)
    nope = uq[:, :, :dn].reshape(-1, H * dn)
    rope = uq[:, :, dn:]
    rope_p = _pad_cols(rope, 128).reshape(-1, H * 128)
    rope_r = _pad_cols(_rot_half_cols(rope, dr // 2), 128).reshape(-1, H * 128)
    ukv = w_ukv.reshape(-1, H, dn + dv)
    wkv = jnp.concatenate([ukv[:, :, :dn].reshape(-1, H * dn), ukv[:, :, dn:].reshape(-1, H * dv)], axis=1)
    return jnp.concatenate([nope, rope_p, rope_r], axis=1).astype(BF16), wkv.astype(BF16)


def _pad_rows(w, rows):
    return jnp.pad(w, ((0, rows - w.shape[0]), (0, 0)))


def kernel(x, w_in, w_branch, w_out, w_up, w_down, norm_gains, mla_g_q, mla_g_kv, mla_w_uq, mla_w_ukv,
           nsa_cmp_pos, nsa_cmp_w1, nsa_cmp_w2, rwkv_mu, rwkv_w0, rwkv_w2, rwkv_a0, rwkv_a2, rwkv_g2,
           rwkv_k_k, rwkv_k_a, rwkv_r_k, rwkv_gn_w, rwkv_gn_b):
    B, S, D = x.shape
    depth = w_in.shape[0]
    T = B * S
    tm = min(512, T)
    mla_cos, mla_sin = _rope_tables(S, jnp.float32(ROPE_THETA) ** (-jnp.arange(0, MLA_ROPE_DIM, 2, dtype=F32) / MLA_ROPE_DIM),
                                    128, 0.0)
    nsa_cos, nsa_sin = _rope_tables(S, jnp.float32(ROPE_THETA) ** (-jnp.arange(0, NSA_ROT_DIM, 2, dtype=F32) / NSA_ROT_DIM),
                                    128, 1.0)
    ret_cos, ret_sin = _rope_tables(S, jnp.float32(RET_THETA) ** (-jnp.linspace(0.0, 1.0, RET_K_DIM // 2, dtype=F32)),
                                    RET_K_DIM, 0.0)
    ret_cos, ret_sin = jnp.tile(ret_cos, (1, RET_HEADS)), jnp.tile(ret_sin, (1, RET_HEADS))
    c = BRANCH_WIDTH
    x = x.reshape(T, D)
    for l in range(depth):
        w_mla, w_nsa, w_ret, w_rwkv, w_gate = _pack_in_proj(w_in[l])
        g_pre = norm_gains[l, 0]
        p_mla = norm_matmul(x, g_pre, w_mla, BF16, tm, w_mla.shape[1])
        p_nsa = norm_matmul(x, g_pre, w_nsa, BF16, tm, w_nsa.shape[1] // 3)
        p_ret = norm_matmul(x, g_pre, w_ret, BF16, tm, w_ret.shape[1] // 2)
        p_rwkv = norm_matmul(x, g_pre, w_rwkv, F32, tm, w_rwkv.shape[1] // 2)
        gates = norm_matmul(x, g_pre, w_gate, BF16, tm, 1024)

        wq, wkv = _pack_mla_weights(mla_w_uq[l], mla_w_ukv[l])
        y_mla = mla_mixer(p_mla.reshape(B, S, -1), mla_g_q[l], mla_g_kv[l], wq, wkv, mla_cos, mla_sin)
        y_nsa = nsa_mixer(p_nsa.reshape(B, S, -1), nsa_cmp_pos[l], nsa_cmp_w1[l], nsa_cmp_w2[l], nsa_cos, nsa_sin)
        mu = rwkv_mu[l]
        mu_p = jnp.concatenate([mu[:3 * c], _pad_cols(mu[3 * c:3 * c + 96], 128), _pad_cols(mu[3 * c + 96:3 * c + 192], 128),
                                mu[3 * c + 192:]])
        y_rwkv = rwkv7_mixer(p_rwkv.reshape(B, S, -1), mu_p, rwkv_w0[l], _pad_rows(rwkv_w2[l], 128), rwkv_a0[l],
                             _pad_rows(rwkv_a2[l], 128), rwkv_g2[l], rwkv_k_k[l], rwkv_k_a[l], rwkv_r_k[l].reshape(-1),
                             rwkv_gn_w[l], rwkv_gn_b[l])
        y_ret = retention_mixer(p_ret.reshape(B, S, -1), ret_cos, ret_sin)

        ys = [y.reshape(T, c) for y in (y_mla, y_nsa, y_rwkv, y_ret)]
        merged = merge_branches(ys, gates, w_branch[l].astype(BF16), tm, 512)
        x = matmul_norm_residual(merged, w_out[l].astype(BF16), norm_gains[l, 1], x, tm)
        x = ffn_block(x, norm_gains[l, 2], w_up[l].astype(BF16), w_down[l].astype(BF16), norm_gains[l, 3], tm, 1024)
    return x.reshape(B, S, D)
```

```python
import functools

import numpy as np
import jax
import jax.numpy as jnp
from jax import lax
from jax.experimental import pallas as pl
from jax.experimental.pallas import tpu as pltpu

F32 = jnp.float32
BF16 = jnp.bfloat16
HIGHEST = lax.Precision.HIGHEST

V7X_LANES = 128
V7X_VMEM_LIMIT_BYTES = 56 * 1024 * 1024

D_MODEL = 2048
N_MIXERS = 4
BRANCH_WIDTH = D_MODEL // N_MIXERS
D_FF = 4 * D_MODEL
ROPE_THETA = 500000.0
NORM_EPS = 1e-6
MASK_VALUE = -1e30

MLA_NOPE_DIM = 128
MLA_ROPE_DIM = 64
MLA_V_DIM = 128
MLA_HEADS = 4
MLA_Q_RANK = 384
MLA_KV_RANK = 128

NSA_HEAD_DIM = 128
NSA_HEADS = 4
NSA_ROT_DIM = 32
NSA_CMP_LEN = 32
NSA_CMP_STRIDE = 16
NSA_SEL_BLOCK = 64
NSA_TOP_N = 16
NSA_WINDOW = 512
NSA_FORCED_SCORE = 1000.0

RWKV_HEAD_DIM = 64
RWKV_HEADS = 8
RWKV_DECAY_LORA = 96
RWKV_A_LORA = 96
RWKV_GATE_LORA = 256
RWKV_GN_EPS = 64e-5
RWKV_CHUNK = 64

RET_HEADS = 4
RET_V_DIM = 128
RET_K_DIM = 64
RET_CHUNK = 128
RET_THETA = 10000.0


def _params(*sem):
    return pltpu.CompilerParams(dimension_semantics=sem, vmem_limit_bytes=V7X_VMEM_LIMIT_BYTES)


def _dot(a, b, precision=None):
    return jnp.dot(a, b, preferred_element_type=F32, precision=precision)


def _dot_nt(a, b, precision=None):
    return lax.dot_general(a, b, (((1,), (1,)), ((), ())), preferred_element_type=F32, precision=precision)


def _dot_tn(a, b, precision=None):
    return lax.dot_general(a, b, (((0,), (0,)), ((), ())), preferred_element_type=F32, precision=precision)


def _rms(x, g, eps=NORM_EPS):
    return x * lax.rsqrt(jnp.mean(x * x, axis=-1, keepdims=True) + eps) * g


def _norm_matmul_kernel(x_ref, g_ref, w_ref, o_ref, h_ref):
    @pl.when(pl.program_id(1) == 0)
    def _():
        h_ref[...] = _rms(x_ref[...], g_ref[...]).astype(BF16)

    o_ref[...] = _dot(h_ref[...], w_ref[...]).astype(o_ref.dtype)


def norm_matmul(x, g, w, out_dtype, tm, tn):
    m, k = x.shape
    n = w.shape[1]
    assert m % tm == 0 and n % tn == 0
    return pl.pallas_call(
        _norm_matmul_kernel,
        grid=(m // tm, n // tn),
        in_specs=[pl.BlockSpec((tm, k), lambda i, j: (i, 0)),
                  pl.BlockSpec((1, k), lambda i, j: (0, 0)),
                  pl.BlockSpec((k, tn), lambda i, j: (0, j))],
        out_specs=pl.BlockSpec((tm, tn), lambda i, j: (i, j)),
        out_shape=jax.ShapeDtypeStruct((m, n), out_dtype),
        scratch_shapes=[pltpu.VMEM((tm, k), BF16)],
        compiler_params=_params("parallel", "arbitrary"),
        name="norm_matmul",
    )(x, g.reshape(1, k), w)


def _merge_kernel(y0_ref, y1_ref, y2_ref, y3_ref, g0_ref, g1_ref, g2_ref, g3_ref, wb_ref, o_ref):
    acc = None
    for m, (y_ref, g_ref) in enumerate(((y0_ref, g0_ref), (y1_ref, g1_ref), (y2_ref, g2_ref), (y3_ref, g3_ref))):
        term = jax.nn.sigmoid(g_ref[...].astype(F32)) * _dot(y_ref[...], wb_ref[m])
        acc = term if acc is None else acc + term
    o_ref[...] = acc.astype(o_ref.dtype)


def merge_branches(ys, gates, wb, tm, tn):
    m, c = ys[0].shape
    d = wb.shape[2]
    nj = d // tn
    y_spec = pl.BlockSpec((tm, c), lambda i, j: (i, 0))
    gate_specs = [pl.BlockSpec((tm, tn), functools.partial(lambda i, j, mm: (i, mm * nj + j), mm=mm))
                  for mm in range(N_MIXERS)]
    return pl.pallas_call(
        _merge_kernel,
        grid=(m // tm, nj),
        in_specs=[y_spec] * N_MIXERS + gate_specs + [pl.BlockSpec((N_MIXERS, c, tn), lambda i, j: (0, 0, j))],
        out_specs=pl.BlockSpec((tm, tn), lambda i, j: (i, j)),
        out_shape=jax.ShapeDtypeStruct((m, d), BF16),
        compiler_params=_params("parallel", "arbitrary"),
        name="merge_branches",
    )(*ys, gates, gates, gates, gates, wb)


def _matmul_norm_res_kernel(a_ref, w_ref, g_ref, x_ref, o_ref):
    y = _dot(a_ref[...], w_ref[...])
    o_ref[...] = x_ref[...] + _rms(y, g_ref[...])


def matmul_norm_residual(a, w, g, x, tm):
    m, k = a.shape
    d = w.shape[1]
    return pl.pallas_call(
        _matmul_norm_res_kernel,
        grid=(m // tm,),
        in_specs=[pl.BlockSpec((tm, k), lambda i: (i, 0)),
                  pl.BlockSpec((k, d), lambda i: (0, 0)),
                  pl.BlockSpec((1, d), lambda i: (0, 0)),
                  pl.BlockSpec((tm, d), lambda i: (i, 0))],
        out_specs=pl.BlockSpec((tm, d), lambda i: (i, 0)),
        out_shape=jax.ShapeDtypeStruct((m, d), F32),
        compiler_params=_params("parallel"),
        name="matmul_norm_residual",
    )(a, w, g.reshape(1, d), x)


def _ffn_kernel(x_ref, g1_ref, wu_ref, wd_ref, g2_ref, o_ref, h_ref, acc_ref):
    f = pl.program_id(1)

    @pl.when(f == 0)
    def _():
        h_ref[...] = _rms(x_ref[...], g1_ref[...]).astype(BF16)
        acc_ref[...] = jnp.zeros_like(acc_ref)

    u = jnp.maximum(_dot(h_ref[...], wu_ref[...]), 0.0)
    acc_ref[...] += _dot((u * u).astype(BF16), wd_ref[...])

    @pl.when(f == pl.num_programs(1) - 1)
    def _():
        o_ref[...] = x_ref[...] + _rms(acc_ref[...], g2_ref[...])


def ffn_block(x, g1, wu, wd, g2, tm, tf):
    m, d = x.shape
    ff = wu.shape[1]
    return pl.pallas_call(
        _ffn_kernel,
        grid=(m // tm, ff // tf),
        in_specs=[pl.BlockSpec((tm, d), lambda i, f: (i, 0)),
                  pl.BlockSpec((1, d), lambda i, f: (0, 0)),
                  pl.BlockSpec((d, tf), lambda i, f: (0, f)),
                  pl.BlockSpec((tf, d), lambda i, f: (f, 0)),
                  pl.BlockSpec((1, d), lambda i, f: (0, 0))],
        out_specs=pl.BlockSpec((tm, d), lambda i, f: (i, 0)),
        out_shape=jax.ShapeDtypeStruct((m, d), F32),
        scratch_shapes=[pltpu.VMEM((tm, d), BF16), pltpu.VMEM((tm, d), F32)],
        compiler_params=_params("parallel", "arbitrary"),
        name="ffn_block",
    )(x, g1.reshape(1, d), wu, wd, g2.reshape(1, d))


def _online_softmax_step(carry, s, v_tile, mask=None):
    m_prev, l_prev, acc_prev = carry
    if mask is not None:
        s = jnp.where(mask, s, -jnp.inf)
    m_new = jnp.maximum(m_prev, jnp.max(s, axis=-1, keepdims=True))
    alpha = jnp.exp(m_prev - m_new)
    p = jnp.exp(s - m_new)
    l_new = alpha * l_prev + jnp.sum(p, axis=-1, keepdims=True)
    acc_new = alpha * acc_prev + _dot(p.astype(BF16), v_tile)
    return m_new, l_new, acc_new


def _softmax_init(rows, dv):
    return (jnp.full((rows, 1), MASK_VALUE, F32), jnp.zeros((rows, 1), F32), jnp.zeros((rows, dv), F32))


def _softmax_finish(carry):
    _, l, acc = carry
    return acc / jnp.where(l > 0.0, l, 1.0)


def _mla_prep_kernel(p_ref, gq_ref, gkv_ref, wq_ref, wkv_ref, cos_ref, sin_ref,
                     qn_ref, qr_ref, kn_ref, v_ref, kr_ref, *, scale):
    p = p_ref[0].astype(F32)
    nq = _rms(p[:, :MLA_Q_RANK], gq_ref[...]).astype(BF16)
    nkv = _rms(p[:, MLA_Q_RANK:MLA_Q_RANK + MLA_KV_RANK], gkv_ref[...]).astype(BF16)
    q = _dot(nq, wq_ref[...])
    kv = _dot(nkv, wkv_ref[...])
    cos, sin = cos_ref[...], sin_ref[...]
    hw = MLA_HEADS * MLA_NOPE_DIM
    qn_ref[0] = (q[:, :hw] * scale).astype(BF16)
    for h in range(MLA_HEADS):
        a = q[:, hw + h * 128: hw + (h + 1) * 128]
        b = q[:, 2 * hw + h * 128: 2 * hw + (h + 1) * 128]
        qr_ref[0, :, h * 128:(h + 1) * 128] = ((a * cos + b * sin) * scale).astype(BF16)
    kn_ref[0] = kv[:, :hw].astype(BF16)
    v_ref[0] = kv[:, hw:].astype(BF16)
    kr_ref[0] = (p[:, 512:640] * cos + p[:, 640:768] * sin).astype(BF16)


def _mla_attn_kernel(qn_ref, qr_ref, kn_ref, kr_ref, v_ref, o_ref, *, tq, tk):
    i = pl.program_id(2)
    qn, qr = qn_ref[0], qr_ref[0]

    def tile(t, carry, mask):
        k0 = pl.multiple_of(t * tk, tk)
        s = _dot_nt(qn, kn_ref[0, pl.ds(k0, tk), :]) + _dot_nt(qr, kr_ref[0, pl.ds(k0, tk), :])
        return _online_softmax_step(carry, s, v_ref[0, pl.ds(k0, tk), :], mask)

    carry = lax.fori_loop(0, i, lambda t, c: tile(t, c, None), _softmax_init(tq, MLA_V_DIM))
    causal = lax.broadcasted_iota(jnp.int32, (tq, tk), 1) <= lax.broadcasted_iota(jnp.int32, (tq, tk), 0)
    o_ref[0] = _softmax_finish(tile(i, carry, causal)).astype(o_ref.dtype)


def mla_mixer(p, g_q, g_kv, wq, wkv, cos, sin, *, tm=512, tq=256, tk=256):
    b, s, _ = p.shape
    tm, tq, tk = min(tm, s), min(tq, s), min(tk, s)
    assert tq == tk
    hw = MLA_HEADS * 128
    scale = (MLA_NOPE_DIM + MLA_ROPE_DIM) ** -0.5
    full = lambda shape: pl.BlockSpec(shape, lambda bi, i: (0,) * len(shape))
    row = lambda w: pl.BlockSpec((1, tm, w), lambda bi, i: (bi, i, 0))
    outs = pl.pallas_call(
        functools.partial(_mla_prep_kernel, scale=scale),
        grid=(b, s // tm),
        in_specs=[row(768), full((1, MLA_Q_RANK)), full((1, MLA_KV_RANK)), full(wq.shape), full(wkv.shape),
                  pl.BlockSpec((tm, 128), lambda bi, i: (i, 0)), pl.BlockSpec((tm, 128), lambda bi, i: (i, 0))],
        out_specs=[row(hw), row(hw), row(hw), row(hw), row(128)],
        out_shape=[jax.ShapeDtypeStruct((b, s, hw), BF16)] * 4 + [jax.ShapeDtypeStruct((b, s, 128), BF16)],
        compiler_params=_params("parallel", "parallel"),
        name="mla_prep",
    )(p, g_q.reshape(1, -1), g_kv.reshape(1, -1), wq, wkv, cos, sin)
    qn, qr, kn, v, kr = outs
    q_spec = pl.BlockSpec((1, tq, 128), lambda bi, h, i: (bi, i, h))
    kv_spec = pl.BlockSpec((1, s, 128), lambda bi, h, i: (bi, 0, h))
    return pl.pallas_call(
        functools.partial(_mla_attn_kernel, tq=tq, tk=tk),
        grid=(b, MLA_HEADS, s // tq),
        in_specs=[q_spec, q_spec, kv_spec, pl.BlockSpec((1, s, 128), lambda bi, h, i: (bi, 0, 0)), kv_spec],
        out_specs=q_spec,
        out_shape=jax.ShapeDtypeStruct((b, s, hw), BF16),
        compiler_params=_params("parallel", "parallel", "arbitrary"),
        name="mla_attn",
    )(qn, qr, kn, kr, v)


def _nsa_prep_kernel(p_ref, cos_ref, sin_ref, q_ref, kc_ref, vc_ref, ks_ref, kw_ref, *, scale):
    cos, sin = cos_ref[...], sin_ref[...]
    hw = NSA_HEADS * NSA_HEAD_DIM
    for h in range(NSA_HEADS):
        a = p_ref[0, :, h * 128:(h + 1) * 128].astype(F32)
        b = p_ref[0, :, hw + h * 128: hw + (h + 1) * 128].astype(F32)
        q_ref[0, :, h * 128:(h + 1) * 128] = ((a * cos + b * sin) * scale).astype(BF16)
    base = 2 * hw

    def rot(col):
        a = p_ref[0, :, base + col * 128: base + (col + 1) * 128].astype(F32)
        b = p_ref[0, :, base + (col + 1) * 128: base + (col + 2) * 128].astype(F32)
        return a * cos + b * sin

    kc_ref[0] = rot(0)
    vc_ref[0] = p_ref[0, :, base + 256: base + 384].astype(F32)
    ks_ref[0] = rot(3).astype(BF16)
    kw_ref[0] = rot(6).astype(BF16)


def _gelu_tanh(x):
    return 0.5 * x * (1.0 + jnp.tanh(np.sqrt(2.0 / np.pi).astype(np.float32) * (x + 0.044715 * (x * x * x))))


def _nsa_compress_kernel(k_ref, v_ref, pos_ref, w1_ref, w2_ref, kc_ref, vc_ref):
    nrow = k_ref.shape[1]
    row = lax.broadcasted_iota(jnp.int32, (nrow, NSA_HEAD_DIM), 0)
    for z, (x_ref, o_ref) in enumerate(((k_ref, kc_ref), (v_ref, vc_ref))):
        res = _dot(x_ref[0], w1_ref[z], HIGHEST)
        pb = _dot(pos_ref[z], w1_ref[z], HIGHEST)
        bias = pb[0:1, :NSA_HEAD_DIM] + pb[1:2, NSA_HEAD_DIM:]
        nxt = pltpu.roll(res[:, NSA_HEAD_DIM:], nrow - 1, axis=0)
        hid = _gelu_tanh(res[:, :NSA_HEAD_DIM] + nxt + bias)
        out = _dot(hid, w2_ref[z], HIGHEST)
        o_ref[0] = jnp.where(row < nrow - 1, out, 0.0)


def _nsa_cmp_attn_kernel(q_ref, kc_ref, vc_ref, cover_ref, o_ref, sel_ref, *, tq, n_blk):
    i = pl.program_id(1)
    ncmp = kc_ref.shape[1]
    pos = i * tq + lax.broadcasted_iota(jnp.int32, (tq, ncmp), 0)
    n = lax.broadcasted_iota(jnp.int32, (tq, ncmp), 1)
    mask = (n * NSA_CMP_STRIDE + (NSA_CMP_LEN - 1) <= pos) & (n < ncmp - 1)
    kc, vc = kc_ref[0], vc_ref[0]
    imp = jnp.zeros((tq, 128), F32)
    for h in range(NSA_HEADS):
        q = q_ref[0, :, h * 128:(h + 1) * 128].astype(F32)
        s = jnp.where(mask, _dot_nt(q, kc, HIGHEST), MASK_VALUE)
        e = jnp.where(mask, jnp.exp(s - jnp.max(s, axis=-1, keepdims=True)), 0.0)
        l = jnp.sum(e, axis=-1, keepdims=True)
        p = e / jnp.where(l > 0.0, l, 1.0)
        o_ref[0, :, h * 128:(h + 1) * 128] = _dot(p, vc, HIGHEST)
        imp = imp + _dot(p, cover_ref[...], HIGHEST)
    jb = lax.broadcasted_iota(jnp.int32, (tq, 128), 1)
    cur = (i * tq + lax.broadcasted_iota(jnp.int32, (tq, 128), 0)) // NSA_SEL_BLOCK
    forced = (jb == 0) | (jb == cur) | (jb == cur - 1)
    visible = jb <= cur
    score = jnp.where(visible, imp + jnp.where(forced, NSA_FORCED_SCORE, 0.0), -jnp.inf)
    beaten = jnp.zeros((tq, 128), F32)
    for j in range(n_blk):
        col = jnp.broadcast_to(score[:, j:j + 1], (tq, 128))
        beaten = beaten + jnp.where((col > score) | ((col == score) & (jb > j)), 1.0, 0.0)
    keep = (beaten < float(NSA_TOP_N)) & visible
    sel_ref[0] = jnp.where(keep, 0.0, MASK_VALUE).astype(BF16)


def _nsa_attn_kernel(q_ref, ks_ref, vs_ref, kw_ref, vw_ref, sel_ref, e_ref, oc_ref, g_ref, o_ref, *, tq, tk):
    i = pl.program_id(1)
    q0 = i * tq
    rows = NSA_HEADS * tq
    q4 = jnp.concatenate([q_ref[0, :, h * 128:(h + 1) * 128] for h in range(NSA_HEADS)], axis=0)
    sel4 = jnp.concatenate([sel_ref[0]] * NSA_HEADS, axis=0)
    qpos = q0 + (lax.broadcasted_iota(jnp.int32, (rows, tk), 0) & (tq - 1))
    kiota = lax.broadcasted_iota(jnp.int32, (rows, tk), 1)

    def slc_tile(t, carry, causal):
        k0 = pl.multiple_of(t * tk, tk)
        s = _dot_nt(q4, ks_ref[0, pl.ds(k0, tk), :]) + _dot(sel4, e_ref[:, pl.ds(k0, tk)])
        mask = (k0 + kiota <= qpos) if causal else None
        return _online_softmax_step(carry, s, vs_ref[0, pl.ds(k0, tk), :], mask)

    t_diag = q0 // tk
    carry = lax.fori_loop(0, t_diag, lambda t, c: slc_tile(t, c, False), _softmax_init(rows, NSA_HEAD_DIM))
    o_slc = _softmax_finish(slc_tile(t_diag, carry, True))

    def win_body(t, carry):
        k0 = pl.multiple_of(t * tk, tk)
        s = _dot_nt(q4, kw_ref[0, pl.ds(k0, tk), :])
        dist = qpos - (k0 + kiota)
        return _online_softmax_step(carry, s, vw_ref[0, pl.ds(k0, tk), :], (dist >= 0) & (dist < NSA_WINDOW))

    t_lo = jnp.maximum(q0 - NSA_WINDOW + 1, 0) // tk
    o_win = _softmax_finish(lax.fori_loop(t_lo, (q0 + tq + tk - 1) // tk, win_body, _softmax_init(rows, NSA_HEAD_DIM)))

    g = jax.nn.sigmoid(g_ref[0].astype(F32))
    for h in range(NSA_HEADS):
        sl = slice(h * tq, (h + 1) * tq)
        o = (g[:, 3 * h:3 * h + 1] * oc_ref[0, :, h * 128:(h + 1) * 128]
             + g[:, 3 * h + 1:3 * h + 2] * o_slc[sl] + g[:, 3 * h + 2:3 * h + 3] * o_win[sl])
        o_ref[0, :, h * 128:(h + 1) * 128] = o.astype(o_ref.dtype)


def nsa_mixer(p, pos_emb, w1, w2, cos, sin, *, tm=512, tq=128, tk=256):
    b, s, _ = p.shape
    tm, tk = min(tm, s), min(tk, s)
    hw = NSA_HEADS * NSA_HEAD_DIM
    scale = NSA_HEAD_DIM ** -0.5
    row = lambda w: pl.BlockSpec((1, tm, w), lambda bi, i: (bi, i, 0))
    tab = pl.BlockSpec((tm, 128), lambda bi, i: (i, 0))
    q, kc_in, vc_in, ks, kw = pl.pallas_call(
        functools.partial(_nsa_prep_kernel, scale=scale),
        grid=(b, s // tm),
        in_specs=[row(p.shape[2]), tab, tab],
        out_specs=[row(hw), row(128), row(128), row(128), row(128)],
        out_shape=[jax.ShapeDtypeStruct((b, s, hw), BF16), jax.ShapeDtypeStruct((b, s, 128), F32),
                   jax.ShapeDtypeStruct((b, s, 128), F32), jax.ShapeDtypeStruct((b, s, 128), BF16),
                   jax.ShapeDtypeStruct((b, s, 128), BF16)],
        compiler_params=_params("parallel", "parallel"),
        name="nsa_prep",
    )(p, cos, sin)

    nrow = s // NSA_CMP_STRIDE
    fw = NSA_CMP_STRIDE * NSA_HEAD_DIM
    half = w1.shape[1] // 2
    w1r = jnp.concatenate([w1[:, :half], w1[:, half:]], axis=2)
    pos2 = jnp.pad(pos_emb.reshape(2, 2, fw), ((0, 0), (0, 6), (0, 0)))
    per_b = lambda shape: pl.BlockSpec((1,) + shape, lambda bi: (bi, 0, 0))
    const3 = lambda shape: pl.BlockSpec(shape, lambda bi: (0, 0, 0))
    kc, vc = pl.pallas_call(
        _nsa_compress_kernel,
        grid=(b,),
        in_specs=[per_b((nrow, fw)), per_b((nrow, fw)), const3(pos2.shape), const3(w1r.shape), const3(w2.shape)],
        out_specs=[per_b((nrow, NSA_HEAD_DIM))] * 2,
        out_shape=[jax.ShapeDtypeStruct((b, nrow, NSA_HEAD_DIM), F32)] * 2,
        compiler_params=_params("parallel"),
        name="nsa_compress",
    )(kc_in.reshape(b, nrow, fw), vc_in.reshape(b, nrow, fw), pos2, w1r, w2)

    n_blk = s // NSA_SEL_BLOCK
    assert n_blk <= 128 and tq & (tq - 1) == 0 and tk % tq == 0
    cmp_start = np.arange(nrow) * NSA_CMP_STRIDE
    cmp_end = cmp_start + NSA_CMP_LEN - 1
    blk_start = np.arange(128) * NSA_SEL_BLOCK
    cover = ((cmp_start[:, None] <= blk_start[None, :] + NSA_SEL_BLOCK - 1)
             & (cmp_end[:, None] >= blk_start[None, :]) & (np.arange(128)[None, :] < n_blk)
             & (np.arange(nrow)[:, None] < nrow - 1)).astype(np.float32)
    qrow = lambda w: pl.BlockSpec((1, tq, w), lambda bi, i: (bi, i, 0))
    cmp_spec = pl.BlockSpec((1, nrow, NSA_HEAD_DIM), lambda bi, i: (bi, 0, 0))
    o_cmp, sel = pl.pallas_call(
        functools.partial(_nsa_cmp_attn_kernel, tq=tq, n_blk=n_blk),
        grid=(b, s // tq),
        in_specs=[qrow(hw), cmp_spec, cmp_spec, pl.BlockSpec((nrow, 128), lambda bi, i: (0, 0))],
        out_specs=[qrow(hw), qrow(128)],
        out_shape=[jax.ShapeDtypeStruct((b, s, hw), F32), jax.ShapeDtypeStruct((b, s, 128), BF16)],
        compiler_params=_params("parallel", "parallel"),
        name="nsa_cmp_attn",
    )(q, kc, vc, jnp.asarray(cover))

    expand = (np.arange(128)[:, None] == (np.arange(s)[None, :] // NSA_SEL_BLOCK)).astype(np.float32)
    kv_spec = pl.BlockSpec((1, s, 128), lambda bi, i: (bi, 0, 0))
    base = 2 * hw
    col = lambda c: pl.BlockSpec((1, s, 128), functools.partial(lambda bi, i, cc: (bi, 0, cc), cc=(base // 128) + c))
    return pl.pallas_call(
        functools.partial(_nsa_attn_kernel, tq=tq, tk=tk),
        grid=(b, s // tq),
        in_specs=[qrow(hw), kv_spec, col(5), kv_spec, col(8), qrow(128),
                  pl.BlockSpec((128, s), lambda bi, i: (0, 0)), qrow(hw),
                  pl.BlockSpec((1, tq, 128), lambda bi, i: (bi, i, (base // 128) + 9))],
        out_specs=qrow(hw),
        out_shape=jax.ShapeDtypeStruct((b, s, hw), BF16),
        compiler_params=_params("parallel", "arbitrary"),
        name="nsa_attn",
    )(q, ks, p, kw, p, sel, jnp.asarray(expand, dtype=BF16), o_cmp, p)


def _split3_bf16(x):
    hi = x.astype(BF16)
    r1 = x - hi.astype(F32)
    mid = r1.astype(BF16)
    return hi, mid, (r1 - mid.astype(F32)).astype(BF16)


def _rwkv_prep_kernel(z_ref, zp_ref, mu_ref, w0_ref, a0_ref, kk_ref, ka_ref, rk_ref, w2_ref, a2_ref, g2_ref, bd_ref,
                      csum_ref, kt_ref, rt_ref, kh_ref, bh_ref, kbar_ref, bbar_ref, v_ref, etot_ref, g_ref, bonus_ref):
    i = pl.program_id(1)
    z = z_ref[0]
    tm = z.shape[0]
    prev_row = jnp.where(i > 0, zp_ref[0, 7:8, :], 0.0)
    rowid = lax.broadcasted_iota(jnp.int32, z.shape, 0)
    z_prev = jnp.where(rowid == 0, prev_row, pltpu.roll(z, 1, axis=0))
    z = z + (z_prev - z) * mu_ref[...]
    c = BRANCH_WIDTH
    r, k, v = z[:, :c], z[:, c:2 * c], z[:, 2 * c:3 * c]
    wd, ad, gd = z[:, 3 * c:3 * c + 128], z[:, 3 * c + 128:3 * c + 256], z[:, 3 * c + 256:3 * c + 512]
    wpre = -(w0_ref[...] + _dot(jnp.tanh(wd), w2_ref[...], HIGHEST))
    w_log = -(jnp.maximum(wpre, 0.0) + jnp.log1p(jnp.exp(-jnp.abs(wpre)))) - 0.5
    lw = -jnp.exp(w_log)
    a = jax.nn.sigmoid(a0_ref[...] + _dot(ad, a2_ref[...], HIGHEST))
    g_ref[0] = _dot(jax.nn.sigmoid(gd), g2_ref[...], HIGHEST)
    kk = k * kk_ref[...]
    kk = kk * lax.rsqrt(jnp.maximum(_dot(kk * kk, bd_ref[...], HIGHEST), 1e-24))
    k = k * (1.0 + (a - 1.0) * ka_ref[...])
    bonus_ref[0] = _dot(r * k * rk_ref[...], bd_ref[...], HIGHEST) * v
    sums = sum(_dot(csum_ref[...], piece) for piece in _split3_bf16(lw))
    cum, total = sums[:tm], sums[tm:]
    dec_out, dec_end = jnp.exp(-cum), jnp.exp(total - cum)
    b = kk * a
    kt_ref[0] = (kk * jnp.exp(cum - lw)).astype(BF16)
    rt_ref[0] = (r * jnp.exp(cum)).astype(BF16)
    kh_ref[0] = (k * dec_out).astype(BF16)
    bh_ref[0] = (b * dec_out).astype(BF16)
    kbar_ref[0] = (k * dec_end).astype(BF16)
    bbar_ref[0] = (b * dec_end).astype(BF16)
    v_ref[0] = v.astype(BF16)
    etot_ref[0] = jnp.exp(total)


def _rwkv_scan_kernel(kt_ref, rt_ref, kh_ref, bh_ref, kbar_ref, bbar_ref, v_ref, etot_ref, g_ref, bonus_ref,
                      gw_ref, gb_ref, y_ref, state_ref):
    L = kt_ref.shape[1]
    N = RWKV_HEAD_DIM
    n_pairs = kt_ref.shape[2] // 128

    @pl.when(pl.program_id(1) == 0)
    def _():
        state_ref[...] = jnp.zeros_like(state_ref)

    ri = lax.broadcasted_iota(jnp.int32, (L, 2 * L), 0)
    ci = lax.broadcasted_iota(jnp.int32, (L, 2 * L), 1) & (L - 1)
    strict2, incl2 = ci < ri, ci <= ri
    eye = jnp.where(lax.broadcasted_iota(jnp.int32, (L, L), 0) == lax.broadcasted_iota(jnp.int32, (L, L), 1), 1.0, 0.0)
    first2 = lax.broadcasted_iota(jnp.int32, (2 * L, 128), 1) < N
    first = lax.broadcasted_iota(jnp.int32, (L, 128), 1) < N
    bd = (lax.broadcasted_iota(jnp.int32, (128, 128), 0) < N) == (lax.broadcasted_iota(jnp.int32, (128, 128), 1) < N)
    zero = jnp.zeros((), BF16)

    xs_in, a_k, a_r = [], [], []
    for p in range(n_pairs):
        sl = slice(p * 128, (p + 1) * 128)
        x = jnp.concatenate([kt_ref[0, :, sl], rt_ref[0, :, sl]], axis=0)
        y = jnp.concatenate([kh_ref[0, :, sl], bh_ref[0, :, sl]], axis=0)
        xs_in.append(x)
        for sub in range(2):
            gm = _dot_nt(jnp.where(first2 if sub == 0 else ~first2, x, zero), y)
            a_k.append(jnp.where(strict2, gm[:L], 0.0))
            a_r.append(jnp.where(incl2, gm[L:], 0.0))
    a_k, a_r = jnp.stack(a_k), jnp.stack(a_r)
    a_kb = a_k[:, :, L:]
    bmm = lambda a, b: jnp.einsum("hij,hjk->hik", a.astype(BF16), b.astype(BF16), preferred_element_type=F32)
    tinv = eye[None] - a_kb
    pw = a_kb
    for _ in range(int(np.log2(L)) - 1):
        pw = bmm(pw, pw)
        tinv = tinv + bmm(tinv, pw)

    pick = lambda t: jnp.where(first, t[:L], t[L:])
    for p in range(n_pairs):
        sl = slice(p * 128, (p + 1) * 128)
        h0, h1 = 2 * p, 2 * p + 1
        st = state_ref[p]
        xs = _dot_nt(xs_in[p], st.astype(BF16))
        v = v_ref[0, :, sl]
        akkv = pick(_dot(jnp.concatenate([a_k[h0, :, :L], a_k[h1, :, :L]], axis=0).astype(BF16), v))
        rhs = -(xs[:L] + akkv)
        u = pick(_dot(jnp.concatenate([tinv[h0], tinv[h1]], axis=0).astype(BF16), rhs.astype(BF16)))
        vu = jnp.concatenate([v, u.astype(BF16)], axis=0)
        o = xs[L:] + pick(_dot(jnp.concatenate([a_r[h0], a_r[h1]], axis=0).astype(BF16), vu))
        kb = jnp.concatenate([kbar_ref[0, :, sl], bbar_ref[0, :, sl]], axis=0)
        state_ref[p] = st * etot_ref[0, 0:1, sl] + jnp.where(bd, _dot_tn(vu, kb), 0.0)
        hsum = lambda t: jnp.where(first, jnp.sum(jnp.where(first, t, 0.0), axis=-1, keepdims=True),
                                   jnp.sum(jnp.where(first, 0.0, t), axis=-1, keepdims=True))
        c = o - hsum(o) * (1.0 / N)
        on = c * lax.rsqrt(hsum(c * c) * (1.0 / N) + RWKV_GN_EPS)
        y_ref[0, :, sl] = ((on * gw_ref[:, sl] + gb_ref[:, sl] + bonus_ref[0, :, sl]) * g_ref[0, :, sl]).astype(y_ref.dtype)


def rwkv7_mixer(z, mu, w0, w2, a0, a2, g2, k_k, k_a, r_k, gn_w, gn_b, *, tm=256):
    b, s, zw = z.shape
    tm = min(tm, s)
    c, hd, nh = BRANCH_WIDTH, RWKV_HEAD_DIM, RWKV_HEADS
    L = min(RWKV_CHUNK, s)
    assert tm % L == 0 and L & (L - 1) == 0
    bd = jnp.asarray(np.kron(np.eye(nh, dtype=np.float32), np.ones((hd, hd), np.float32)))
    chunk = np.arange(tm) // L
    same = chunk[:, None] == chunk[None, :]
    csum = np.concatenate([same & (np.arange(tm)[None, :] <= np.arange(tm)[:, None]), same], axis=0)
    csum = jnp.asarray(csum.astype(np.float32), dtype=BF16)
    row = lambda w: pl.BlockSpec((1, tm, w), lambda bi, i: (bi, i, 0))
    vec = lambda w: pl.BlockSpec((1, w), lambda bi, i: (0, 0))
    mat = lambda shape: pl.BlockSpec(shape, lambda bi, i: (0, 0))
    outs = pl.pallas_call(
        _rwkv_prep_kernel,
        grid=(b, s // tm),
        in_specs=[row(zw), pl.BlockSpec((1, 8, zw), lambda bi, i: (bi, jnp.maximum(i * (tm // 8) - 1, 0), 0)),
                  vec(zw), vec(c), vec(c), vec(c), vec(c), vec(c), mat(w2.shape), mat(a2.shape), mat(g2.shape),
                  mat(bd.shape), mat(csum.shape)],
        out_specs=[row(c)] * 10,
        out_shape=[jax.ShapeDtypeStruct((b, s, c), BF16)] * 7 + [jax.ShapeDtypeStruct((b, s, c), F32)] * 3,
        compiler_params=_params("parallel", "parallel"),
        name="rwkv_prep",
    )(z, z, mu.reshape(1, zw), w0.reshape(1, c), a0.reshape(1, c), k_k.reshape(1, c), k_a.reshape(1, c),
      r_k.reshape(1, c), w2, a2, g2, bd, csum)
    blk = pl.BlockSpec((1, L, c), lambda bi, ci: (bi, ci, 0))
    gvec = pl.BlockSpec((1, c), lambda bi, ci: (0, 0))
    return pl.pallas_call(
        _rwkv_scan_kernel,
        grid=(b, s // L),
        in_specs=[blk] * 10 + [gvec, gvec],
        out_specs=blk,
        out_shape=jax.ShapeDtypeStruct((b, s, c), BF16),
        scratch_shapes=[pltpu.VMEM((c // 128, 128, 128), F32)],
        compiler_params=_params("parallel", "arbitrary"),
        name="rwkv_scan",
    )(*outs, gn_w.reshape(1, c), gn_b.reshape(1, c))


def _retention_kernel(p_ref, cos_ref, sin_ref, inner_ref, qd_ref, kd_ref, cd_ref, o_ref, state_ref):
    @pl.when(pl.program_id(1) == 0)
    def _():
        state_ref[...] = jnp.zeros_like(state_ref)

    C = p_ref.shape[1]
    kw = RET_HEADS * RET_K_DIM
    cos, sin = cos_ref[...], sin_ref[...]
    q = p_ref[0, :, 0:kw].astype(F32) * cos + p_ref[0, :, kw:2 * kw].astype(F32) * sin
    k = (p_ref[0, :, 2 * kw:3 * kw].astype(F32) * cos + p_ref[0, :, 3 * kw:4 * kw].astype(F32) * sin) * RET_K_DIM ** -0.5
    lane = lax.broadcasted_iota(jnp.int32, (C, 128), 1)
    for h in range(RET_HEADS):
        pair, sub = h // 2, h % 2
        in_head = (lane // RET_K_DIM) == sub
        qh = jnp.where(in_head, q[:, pair * 128:(pair + 1) * 128], 0.0)
        kh = jnp.where(in_head, k[:, pair * 128:(pair + 1) * 128], 0.0)
        vh = p_ref[0, :, 4 * kw + h * RET_V_DIM: 4 * kw + (h + 1) * RET_V_DIM]
        gate = p_ref[0, :, 4 * kw + RET_HEADS * RET_V_DIM + h * RET_V_DIM:
                     4 * kw + RET_HEADS * RET_V_DIM + (h + 1) * RET_V_DIM].astype(F32)
        scores = _dot_nt(qh.astype(BF16), kh.astype(BF16)) * inner_ref[h]
        st = state_ref[h]
        o = _dot(scores.astype(BF16), vh) + _dot((qh * qd_ref[h]).astype(BF16), st.astype(BF16))
        state_ref[h] = st * cd_ref[h, 0:1, :] + _dot_tn((kh * kd_ref[h]).astype(BF16), vh)
        c = o - jnp.mean(o, axis=-1, keepdims=True)
        y = c * lax.rsqrt(jnp.mean(c * c, axis=-1, keepdims=True) + NORM_EPS)
        o_ref[0, :, h * RET_V_DIM:(h + 1) * RET_V_DIM] = (y * (gate * jax.nn.sigmoid(gate))).astype(o_ref.dtype)


def retention_mixer(p, cos, sin):
    b, s, pw = p.shape
    H, C = RET_HEADS, min(RET_CHUNK, s)
    log_gamma = jnp.log1p(-jnp.exp2(-5.0 - jnp.arange(H, dtype=F32)))
    n = jnp.arange(C, dtype=F32)
    dist = n[:, None] - n[None, :]
    inner = jnp.where(dist >= 0, jnp.exp(jnp.maximum(dist, 0.0) * log_gamma[:, None, None]), 0.0)
    q_decay = jnp.exp((n + 1.0) * log_gamma[:, None])
    k_decay = jnp.exp((C - 1.0 - n) * log_gamma[:, None])
    chunk_decay = jnp.exp(C * log_gamma)
    lanes = lambda t: jnp.broadcast_to(t[:, :, None], (H, C, 128))
    const = lambda shape: pl.BlockSpec(shape, lambda bi, ci: (0,) * len(shape))
    kw = RET_HEADS * RET_K_DIM
    return pl.pallas_call(
        _retention_kernel,
        grid=(b, s // C),
        in_specs=[pl.BlockSpec((1, C, pw), lambda bi, ci: (bi, ci, 0)),
                  pl.BlockSpec((C, kw), lambda bi, ci: (ci, 0)), pl.BlockSpec((C, kw), lambda bi, ci: (ci, 0)),
                  const((H, C, C)), const((H, C, 128)), const((H, C, 128)), const((H, 8, 128))],
        out_specs=pl.BlockSpec((1, C, H * RET_V_DIM), lambda bi, ci: (bi, ci, 0)),
        out_shape=jax.ShapeDtypeStruct((b, s, H * RET_V_DIM), BF16),
        scratch_shapes=[pltpu.VMEM((H, 128, 128), F32)],
        compiler_params=_params("parallel", "arbitrary"),
        name="retention",
    )(p, cos, sin, inner, lanes(q_decay), lanes(k_decay), jnp.broadcast_to(chunk_decay[:, None, None], (H, 8, 128)))


def _rot_half_cols(w, half):
    return jnp.concatenate([-w[..., half:2 * half], w[..., :half]], axis=-1)


def _pad_cols(w, width):
    return jnp.pad(w, [(0, 0)] * (w.ndim - 1) + [(0, width - w.shape[-1])])


def _rope_tables(s, inv_freq, width, fill_cos):
    ang = jnp.arange(s, dtype=F32)[:, None] * inv_freq[None, :]
    cos, sin = jnp.cos(ang), jnp.sin(ang)
    rot = 2 * cos.shape[1]
    cos2 = jnp.concatenate([cos, cos, jnp.full((s, width - rot), fill_cos, F32)], axis=1)
    sin2 = jnp.concatenate([sin, sin, jnp.zeros((s, width - rot), F32)], axis=1)
    return cos2, sin2


def _split_offsets():
    sizes = (MLA_Q_RANK, MLA_KV_RANK, MLA_ROPE_DIM, 512, 128, 128, 128, 128, 128, 128, 12,
             3 * BRANCH_WIDTH + RWKV_DECAY_LORA + RWKV_A_LORA + RWKV_GATE_LORA, 256, 256, 512, 512, N_MIXERS * D_MODEL)
    offs = np.concatenate([[0], np.cumsum(sizes)])
    return [(int(offs[i]), int(offs[i + 1])) for i in range(len(sizes))]


def _pack_in_proj(w):
    sl = [w[:, a:b] for a, b in _split_offsets()]
    (ql, kvl, kr, nq, nkc, nvc, nks, nvs, nkw, nvw, ng, rz, rq, rk, rv, rg, mg) = sl
    w_mla = jnp.concatenate([ql, kvl, _pad_cols(kr, 128), _pad_cols(_rot_half_cols(kr, 32), 128)], axis=1)
    nh = NSA_ROT_DIM // 2
    rot128 = lambda t: _pad_cols(_rot_half_cols(t[:, :NSA_ROT_DIM], nh), 128)
    nq_rot = jnp.concatenate([rot128(nq[:, h * 128:(h + 1) * 128]) for h in range(NSA_HEADS)], axis=1)
    w_nsa = jnp.concatenate([nq, nq_rot, nkc, rot128(nkc), nvc, nks, rot128(nks), nvs, nkw, rot128(nkw), nvw,
                             _pad_cols(ng, 128)], axis=1)
    rh = RET_K_DIM // 2
    ret_rot = lambda t: jnp.concatenate([_rot_half_cols(t[:, h * 64:(h + 1) * 64], rh) for h in range(RET_HEADS)], axis=1)
    w_ret = jnp.concatenate([rq, ret_rot(rq), rk, ret_rot(rk), rv, rg], axis=1)
    c = BRANCH_WIDTH
    w_rwkv = jnp.concatenate([rz[:, :3 * c], _pad_cols(rz[:, 3 * c:3 * c + 96], 128),
                              _pad_cols(rz[:, 3 * c + 96:3 * c + 192], 128), rz[:, 3 * c + 192:]], axis=1)
    bf = lambda t: t.astype(BF16)
    return bf(w_mla), bf(w_nsa), bf(w_ret), bf(w_rwkv), bf(mg)


def _pack_mla_weights(w_uq, w_ukv):
    H, dn, dr, dv = MLA_HEADS, MLA_NOPE_DIM, MLA_ROPE_DIM, MLA_V_DIM
    uq = w_uq.reshape(-1, H, dn + dr)
    nope = uq[:, :, :dn].reshape(-1, H * dn)
    rope = uq[:, :, dn:]
    rope_p = _pad_cols(rope, 128).reshape(-1, H * 128)
    rope_r = _pad_cols(_rot_half_cols(rope, dr // 2), 128).reshape(-1, H * 128)
    ukv = w_ukv.reshape(-1, H, dn + dv)
    wkv = jnp.concatenate([ukv[:, :, :dn].reshape(-1, H * dn), ukv[:, :, dn:].reshape(-1, H * dv)], axis=1)
    return jnp.concatenate([nope, rope_p, rope_r], axis=1).astype(BF16), wkv.astype(BF16)


def _pad_rows(w, rows):
    return jnp.pad(w, ((0, rows - w.shape[0]), (0, 0)))


def kernel(x, w_in, w_branch, w_out, w_up, w_down, norm_gains, mla_g_q, mla_g_kv, mla_w_uq, mla_w_ukv,
           nsa_cmp_pos, nsa_cmp_w1, nsa_cmp_w2, rwkv_mu, rwkv_w0, rwkv_w2, rwkv_a0, rwkv_a2, rwkv_g2,
           rwkv_k_k, rwkv_k_a, rwkv_r_k, rwkv_gn_w, rwkv_gn_b):
    B, S, D = x.shape
    depth = w_in.shape[0]
    T = B * S
    tm = min(512, T)
    mla_cos, mla_sin = _rope_tables(S, jnp.float32(ROPE_THETA) ** (-jnp.arange(0, MLA_ROPE_DIM, 2, dtype=F32) / MLA_ROPE_DIM),
                                    128, 0.0)
    nsa_cos, nsa_sin = _rope_tables(S, jnp.float32(ROPE_THETA) ** (-jnp.arange(0, NSA_ROT_DIM, 2, dtype=F32) / NSA_ROT_DIM),
                                    128, 1.0)
    ret_cos, ret_sin = _rope_tables(S, jnp.float32(RET_THETA) ** (-jnp.linspace(0.0, 1.0, RET_K_DIM // 2, dtype=F32)),
                                    RET_K_DIM, 0.0)
    ret_cos, ret_sin = jnp.tile(ret_cos, (1, RET_HEADS)), jnp.tile(ret_sin, (1, RET_HEADS))
    c = BRANCH_WIDTH
    x = x.reshape(T, D)
    for l in range(depth):
        w_mla, w_nsa, w_ret, w_rwkv, w_gate = _pack_in_proj(w_in[l])
        g_pre = norm_gains[l, 0]
        p_mla = norm_matmul(x, g_pre, w_mla, BF16, tm, w_mla.shape[1])
        p_nsa = norm_matmul(x, g_pre, w_nsa, BF16, tm, w_nsa.shape[1] // 3)
        p_ret = norm_matmul(x, g_pre, w_ret, BF16, tm, w_ret.shape[1] // 2)
        p_rwkv = norm_matmul(x, g_pre, w_rwkv, F32, tm, w_rwkv.shape[1] // 2)
        gates = norm_matmul(x, g_pre, w_gate, BF16, tm, 1024)

        wq, wkv = _pack_mla_weights(mla_w_uq[l], mla_w_ukv[l])
        y_mla = mla_mixer(p_mla.reshape(B, S, -1), mla_g_q[l], mla_g_kv[l], wq, wkv, mla_cos, mla_sin)
        y_nsa = nsa_mixer(p_nsa.reshape(B, S, -1), nsa_cmp_pos[l], nsa_cmp_w1[l], nsa_cmp_w2[l], nsa_cos, nsa_sin)
        mu = rwkv_mu[l]
        mu_p = jnp.concatenate([mu[:3 * c], _pad_cols(mu[3 * c:3 * c + 96], 128), _pad_cols(mu[3 * c + 96:3 * c + 192], 128),
                                mu[3 * c + 192:]])
        y_rwkv = rwkv7_mixer(p_rwkv.reshape(B, S, -1), mu_p, rwkv_w0[l], _pad_rows(rwkv_w2[l], 128), rwkv_a0[l],
                             _pad_rows(rwkv_a2[l], 128), rwkv_g2[l], rwkv_k_k[l], rwkv_k_a[l], rwkv_r_k[l].reshape(-1),
                             rwkv_gn_w[l], rwkv_gn_b[l])
        y_ret = retention_mixer(p_ret.reshape(B, S, -1), ret_cos, ret_sin)

        ys = [y.reshape(T, c) for y in (y_mla, y_nsa, y_rwkv, y_ret)]
        merged = merge_branches(ys, gates, w_branch[l].astype(BF16), tm, 512)
        x = matmul_norm_residual(merged, w_out[l].astype(BF16), norm_gains[l, 1], x, tm)
        x = ffn_block(x, norm_gains[l, 2], w_up[l].astype(BF16), w_down[l].astype(BF16), norm_gains[l, 3], tm, 1024)
    return x.reshape(B, S, D)
```

```python
import functools

import numpy as np
import jax
import jax.numpy as jnp
from jax import lax
from jax.experimental import pallas as pl
from jax.experimental.pallas import tpu as pltpu

F32 = jnp.float32
BF16 = jnp.bfloat16
HIGHEST = lax.Precision.HIGHEST

V7X_LANES = 128
V7X_VMEM_LIMIT_BYTES = 56 * 1024 * 1024

D_MODEL = 2048
N_MIXERS = 4
BRANCH_WIDTH = D_MODEL // N_MIXERS
D_FF = 4 * D_MODEL
ROPE_THETA = 500000.0
NORM_EPS = 1e-6
MASK_VALUE = -1e30

MLA_NOPE_DIM = 128
MLA_ROPE_DIM = 64
MLA_V_DIM = 128
MLA_HEADS = 4
MLA_Q_RANK = 384
MLA_KV_RANK = 128

NSA_HEAD_DIM = 128
NSA_HEADS = 4
NSA_ROT_DIM = 32
NSA_CMP_LEN = 32
NSA_CMP_STRIDE = 16
NSA_SEL_BLOCK = 64
NSA_TOP_N = 16
NSA_WINDOW = 512
NSA_FORCED_SCORE = 1000.0

RWKV_HEAD_DIM = 64
RWKV_HEADS = 8
RWKV_DECAY_LORA = 96
RWKV_A_LORA = 96
RWKV_GATE_LORA = 256
RWKV_GN_EPS = 64e-5
RWKV_CHUNK = 64

RET_HEADS = 4
RET_V_DIM = 128
RET_K_DIM = 64
RET_CHUNK = 128
RET_THETA = 10000.0


def _params(*sem):
    return pltpu.CompilerParams(dimension_semantics=sem, vmem_limit_bytes=V7X_VMEM_LIMIT_BYTES)


def _dot(a, b, precision=None):
    return jnp.dot(a, b, preferred_element_type=F32, precision=precision)


def _dot_nt(a, b, precision=None):
    return lax.dot_general(a, b, (((1,), (1,)), ((), ())), preferred_element_type=F32, precision=precision)


def _dot_tn(a, b, precision=None):
    return lax.dot_general(a, b, (((0,), (0,)), ((), ())), preferred_element_type=F32, precision=precision)


def _rms(x, g, eps=NORM_EPS):
    return x * lax.rsqrt(jnp.mean(x * x, axis=-1, keepdims=True) + eps) * g


def _norm_matmul_kernel(x_ref, g_ref, w_ref, o_ref, h_ref):
    @pl.when(pl.program_id(1) == 0)
    def _():
        h_ref[...] = _rms(x_ref[...], g_ref[...]).astype(BF16)

    o_ref[...] = _dot(h_ref[...], w_ref[...]).astype(o_ref.dtype)


def norm_matmul(x, g, w, out_dtype, tm, tn):
    m, k = x.shape
    n = w.shape[1]
    assert m % tm == 0 and n % tn == 0
    return pl.pallas_call(
        _norm_matmul_kernel,
        grid=(m // tm, n // tn),
        in_specs=[pl.BlockSpec((tm, k), lambda i, j: (i, 0)),
                  pl.BlockSpec((1, k), lambda i, j: (0, 0)),
                  pl.BlockSpec((k, tn), lambda i, j: (0, j))],
        out_specs=pl.BlockSpec((tm, tn), lambda i, j: (i, j)),
        out_shape=jax.ShapeDtypeStruct((m, n), out_dtype),
        scratch_shapes=[pltpu.VMEM((tm, k), BF16)],
        compiler_params=_params("parallel", "arbitrary"),
        name="norm_matmul",
    )(x, g.reshape(1, k), w)


def _merge_kernel(y0_ref, y1_ref, y2_ref, y3_ref, g0_ref, g1_ref, g2_ref, g3_ref, wb_ref, o_ref):
    acc = None
    for m, (y_ref, g_ref) in enumerate(((y0_ref, g0_ref), (y1_ref, g1_ref), (y2_ref, g2_ref), (y3_ref, g3_ref))):
        term = jax.nn.sigmoid(g_ref[...].astype(F32)) * _dot(y_ref[...], wb_ref[m])
        acc = term if acc is None else acc + term
    o_ref[...] = acc.astype(o_ref.dtype)


def merge_branches(ys, gates, wb, tm, tn):
    m, c = ys[0].shape
    d = wb.shape[2]
    nj = d // tn
    y_spec = pl.BlockSpec((tm, c), lambda i, j: (i, 0))
    gate_specs = [pl.BlockSpec((tm, tn), functools.partial(lambda i, j, mm: (i, mm * nj + j), mm=mm))
                  for mm in range(N_MIXERS)]
    return pl.pallas_call(
        _merge_kernel,
        grid=(m // tm, nj),
        in_specs=[y_spec] * N_MIXERS + gate_specs + [pl.BlockSpec((N_MIXERS, c, tn), lambda i, j: (0, 0, j))],
        out_specs=pl.BlockSpec((tm, tn), lambda i, j: (i, j)),
        out_shape=jax.ShapeDtypeStruct((m, d), BF16),
        compiler_params=_params("parallel", "arbitrary"),
        name="merge_branches",
    )(*ys, gates, gates, gates, gates, wb)


def _matmul_norm_res_kernel(a_ref, w_ref, g_ref, x_ref, o_ref):
    y = _dot(a_ref[...], w_ref[...])
    o_ref[...] = x_ref[...] + _rms(y, g_ref[...])


def matmul_norm_residual(a, w, g, x, tm):
    m, k = a.shape
    d = w.shape[1]
    return pl.pallas_call(
        _matmul_norm_res_kernel,
        grid=(m // tm,),
        in_specs=[pl.BlockSpec((tm, k), lambda i: (i, 0)),
                  pl.BlockSpec((k, d), lambda i: (0, 0)),
                  pl.BlockSpec((1, d), lambda i: (0, 0)),
                  pl.BlockSpec((tm, d), lambda i: (i, 0))],
        out_specs=pl.BlockSpec((tm, d), lambda i: (i, 0)),
        out_shape=jax.ShapeDtypeStruct((m, d), F32),
        compiler_params=_params("parallel"),
        name="matmul_norm_residual",
    )(a, w, g.reshape(1, d), x)


def _ffn_kernel(x_ref, g1_ref, wu_ref, wd_ref, g2_ref, o_ref, h_ref, acc_ref):
    f = pl.program_id(1)

    @pl.when(f == 0)
    def _():
        h_ref[...] = _rms(x_ref[...], g1_ref[...]).astype(BF16)
        acc_ref[...] = jnp.zeros_like(acc_ref)

    u = jnp.maximum(_dot(h_ref[...], wu_ref[...]), 0.0)
    acc_ref[...] += _dot((u * u).astype(BF16), wd_ref[...])

    @pl.when(f == pl.num_programs(1) - 1)
    def _():
        o_ref[...] = x_ref[...] + _rms(acc_ref[...], g2_ref[...])


def ffn_block(x, g1, wu, wd, g2, tm, tf):
    m, d = x.shape
    ff = wu.shape[1]
    return pl.pallas_call(
        _ffn_kernel,
        grid=(m // tm, ff // tf),
        in_specs=[pl.BlockSpec((tm, d), lambda i, f: (i, 0)),
                  pl.BlockSpec((1, d), lambda i, f: (0, 0)),
                  pl.BlockSpec((d, tf), lambda i, f: (0, f)),
                  pl.BlockSpec((tf, d), lambda i, f: (f, 0)),
                  pl.BlockSpec((1, d), lambda i, f: (0, 0))],
        out_specs=pl.BlockSpec((tm, d), lambda i, f: (i, 0)),
        out_shape=jax.ShapeDtypeStruct((m, d), F32),
        scratch_shapes=[pltpu.VMEM((tm, d), BF16), pltpu.VMEM((tm, d), F32)],
        compiler_params=_params("parallel", "arbitrary"),
        name="ffn_block",
    )(x, g1.reshape(1, d), wu, wd, g2.reshape(1, d))


def _online_softmax_steps(carries, scores, vt_tiles, mask=None):
    if mask is not None:
        scores = [jnp.where(mask, s, -jnp.inf) for s in scores]
    stats = []
    for (m_prev, l_prev, _), s in zip(carries, scores):
        m_new = jnp.maximum(m_prev, jnp.max(s, axis=0, keepdims=True))
        alpha = jnp.exp(m_prev - m_new)
        p = jnp.exp(s - m_new)
        stats.append((m_new, alpha, alpha * l_prev + jnp.sum(p, axis=0, keepdims=True), p.astype(BF16)))
    return tuple((m_new, l_new, alpha * acc_prev + _dot(vt, p))
                 for (m_new, alpha, l_new, p), (_, _, acc_prev), vt in zip(stats, carries, vt_tiles))


def _softmax_init(cols, dv):
    return (jnp.full((1, cols), MASK_VALUE, F32), jnp.zeros((1, cols), F32), jnp.zeros((dv, cols), F32))


def _softmax_finish(carry):
    _, l, acc = carry
    return acc / jnp.where(l > 0.0, l, 1.0)


def _mla_prep_kernel(p_ref, gq_ref, gkv_ref, wq_ref, wkv_ref, cos_ref, sin_ref,
                     qn_ref, qr_ref, kn_ref, vt_ref, kr_ref, *, scale):
    p = p_ref[0].astype(F32)
    nq = _rms(p[:, :MLA_Q_RANK], gq_ref[...]).astype(BF16)
    nkv = _rms(p[:, MLA_Q_RANK:MLA_Q_RANK + MLA_KV_RANK], gkv_ref[...]).astype(BF16)
    q = _dot(nq, wq_ref[...])
    kv = _dot(nkv, wkv_ref[...])
    cos, sin = cos_ref[...], sin_ref[...]
    hw = MLA_HEADS * MLA_NOPE_DIM
    qn_ref[0] = (q[:, :hw] * scale).astype(BF16)
    for h in range(MLA_HEADS):
        a = q[:, hw + h * 128: hw + (h + 1) * 128]
        b = q[:, 2 * hw + h * 128: 2 * hw + (h + 1) * 128]
        qr_ref[0, :, h * 128:(h + 1) * 128] = ((a * cos + b * sin) * scale).astype(BF16)
    kn_ref[0] = kv[:, :hw].astype(BF16)
    vt_ref[0] = kv[:, hw:].T.astype(BF16)
    kr_ref[0] = (p[:, 512:640] * cos + p[:, 640:768] * sin).astype(BF16)


def _mla_attn_kernel(qn_ref, qr_ref, kn_ref, kr_ref, vt_ref, o_ref, *, tq, tk):
    i = pl.program_id(1)
    hs = [slice(h * 128, (h + 1) * 128) for h in range(MLA_HEADS)]
    qn = [qn_ref[0, :, sl] for sl in hs]
    qr = [qr_ref[0, :, sl] for sl in hs]

    def tile(t, carries, mask):
        k0 = pl.multiple_of(t * tk, tk)
        kr = kr_ref[0, pl.ds(k0, tk), :]
        scores = [_dot_nt(kn_ref[0, pl.ds(k0, tk), sl], qn[h]) + _dot_nt(kr, qr[h]) for h, sl in enumerate(hs)]
        return _online_softmax_steps(carries, scores, [vt_ref[0, sl, pl.ds(k0, tk)] for sl in hs], mask)

    init = tuple(_softmax_init(tq, MLA_V_DIM) for _ in hs)
    carries = lax.fori_loop(0, i, lambda t, c: tile(t, c, None), init)
    causal = lax.broadcasted_iota(jnp.int32, (tk, tq), 0) <= lax.broadcasted_iota(jnp.int32, (tk, tq), 1)
    for sl, carry in zip(hs, tile(i, carries, causal)):
        o_ref[0, :, sl] = _softmax_finish(carry).T.astype(o_ref.dtype)


def mla_mixer(p, g_q, g_kv, wq, wkv, cos, sin, *, tm=512, tq=256, tk=256):
    b, s, _ = p.shape
    tm, tq, tk = min(tm, s), min(tq, s), min(tk, s)
    assert tq == tk
    hw = MLA_HEADS * 128
    scale = (MLA_NOPE_DIM + MLA_ROPE_DIM) ** -0.5
    full = lambda shape: pl.BlockSpec(shape, lambda bi, i: (0,) * len(shape))
    row = lambda w: pl.BlockSpec((1, tm, w), lambda bi, i: (bi, i, 0))
    outs = pl.pallas_call(
        functools.partial(_mla_prep_kernel, scale=scale),
        grid=(b, s // tm),
        in_specs=[row(768), full((1, MLA_Q_RANK)), full((1, MLA_KV_RANK)), full(wq.shape), full(wkv.shape),
                  pl.BlockSpec((tm, 128), lambda bi, i: (i, 0)), pl.BlockSpec((tm, 128), lambda bi, i: (i, 0))],
        out_specs=[row(hw), row(hw), row(hw), pl.BlockSpec((1, hw, tm), lambda bi, i: (bi, 0, i)), row(128)],
        out_shape=[jax.ShapeDtypeStruct((b, s, hw), BF16)] * 3 + [jax.ShapeDtypeStruct((b, hw, s), BF16),
                                                                   jax.ShapeDtypeStruct((b, s, 128), BF16)],
        compiler_params=_params("parallel", "parallel"),
        name="mla_prep",
    )(p, g_q.reshape(1, -1), g_kv.reshape(1, -1), wq, wkv, cos, sin)
    qn, qr, kn, vt, kr = outs
    q_spec = pl.BlockSpec((1, tq, hw), lambda bi, i: (bi, i, 0))
    per_batch = lambda shape: pl.BlockSpec((1,) + shape, lambda bi, i: (bi, 0, 0))
    return pl.pallas_call(
        functools.partial(_mla_attn_kernel, tq=tq, tk=tk),
        grid=(b, s // tq),
        in_specs=[q_spec, q_spec, per_batch((s, hw)), per_batch((s, 128)), per_batch((hw, s))],
        out_specs=q_spec,
        out_shape=jax.ShapeDtypeStruct((b, s, hw), BF16),
        compiler_params=_params("parallel", "arbitrary"),
        name="mla_attn",
    )(qn, qr, kn, kr, vt)


def _nsa_prep_kernel(p_ref, cos_ref, sin_ref, q_ref, kc_ref, vc_ref, ks_ref, kw_ref, vst_ref, vwt_ref, *, scale):
    cos, sin = cos_ref[...], sin_ref[...]
    hw = NSA_HEADS * NSA_HEAD_DIM
    for h in range(NSA_HEADS):
        a = p_ref[0, :, h * 128:(h + 1) * 128].astype(F32)
        b = p_ref[0, :, hw + h * 128: hw + (h + 1) * 128].astype(F32)
        q_ref[0, :, h * 128:(h + 1) * 128] = ((a * cos + b * sin) * scale).astype(BF16)
    base = 2 * hw

    def rot(col):
        a = p_ref[0, :, base + col * 128: base + (col + 1) * 128].astype(F32)
        b = p_ref[0, :, base + (col + 1) * 128: base + (col + 2) * 128].astype(F32)
        return a * cos + b * sin

    kc_ref[0] = rot(0)
    vc_ref[0] = p_ref[0, :, base + 256: base + 384].astype(F32)
    ks_ref[0] = rot(3).astype(BF16)
    kw_ref[0] = rot(6).astype(BF16)
    transposed = lambda col: p_ref[0, :, base + col * 128: base + (col + 1) * 128].astype(F32).T.astype(BF16)
    vst_ref[0] = transposed(5)
    vwt_ref[0] = transposed(8)


def _gelu_tanh(x):
    return 0.5 * x * (1.0 + jnp.tanh(np.sqrt(2.0 / np.pi).astype(np.float32) * (x + 0.044715 * (x * x * x))))


def _split3_bf16(x):
    hi = x.astype(BF16)
    r1 = x - hi.astype(F32)
    mid = r1.astype(BF16)
    return hi, mid, (r1 - mid.astype(F32)).astype(BF16)


def _nsa_compress_kernel(k_ref, v_ref, pos_ref, w1_ref, w2_ref, kc3_ref, vct_ref):
    nrow = k_ref.shape[1]
    row = lax.broadcasted_iota(jnp.int32, (nrow, NSA_HEAD_DIM), 0)
    outs = []
    for z, x_ref in enumerate((k_ref, v_ref)):
        res = _dot(x_ref[0], w1_ref[z], HIGHEST)
        pb = _dot(pos_ref[z], w1_ref[z], HIGHEST)
        bias = pb[0:1, :NSA_HEAD_DIM] + pb[1:2, NSA_HEAD_DIM:]
        nxt = pltpu.roll(res[:, NSA_HEAD_DIM:], nrow - 1, axis=0)
        hid = _gelu_tanh(res[:, :NSA_HEAD_DIM] + nxt + bias)
        outs.append(jnp.where(row < nrow - 1, _dot(hid, w2_ref[z], HIGHEST), 0.0))
    for z, piece in enumerate(_split3_bf16(outs[0])):
        kc3_ref[0, z] = piece
    vct_ref[0] = outs[1].T.astype(BF16)


def _nsa_attn_kernel(q_ref, kc3_ref, vct_ref, covert_ref, ks_ref, vst_ref, kw_ref, vwt_ref, et_ref, g_ref, o_ref,
                     *, tq, tk, n_blk):
    i = pl.program_id(1)
    q0 = i * tq
    cols = NSA_HEADS * tq
    q4 = jnp.concatenate([q_ref[0, :, h * 128:(h + 1) * 128] for h in range(NSA_HEADS)], axis=0)

    ncmp = kc3_ref.shape[2]
    s = sum(_dot_nt(kc3_ref[0, z], q4) for z in range(3))
    n = lax.broadcasted_iota(jnp.int32, (ncmp, cols), 0)
    qpos_c = q0 + (lax.broadcasted_iota(jnp.int32, (ncmp, cols), 1) & (tq - 1))
    s = jnp.where((n * NSA_CMP_STRIDE + (NSA_CMP_LEN - 1) <= qpos_c) & (n < ncmp - 1), s, -jnp.inf)
    e = jnp.exp(s - jnp.maximum(jnp.max(s, axis=0, keepdims=True), MASK_VALUE))
    l = jnp.sum(e, axis=0, keepdims=True)
    p = e / jnp.where(l > 0.0, l, 1.0)
    o_cmp = _dot(vct_ref[0], p.astype(BF16))
    psum = sum(p[:, h * tq:(h + 1) * tq] for h in range(NSA_HEADS))
    imp = sum(_dot(covert_ref[...], piece) for piece in _split3_bf16(psum))

    nb = covert_ref.shape[0]
    jb = lax.broadcasted_iota(jnp.int32, (nb, tq), 0)
    cur = (q0 + lax.broadcasted_iota(jnp.int32, (nb, tq), 1)) // NSA_SEL_BLOCK
    forced = (jb == 0) | (jb == cur) | (jb == cur - 1)
    visible = jb <= cur
    score = jnp.where(visible, imp + jnp.where(forced, NSA_FORCED_SCORE, 0.0), -jnp.inf)
    beaten = jnp.zeros((nb, tq), F32)
    for j in range(n_blk):
        row = score[j:j + 1, :]
        beaten = beaten + jnp.where((row > score) | ((row == score) & (jb > j)), 1.0, 0.0)
    keep = (beaten < float(NSA_TOP_N)) & visible
    sel = jnp.concatenate([jnp.where(keep, 0.0, MASK_VALUE), jnp.full((128 - nb, tq), MASK_VALUE, F32)], axis=0).astype(BF16)

    qh = [q_ref[0, :, h * 128:(h + 1) * 128] for h in range(NSA_HEADS)]
    kio = lax.broadcasted_iota(jnp.int32, (tk, tq), 0)
    qpos = q0 + lax.broadcasted_iota(jnp.int32, (tk, tq), 1)
    init = tuple(_softmax_init(tq, NSA_HEAD_DIM) for _ in qh)

    def slc_tile(t, carries, causal):
        k0 = pl.multiple_of(t * tk, tk)
        k = ks_ref[0, pl.ds(k0, tk), :]
        bias = _dot(et_ref[pl.ds(k0, tk), :], sel)
        vt = vst_ref[0, :, pl.ds(k0, tk)]
        return _online_softmax_steps(carries, [_dot_nt(k, q) + bias for q in qh], [vt] * NSA_HEADS,
                                     (k0 + kio <= qpos) if causal else None)

    t_diag = q0 // tk
    o_slc = slc_tile(t_diag, lax.fori_loop(0, t_diag, lambda t, c: slc_tile(t, c, False), init), True)

    def win_tile(t, carries):
        k0 = pl.multiple_of(t * tk, tk)
        k = kw_ref[0, pl.ds(k0, tk), :]
        dist = qpos - (k0 + kio)
        return _online_softmax_steps(carries, [_dot_nt(k, q) for q in qh], [vwt_ref[0, :, pl.ds(k0, tk)]] * NSA_HEADS,
                                     (dist >= 0) & (dist < NSA_WINDOW))

    t_lo = jnp.maximum(q0 - NSA_WINDOW + 1, 0) // tk
    o_win = lax.fori_loop(t_lo, t_diag + 1, win_tile, init)

    g = jax.nn.sigmoid(g_ref[0].astype(F32).T)
    for h in range(NSA_HEADS):
        o = (g[3 * h:3 * h + 1] * o_cmp[:, h * tq:(h + 1) * tq] + g[3 * h + 1:3 * h + 2] * _softmax_finish(o_slc[h])
             + g[3 * h + 2:3 * h + 3] * _softmax_finish(o_win[h]))
        o_ref[0, :, h * 128:(h + 1) * 128] = o.T.astype(o_ref.dtype)


def nsa_mixer(p, pos_emb, w1, w2, cos, sin, *, tm=512, tq=128, tk=256):
    b, s, _ = p.shape
    tm, tk = min(tm, s), min(tk, s)
    hw = NSA_HEADS * NSA_HEAD_DIM
    scale = NSA_HEAD_DIM ** -0.5
    row = lambda w: pl.BlockSpec((1, tm, w), lambda bi, i: (bi, i, 0))
    tab = pl.BlockSpec((tm, 128), lambda bi, i: (i, 0))
    tr_spec = pl.BlockSpec((1, 128, tm), lambda bi, i: (bi, 0, i))
    q, kc_in, vc_in, ks, kw, vst, vwt = pl.pallas_call(
        functools.partial(_nsa_prep_kernel, scale=scale),
        grid=(b, s // tm),
        in_specs=[row(p.shape[2]), tab, tab],
        out_specs=[row(hw), row(128), row(128), row(128), row(128), tr_spec, tr_spec],
        out_shape=[jax.ShapeDtypeStruct((b, s, hw), BF16), jax.ShapeDtypeStruct((b, s, 128), F32),
                   jax.ShapeDtypeStruct((b, s, 128), F32), jax.ShapeDtypeStruct((b, s, 128), BF16),
                   jax.ShapeDtypeStruct((b, s, 128), BF16), jax.ShapeDtypeStruct((b, 128, s), BF16),
                   jax.ShapeDtypeStruct((b, 128, s), BF16)],
        compiler_params=_params("parallel", "parallel"),
        name="nsa_prep",
    )(p, cos, sin)

    nrow = s // NSA_CMP_STRIDE
    fw = NSA_CMP_STRIDE * NSA_HEAD_DIM
    half = w1.shape[1] // 2
    w1r = jnp.concatenate([w1[:, :half], w1[:, half:]], axis=2)
    pos2 = jnp.pad(pos_emb.reshape(2, 2, fw), ((0, 0), (0, 6), (0, 0)))
    per_b = lambda shape: pl.BlockSpec((1,) + shape, lambda bi: (bi, 0, 0))
    const3 = lambda shape: pl.BlockSpec(shape, lambda bi: (0, 0, 0))
    kc3, vct = pl.pallas_call(
        _nsa_compress_kernel,
        grid=(b,),
        in_specs=[per_b((nrow, fw)), per_b((nrow, fw)), const3(pos2.shape), const3(w1r.shape), const3(w2.shape)],
        out_specs=[pl.BlockSpec((1, 3, nrow, NSA_HEAD_DIM), lambda bi: (bi, 0, 0, 0)), per_b((NSA_HEAD_DIM, nrow))],
        out_shape=[jax.ShapeDtypeStruct((b, 3, nrow, NSA_HEAD_DIM), BF16), jax.ShapeDtypeStruct((b, NSA_HEAD_DIM, nrow), BF16)],
        compiler_params=_params("parallel"),
        name="nsa_compress",
    )(kc_in.reshape(b, nrow, fw), vc_in.reshape(b, nrow, fw), pos2, w1r, w2)

    n_blk = s // NSA_SEL_BLOCK
    nb = 64
    assert n_blk <= nb and tq == 128 and tk % tq == 0
    cmp_start = np.arange(nrow) * NSA_CMP_STRIDE
    cmp_end = cmp_start + NSA_CMP_LEN - 1
    blk_start = np.arange(nb) * NSA_SEL_BLOCK
    cover_t = ((cmp_start[None, :] <= blk_start[:, None] + NSA_SEL_BLOCK - 1)
               & (cmp_end[None, :] >= blk_start[:, None]) & (np.arange(nb)[:, None] < n_blk)
               & (np.arange(nrow)[None, :] < nrow - 1)).astype(np.float32)
    expand_t = ((np.arange(s)[:, None] // NSA_SEL_BLOCK) == np.arange(128)[None, :]).astype(np.float32)
    qrow = lambda w: pl.BlockSpec((1, tq, w), lambda bi, i: (bi, i, 0))
    k_spec = pl.BlockSpec((1, s, 128), lambda bi, i: (bi, 0, 0))
    vt_spec = pl.BlockSpec((1, 128, s), lambda bi, i: (bi, 0, 0))
    gate_block = (2 * hw) // 128 + 9
    return pl.pallas_call(
        functools.partial(_nsa_attn_kernel, tq=tq, tk=tk, n_blk=n_blk),
        grid=(b, s // tq),
        in_specs=[qrow(hw), pl.BlockSpec((1, 3, nrow, NSA_HEAD_DIM), lambda bi, i: (bi, 0, 0, 0)),
                  pl.BlockSpec((1, NSA_HEAD_DIM, nrow), lambda bi, i: (bi, 0, 0)),
                  pl.BlockSpec((nb, nrow), lambda bi, i: (0, 0)),
                  k_spec, vt_spec, k_spec, vt_spec, pl.BlockSpec((s, 128), lambda bi, i: (0, 0)),
                  pl.BlockSpec((1, tq, 128), lambda bi, i: (bi, i, gate_block))],
        out_specs=qrow(hw),
        out_shape=jax.ShapeDtypeStruct((b, s, hw), BF16),
        compiler_params=_params("parallel", "arbitrary"),
        name="nsa_attn",
    )(q, kc3, vct, jnp.asarray(cover_t, dtype=BF16), ks, vst, kw, vwt, jnp.asarray(expand_t, dtype=BF16), p)


def _rwkv_prep_kernel(z_ref, zp_ref, mu_ref, w0_ref, a0_ref, kk_ref, ka_ref, rk_ref, w2_ref, a2_ref, g2_ref, bd_ref,
                      csum_ref, kt_ref, rt_ref, kh_ref, bh_ref, kbar_ref, bbar_ref, v_ref, etot_ref, g_ref, bonus_ref):
    i = pl.program_id(1)
    z = z_ref[0]
    tm = z.shape[0]
    prev_row = jnp.where(i > 0, zp_ref[0, 7:8, :], 0.0)
    rowid = lax.broadcasted_iota(jnp.int32, z.shape, 0)
    z_prev = jnp.where(rowid == 0, prev_row, pltpu.roll(z, 1, axis=0))
    z = z + (z_prev - z) * mu_ref[...]
    c = BRANCH_WIDTH
    r, k, v = z[:, :c], z[:, c:2 * c], z[:, 2 * c:3 * c]
    wd, ad, gd = z[:, 3 * c:3 * c + 128], z[:, 3 * c + 128:3 * c + 256], z[:, 3 * c + 256:3 * c + 512]
    wpre = -(w0_ref[...] + _dot(jnp.tanh(wd), w2_ref[...], HIGHEST))
    w_log = -(jnp.maximum(wpre, 0.0) + jnp.log1p(jnp.exp(-jnp.abs(wpre)))) - 0.5
    lw = -jnp.exp(w_log)
    a = jax.nn.sigmoid(a0_ref[...] + _dot(ad, a2_ref[...], HIGHEST))
    g_ref[0] = _dot(jax.nn.sigmoid(gd), g2_ref[...], HIGHEST)
    kk = k * kk_ref[...]
    kk = kk * lax.rsqrt(jnp.maximum(_dot(kk * kk, bd_ref[...], HIGHEST), 1e-24))
    k = k * (1.0 + (a - 1.0) * ka_ref[...])
    bonus_ref[0] = _dot(r * k * rk_ref[...], bd_ref[...], HIGHEST) * v
    sums = sum(_dot(csum_ref[...], piece) for piece in _split3_bf16(lw))
    cum, total = sums[:tm], sums[tm:]
    dec_out, dec_end = jnp.exp(-cum), jnp.exp(total - cum)
    b = kk * a
    kt_ref[0] = (kk * jnp.exp(cum - lw)).astype(BF16)
    rt_ref[0] = (r * jnp.exp(cum)).astype(BF16)
    kh_ref[0] = (k * dec_out).astype(BF16)
    bh_ref[0] = (b * dec_out).astype(BF16)
    kbar_ref[0] = (k * dec_end).astype(BF16)
    bbar_ref[0] = (b * dec_end).astype(BF16)
    v_ref[0] = v.astype(BF16)
    etot_ref[0] = jnp.exp(total)


def _rwkv_scan_kernel(kt_ref, rt_ref, kh_ref, bh_ref, kbar_ref, bbar_ref, v_ref, etot_ref, g_ref, bonus_ref,
                      gw_ref, gb_ref, y_ref, state_ref):
    L = kt_ref.shape[1]
    N = RWKV_HEAD_DIM
    n_pairs = kt_ref.shape[2] // 128

    @pl.when(pl.program_id(1) == 0)
    def _():
        state_ref[...] = jnp.zeros_like(state_ref)

    ri = lax.broadcasted_iota(jnp.int32, (L, 2 * L), 0)
    ci = lax.broadcasted_iota(jnp.int32, (L, 2 * L), 1) & (L - 1)
    strict2, incl2 = ci < ri, ci <= ri
    eye = jnp.where(lax.broadcasted_iota(jnp.int32, (L, L), 0) == lax.broadcasted_iota(jnp.int32, (L, L), 1), 1.0, 0.0)
    first2 = lax.broadcasted_iota(jnp.int32, (2 * L, 128), 1) < N
    first = lax.broadcasted_iota(jnp.int32, (L, 128), 1) < N
    bd = (lax.broadcasted_iota(jnp.int32, (128, 128), 0) < N) == (lax.broadcasted_iota(jnp.int32, (128, 128), 1) < N)
    zero = jnp.zeros((), BF16)

    xs_in, a_k, a_r = [], [], []
    for p in range(n_pairs):
        sl = slice(p * 128, (p + 1) * 128)
        x = jnp.concatenate([kt_ref[0, :, sl], rt_ref[0, :, sl]], axis=0)
        y = jnp.concatenate([kh_ref[0, :, sl], bh_ref[0, :, sl]], axis=0)
        xs_in.append(x)
        for sub in range(2):
            gm = _dot_nt(jnp.where(first2 if sub == 0 else ~first2, x, zero), y)
            a_k.append(jnp.where(strict2, gm[:L], 0.0))
            a_r.append(jnp.where(incl2, gm[L:], 0.0))
    a_k, a_r = jnp.stack(a_k), jnp.stack(a_r)
    a_kb = a_k[:, :, L:]
    bmm = lambda a, b: jnp.einsum("hij,hjk->hik", a.astype(BF16), b.astype(BF16), preferred_element_type=F32)
    tinv = eye[None] - a_kb
    pw = a_kb
    for _ in range(int(np.log2(L)) - 1):
        pw = bmm(pw, pw)
        tinv = tinv + bmm(tinv, pw)

    pick = lambda t: jnp.where(first, t[:L], t[L:])
    for p in range(n_pairs):
        sl = slice(p * 128, (p + 1) * 128)
        h0, h1 = 2 * p, 2 * p + 1
        st = state_ref[p]
        xs = _dot_nt(xs_in[p], st.astype(BF16))
        v = v_ref[0, :, sl]
        akkv = pick(_dot(jnp.concatenate([a_k[h0, :, :L], a_k[h1, :, :L]], axis=0).astype(BF16), v))
        rhs = -(xs[:L] + akkv)
        u = pick(_dot(jnp.concatenate([tinv[h0], tinv[h1]], axis=0).astype(BF16), rhs.astype(BF16)))
        vu = jnp.concatenate([v, u.astype(BF16)], axis=0)
        o = xs[L:] + pick(_dot(jnp.concatenate([a_r[h0], a_r[h1]], axis=0).astype(BF16), vu))
        kb = jnp.concatenate([kbar_ref[0, :, sl], bbar_ref[0, :, sl]], axis=0)
        state_ref[p] = st * etot_ref[0, 0:1, sl] + jnp.where(bd, _dot_tn(vu, kb), 0.0)
        hsum = lambda t: jnp.where(first, jnp.sum(jnp.where(first, t, 0.0), axis=-1, keepdims=True),
                                   jnp.sum(jnp.where(first, 0.0, t), axis=-1, keepdims=True))
        c = o - hsum(o) * (1.0 / N)
        on = c * lax.rsqrt(hsum(c * c) * (1.0 / N) + RWKV_GN_EPS)
        y_ref[0, :, sl] = ((on * gw_ref[:, sl] + gb_ref[:, sl] + bonus_ref[0, :, sl]) * g_ref[0, :, sl]).astype(y_ref.dtype)


def rwkv7_mixer(z, mu, w0, w2, a0, a2, g2, k_k, k_a, r_k, gn_w, gn_b, *, tm=256):
    b, s, zw = z.shape
    tm = min(tm, s)
    c, hd, nh = BRANCH_WIDTH, RWKV_HEAD_DIM, RWKV_HEADS
    L = min(RWKV_CHUNK, s)
    assert tm % L == 0 and L & (L - 1) == 0
    bd = jnp.asarray(np.kron(np.eye(nh, dtype=np.float32), np.ones((hd, hd), np.float32)))
    chunk = np.arange(tm) // L
    same = chunk[:, None] == chunk[None, :]
    csum = np.concatenate([same & (np.arange(tm)[None, :] <= np.arange(tm)[:, None]), same], axis=0)
    csum = jnp.asarray(csum.astype(np.float32), dtype=BF16)
    row = lambda w: pl.BlockSpec((1, tm, w), lambda bi, i: (bi, i, 0))
    vec = lambda w: pl.BlockSpec((1, w), lambda bi, i: (0, 0))
    mat = lambda shape: pl.BlockSpec(shape, lambda bi, i: (0, 0))
    outs = pl.pallas_call(
        _rwkv_prep_kernel,
        grid=(b, s // tm),
        in_specs=[row(zw), pl.BlockSpec((1, 8, zw), lambda bi, i: (bi, jnp.maximum(i * (tm // 8) - 1, 0), 0)),
                  vec(zw), vec(c), vec(c), vec(c), vec(c), vec(c), mat(w2.shape), mat(a2.shape), mat(g2.shape),
                  mat(bd.shape), mat(csum.shape)],
        out_specs=[row(c)] * 10,
        out_shape=[jax.ShapeDtypeStruct((b, s, c), BF16)] * 7 + [jax.ShapeDtypeStruct((b, s, c), F32)] * 3,
        compiler_params=_params("parallel", "parallel"),
        name="rwkv_prep",
    )(z, z, mu.reshape(1, zw), w0.reshape(1, c), a0.reshape(1, c), k_k.reshape(1, c), k_a.reshape(1, c),
      r_k.reshape(1, c), w2, a2, g2, bd, csum)
    blk = pl.BlockSpec((1, L, c), lambda bi, ci: (bi, ci, 0))
    gvec = pl.BlockSpec((1, c), lambda bi, ci: (0, 0))
    return pl.pallas_call(
        _rwkv_scan_kernel,
        grid=(b, s // L),
        in_specs=[blk] * 10 + [gvec, gvec],
        out_specs=blk,
        out_shape=jax.ShapeDtypeStruct((b, s, c), BF16),
        scratch_shapes=[pltpu.VMEM((c // 128, 128, 128), F32)],
        compiler_params=_params("parallel", "arbitrary"),
        name="rwkv_scan",
    )(*outs, gn_w.reshape(1, c), gn_b.reshape(1, c))


def _retention_kernel(p_ref, cos_ref, sin_ref, inner_ref, qd_ref, kd_ref, cd_ref, o_ref, state_ref):
    @pl.when(pl.program_id(1) == 0)
    def _():
        state_ref[...] = jnp.zeros_like(state_ref)

    C = p_ref.shape[1]
    kw = RET_HEADS * RET_K_DIM
    cos, sin = cos_ref[...], sin_ref[...]
    q = p_ref[0, :, 0:kw].astype(F32) * cos + p_ref[0, :, kw:2 * kw].astype(F32) * sin
    k = (p_ref[0, :, 2 * kw:3 * kw].astype(F32) * cos + p_ref[0, :, 3 * kw:4 * kw].astype(F32) * sin) * RET_K_DIM ** -0.5
    lane = lax.broadcasted_iota(jnp.int32, (C, 128), 1)
    for h in range(RET_HEADS):
        pair, sub = h // 2, h % 2
        in_head = (lane // RET_K_DIM) == sub
        qh = jnp.where(in_head, q[:, pair * 128:(pair + 1) * 128], 0.0)
        kh = jnp.where(in_head, k[:, pair * 128:(pair + 1) * 128], 0.0)
        vh = p_ref[0, :, 4 * kw + h * RET_V_DIM: 4 * kw + (h + 1) * RET_V_DIM]
        gate = p_ref[0, :, 4 * kw + RET_HEADS * RET_V_DIM + h * RET_V_DIM:
                     4 * kw + RET_HEADS * RET_V_DIM + (h + 1) * RET_V_DIM].astype(F32)
        scores = _dot_nt(qh.astype(BF16), kh.astype(BF16)) * inner_ref[h]
        st = state_ref[h]
        o = _dot(scores.astype(BF16), vh) + _dot((qh * qd_ref[h]).astype(BF16), st.astype(BF16))
        state_ref[h] = st * cd_ref[h, 0:1, :] + _dot_tn((kh * kd_ref[h]).astype(BF16), vh)
        c = o - jnp.mean(o, axis=-1, keepdims=True)
        y = c * lax.rsqrt(jnp.mean(c * c, axis=-1, keepdims=True) + NORM_EPS)
        o_ref[0, :, h * RET_V_DIM:(h + 1) * RET_V_DIM] = (y * (gate * jax.nn.sigmoid(gate))).astype(o_ref.dtype)


def retention_mixer(p, cos, sin):
    b, s, pw = p.shape
    H, C = RET_HEADS, min(RET_CHUNK, s)
    log_gamma = jnp.log1p(-jnp.exp2(-5.0 - jnp.arange(H, dtype=F32)))
    n = jnp.arange(C, dtype=F32)
    dist = n[:, None] - n[None, :]
    inner = jnp.where(dist >= 0, jnp.exp(jnp.maximum(dist, 0.0) * log_gamma[:, None, None]), 0.0)
    q_decay = jnp.exp((n + 1.0) * log_gamma[:, None])
    k_decay = jnp.exp((C - 1.0 - n) * log_gamma[:, None])
    chunk_decay = jnp.exp(C * log_gamma)
    lanes = lambda t: jnp.broadcast_to(t[:, :, None], (H, C, 128))
    const = lambda shape: pl.BlockSpec(shape, lambda bi, ci: (0,) * len(shape))
    kw = RET_HEADS * RET_K_DIM
    return pl.pallas_call(
        _retention_kernel,
        grid=(b, s // C),
        in_specs=[pl.BlockSpec((1, C, pw), lambda bi, ci: (bi, ci, 0)),
                  pl.BlockSpec((C, kw), lambda bi, ci: (ci, 0)), pl.BlockSpec((C, kw), lambda bi, ci: (ci, 0)),
                  const((H, C, C)), const((H, C, 128)), const((H, C, 128)), const((H, 8, 128))],
        out_specs=pl.BlockSpec((1, C, H * RET_V_DIM), lambda bi, ci: (bi, ci, 0)),
        out_shape=jax.ShapeDtypeStruct((b, s, H * RET_V_DIM), BF16),
        scratch_shapes=[pltpu.VMEM((H, 128, 128), F32)],
        compiler_params=_params("parallel", "arbitrary"),
        name="retention",
    )(p, cos, sin, inner, lanes(q_decay), lanes(k_decay), jnp.broadcast_to(chunk_decay[:, None, None], (H, 8, 128)))


def _rot_half_cols(w, half):
    return jnp.concatenate([-w[..., half:2 * half], w[..., :half]], axis=-1)


def _pad_cols(w, width):
    return jnp.pad(w, [(0, 0)] * (w.ndim - 1) + [(0, width - w.shape[-1])])


def _rope_tables(s, inv_freq, width, fill_cos):
    ang = jnp.arange(s, dtype=F32)[:, None] * inv_freq[None, :]
    cos, sin = jnp.cos(ang), jnp.sin(ang)
    rot = 2 * cos.shape[1]
    cos2 = jnp.concatenate([cos, cos, jnp.full((s, width - rot), fill_cos, F32)], axis=1)
    sin2 = jnp.concatenate([sin, sin, jnp.zeros((s, width - rot), F32)], axis=1)
    return cos2, sin2


def _split_offsets():
    sizes = (MLA_Q_RANK, MLA_KV_RANK, MLA_ROPE_DIM, 512, 128, 128, 128, 128, 128, 128, 12,
             3 * BRANCH_WIDTH + RWKV_DECAY_LORA + RWKV_A_LORA + RWKV_GATE_LORA, 256, 256, 512, 512, N_MIXERS * D_MODEL)
    offs = np.concatenate([[0], np.cumsum(sizes)])
    return [(int(offs[i]), int(offs[i + 1])) for i in range(len(sizes))]


def _pack_in_proj(w):
    sl = [w[:, a:b] for a, b in _split_offsets()]
    (ql, kvl, kr, nq, nkc, nvc, nks, nvs, nkw, nvw, ng, rz, rq, rk, rv, rg, mg) = sl
    w_mla = jnp.concatenate([ql, kvl, _pad_cols(kr, 128), _pad_cols(_rot_half_cols(kr, 32), 128)], axis=1)
    nh = NSA_ROT_DIM // 2
    rot128 = lambda t: _pad_cols(_rot_half_cols(t[:, :NSA_ROT_DIM], nh), 128)
    nq_rot = jnp.concatenate([rot128(nq[:, h * 128:(h + 1) * 128]) for h in range(NSA_HEADS)], axis=1)
    w_nsa = jnp.concatenate([nq, nq_rot, nkc, rot128(nkc), nvc, nks, rot128(nks), nvs, nkw, rot128(nkw), nvw,
                             _pad_cols(ng, 128)], axis=1)
    rh = RET_K_DIM // 2
    ret_rot = lambda t: jnp.concatenate([_rot_half_cols(t[:, h * 64:(h + 1) * 64], rh) for h in range(RET_HEADS)], axis=1)
    w_ret = jnp.concatenate([rq, ret_rot(rq), rk, ret_rot(rk), rv, rg], axis=1)
    c = BRANCH_WIDTH
    w_rwkv = jnp.concatenate([rz[:, :3 * c], _pad_cols(rz[:, 3 * c:3 * c + 96], 128),
                              _pad_cols(rz[:, 3 * c + 96:3 * c + 192], 128), rz[:, 3 * c + 192:]], axis=1)
    bf = lambda t: t.astype(BF16)
    return bf(w_mla), bf(w_nsa), bf(w_ret), bf(w_rwkv), bf(mg)


def _pack_mla_weights(w_uq, w_ukv):
    H, dn, dr, dv = MLA_HEADS, MLA_NOPE_DIM, MLA_ROPE_DIM, MLA_V_DIM
    uq = w_uq.reshape(-1, H, dn + dr)
    nope = uq[:, :, :dn].reshape(-1, H * dn)
    rope = uq[:, :, dn:]
    rope_p = _pad_cols(rope, 128).reshape(-1, H * 128)
    rope_r = _pad_cols(_rot_half_cols(rope, dr // 2), 128).reshape(-1, H * 128)
    ukv = w_ukv.reshape(-1, H, dn + dv)
    wkv = jnp.concatenate([ukv[:, :, :dn].reshape(-1, H * dn), ukv[:, :, dn:].reshape(-1, H * dv)], axis=1)
    return jnp.concatenate([nope, rope_p, rope_r], axis=1).astype(BF16), wkv.astype(BF16)


def _pad_rows(w, rows):
    return jnp.pad(w, ((0, rows - w.shape[0]), (0, 0)))


def kernel(x, w_in, w_branch, w_out, w_up, w_down, norm_gains, mla_g_q, mla_g_kv, mla_w_uq, mla_w_ukv,
           nsa_cmp_pos, nsa_cmp_w1, nsa_cmp_w2, rwkv_mu, rwkv_w0, rwkv_w2, rwkv_a0, rwkv_a2, rwkv_g2,
           rwkv_k_k, rwkv_k_a, rwkv_r_k, rwkv_gn_w, rwkv_gn_b):
    B, S, D = x.shape
    depth = w_in.shape[0]
    T = B * S
    tm = min(512, T)
    mla_cos, mla_sin = _rope_tables(S, jnp.float32(ROPE_THETA) ** (-jnp.arange(0, MLA_ROPE_DIM, 2, dtype=F32) / MLA_ROPE_DIM),
                                    128, 0.0)
    nsa_cos, nsa_sin = _rope_tables(S, jnp.float32(ROPE_THETA) ** (-jnp.arange(0, NSA_ROT_DIM, 2, dtype=F32) / NSA_ROT_DIM),
                                    128, 1.0)
    ret_cos, ret_sin = _rope_tables(S, jnp.float32(RET_THETA) ** (-jnp.linspace(0.0, 1.0, RET_K_DIM // 2, dtype=F32)),
                                    RET_K_DIM, 0.0)
    ret_cos, ret_sin = jnp.tile(ret_cos, (1, RET_HEADS)), jnp.tile(ret_sin, (1, RET_HEADS))
    c = BRANCH_WIDTH
    x = x.reshape(T, D)
    for l in range(depth):
        w_mla, w_nsa, w_ret, w_rwkv, w_gate = _pack_in_proj(w_in[l])
        g_pre = norm_gains[l, 0]
        p_mla = norm_matmul(x, g_pre, w_mla, BF16, tm, w_mla.shape[1])
        p_nsa = norm_matmul(x, g_pre, w_nsa, BF16, tm, w_nsa.shape[1] // 3)
        p_ret = norm_matmul(x, g_pre, w_ret, BF16, tm, w_ret.shape[1] // 2)
        p_rwkv = norm_matmul(x, g_pre, w_rwkv, F32, tm, w_rwkv.shape[1] // 2)
        gates = norm_matmul(x, g_pre, w_gate, BF16, tm, 1024)

        wq, wkv = _pack_mla_weights(mla_w_uq[l], mla_w_ukv[l])
        y_mla = mla_mixer(p_mla.reshape(B, S, -1), mla_g_q[l], mla_g_kv[l], wq, wkv, mla_cos, mla_sin)
        y_nsa = nsa_mixer(p_nsa.reshape(B, S, -1), nsa_cmp_pos[l], nsa_cmp_w1[l], nsa_cmp_w2[l], nsa_cos, nsa_sin)
        mu = rwkv_mu[l]
        mu_p = jnp.concatenate([mu[:3 * c], _pad_cols(mu[3 * c:3 * c + 96], 128), _pad_cols(mu[3 * c + 96:3 * c + 192], 128),
                                mu[3 * c + 192:]])
        y_rwkv = rwkv7_mixer(p_rwkv.reshape(B, S, -1), mu_p, rwkv_w0[l], _pad_rows(rwkv_w2[l], 128), rwkv_a0[l],
                             _pad_rows(rwkv_a2[l], 128), rwkv_g2[l], rwkv_k_k[l], rwkv_k_a[l], rwkv_r_k[l].reshape(-1),
                             rwkv_gn_w[l], rwkv_gn_b[l])
        y_ret = retention_mixer(p_ret.reshape(B, S, -1), ret_cos, ret_sin)

        ys = [y.reshape(T, c) for y in (y_mla, y_nsa, y_rwkv, y_ret)]
        merged = merge_branches(ys, gates, w_branch[l].astype(BF16), tm, 512)
        x = matmul_norm_residual(merged, w_out[l].astype(BF16), norm_gains[l, 1], x, tm)
        x = ffn_block(x, norm_gains[l, 2], w_up[l].astype(BF16), w_down[l].astype(BF16), norm_gains[l, 3], tm, 1024)
    return x.reshape(B, S, D)
```

```python
import functools

import numpy as np
import jax
import jax.numpy as jnp
from jax import lax
from jax.experimental import pallas as pl
from jax.experimental.pallas import tpu as pltpu

F32 = jnp.float32
BF16 = jnp.bfloat16
HIGHEST = lax.Precision.HIGHEST

V7X_LANES = 128
V7X_VMEM_LIMIT_BYTES = 56 * 1024 * 1024

D_MODEL = 2048
N_MIXERS = 4
BRANCH_WIDTH = D_MODEL // N_MIXERS
D_FF = 4 * D_MODEL
ROPE_THETA = 500000.0
NORM_EPS = 1e-6
MASK_VALUE = -1e30

MLA_NOPE_DIM = 128
MLA_ROPE_DIM = 64
MLA_V_DIM = 128
MLA_HEADS = 4
MLA_Q_RANK = 384
MLA_KV_RANK = 128

NSA_HEAD_DIM = 128
NSA_HEADS = 4
NSA_ROT_DIM = 32
NSA_CMP_LEN = 32
NSA_CMP_STRIDE = 16
NSA_SEL_BLOCK = 64
NSA_TOP_N = 16
NSA_WINDOW = 512
NSA_FORCED_SCORE = 1000.0

RWKV_HEAD_DIM = 64
RWKV_HEADS = 8
RWKV_DECAY_LORA = 96
RWKV_A_LORA = 96
RWKV_GATE_LORA = 256
RWKV_GN_EPS = 64e-5
RWKV_CHUNK = 64
RWKV_CHUNKS_PER_STEP = 4

RET_HEADS = 4
RET_V_DIM = 128
RET_K_DIM = 64
RET_CHUNK = 128
RET_THETA = 10000.0


def _params(*sem):
    return pltpu.CompilerParams(dimension_semantics=sem, vmem_limit_bytes=V7X_VMEM_LIMIT_BYTES)


def _dot(a, b, precision=None):
    return jnp.dot(a, b, preferred_element_type=F32, precision=precision)


def _dot_nt(a, b, precision=None):
    return lax.dot_general(a, b, (((1,), (1,)), ((), ())), preferred_element_type=F32, precision=precision)


def _dot_tn(a, b, precision=None):
    return lax.dot_general(a, b, (((0,), (0,)), ((), ())), preferred_element_type=F32, precision=precision)


def _rms(x, g, eps=NORM_EPS):
    return x * lax.rsqrt(jnp.mean(x * x, axis=-1, keepdims=True) + eps) * g


def _norm_matmul_kernel(x_ref, g_ref, w_ref, o_ref, h_ref):
    @pl.when(pl.program_id(1) == 0)
    def _():
        h_ref[...] = _rms(x_ref[...], g_ref[...]).astype(BF16)

    o_ref[...] = _dot(h_ref[...], w_ref[...]).astype(o_ref.dtype)


def norm_matmul(x, g, w, out_dtype, tm, tn):
    m, k = x.shape
    n = w.shape[1]
    assert m % tm == 0 and n % tn == 0
    return pl.pallas_call(
        _norm_matmul_kernel,
        grid=(m // tm, n // tn),
        in_specs=[pl.BlockSpec((tm, k), lambda i, j: (i, 0)),
                  pl.BlockSpec((1, k), lambda i, j: (0, 0)),
                  pl.BlockSpec((k, tn), lambda i, j: (0, j))],
        out_specs=pl.BlockSpec((tm, tn), lambda i, j: (i, j)),
        out_shape=jax.ShapeDtypeStruct((m, n), out_dtype),
        scratch_shapes=[pltpu.VMEM((tm, k), BF16)],
        compiler_params=_params("parallel", "arbitrary"),
        name="norm_matmul",
    )(x, g.reshape(1, k), w)


def _merge_kernel(y0_ref, y1_ref, y2_ref, y3_ref, g0_ref, g1_ref, g2_ref, g3_ref, wb_ref, o_ref):
    acc = None
    for m, (y_ref, g_ref) in enumerate(((y0_ref, g0_ref), (y1_ref, g1_ref), (y2_ref, g2_ref), (y3_ref, g3_ref))):
        gate = 0.5 * jnp.tanh(0.5 * g_ref[...].astype(F32)) + 0.5
        term = gate * _dot(y_ref[...], wb_ref[m])
        acc = term if acc is None else acc + term
    o_ref[...] = acc.astype(o_ref.dtype)


def merge_branches(ys, gates, col0, wb, tm, tn):
    m, c = ys[0].shape
    d = wb.shape[2]
    nj = d // tn
    assert col0 % tn == 0
    y_spec = pl.BlockSpec((tm, c), lambda i, j: (i, 0))
    gate_specs = [pl.BlockSpec((tm, tn), functools.partial(lambda i, j, mm: (i, col0 // tn + mm * nj + j), mm=mm))
                  for mm in range(N_MIXERS)]
    return pl.pallas_call(
        _merge_kernel,
        grid=(m // tm, nj),
        in_specs=[y_spec] * N_MIXERS + gate_specs + [pl.BlockSpec((N_MIXERS, c, tn), lambda i, j: (0, 0, j))],
        out_specs=pl.BlockSpec((tm, tn), lambda i, j: (i, j)),
        out_shape=jax.ShapeDtypeStruct((m, d), BF16),
        compiler_params=_params("parallel", "arbitrary"),
        name="merge_branches",
    )(*ys, gates, gates, gates, gates, wb)


def _matmul_norm_res_kernel(a_ref, w_ref, g_ref, x_ref, o_ref):
    y = _dot(a_ref[...], w_ref[...])
    o_ref[...] = x_ref[...] + _rms(y, g_ref[...])


def matmul_norm_residual(a, w, g, x, tm):
    m, k = a.shape
    d = w.shape[1]
    return pl.pallas_call(
        _matmul_norm_res_kernel,
        grid=(m // tm,),
        in_specs=[pl.BlockSpec((tm, k), lambda i: (i, 0)),
                  pl.BlockSpec((k, d), lambda i: (0, 0)),
                  pl.BlockSpec((1, d), lambda i: (0, 0)),
                  pl.BlockSpec((tm, d), lambda i: (i, 0))],
        out_specs=pl.BlockSpec((tm, d), lambda i: (i, 0)),
        out_shape=jax.ShapeDtypeStruct((m, d), F32),
        compiler_params=_params("parallel"),
        name="matmul_norm_residual",
    )(a, w, g.reshape(1, d), x)


def _ffn_kernel(x_ref, g1_ref, wu_ref, wd_ref, g2_ref, o_ref, h_ref, acc_ref):
    f = pl.program_id(1)

    @pl.when(f == 0)
    def _():
        h_ref[...] = _rms(x_ref[...], g1_ref[...]).astype(BF16)
        acc_ref[...] = jnp.zeros_like(acc_ref)

    u = jnp.maximum(_dot(h_ref[...], wu_ref[...]), 0.0)
    acc_ref[...] += _dot((u * u).astype(BF16), wd_ref[...])

    @pl.when(f == pl.num_programs(1) - 1)
    def _():
        o_ref[...] = x_ref[...] + _rms(acc_ref[...], g2_ref[...])


def ffn_block(x, g1, wu, wd, g2, tm, tf):
    m, d = x.shape
    ff = wu.shape[1]
    return pl.pallas_call(
        _ffn_kernel,
        grid=(m // tm, ff // tf),
        in_specs=[pl.BlockSpec((tm, d), lambda i, f: (i, 0)),
                  pl.BlockSpec((1, d), lambda i, f: (0, 0)),
                  pl.BlockSpec((d, tf), lambda i, f: (0, f)),
                  pl.BlockSpec((tf, d), lambda i, f: (f, 0)),
                  pl.BlockSpec((1, d), lambda i, f: (0, 0))],
        out_specs=pl.BlockSpec((tm, d), lambda i, f: (i, 0)),
        out_shape=jax.ShapeDtypeStruct((m, d), F32),
        scratch_shapes=[pltpu.VMEM((tm, d), BF16), pltpu.VMEM((tm, d), F32)],
        compiler_params=_params("parallel", "arbitrary"),
        name="ffn_block",
    )(x, g1.reshape(1, d), wu, wd, g2.reshape(1, d))


def _online_softmax_steps(carries, scores, vt_tiles, mask=None):
    if mask is not None:
        scores = [jnp.where(mask, s, -jnp.inf) for s in scores]
    stats = []
    for (m_prev, l_prev, _), s in zip(carries, scores):
        m_new = jnp.maximum(m_prev, jnp.max(s, axis=0, keepdims=True))
        alpha = jnp.exp(m_prev - m_new)
        p = jnp.exp(s - m_new)
        stats.append((m_new, alpha, alpha * l_prev + jnp.sum(p, axis=0, keepdims=True), p.astype(BF16)))
    return tuple((m_new, l_new, alpha * acc_prev + _dot(vt, p))
                 for (m_new, alpha, l_new, p), (_, _, acc_prev), vt in zip(stats, carries, vt_tiles))


def _softmax_init(cols, dv):
    return (jnp.full((1, cols), MASK_VALUE, F32), jnp.zeros((1, cols), F32), jnp.zeros((dv, cols), F32))


def _softmax_finish(carry):
    _, l, acc = carry
    return acc / jnp.where(l > 0.0, l, 1.0)


def _mla_prep_kernel(p_ref, gq_ref, gkv_ref, wq_ref, wkv_ref, cos_ref, sin_ref,
                     qn_ref, qr_ref, kn_ref, vt_ref, kr_ref, *, scale):
    p = p_ref[0].astype(F32)
    nq = _rms(p[:, :MLA_Q_RANK], gq_ref[...]).astype(BF16)
    nkv = _rms(p[:, MLA_Q_RANK:MLA_Q_RANK + MLA_KV_RANK], gkv_ref[...]).astype(BF16)
    q = _dot(nq, wq_ref[...])
    kv = _dot(nkv, wkv_ref[...])
    cos, sin = cos_ref[...], sin_ref[...]
    hw = MLA_HEADS * MLA_NOPE_DIM
    qn_ref[0] = (q[:, :hw] * scale).astype(BF16)
    for h in range(MLA_HEADS):
        a = q[:, hw + h * 128: hw + (h + 1) * 128]
        b = q[:, 2 * hw + h * 128: 2 * hw + (h + 1) * 128]
        qr_ref[0, :, h * 128:(h + 1) * 128] = ((a * cos + b * sin) * scale).astype(BF16)
    kn_ref[0] = kv[:, :hw].astype(BF16)
    vt_ref[0] = kv[:, hw:].T.astype(BF16)
    kr_ref[0] = (p[:, 512:640] * cos + p[:, 640:768] * sin).astype(BF16)


def _mla_attn_kernel(qn_ref, qr_ref, kn_ref, kr_ref, vt_ref, o_ref, *, tq, tk):
    i = pl.program_id(1)
    hs = [slice(h * 128, (h + 1) * 128) for h in range(MLA_HEADS)]
    qn = [qn_ref[0, :, sl] for sl in hs]
    qr = [qr_ref[0, :, sl] for sl in hs]

    def tile(t, carries, mask):
        k0 = pl.multiple_of(t * tk, tk)
        kr = kr_ref[0, pl.ds(k0, tk), :]
        scores = [_dot_nt(kn_ref[0, pl.ds(k0, tk), sl], qn[h]) + _dot_nt(kr, qr[h]) for h, sl in enumerate(hs)]
        return _online_softmax_steps(carries, scores, [vt_ref[0, sl, pl.ds(k0, tk)] for sl in hs], mask)

    init = tuple(_softmax_init(tq, MLA_V_DIM) for _ in hs)
    carries = lax.fori_loop(0, i, lambda t, c: tile(t, c, None), init)
    causal = lax.broadcasted_iota(jnp.int32, (tk, tq), 0) <= lax.broadcasted_iota(jnp.int32, (tk, tq), 1)
    for sl, carry in zip(hs, tile(i, carries, causal)):
        o_ref[0, :, sl] = _softmax_finish(carry).T.astype(o_ref.dtype)


def mla_mixer(p, col0, g_q, g_kv, wq, wkv, cos, sin, *, tm=512, tq=256, tk=256):
    b, s, _ = p.shape
    tm, tq, tk = min(tm, s), min(tq, s), min(tk, s)
    assert tq == tk and col0 % 768 == 0
    hw = MLA_HEADS * 128
    scale = (MLA_NOPE_DIM + MLA_ROPE_DIM) ** -0.5
    full = lambda shape: pl.BlockSpec(shape, lambda bi, i: (0,) * len(shape))
    row = lambda w: pl.BlockSpec((1, tm, w), lambda bi, i: (bi, i, 0))
    outs = pl.pallas_call(
        functools.partial(_mla_prep_kernel, scale=scale),
        grid=(b, s // tm),
        in_specs=[pl.BlockSpec((1, tm, 768), lambda bi, i: (bi, i, col0 // 768)),
                  full((1, MLA_Q_RANK)), full((1, MLA_KV_RANK)), full(wq.shape), full(wkv.shape),
                  pl.BlockSpec((tm, 128), lambda bi, i: (i, 0)), pl.BlockSpec((tm, 128), lambda bi, i: (i, 0))],
        out_specs=[row(hw), row(hw), row(hw), pl.BlockSpec((1, hw, tm), lambda bi, i: (bi, 0, i)), row(128)],
        out_shape=[jax.ShapeDtypeStruct((b, s, hw), BF16)] * 3 + [jax.ShapeDtypeStruct((b, hw, s), BF16),
                                                                   jax.ShapeDtypeStruct((b, s, 128), BF16)],
        compiler_params=_params("parallel", "parallel"),
        name="mla_prep",
    )(p, g_q.reshape(1, -1), g_kv.reshape(1, -1), wq, wkv, cos, sin)
    qn, qr, kn, vt, kr = outs
    q_spec = pl.BlockSpec((1, tq, hw), lambda bi, i: (bi, i, 0))
    per_batch = lambda shape: pl.BlockSpec((1,) + shape, lambda bi, i: (bi, 0, 0))
    return pl.pallas_call(
        functools.partial(_mla_attn_kernel, tq=tq, tk=tk),
        grid=(b, s // tq),
        in_specs=[q_spec, q_spec, per_batch((s, hw)), per_batch((s, 128)), per_batch((hw, s))],
        out_specs=q_spec,
        out_shape=jax.ShapeDtypeStruct((b, s, hw), BF16),
        compiler_params=_params("parallel", "arbitrary"),
        name="mla_attn",
    )(qn, qr, kn, kr, vt)


def _nsa_prep_kernel(p_ref, cos_ref, sin_ref, q_ref, kc_ref, vc_ref, ks_ref, kw_ref, vst_ref, vwt_ref, *, scale):
    cos, sin = cos_ref[...], sin_ref[...]
    hw = NSA_HEADS * NSA_HEAD_DIM
    for h in range(NSA_HEADS):
        a = p_ref[0, :, h * 128:(h + 1) * 128].astype(F32)
        b = p_ref[0, :, hw + h * 128: hw + (h + 1) * 128].astype(F32)
        q_ref[0, :, h * 128:(h + 1) * 128] = ((a * cos + b * sin) * scale).astype(BF16)
    base = 2 * hw

    def rot(col):
        a = p_ref[0, :, base + col * 128: base + (col + 1) * 128].astype(F32)
        b = p_ref[0, :, base + (col + 1) * 128: base + (col + 2) * 128].astype(F32)
        return a * cos + b * sin

    kc_ref[0] = rot(0)
    vc_ref[0] = p_ref[0, :, base + 256: base + 384].astype(F32)
    ks_ref[0] = rot(3).astype(BF16)
    kw_ref[0] = rot(6).astype(BF16)
    transposed = lambda col: p_ref[0, :, base + col * 128: base + (col + 1) * 128].astype(F32).T.astype(BF16)
    vst_ref[0] = transposed(5)
    vwt_ref[0] = transposed(8)


def _gelu_tanh(x):
    return 0.5 * x * (1.0 + jnp.tanh(np.sqrt(2.0 / np.pi).astype(np.float32) * (x + 0.044715 * (x * x * x))))


def _split3_bf16(x):
    hi = x.astype(BF16)
    r1 = x - hi.astype(F32)
    mid = r1.astype(BF16)
    return hi, mid, (r1 - mid.astype(F32)).astype(BF16)


def _nsa_compress_kernel(k_ref, v_ref, pos_ref, w1_ref, w2_ref, kc3_ref, vct_ref):
    nrow = k_ref.shape[1]
    row = lax.broadcasted_iota(jnp.int32, (nrow, NSA_HEAD_DIM), 0)
    outs = []
    for z, x_ref in enumerate((k_ref, v_ref)):
        res = _dot(x_ref[0], w1_ref[z], HIGHEST)
        pb = _dot(pos_ref[z], w1_ref[z], HIGHEST)
        bias = pb[0:1, :NSA_HEAD_DIM] + pb[1:2, NSA_HEAD_DIM:]
        nxt = pltpu.roll(res[:, NSA_HEAD_DIM:], nrow - 1, axis=0)
        hid = _gelu_tanh(res[:, :NSA_HEAD_DIM] + nxt + bias)
        outs.append(jnp.where(row < nrow - 1, _dot(hid, w2_ref[z], HIGHEST), 0.0))
    for z, piece in enumerate(_split3_bf16(outs[0])):
        kc3_ref[0, z] = piece
    vct_ref[0] = outs[1].T.astype(BF16)


def _nsa_attn_kernel(q_ref, kc3_ref, vct_ref, covert_ref, ks_ref, vst_ref, kw_ref, vwt_ref, et_ref, g_ref, o_ref,
                     *, tq, tk, n_blk):
    i = pl.program_id(1)
    q0 = i * tq
    cols = NSA_HEADS * tq
    q4 = jnp.concatenate([q_ref[0, :, h * 128:(h + 1) * 128] for h in range(NSA_HEADS)], axis=0)

    ncmp = kc3_ref.shape[2]
    s = sum(_dot_nt(kc3_ref[0, z], q4) for z in range(3))
    n = lax.broadcasted_iota(jnp.int32, (ncmp, cols), 0)
    qpos_c = q0 + (lax.broadcasted_iota(jnp.int32, (ncmp, cols), 1) & (tq - 1))
    s = jnp.where((n * NSA_CMP_STRIDE + (NSA_CMP_LEN - 1) <= qpos_c) & (n < ncmp - 1), s, -jnp.inf)
    e = jnp.exp(s - jnp.maximum(jnp.max(s, axis=0, keepdims=True), MASK_VALUE))
    l = jnp.sum(e, axis=0, keepdims=True)
    p = e / jnp.where(l > 0.0, l, 1.0)
    o_cmp = _dot(vct_ref[0], p.astype(BF16))
    psum = sum(p[:, h * tq:(h + 1) * tq] for h in range(NSA_HEADS))
    imp = sum(_dot(covert_ref[...], piece) for piece in _split3_bf16(psum))

    nb = covert_ref.shape[0]
    jb = lax.broadcasted_iota(jnp.int32, (nb, tq), 0)
    cur = (q0 + lax.broadcasted_iota(jnp.int32, (nb, tq), 1)) // NSA_SEL_BLOCK
    forced = (jb == 0) | (jb == cur) | (jb == cur - 1)
    visible = jb <= cur
    score = jnp.where(visible, imp + jnp.where(forced, NSA_FORCED_SCORE, 0.0), -jnp.inf)
    beaten = jnp.zeros((nb, tq), F32)
    for j in range(n_blk):
        row = score[j:j + 1, :]
        beaten = beaten + jnp.where((row > score) | ((row == score) & (jb > j)), 1.0, 0.0)
    keep = (beaten < float(NSA_TOP_N)) & visible
    sel = jnp.concatenate([jnp.where(keep, 0.0, MASK_VALUE), jnp.full((128 - nb, tq), MASK_VALUE, F32)], axis=0).astype(BF16)

    qh = [q_ref[0, :, h * 128:(h + 1) * 128] for h in range(NSA_HEADS)]
    kio = lax.broadcasted_iota(jnp.int32, (tk, tq), 0)
    qpos = q0 + lax.broadcasted_iota(jnp.int32, (tk, tq), 1)
    init = tuple(_softmax_init(tq, NSA_HEAD_DIM) for _ in qh)

    def slc_tile(t, carries, causal):
        k0 = pl.multiple_of(t * tk, tk)
        k = ks_ref[0, pl.ds(k0, tk), :]
        bias = _dot(et_ref[pl.ds(k0, tk), :], sel)
        vt = vst_ref[0, :, pl.ds(k0, tk)]
        return _online_softmax_steps(carries, [_dot_nt(k, q) + bias for q in qh], [vt] * NSA_HEADS,
                                     (k0 + kio <= qpos) if causal else None)

    t_diag = q0 // tk
    o_slc = slc_tile(t_diag, lax.fori_loop(0, t_diag, lambda t, c: slc_tile(t, c, False), init), True)

    def win_tile(t, carries):
        k0 = pl.multiple_of(t * tk, tk)
        k = kw_ref[0, pl.ds(k0, tk), :]
        dist = qpos - (k0 + kio)
        return _online_softmax_steps(carries, [_dot_nt(k, q) for q in qh], [vwt_ref[0, :, pl.ds(k0, tk)]] * NSA_HEADS,
                                     (dist >= 0) & (dist < NSA_WINDOW))

    t_lo = jnp.maximum(q0 - NSA_WINDOW + 1, 0) // tk
    o_win = lax.fori_loop(t_lo, t_diag + 1, win_tile, init)

    g = jax.nn.sigmoid(g_ref[0].astype(F32).T)
    for h in range(NSA_HEADS):
        o = (g[3 * h:3 * h + 1] * o_cmp[:, h * tq:(h + 1) * tq] + g[3 * h + 1:3 * h + 2] * _softmax_finish(o_slc[h])
             + g[3 * h + 2:3 * h + 3] * _softmax_finish(o_win[h]))
        o_ref[0, :, h * 128:(h + 1) * 128] = o.T.astype(o_ref.dtype)


def nsa_mixer(p, col0, pos_emb, w1, w2, cos, sin, *, tm=512, tq=128, tk=256):
    b, s, _ = p.shape
    assert col0 % 2304 == 0
    tm, tk = min(tm, s), min(tk, s)
    hw = NSA_HEADS * NSA_HEAD_DIM
    scale = NSA_HEAD_DIM ** -0.5
    row = lambda w: pl.BlockSpec((1, tm, w), lambda bi, i: (bi, i, 0))
    tab = pl.BlockSpec((tm, 128), lambda bi, i: (i, 0))
    tr_spec = pl.BlockSpec((1, 128, tm), lambda bi, i: (bi, 0, i))
    q, kc_in, vc_in, ks, kw, vst, vwt = pl.pallas_call(
        functools.partial(_nsa_prep_kernel, scale=scale),
        grid=(b, s // tm),
        in_specs=[pl.BlockSpec((1, tm, 2304), lambda bi, i: (bi, i, col0 // 2304)), tab, tab],
        out_specs=[row(hw), row(128), row(128), row(128), row(128), tr_spec, tr_spec],
        out_shape=[jax.ShapeDtypeStruct((b, s, hw), BF16), jax.ShapeDtypeStruct((b, s, 128), F32),
                   jax.ShapeDtypeStruct((b, s, 128), F32), jax.ShapeDtypeStruct((b, s, 128), BF16),
                   jax.ShapeDtypeStruct((b, s, 128), BF16), jax.ShapeDtypeStruct((b, 128, s), BF16),
                   jax.ShapeDtypeStruct((b, 128, s), BF16)],
        compiler_params=_params("parallel", "parallel"),
        name="nsa_prep",
    )(p, cos, sin)

    nrow = s // NSA_CMP_STRIDE
    fw = NSA_CMP_STRIDE * NSA_HEAD_DIM
    half = w1.shape[1] // 2
    w1r = jnp.concatenate([w1[:, :half], w1[:, half:]], axis=2)
    pos2 = jnp.pad(pos_emb.reshape(2, 2, fw), ((0, 0), (0, 6), (0, 0)))
    per_b = lambda shape: pl.BlockSpec((1,) + shape, lambda bi: (bi, 0, 0))
    const3 = lambda shape: pl.BlockSpec(shape, lambda bi: (0, 0, 0))
    kc3, vct = pl.pallas_call(
        _nsa_compress_kernel,
        grid=(b,),
        in_specs=[per_b((nrow, fw)), per_b((nrow, fw)), const3(pos2.shape), const3(w1r.shape), const3(w2.shape)],
        out_specs=[pl.BlockSpec((1, 3, nrow, NSA_HEAD_DIM), lambda bi: (bi, 0, 0, 0)), per_b((NSA_HEAD_DIM, nrow))],
        out_shape=[jax.ShapeDtypeStruct((b, 3, nrow, NSA_HEAD_DIM), BF16), jax.ShapeDtypeStruct((b, NSA_HEAD_DIM, nrow), BF16)],
        compiler_params=_params("parallel"),
        name="nsa_compress",
    )(kc_in.reshape(b, nrow, fw), vc_in.reshape(b, nrow, fw), pos2, w1r, w2)

    n_blk = s // NSA_SEL_BLOCK
    nb = 64
    assert n_blk <= nb and tq == 128 and tk % tq == 0
    cmp_start = np.arange(nrow) * NSA_CMP_STRIDE
    cmp_end = cmp_start + NSA_CMP_LEN - 1
    blk_start = np.arange(nb) * NSA_SEL_BLOCK
    cover_t = ((cmp_start[None, :] <= blk_start[:, None] + NSA_SEL_BLOCK - 1)
               & (cmp_end[None, :] >= blk_start[:, None]) & (np.arange(nb)[:, None] < n_blk)
               & (np.arange(nrow)[None, :] < nrow - 1)).astype(np.float32)
    expand_t = ((np.arange(s)[:, None] // NSA_SEL_BLOCK) == np.arange(128)[None, :]).astype(np.float32)
    qrow = lambda w: pl.BlockSpec((1, tq, w), lambda bi, i: (bi, i, 0))
    k_spec = pl.BlockSpec((1, s, 128), lambda bi, i: (bi, 0, 0))
    vt_spec = pl.BlockSpec((1, 128, s), lambda bi, i: (bi, 0, 0))
    gate_block = (col0 + 2 * hw) // 128 + 9
    return pl.pallas_call(
        functools.partial(_nsa_attn_kernel, tq=tq, tk=tk, n_blk=n_blk),
        grid=(b, s // tq),
        in_specs=[qrow(hw), pl.BlockSpec((1, 3, nrow, NSA_HEAD_DIM), lambda bi, i: (bi, 0, 0, 0)),
                  pl.BlockSpec((1, NSA_HEAD_DIM, nrow), lambda bi, i: (bi, 0, 0)),
                  pl.BlockSpec((nb, nrow), lambda bi, i: (0, 0)),
                  k_spec, vt_spec, k_spec, vt_spec, pl.BlockSpec((s, 128), lambda bi, i: (0, 0)),
                  pl.BlockSpec((1, tq, 128), lambda bi, i: (bi, i, gate_block))],
        out_specs=qrow(hw),
        out_shape=jax.ShapeDtypeStruct((b, s, hw), BF16),
        compiler_params=_params("parallel", "arbitrary"),
        name="nsa_attn",
    )(q, kc3, vct, jnp.asarray(cover_t, dtype=BF16), ks, vst, kw, vwt, jnp.asarray(expand_t, dtype=BF16), p)


def _rwkv_prep_kernel(z_ref, zp_ref, mu_ref, w0_ref, a0_ref, kk_ref, ka_ref, rk_ref, w2_ref, a2_ref, g2_ref, bd_ref,
                      csum_ref, kt_ref, rt_ref, kh_ref, bh_ref, kbar_ref, bbar_ref, v_ref, etot_ref, g_ref, bonus_ref):
    i = pl.program_id(1)
    z = z_ref[0]
    tm = z.shape[0]
    prev_row = jnp.where(i > 0, zp_ref[0, 7:8, :], 0.0)
    rowid = lax.broadcasted_iota(jnp.int32, z.shape, 0)
    z_prev = jnp.where(rowid == 0, prev_row, pltpu.roll(z, 1, axis=0))
    z = z + (z_prev - z) * mu_ref[...]
    c = BRANCH_WIDTH
    r, k, v = z[:, :c], z[:, c:2 * c], z[:, 2 * c:3 * c]
    wd, ad, gd = z[:, 3 * c:3 * c + 128], z[:, 3 * c + 128:3 * c + 256], z[:, 3 * c + 256:3 * c + 512]
    wpre = -(w0_ref[...] + _dot(jnp.tanh(wd), w2_ref[...], HIGHEST))
    w_log = -(jnp.maximum(wpre, 0.0) + jnp.log1p(jnp.exp(-jnp.abs(wpre)))) - 0.5
    lw = -jnp.exp(w_log)
    a = jax.nn.sigmoid(a0_ref[...] + _dot(ad, a2_ref[...], HIGHEST))
    g_ref[0] = _dot(jax.nn.sigmoid(gd), g2_ref[...], HIGHEST)
    kk = k * kk_ref[...]
    kk = kk * lax.rsqrt(jnp.maximum(_dot(kk * kk, bd_ref[...], HIGHEST), 1e-24))
    k = k * (1.0 + (a - 1.0) * ka_ref[...])
    bonus_ref[0] = _dot(r * k * rk_ref[...], bd_ref[...], HIGHEST) * v
    sums = sum(_dot(csum_ref[...], piece) for piece in _split3_bf16(lw))
    cum, total = sums[:tm], sums[tm:]
    dec_out, dec_end = jnp.exp(-cum), jnp.exp(total - cum)
    b = kk * a
    kt_ref[0] = (kk * jnp.exp(cum - lw)).astype(BF16)
    rt_ref[0] = (r * jnp.exp(cum)).astype(BF16)
    kh_ref[0] = (k * dec_out).astype(BF16)
    bh_ref[0] = (b * dec_out).astype(BF16)
    kbar_ref[0] = (k * dec_end).astype(BF16)
    bbar_ref[0] = (b * dec_end).astype(BF16)
    v_ref[0] = v.astype(BF16)
    etot_ref[0] = jnp.exp(total)


def _rwkv_scan_kernel(kt_ref, rt_ref, kh_ref, bh_ref, kbar_ref, bbar_ref, v_ref, etot_ref, g_ref, bonus_ref,
                      gw_ref, gb_ref, y_ref, state_ref, *, L):
    n_chunks = kt_ref.shape[1] // L
    N = RWKV_HEAD_DIM
    n_pairs = kt_ref.shape[2] // 128

    @pl.when(pl.program_id(1) == 0)
    def _():
        state_ref[...] = jnp.zeros_like(state_ref)

    ri = lax.broadcasted_iota(jnp.int32, (L, 2 * L), 0)
    ci = lax.broadcasted_iota(jnp.int32, (L, 2 * L), 1) & (L - 1)
    strict2, incl2 = ci < ri, ci <= ri
    eye = jnp.where(lax.broadcasted_iota(jnp.int32, (L, L), 0) == lax.broadcasted_iota(jnp.int32, (L, L), 1), 1.0, 0.0)
    first2 = lax.broadcasted_iota(jnp.int32, (2 * L, 128), 1) < N
    first = lax.broadcasted_iota(jnp.int32, (L, 128), 1) < N
    bd = (lax.broadcasted_iota(jnp.int32, (128, 128), 0) < N) == (lax.broadcasted_iota(jnp.int32, (128, 128), 1) < N)
    zero = jnp.zeros((), BF16)

    xs_in, a_k, a_r = [], [], []
    for c in range(n_chunks):
        rs = slice(c * L, (c + 1) * L)
        for p in range(n_pairs):
            sl = slice(p * 128, (p + 1) * 128)
            x = jnp.concatenate([kt_ref[0, rs, sl], rt_ref[0, rs, sl]], axis=0)
            y = jnp.concatenate([kh_ref[0, rs, sl], bh_ref[0, rs, sl]], axis=0)
            xs_in.append(x)
            for sub in range(2):
                gm = _dot_nt(jnp.where(first2 if sub == 0 else ~first2, x, zero), y)
                a_k.append(jnp.where(strict2, gm[:L], 0.0))
                a_r.append(jnp.where(incl2, gm[L:], 0.0))
    a_k, a_r = jnp.stack(a_k), jnp.stack(a_r)
    a_kb = a_k[:, :, L:]
    bmm = lambda a, b: jnp.einsum("hij,hjk->hik", a.astype(BF16), b.astype(BF16), preferred_element_type=F32)
    tinv = eye[None] - a_kb
    pw = a_kb
    for _ in range(int(np.log2(L)) - 1):
        pw = bmm(pw, pw)
        tinv = tinv + bmm(tinv, pw)

    pick = lambda t: jnp.where(first, t[:L], t[L:])
    hsum = lambda t: jnp.where(first, jnp.sum(jnp.where(first, t, 0.0), axis=-1, keepdims=True),
                               jnp.sum(jnp.where(first, 0.0, t), axis=-1, keepdims=True))
    states = [state_ref[p] for p in range(n_pairs)]
    for c in range(n_chunks):
        rs = slice(c * L, (c + 1) * L)
        for p in range(n_pairs):
            sl = slice(p * 128, (p + 1) * 128)
            h0 = (c * n_pairs + p) * 2
            h1 = h0 + 1
            st = states[p]
            xs = _dot_nt(xs_in[c * n_pairs + p], st.astype(BF16))
            v = v_ref[0, rs, sl]
            akkv = pick(_dot(jnp.concatenate([a_k[h0, :, :L], a_k[h1, :, :L]], axis=0).astype(BF16), v))
            rhs = -(xs[:L] + akkv)
            u = pick(_dot(jnp.concatenate([tinv[h0], tinv[h1]], axis=0).astype(BF16), rhs.astype(BF16)))
            vu = jnp.concatenate([v, u.astype(BF16)], axis=0)
            o = xs[L:] + pick(_dot(jnp.concatenate([a_r[h0], a_r[h1]], axis=0).astype(BF16), vu))
            kb = jnp.concatenate([kbar_ref[0, rs, sl], bbar_ref[0, rs, sl]], axis=0)
            states[p] = st * etot_ref[0, c * L:c * L + 1, sl] + jnp.where(bd, _dot_tn(vu, kb), 0.0)
            ctr = o - hsum(o) * (1.0 / N)
            on = ctr * lax.rsqrt(hsum(ctr * ctr) * (1.0 / N) + RWKV_GN_EPS)
            y_ref[0, rs, sl] = ((on * gw_ref[:, sl] + gb_ref[:, sl] + bonus_ref[0, rs, sl]) * g_ref[0, rs, sl]).astype(y_ref.dtype)
    for p in range(n_pairs):
        state_ref[p] = states[p]


def rwkv7_mixer(z, mu, w0, w2, a0, a2, g2, k_k, k_a, r_k, gn_w, gn_b, *, tm=256):
    b, s, zw = z.shape
    tm = min(tm, s)
    c, hd, nh = BRANCH_WIDTH, RWKV_HEAD_DIM, RWKV_HEADS
    L = min(RWKV_CHUNK, s)
    assert tm % L == 0 and L & (L - 1) == 0
    bd = jnp.asarray(np.kron(np.eye(nh, dtype=np.float32), np.ones((hd, hd), np.float32)))
    chunk = np.arange(tm) // L
    same = chunk[:, None] == chunk[None, :]
    csum = np.concatenate([same & (np.arange(tm)[None, :] <= np.arange(tm)[:, None]), same], axis=0)
    csum = jnp.asarray(csum.astype(np.float32), dtype=BF16)
    row = lambda w: pl.BlockSpec((1, tm, w), lambda bi, i: (bi, i, 0))
    vec = lambda w: pl.BlockSpec((1, w), lambda bi, i: (0, 0))
    mat = lambda shape: pl.BlockSpec(shape, lambda bi, i: (0, 0))
    outs = pl.pallas_call(
        _rwkv_prep_kernel,
        grid=(b, s // tm),
        in_specs=[row(zw), pl.BlockSpec((1, 8, zw), lambda bi, i: (bi, jnp.maximum(i * (tm // 8) - 1, 0), 0)),
                  vec(zw), vec(c), vec(c), vec(c), vec(c), vec(c), mat(w2.shape), mat(a2.shape), mat(g2.shape),
                  mat(bd.shape), mat(csum.shape)],
        out_specs=[row(c)] * 10,
        out_shape=[jax.ShapeDtypeStruct((b, s, c), BF16)] * 7 + [jax.ShapeDtypeStruct((b, s, c), F32)] * 3,
        compiler_params=_params("parallel", "parallel"),
        name="rwkv_prep",
    )(z, z, mu.reshape(1, zw), w0.reshape(1, c), a0.reshape(1, c), k_k.reshape(1, c), k_a.reshape(1, c),
      r_k.reshape(1, c), w2, a2, g2, bd, csum)
    rows = min(RWKV_CHUNKS_PER_STEP * L, s)
    blk = pl.BlockSpec((1, rows, c), lambda bi, ci: (bi, ci, 0))
    gvec = pl.BlockSpec((1, c), lambda bi, ci: (0, 0))
    return pl.pallas_call(
        functools.partial(_rwkv_scan_kernel, L=L),
        grid=(b, s // rows),
        in_specs=[blk] * 10 + [gvec, gvec],
        out_specs=blk,
        out_shape=jax.ShapeDtypeStruct((b, s, c), BF16),
        scratch_shapes=[pltpu.VMEM((c // 128, 128, 128), F32)],
        compiler_params=_params("parallel", "arbitrary"),
        name="rwkv_scan",
    )(*outs, gn_w.reshape(1, c), gn_b.reshape(1, c))


def _retention_kernel(q_ref, k_ref, v_ref, g_ref, cos_ref, sin_ref, inner_ref, qd_ref, kd_ref, cd_ref, o_ref, state_ref):
    @pl.when(pl.program_id(1) == 0)
    def _():
        state_ref[...] = jnp.zeros_like(state_ref)

    C = q_ref.shape[1]
    kw = RET_HEADS * RET_K_DIM
    cos, sin = cos_ref[...], sin_ref[...]
    q = q_ref[0, :, :kw].astype(F32) * cos + q_ref[0, :, kw:].astype(F32) * sin
    k = (k_ref[0, :, :kw].astype(F32) * cos + k_ref[0, :, kw:].astype(F32) * sin) * RET_K_DIM ** -0.5
    lane = lax.broadcasted_iota(jnp.int32, (C, 128), 1)
    for h in range(RET_HEADS):
        pair, sub = h // 2, h % 2
        in_head = (lane // RET_K_DIM) == sub
        qh = jnp.where(in_head, q[:, pair * 128:(pair + 1) * 128], 0.0)
        kh = jnp.where(in_head, k[:, pair * 128:(pair + 1) * 128], 0.0)
        vh = v_ref[0, :, h * RET_V_DIM:(h + 1) * RET_V_DIM]
        gate = g_ref[0, :, h * RET_V_DIM:(h + 1) * RET_V_DIM].astype(F32)
        scores = _dot_nt(qh.astype(BF16), kh.astype(BF16)) * inner_ref[h]
        st = state_ref[h]
        o = _dot(scores.astype(BF16), vh) + _dot((qh * qd_ref[h]).astype(BF16), st.astype(BF16))
        state_ref[h] = st * cd_ref[h, 0:1, :] + _dot_tn((kh * kd_ref[h]).astype(BF16), vh)
        c = o - jnp.mean(o, axis=-1, keepdims=True)
        y = c * lax.rsqrt(jnp.mean(c * c, axis=-1, keepdims=True) + NORM_EPS)
        o_ref[0, :, h * RET_V_DIM:(h + 1) * RET_V_DIM] = (y * (gate * jax.nn.sigmoid(gate))).astype(o_ref.dtype)


def retention_mixer(p, col0, cos, sin):
    b, s, _ = p.shape
    assert col0 % 512 == 0
    H, C = RET_HEADS, min(RET_CHUNK, s)
    log_gamma = jnp.log1p(-jnp.exp2(-5.0 - jnp.arange(H, dtype=F32)))
    n = jnp.arange(C, dtype=F32)
    dist = n[:, None] - n[None, :]
    inner = jnp.where(dist >= 0, jnp.exp(jnp.maximum(dist, 0.0) * log_gamma[:, None, None]), 0.0)
    q_decay = jnp.exp((n + 1.0) * log_gamma[:, None])
    k_decay = jnp.exp((C - 1.0 - n) * log_gamma[:, None])
    chunk_decay = jnp.exp(C * log_gamma)
    lanes = lambda t: jnp.broadcast_to(t[:, :, None], (H, C, 128))
    const = lambda shape: pl.BlockSpec(shape, lambda bi, ci: (0,) * len(shape))
    kw = RET_HEADS * RET_K_DIM
    return pl.pallas_call(
        _retention_kernel,
        grid=(b, s // C),
        in_specs=[pl.BlockSpec((1, C, 512), functools.partial(lambda bi, ci, j: (bi, ci, j), j=col0 // 512 + j))
                  for j in range(4)] + [
                  pl.BlockSpec((C, kw), lambda bi, ci: (ci, 0)), pl.BlockSpec((C, kw), lambda bi, ci: (ci, 0)),
                  const((H, C, C)), const((H, C, 128)), const((H, C, 128)), const((H, 8, 128))],
        out_specs=pl.BlockSpec((1, C, H * RET_V_DIM), lambda bi, ci: (bi, ci, 0)),
        out_shape=jax.ShapeDtypeStruct((b, s, H * RET_V_DIM), BF16),
        scratch_shapes=[pltpu.VMEM((H, 128, 128), F32)],
        compiler_params=_params("parallel", "arbitrary"),
        name="retention",
    )(p, p, p, p, cos, sin, inner, lanes(q_decay), lanes(k_decay), jnp.broadcast_to(chunk_decay[:, None, None], (H, 8, 128)))


def _rot_half_cols(w, half):
    return jnp.concatenate([-w[..., half:2 * half], w[..., :half]], axis=-1)


def _pad_cols(w, width):
    return jnp.pad(w, [(0, 0)] * (w.ndim - 1) + [(0, width - w.shape[-1])])


def _rope_tables(s, inv_freq, width, fill_cos):
    ang = jnp.arange(s, dtype=F32)[:, None] * inv_freq[None, :]
    cos, sin = jnp.cos(ang), jnp.sin(ang)
    rot = 2 * cos.shape[1]
    cos2 = jnp.concatenate([cos, cos, jnp.full((s, width - rot), fill_cos, F32)], axis=1)
    sin2 = jnp.concatenate([sin, sin, jnp.zeros((s, width - rot), F32)], axis=1)
    return cos2, sin2


def _split_offsets():
    sizes = (MLA_Q_RANK, MLA_KV_RANK, MLA_ROPE_DIM, 512, 128, 128, 128, 128, 128, 128, 12,
             3 * BRANCH_WIDTH + RWKV_DECAY_LORA + RWKV_A_LORA + RWKV_GATE_LORA, 256, 256, 512, 512, N_MIXERS * D_MODEL)
    offs = np.concatenate([[0], np.cumsum(sizes)])
    return [(int(offs[i]), int(offs[i + 1])) for i in range(len(sizes))]


def _pack_in_proj(w):
    sl = [w[:, a:b] for a, b in _split_offsets()]
    (ql, kvl, kr, nq, nkc, nvc, nks, nvs, nkw, nvw, ng, rz, rq, rk, rv, rg, mg) = sl
    w_mla = jnp.concatenate([ql, kvl, _pad_cols(kr, 128), _pad_cols(_rot_half_cols(kr, 32), 128)], axis=1)
    nh = NSA_ROT_DIM // 2
    rot128 = lambda t: _pad_cols(_rot_half_cols(t[:, :NSA_ROT_DIM], nh), 128)
    nq_rot = jnp.concatenate([rot128(nq[:, h * 128:(h + 1) * 128]) for h in range(NSA_HEADS)], axis=1)
    w_nsa = jnp.concatenate([nq, nq_rot, nkc, rot128(nkc), nvc, nks, rot128(nks), nvs, nkw, rot128(nkw), nvw,
                             _pad_cols(ng, 128)], axis=1)
    rh = RET_K_DIM // 2
    ret_rot = lambda t: jnp.concatenate([_rot_half_cols(t[:, h * 64:(h + 1) * 64], rh) for h in range(RET_HEADS)], axis=1)
    w_ret = jnp.concatenate([rq, ret_rot(rq), rk, ret_rot(rk), rv, rg], axis=1)
    c = BRANCH_WIDTH
    w_rwkv = jnp.concatenate([rz[:, :3 * c], _pad_cols(rz[:, 3 * c:3 * c + 96], 128),
                              _pad_cols(rz[:, 3 * c + 96:3 * c + 192], 128), rz[:, 3 * c + 192:]], axis=1)
    bf = lambda t: t.astype(BF16)
    return bf(w_mla), bf(w_nsa), bf(w_ret), bf(w_rwkv), bf(mg)


def _pack_mla_weights(w_uq, w_ukv):
    H, dn, dr, dv = MLA_HEADS, MLA_NOPE_DIM, MLA_ROPE_DIM, MLA_V_DIM
    uq = w_uq.reshape(-1, H, dn + dr)
    nope = uq[:, :, :dn].reshape(-1, H * dn)
    rope = uq[:, :, dn:]
    rope_p = _pad_cols(rope, 128).reshape(-1, H * 128)
    rope_r = _pad_cols(_rot_half_cols(rope, dr // 2), 128).reshape(-1, H * 128)
    ukv = w_ukv.reshape(-1, H, dn + dv)
    wkv = jnp.concatenate([ukv[:, :, :dn].reshape(-1, H * dn), ukv[:, :, dn:].reshape(-1, H * dv)], axis=1)
    return jnp.concatenate([nope, rope_p, rope_r], axis=1).astype(BF16), wkv.astype(BF16)


def _pad_rows(w, rows):
    return jnp.pad(w, ((0, rows - w.shape[0]), (0, 0)))


def kernel(x, w_in, w_branch, w_out, w_up, w_down, norm_gains, mla_g_q, mla_g_kv, mla_w_uq, mla_w_ukv,
           nsa_cmp_pos, nsa_cmp_w1, nsa_cmp_w2, rwkv_mu, rwkv_w0, rwkv_w2, rwkv_a0, rwkv_a2, rwkv_g2,
           rwkv_k_k, rwkv_k_a, rwkv_r_k, rwkv_gn_w, rwkv_gn_b):
    B, S, D = x.shape
    depth = w_in.shape[0]
    T = B * S
    tm, tm_in = min(512, T), min(1024, T)
    mla_cos, mla_sin = _rope_tables(S, jnp.float32(ROPE_THETA) ** (-jnp.arange(0, MLA_ROPE_DIM, 2, dtype=F32) / MLA_ROPE_DIM),
                                    128, 0.0)
    nsa_cos, nsa_sin = _rope_tables(S, jnp.float32(ROPE_THETA) ** (-jnp.arange(0, NSA_ROT_DIM, 2, dtype=F32) / NSA_ROT_DIM),
                                    128, 1.0)
    ret_cos, ret_sin = _rope_tables(S, jnp.float32(RET_THETA) ** (-jnp.linspace(0.0, 1.0, RET_K_DIM // 2, dtype=F32)),
                                    RET_K_DIM, 0.0)
    ret_cos, ret_sin = jnp.tile(ret_cos, (1, RET_HEADS)), jnp.tile(ret_sin, (1, RET_HEADS))
    c = BRANCH_WIDTH
    x = x.reshape(T, D)
    for l in range(depth):
        w_mla, w_nsa, w_ret, w_rwkv, w_gate = _pack_in_proj(w_in[l])
        g_pre = norm_gains[l, 0]
        groups = (w_nsa, w_mla, w_gate, w_ret)
        col_nsa, col_mla, col_gate, col_ret = np.cumsum([0] + [w.shape[1] for w in groups[:-1]]).tolist()
        p2 = norm_matmul(x, g_pre, jnp.concatenate(groups, axis=1), BF16, tm_in, 1024)
        p = p2.reshape(B, S, -1)
        p_rwkv = norm_matmul(x, g_pre, w_rwkv, F32, tm_in, 1024)

        wq, wkv = _pack_mla_weights(mla_w_uq[l], mla_w_ukv[l])
        y_mla = mla_mixer(p, col_mla, mla_g_q[l], mla_g_kv[l], wq, wkv, mla_cos, mla_sin)
        y_nsa = nsa_mixer(p, col_nsa, nsa_cmp_pos[l], nsa_cmp_w1[l], nsa_cmp_w2[l], nsa_cos, nsa_sin)
        mu = rwkv_mu[l]
        mu_p = jnp.concatenate([mu[:3 * c], _pad_cols(mu[3 * c:3 * c + 96], 128), _pad_cols(mu[3 * c + 96:3 * c + 192], 128),
                                mu[3 * c + 192:]])
        y_rwkv = rwkv7_mixer(p_rwkv.reshape(B, S, -1), mu_p, rwkv_w0[l], _pad_rows(rwkv_w2[l], 128), rwkv_a0[l],
                             _pad_rows(rwkv_a2[l], 128), rwkv_g2[l], rwkv_k_k[l], rwkv_k_a[l], rwkv_r_k[l].reshape(-1),
                             rwkv_gn_w[l], rwkv_gn_b[l])
        y_ret = retention_mixer(p, col_ret, ret_cos, ret_sin)

        ys = [y.reshape(T, c) for y in (y_mla, y_nsa, y_rwkv, y_ret)]
        merged = merge_branches(ys, p2, col_gate, w_branch[l].astype(BF16), tm, 512)
        x = matmul_norm_residual(merged, w_out[l].astype(BF16), norm_gains[l, 1], x, tm)
        x = ffn_block(x, norm_gains[l, 2], w_up[l].astype(BF16), w_down[l].astype(BF16), norm_gains[l, 3], tm, 1024)
    return x.reshape(B, S, D)
```

```python
import functools

import numpy as np
import jax
import jax.numpy as jnp
from jax import lax
from jax.experimental import pallas as pl
from jax.experimental.pallas import tpu as pltpu

F32 = jnp.float32
BF16 = jnp.bfloat16
HIGHEST = lax.Precision.HIGHEST

V7X_LANES = 128
V7X_VMEM_LIMIT_BYTES = 56 * 1024 * 1024

D_MODEL = 2048
N_MIXERS = 4
BRANCH_WIDTH = D_MODEL // N_MIXERS
D_FF = 4 * D_MODEL
ROPE_THETA = 500000.0
NORM_EPS = 1e-6
MASK_VALUE = -1e30

MLA_NOPE_DIM = 128
MLA_ROPE_DIM = 64
MLA_V_DIM = 128
MLA_HEADS = 4
MLA_Q_RANK = 384
MLA_KV_RANK = 128

NSA_HEAD_DIM = 128
NSA_HEADS = 4
NSA_ROT_DIM = 32
NSA_CMP_LEN = 32
NSA_CMP_STRIDE = 16
NSA_SEL_BLOCK = 64
NSA_TOP_N = 16
NSA_WINDOW = 512
NSA_FORCED_SCORE = 1000.0

RWKV_HEAD_DIM = 64
RWKV_HEADS = 8
RWKV_DECAY_LORA = 96
RWKV_A_LORA = 96
RWKV_GATE_LORA = 256
RWKV_GN_EPS = 64e-5
RWKV_CHUNK = 64
RWKV_CHUNKS_PER_STEP = 4

RET_HEADS = 4
RET_V_DIM = 128
RET_K_DIM = 64
RET_CHUNK = 128
RET_THETA = 10000.0


def _params(*sem):
    return pltpu.CompilerParams(dimension_semantics=sem, vmem_limit_bytes=V7X_VMEM_LIMIT_BYTES)


def _dot(a, b, precision=None):
    return jnp.dot(a, b, preferred_element_type=F32, precision=precision)


def _dot_nt(a, b, precision=None):
    return lax.dot_general(a, b, (((1,), (1,)), ((), ())), preferred_element_type=F32, precision=precision)


def _dot_tn(a, b, precision=None):
    return lax.dot_general(a, b, (((0,), (0,)), ((), ())), preferred_element_type=F32, precision=precision)


def _rms(x, g, eps=NORM_EPS):
    return x * lax.rsqrt(jnp.mean(x * x, axis=-1, keepdims=True) + eps) * g


def _norm_matmul_kernel(x_ref, g_ref, w_ref, o_ref, h_ref):
    @pl.when(pl.program_id(1) == 0)
    def _():
        h_ref[...] = _rms(x_ref[...], g_ref[...]).astype(BF16)

    o_ref[...] = _dot(h_ref[...], w_ref[...]).astype(o_ref.dtype)


def norm_matmul(x, g, w, out_dtype, tm, tn):
    m, k = x.shape
    n = w.shape[1]
    assert m % tm == 0 and n % tn == 0
    return pl.pallas_call(
        _norm_matmul_kernel,
        grid=(m // tm, n // tn),
        in_specs=[pl.BlockSpec((tm, k), lambda i, j: (i, 0)),
                  pl.BlockSpec((1, k), lambda i, j: (0, 0)),
                  pl.BlockSpec((k, tn), lambda i, j: (0, j))],
        out_specs=pl.BlockSpec((tm, tn), lambda i, j: (i, j)),
        out_shape=jax.ShapeDtypeStruct((m, n), out_dtype),
        scratch_shapes=[pltpu.VMEM((tm, k), BF16)],
        compiler_params=_params("parallel", "arbitrary"),
        name="norm_matmul",
    )(x, g.reshape(1, k), w)


def _merge_kernel(y0_ref, y1_ref, y2_ref, y3_ref, g0_ref, g1_ref, g2_ref, g3_ref, wb_ref, o_ref):
    acc = None
    for m, (y_ref, g_ref) in enumerate(((y0_ref, g0_ref), (y1_ref, g1_ref), (y2_ref, g2_ref), (y3_ref, g3_ref))):
        gate = 0.5 * jnp.tanh(0.5 * g_ref[...].astype(F32)) + 0.5
        term = gate * _dot(y_ref[...], wb_ref[m])
        acc = term if acc is None else acc + term
    o_ref[...] = acc.astype(o_ref.dtype)


def merge_branches(ys, gates, col0, wb, tm, tn):
    m, c = ys[0].shape
    d = wb.shape[2]
    nj = d // tn
    assert col0 % tn == 0
    y_spec = pl.BlockSpec((tm, c), lambda i, j: (i, 0))
    gate_specs = [pl.BlockSpec((tm, tn), functools.partial(lambda i, j, mm: (i, col0 // tn + mm * nj + j), mm=mm))
                  for mm in range(N_MIXERS)]
    return pl.pallas_call(
        _merge_kernel,
        grid=(m // tm, nj),
        in_specs=[y_spec] * N_MIXERS + gate_specs + [pl.BlockSpec((N_MIXERS, c, tn), lambda i, j: (0, 0, j))],
        out_specs=pl.BlockSpec((tm, tn), lambda i, j: (i, j)),
        out_shape=jax.ShapeDtypeStruct((m, d), BF16),
        compiler_params=_params("parallel", "arbitrary"),
        name="merge_branches",
    )(*ys, gates, gates, gates, gates, wb)


def _matmul_norm_res_kernel(a_ref, w_ref, g_ref, x_ref, o_ref):
    y = _dot(a_ref[...], w_ref[...])
    o_ref[...] = x_ref[...] + _rms(y, g_ref[...])


def matmul_norm_residual(a, w, g, x, tm):
    m, k = a.shape
    d = w.shape[1]
    return pl.pallas_call(
        _matmul_norm_res_kernel,
        grid=(m // tm,),
        in_specs=[pl.BlockSpec((tm, k), lambda i: (i, 0)),
                  pl.BlockSpec((k, d), lambda i: (0, 0)),
                  pl.BlockSpec((1, d), lambda i: (0, 0)),
                  pl.BlockSpec((tm, d), lambda i: (i, 0))],
        out_specs=pl.BlockSpec((tm, d), lambda i: (i, 0)),
        out_shape=jax.ShapeDtypeStruct((m, d), F32),
        compiler_params=_params("parallel"),
        name="matmul_norm_residual",
    )(a, w, g.reshape(1, d), x)


def _ffn_kernel(x_ref, g1_ref, wu_ref, wd_ref, g2_ref, o_ref, h_ref, acc_ref):
    f = pl.program_id(1)

    @pl.when(f == 0)
    def _():
        h_ref[...] = _rms(x_ref[...], g1_ref[...]).astype(BF16)
        acc_ref[...] = jnp.zeros_like(acc_ref)

    u = jnp.maximum(_dot(h_ref[...], wu_ref[...]), 0.0)
    acc_ref[...] += _dot((u * u).astype(BF16), wd_ref[...])

    @pl.when(f == pl.num_programs(1) - 1)
    def _():
        o_ref[...] = x_ref[...] + _rms(acc_ref[...], g2_ref[...])


def ffn_block(x, g1, wu, wd, g2, tm, tf):
    m, d = x.shape
    ff = wu.shape[1]
    return pl.pallas_call(
        _ffn_kernel,
        grid=(m // tm, ff // tf),
        in_specs=[pl.BlockSpec((tm, d), lambda i, f: (i, 0)),
                  pl.BlockSpec((1, d), lambda i, f: (0, 0)),
                  pl.BlockSpec((d, tf), lambda i, f: (0, f)),
                  pl.BlockSpec((tf, d), lambda i, f: (f, 0)),
                  pl.BlockSpec((1, d), lambda i, f: (0, 0))],
        out_specs=pl.BlockSpec((tm, d), lambda i, f: (i, 0)),
        out_shape=jax.ShapeDtypeStruct((m, d), F32),
        scratch_shapes=[pltpu.VMEM((tm, d), BF16), pltpu.VMEM((tm, d), F32)],
        compiler_params=_params("parallel", "arbitrary"),
        name="ffn_block",
    )(x, g1.reshape(1, d), wu, wd, g2.reshape(1, d))


def _online_softmax_steps(carries, scores, vt_tiles, masks):
    scores = [s if mask is None else jnp.where(mask, s, -jnp.inf) for s, mask in zip(scores, masks)]
    stats = []
    for (m_prev, l_prev, _), s in zip(carries, scores):
        m_new = jnp.maximum(m_prev, jnp.max(s, axis=0, keepdims=True))
        alpha = jnp.exp(m_prev - m_new)
        p = jnp.exp(s - m_new)
        stats.append((m_new, alpha, alpha * l_prev + jnp.sum(p, axis=0, keepdims=True), p.astype(BF16)))
    return tuple((m_new, l_new, alpha * acc_prev + _dot(vt, p))
                 for (m_new, alpha, l_new, p), (_, _, acc_prev), vt in zip(stats, carries, vt_tiles))


def _softmax_init(cols, dv):
    return (jnp.full((1, cols), MASK_VALUE, F32), jnp.zeros((1, cols), F32), jnp.zeros((dv, cols), F32))


def _softmax_finish(carry):
    _, l, acc = carry
    return acc / jnp.where(l > 0.0, l, 1.0)


def _mla_prep_kernel(p_ref, gq_ref, gkv_ref, wq_ref, wkv_ref, cos_ref, sin_ref,
                     qn_ref, qr_ref, kn_ref, vt_ref, kr_ref, *, scale):
    p = p_ref[0].astype(F32)
    nq = _rms(p[:, :MLA_Q_RANK], gq_ref[...]).astype(BF16)
    nkv = _rms(p[:, MLA_Q_RANK:MLA_Q_RANK + MLA_KV_RANK], gkv_ref[...]).astype(BF16)
    q = _dot(nq, wq_ref[...])
    kv = _dot(nkv, wkv_ref[...])
    cos, sin = cos_ref[...], sin_ref[...]
    hw = MLA_HEADS * MLA_NOPE_DIM
    qn_ref[0] = (q[:, :hw] * scale).astype(BF16)
    for h in range(MLA_HEADS):
        a = q[:, hw + h * 128: hw + (h + 1) * 128]
        b = q[:, 2 * hw + h * 128: 2 * hw + (h + 1) * 128]
        qr_ref[0, :, h * 128:(h + 1) * 128] = ((a * cos + b * sin) * scale).astype(BF16)
    kn_ref[0] = kv[:, :hw].astype(BF16)
    vt_ref[0] = kv[:, hw:].T.astype(BF16)
    kr_ref[0] = (p[:, 512:640] * cos + p[:, 640:768] * sin).astype(BF16)


def _mla_attn_kernel(qn_ref, qr_ref, kn_ref, kr_ref, vt_ref, o_ref, *, tq, tk):
    i = pl.program_id(1)
    hs = [slice(h * 128, (h + 1) * 128) for h in range(MLA_HEADS)]
    qn = [qn_ref[0, :, sl] for sl in hs]
    qr = [qr_ref[0, :, sl] for sl in hs]

    def tile(t, carries, mask):
        k0 = pl.multiple_of(t * tk, tk)
        kr = kr_ref[0, pl.ds(k0, tk), :]
        scores = [_dot_nt(kn_ref[0, pl.ds(k0, tk), sl], qn[h]) + _dot_nt(kr, qr[h]) for h, sl in enumerate(hs)]
        return _online_softmax_steps(carries, scores, [vt_ref[0, sl, pl.ds(k0, tk)] for sl in hs], [mask] * len(hs))

    init = tuple(_softmax_init(tq, MLA_V_DIM) for _ in hs)
    t_diag = (i * tq) // tk
    carries = lax.fori_loop(0, t_diag, lambda t, c: tile(t, c, None), init)
    causal = (t_diag * tk + lax.broadcasted_iota(jnp.int32, (tk, tq), 0)
              <= i * tq + lax.broadcasted_iota(jnp.int32, (tk, tq), 1))
    for sl, carry in zip(hs, tile(t_diag, carries, causal)):
        o_ref[0, :, sl] = _softmax_finish(carry).T.astype(o_ref.dtype)


def mla_mixer(p, col0, g_q, g_kv, wq, wkv, cos, sin, *, tm=512, tq=256, tk=512):
    b, s, _ = p.shape
    tm, tq, tk = min(tm, s), min(tq, s), min(tk, s)
    assert tk % tq == 0 and col0 % 768 == 0
    hw = MLA_HEADS * 128
    scale = (MLA_NOPE_DIM + MLA_ROPE_DIM) ** -0.5
    full = lambda shape: pl.BlockSpec(shape, lambda bi, i: (0,) * len(shape))
    row = lambda w: pl.BlockSpec((1, tm, w), lambda bi, i: (bi, i, 0))
    outs = pl.pallas_call(
        functools.partial(_mla_prep_kernel, scale=scale),
        grid=(b, s // tm),
        in_specs=[pl.BlockSpec((1, tm, 768), lambda bi, i: (bi, i, col0 // 768)),
                  full((1, MLA_Q_RANK)), full((1, MLA_KV_RANK)), full(wq.shape), full(wkv.shape),
                  pl.BlockSpec((tm, 128), lambda bi, i: (i, 0)), pl.BlockSpec((tm, 128), lambda bi, i: (i, 0))],
        out_specs=[row(hw), row(hw), row(hw), pl.BlockSpec((1, hw, tm), lambda bi, i: (bi, 0, i)), row(128)],
        out_shape=[jax.ShapeDtypeStruct((b, s, hw), BF16)] * 3 + [jax.ShapeDtypeStruct((b, hw, s), BF16),
                                                                   jax.ShapeDtypeStruct((b, s, 128), BF16)],
        compiler_params=_params("parallel", "parallel"),
        name="mla_prep",
    )(p, g_q.reshape(1, -1), g_kv.reshape(1, -1), wq, wkv, cos, sin)
    qn, qr, kn, vt, kr = outs
    q_spec = pl.BlockSpec((1, tq, hw), lambda bi, i: (bi, i, 0))
    per_batch = lambda shape: pl.BlockSpec((1,) + shape, lambda bi, i: (bi, 0, 0))
    return pl.pallas_call(
        functools.partial(_mla_attn_kernel, tq=tq, tk=tk),
        grid=(b, s // tq),
        in_specs=[q_spec, q_spec, per_batch((s, hw)), per_batch((s, 128)), per_batch((hw, s))],
        out_specs=q_spec,
        out_shape=jax.ShapeDtypeStruct((b, s, hw), BF16),
        compiler_params=_params("parallel", "arbitrary"),
        name="mla_attn",
    )(qn, qr, kn, kr, vt)


def _nsa_prep_kernel(p_ref, cos_ref, sin_ref, q_ref, kc_ref, vc_ref, ks_ref, kw_ref, vst_ref, vwt_ref, *, scale):
    cos, sin = cos_ref[...], sin_ref[...]
    hw = NSA_HEADS * NSA_HEAD_DIM
    for h in range(NSA_HEADS):
        a = p_ref[0, :, h * 128:(h + 1) * 128].astype(F32)
        b = p_ref[0, :, hw + h * 128: hw + (h + 1) * 128].astype(F32)
        q_ref[0, :, h * 128:(h + 1) * 128] = ((a * cos + b * sin) * scale).astype(BF16)
    base = 2 * hw

    def rot(col):
        a = p_ref[0, :, base + col * 128: base + (col + 1) * 128].astype(F32)
        b = p_ref[0, :, base + (col + 1) * 128: base + (col + 2) * 128].astype(F32)
        return a * cos + b * sin

    kc_ref[0] = rot(0)
    vc_ref[0] = p_ref[0, :, base + 256: base + 384].astype(F32)
    ks_ref[0] = rot(3).astype(BF16)
    kw_ref[0] = rot(6).astype(BF16)
    transposed = lambda col: p_ref[0, :, base + col * 128: base + (col + 1) * 128].astype(F32).T.astype(BF16)
    vst_ref[0] = transposed(5)
    vwt_ref[0] = transposed(8)


def _gelu_tanh(x):
    return 0.5 * x * (1.0 + jnp.tanh(np.sqrt(2.0 / np.pi).astype(np.float32) * (x + 0.044715 * (x * x * x))))


def _split3_bf16(x):
    hi = x.astype(BF16)
    r1 = x - hi.astype(F32)
    mid = r1.astype(BF16)
    return hi, mid, (r1 - mid.astype(F32)).astype(BF16)


def _nsa_compress_kernel(k_ref, v_ref, pos_ref, w1_ref, w2_ref, kc3_ref, vct_ref):
    nrow = k_ref.shape[1]
    row = lax.broadcasted_iota(jnp.int32, (nrow, NSA_HEAD_DIM), 0)
    outs = []
    for z, x_ref in enumerate((k_ref, v_ref)):
        res = _dot(x_ref[0], w1_ref[z], HIGHEST)
        pb = _dot(pos_ref[z], w1_ref[z], HIGHEST)
        bias = pb[0:1, :NSA_HEAD_DIM] + pb[1:2, NSA_HEAD_DIM:]
        nxt = pltpu.roll(res[:, NSA_HEAD_DIM:], nrow - 1, axis=0)
        hid = _gelu_tanh(res[:, :NSA_HEAD_DIM] + nxt + bias)
        outs.append(jnp.where(row < nrow - 1, _dot(hid, w2_ref[z], HIGHEST), 0.0))
    for z, piece in enumerate(_split3_bf16(outs[0])):
        kc3_ref[0, z] = piece
    vct_ref[0] = outs[1].T.astype(BF16)


def _nsa_attn_kernel(q_ref, kc3_ref, vct_ref, covert_ref, ks_ref, vst_ref, kw_ref, vwt_ref, et_ref, g_ref, o_ref,
                     *, tq, tk, n_blk):
    i = pl.program_id(1)
    q0 = i * tq
    cols = NSA_HEADS * tq
    q4 = jnp.concatenate([q_ref[0, :, h * 128:(h + 1) * 128] for h in range(NSA_HEADS)], axis=0)

    ncmp = kc3_ref.shape[2]
    s = sum(_dot_nt(kc3_ref[0, z], q4) for z in range(3))
    n = lax.broadcasted_iota(jnp.int32, (ncmp, cols), 0)
    qpos_c = q0 + (lax.broadcasted_iota(jnp.int32, (ncmp, cols), 1) & (tq - 1))
    s = jnp.where((n * NSA_CMP_STRIDE + (NSA_CMP_LEN - 1) <= qpos_c) & (n < ncmp - 1), s, -jnp.inf)
    e = jnp.exp(s - jnp.maximum(jnp.max(s, axis=0, keepdims=True), MASK_VALUE))
    l = jnp.sum(e, axis=0, keepdims=True)
    p = e / jnp.where(l > 0.0, l, 1.0)
    o_cmp = _dot(vct_ref[0], p.astype(BF16))
    psum = sum(p[:, h * tq:(h + 1) * tq] for h in range(NSA_HEADS))
    imp = sum(_dot(covert_ref[...], piece) for piece in _split3_bf16(psum))

    nb = covert_ref.shape[0]
    jb = lax.broadcasted_iota(jnp.int32, (nb, tq), 0)
    cur = (q0 + lax.broadcasted_iota(jnp.int32, (nb, tq), 1)) // NSA_SEL_BLOCK
    forced = (jb == 0) | (jb == cur) | (jb == cur - 1)
    visible = jb <= cur
    score = jnp.where(visible, imp + jnp.where(forced, NSA_FORCED_SCORE, 0.0), -jnp.inf)
    beaten = jnp.zeros((nb, tq), F32)
    for j in range(n_blk):
        row = score[j:j + 1, :]
        beaten = beaten + jnp.where((row > score) | ((row == score) & (jb > j)), 1.0, 0.0)
    keep = (beaten < float(NSA_TOP_N)) & visible
    sel = jnp.concatenate([jnp.where(keep, 0.0, MASK_VALUE), jnp.full((128 - nb, tq), MASK_VALUE, F32)], axis=0).astype(BF16)

    qh = [q_ref[0, :, h * 128:(h + 1) * 128] for h in range(NSA_HEADS)]
    kio = lax.broadcasted_iota(jnp.int32, (tk, tq), 0)
    qpos = q0 + lax.broadcasted_iota(jnp.int32, (tk, tq), 1)
    init = tuple(_softmax_init(tq, NSA_HEAD_DIM) for _ in qh)

    def tile(t, carries, causal, window):
        k0 = pl.multiple_of(t * tk, tk)
        k = ks_ref[0, pl.ds(k0, tk), :]
        bias = _dot(et_ref[pl.ds(k0, tk), :], sel)
        scores = [_dot_nt(k, q) + bias for q in qh]
        vts = [vst_ref[0, :, pl.ds(k0, tk)]] * NSA_HEADS
        masks = [(k0 + kio <= qpos) if causal else None] * NSA_HEADS
        if window:
            k = kw_ref[0, pl.ds(k0, tk), :]
            dist = qpos - (k0 + kio)
            scores += [_dot_nt(k, q) for q in qh]
            vts += [vwt_ref[0, :, pl.ds(k0, tk)]] * NSA_HEADS
            masks += [(dist >= 0) & (dist < NSA_WINDOW)] * NSA_HEADS
        return _online_softmax_steps(carries, scores, vts, masks)

    t_diag = q0 // tk
    t_lo = jnp.maximum(q0 - NSA_WINDOW + 1, 0) // tk
    o_slc = lax.fori_loop(0, t_lo, lambda t, c: tile(t, c, False, False), init)
    both = lax.fori_loop(t_lo, t_diag, lambda t, c: tile(t, c, False, True), o_slc + init)
    both = tile(t_diag, both, True, True)
    o_slc, o_win = both[:NSA_HEADS], both[NSA_HEADS:]

    g = jax.nn.sigmoid(g_ref[0].astype(F32).T)
    for h in range(NSA_HEADS):
        o = (g[3 * h:3 * h + 1] * o_cmp[:, h * tq:(h + 1) * tq] + g[3 * h + 1:3 * h + 2] * _softmax_finish(o_slc[h])
             + g[3 * h + 2:3 * h + 3] * _softmax_finish(o_win[h]))
        o_ref[0, :, h * 128:(h + 1) * 128] = o.T.astype(o_ref.dtype)


def nsa_mixer(p, col0, pos_emb, w1, w2, cos, sin, *, tm=512, tq=128, tk=256):
    b, s, _ = p.shape
    assert col0 % 2304 == 0
    tm, tk = min(tm, s), min(tk, s)
    hw = NSA_HEADS * NSA_HEAD_DIM
    scale = NSA_HEAD_DIM ** -0.5
    row = lambda w: pl.BlockSpec((1, tm, w), lambda bi, i: (bi, i, 0))
    tab = pl.BlockSpec((tm, 128), lambda bi, i: (i, 0))
    tr_spec = pl.BlockSpec((1, 128, tm), lambda bi, i: (bi, 0, i))
    q, kc_in, vc_in, ks, kw, vst, vwt = pl.pallas_call(
        functools.partial(_nsa_prep_kernel, scale=scale),
        grid=(b, s // tm),
        in_specs=[pl.BlockSpec((1, tm, 2304), lambda bi, i: (bi, i, col0 // 2304)), tab, tab],
        out_specs=[row(hw), row(128), row(128), row(128), row(128), tr_spec, tr_spec],
        out_shape=[jax.ShapeDtypeStruct((b, s, hw), BF16), jax.ShapeDtypeStruct((b, s, 128), F32),
                   jax.ShapeDtypeStruct((b, s, 128), F32), jax.ShapeDtypeStruct((b, s, 128), BF16),
                   jax.ShapeDtypeStruct((b, s, 128), BF16), jax.ShapeDtypeStruct((b, 128, s), BF16),
                   jax.ShapeDtypeStruct((b, 128, s), BF16)],
        compiler_params=_params("parallel", "parallel"),
        name="nsa_prep",
    )(p, cos, sin)

    nrow = s // NSA_CMP_STRIDE
    fw = NSA_CMP_STRIDE * NSA_HEAD_DIM
    half = w1.shape[1] // 2
    w1r = jnp.concatenate([w1[:, :half], w1[:, half:]], axis=2)
    pos2 = jnp.pad(pos_emb.reshape(2, 2, fw), ((0, 0), (0, 6), (0, 0)))
    per_b = lambda shape: pl.BlockSpec((1,) + shape, lambda bi: (bi, 0, 0))
    const3 = lambda shape: pl.BlockSpec(shape, lambda bi: (0, 0, 0))
    kc3, vct = pl.pallas_call(
        _nsa_compress_kernel,
        grid=(b,),
        in_specs=[per_b((nrow, fw)), per_b((nrow, fw)), const3(pos2.shape), const3(w1r.shape), const3(w2.shape)],
        out_specs=[pl.BlockSpec((1, 3, nrow, NSA_HEAD_DIM), lambda bi: (bi, 0, 0, 0)), per_b((NSA_HEAD_DIM, nrow))],
        out_shape=[jax.ShapeDtypeStruct((b, 3, nrow, NSA_HEAD_DIM), BF16), jax.ShapeDtypeStruct((b, NSA_HEAD_DIM, nrow), BF16)],
        compiler_params=_params("parallel"),
        name="nsa_compress",
    )(kc_in.reshape(b, nrow, fw), vc_in.reshape(b, nrow, fw), pos2, w1r, w2)

    n_blk = s // NSA_SEL_BLOCK
    nb = 64
    assert n_blk <= nb and tq == 128 and tk % tq == 0
    cmp_start = np.arange(nrow) * NSA_CMP_STRIDE
    cmp_end = cmp_start + NSA_CMP_LEN - 1
    blk_start = np.arange(nb) * NSA_SEL_BLOCK
    cover_t = ((cmp_start[None, :] <= blk_start[:, None] + NSA_SEL_BLOCK - 1)
               & (cmp_end[None, :] >= blk_start[:, None]) & (np.arange(nb)[:, None] < n_blk)
               & (np.arange(nrow)[None, :] < nrow - 1)).astype(np.float32)
    expand_t = ((np.arange(s)[:, None] // NSA_SEL_BLOCK) == np.arange(128)[None, :]).astype(np.float32)
    qrow = lambda w: pl.BlockSpec((1, tq, w), lambda bi, i: (bi, i, 0))
    k_spec = pl.BlockSpec((1, s, 128), lambda bi, i: (bi, 0, 0))
    vt_spec = pl.BlockSpec((1, 128, s), lambda bi, i: (bi, 0, 0))
    gate_block = (col0 + 2 * hw) // 128 + 9
    return pl.pallas_call(
        functools.partial(_nsa_attn_kernel, tq=tq, tk=tk, n_blk=n_blk),
        grid=(b, s // tq),
        in_specs=[qrow(hw), pl.BlockSpec((1, 3, nrow, NSA_HEAD_DIM), lambda bi, i: (bi, 0, 0, 0)),
                  pl.BlockSpec((1, NSA_HEAD_DIM, nrow), lambda bi, i: (bi, 0, 0)),
                  pl.BlockSpec((nb, nrow), lambda bi, i: (0, 0)),
                  k_spec, vt_spec, k_spec, vt_spec, pl.BlockSpec((s, 128), lambda bi, i: (0, 0)),
                  pl.BlockSpec((1, tq, 128), lambda bi, i: (bi, i, gate_block))],
        out_specs=qrow(hw),
        out_shape=jax.ShapeDtypeStruct((b, s, hw), BF16),
        compiler_params=_params("parallel", "arbitrary"),
        name="nsa_attn",
    )(q, kc3, vct, jnp.asarray(cover_t, dtype=BF16), ks, vst, kw, vwt, jnp.asarray(expand_t, dtype=BF16), p)


def _rwkv_prep_kernel(z_ref, zp_ref, mu_ref, w0_ref, a0_ref, kk_ref, ka_ref, rk_ref, w2_ref, a2_ref, g2_ref, bd_ref,
                      csum_ref, kt_ref, rt_ref, kh_ref, bh_ref, kbar_ref, bbar_ref, v_ref, etot_ref, g_ref, bonus_ref):
    i = pl.program_id(1)
    z = z_ref[0]
    tm = z.shape[0]
    prev_row = jnp.where(i > 0, zp_ref[0, 7:8, :], 0.0)
    rowid = lax.broadcasted_iota(jnp.int32, z.shape, 0)
    z_prev = jnp.where(rowid == 0, prev_row, pltpu.roll(z, 1, axis=0))
    z = z + (z_prev - z) * mu_ref[...]
    c = BRANCH_WIDTH
    r, k, v = z[:, :c], z[:, c:2 * c], z[:, 2 * c:3 * c]
    wd, ad, gd = z[:, 3 * c:3 * c + 128], z[:, 3 * c + 128:3 * c + 256], z[:, 3 * c + 256:3 * c + 512]
    wpre = -(w0_ref[...] + _dot(jnp.tanh(wd), w2_ref[...], HIGHEST))
    w_log = -(jnp.maximum(wpre, 0.0) + jnp.log1p(jnp.exp(-jnp.abs(wpre)))) - 0.5
    lw = -jnp.exp(w_log)
    a = jax.nn.sigmoid(a0_ref[...] + _dot(ad, a2_ref[...], HIGHEST))
    g_ref[0] = _dot(jax.nn.sigmoid(gd), g2_ref[...], HIGHEST)
    kk = k * kk_ref[...]
    head_sum = lambda t: sum(_dot(piece, bd_ref[...]) for piece in _split3_bf16(t))
    kk = kk * lax.rsqrt(jnp.maximum(head_sum(kk * kk), 1e-24))
    k = k * (1.0 + (a - 1.0) * ka_ref[...])
    bonus_ref[0] = head_sum(r * k * rk_ref[...]) * v
    sums = sum(_dot(csum_ref[...], piece) for piece in _split3_bf16(lw))
    cum, total = sums[:tm], sums[tm:]
    dec_out, dec_end = jnp.exp(-cum), jnp.exp(total - cum)
    b = kk * a
    kt_ref[0] = (kk * jnp.exp(cum - lw)).astype(BF16)
    rt_ref[0] = (r * jnp.exp(cum)).astype(BF16)
    kh_ref[0] = (k * dec_out).astype(BF16)
    bh_ref[0] = (b * dec_out).astype(BF16)
    kbar_ref[0] = (k * dec_end).astype(BF16)
    bbar_ref[0] = (b * dec_end).astype(BF16)
    v_ref[0] = v.astype(BF16)
    etot_ref[0] = jnp.exp(total)


def _rwkv_scan_kernel(kt_ref, rt_ref, kh_ref, bh_ref, kbar_ref, bbar_ref, v_ref, etot_ref, g_ref, bonus_ref,
                      gw_ref, gb_ref, y_ref, state_ref, *, L):
    n_chunks = kt_ref.shape[1] // L
    N = RWKV_HEAD_DIM
    n_pairs = kt_ref.shape[2] // 128

    @pl.when(pl.program_id(1) == 0)
    def _():
        state_ref[...] = jnp.zeros_like(state_ref)

    ri = lax.broadcasted_iota(jnp.int32, (L, 2 * L), 0)
    ci = lax.broadcasted_iota(jnp.int32, (L, 2 * L), 1) & (L - 1)
    strict2, incl2 = ci < ri, ci <= ri
    eye = jnp.where(lax.broadcasted_iota(jnp.int32, (L, L), 0) == lax.broadcasted_iota(jnp.int32, (L, L), 1), 1.0, 0.0)
    first2 = lax.broadcasted_iota(jnp.int32, (2 * L, 128), 1) < N
    first = lax.broadcasted_iota(jnp.int32, (L, 128), 1) < N
    bd = (lax.broadcasted_iota(jnp.int32, (128, 128), 0) < N) == (lax.broadcasted_iota(jnp.int32, (128, 128), 1) < N)
    zero = jnp.zeros((), BF16)

    xs_in, a_k, a_r = [], [], []
    for c in range(n_chunks):
        rs = slice(c * L, (c + 1) * L)
        for p in range(n_pairs):
            sl = slice(p * 128, (p + 1) * 128)
            x = jnp.concatenate([kt_ref[0, rs, sl], rt_ref[0, rs, sl]], axis=0)
            y = jnp.concatenate([kh_ref[0, rs, sl], bh_ref[0, rs, sl]], axis=0)
            xs_in.append(x)
            for sub in range(2):
                gm = _dot_nt(jnp.where(first2 if sub == 0 else ~first2, x, zero), y)
                a_k.append(jnp.where(strict2, gm[:L], 0.0))
                a_r.append(jnp.where(incl2, gm[L:], 0.0))
    a_k, a_r = jnp.stack(a_k), jnp.stack(a_r)
    a_kb = a_k[:, :, L:]
    bmm = lambda a, b: jnp.einsum("hij,hjk->hik", a.astype(BF16), b.astype(BF16), preferred_element_type=F32)
    tinv = eye[None] - a_kb
    pw = a_kb
    for _ in range(int(np.log2(L)) - 1):
        pw = bmm(pw, pw)
        tinv = tinv + bmm(tinv, pw)

    pick = lambda t: jnp.where(first, t[:L], t[L:])
    hsum = lambda t: jnp.where(first, jnp.sum(jnp.where(first, t, 0.0), axis=-1, keepdims=True),
                               jnp.sum(jnp.where(first, 0.0, t), axis=-1, keepdims=True))
    states = [state_ref[p] for p in range(n_pairs)]
    for c in range(n_chunks):
        rs = slice(c * L, (c + 1) * L)
        for p in range(n_pairs):
            sl = slice(p * 128, (p + 1) * 128)
            h0 = (c * n_pairs + p) * 2
            h1 = h0 + 1
            st = states[p]
            xs = _dot_nt(xs_in[c * n_pairs + p], st.astype(BF16))
            v = v_ref[0, rs, sl]
            akkv = pick(_dot(jnp.concatenate([a_k[h0, :, :L], a_k[h1, :, :L]], axis=0).astype(BF16), v))
            rhs = -(xs[:L] + akkv)
            u = pick(_dot(jnp.concatenate([tinv[h0], tinv[h1]], axis=0).astype(BF16), rhs.astype(BF16)))
            vu = jnp.concatenate([v, u.astype(BF16)], axis=0)
            o = xs[L:] + pick(_dot(jnp.concatenate([a_r[h0], a_r[h1]], axis=0).astype(BF16), vu))
            kb = jnp.concatenate([kbar_ref[0, rs, sl], bbar_ref[0, rs, sl]], axis=0)
            states[p] = st * etot_ref[0, c * L:c * L + 1, sl] + jnp.where(bd, _dot_tn(vu, kb), 0.0)
            ctr = o - hsum(o) * (1.0 / N)
            on = ctr * lax.rsqrt(hsum(ctr * ctr) * (1.0 / N) + RWKV_GN_EPS)
            y_ref[0, rs, sl] = ((on * gw_ref[:, sl] + gb_ref[:, sl] + bonus_ref[0, rs, sl]) * g_ref[0, rs, sl]).astype(y_ref.dtype)
    for p in range(n_pairs):
        state_ref[p] = states[p]


def rwkv7_mixer(z, mu, w0, w2, a0, a2, g2, k_k, k_a, r_k, gn_w, gn_b, *, tm=256):
    b, s, zw = z.shape
    tm = min(tm, s)
    c, hd, nh = BRANCH_WIDTH, RWKV_HEAD_DIM, RWKV_HEADS
    L = min(RWKV_CHUNK, s)
    assert tm % L == 0 and L & (L - 1) == 0
    bd = jnp.asarray(np.kron(np.eye(nh, dtype=np.float32), np.ones((hd, hd), np.float32)), dtype=BF16)
    chunk = np.arange(tm) // L
    same = chunk[:, None] == chunk[None, :]
    csum = np.concatenate([same & (np.arange(tm)[None, :] <= np.arange(tm)[:, None]), same], axis=0)
    csum = jnp.asarray(csum.astype(np.float32), dtype=BF16)
    row = lambda w: pl.BlockSpec((1, tm, w), lambda bi, i: (bi, i, 0))
    vec = lambda w: pl.BlockSpec((1, w), lambda bi, i: (0, 0))
    mat = lambda shape: pl.BlockSpec(shape, lambda bi, i: (0, 0))
    outs = pl.pallas_call(
        _rwkv_prep_kernel,
        grid=(b, s // tm),
        in_specs=[row(zw), pl.BlockSpec((1, 8, zw), lambda bi, i: (bi, jnp.maximum(i * (tm // 8) - 1, 0), 0)),
                  vec(zw), vec(c), vec(c), vec(c), vec(c), vec(c), mat(w2.shape), mat(a2.shape), mat(g2.shape),
                  mat(bd.shape), mat(csum.shape)],
        out_specs=[row(c)] * 10,
        out_shape=[jax.ShapeDtypeStruct((b, s, c), BF16)] * 7 + [jax.ShapeDtypeStruct((b, s, c), F32)] * 3,
        compiler_params=_params("parallel", "parallel"),
        name="rwkv_prep",
    )(z, z, mu.reshape(1, zw), w0.reshape(1, c), a0.reshape(1, c), k_k.reshape(1, c), k_a.reshape(1, c),
      r_k.reshape(1, c), w2, a2, g2, bd, csum)
    rows = min(RWKV_CHUNKS_PER_STEP * L, s)
    blk = pl.BlockSpec((1, rows, c), lambda bi, ci: (bi, ci, 0))
    gvec = pl.BlockSpec((1, c), lambda bi, ci: (0, 0))
    return pl.pallas_call(
        functools.partial(_rwkv_scan_kernel, L=L),
        grid=(b, s // rows),
        in_specs=[blk] * 10 + [gvec, gvec],
        out_specs=blk,
        out_shape=jax.ShapeDtypeStruct((b, s, c), BF16),
        scratch_shapes=[pltpu.VMEM((c // 128, 128, 128), F32)],
        compiler_params=_params("parallel", "arbitrary"),
        name="rwkv_scan",
    )(*outs, gn_w.reshape(1, c), gn_b.reshape(1, c))


def _retention_kernel(q_ref, k_ref, v_ref, g_ref, cos_ref, sin_ref, inner_ref, qd_ref, kd_ref, cd_ref, o_ref, state_ref):
    @pl.when(pl.program_id(1) == 0)
    def _():
        state_ref[...] = jnp.zeros_like(state_ref)

    C = q_ref.shape[1]
    kw = RET_HEADS * RET_K_DIM
    cos, sin = cos_ref[...], sin_ref[...]
    q = q_ref[0, :, :kw].astype(F32) * cos + q_ref[0, :, kw:].astype(F32) * sin
    k = (k_ref[0, :, :kw].astype(F32) * cos + k_ref[0, :, kw:].astype(F32) * sin) * RET_K_DIM ** -0.5
    lane = lax.broadcasted_iota(jnp.int32, (C, 128), 1)
    for h in range(RET_HEADS):
        pair, sub = h // 2, h % 2
        in_head = (lane // RET_K_DIM) == sub
        qh = jnp.where(in_head, q[:, pair * 128:(pair + 1) * 128], 0.0)
        kh = jnp.where(in_head, k[:, pair * 128:(pair + 1) * 128], 0.0)
        vh = v_ref[0, :, h * RET_V_DIM:(h + 1) * RET_V_DIM]
        gate = g_ref[0, :, h * RET_V_DIM:(h + 1) * RET_V_DIM].astype(F32)
        scores = _dot_nt(qh.astype(BF16), kh.astype(BF16)) * inner_ref[h]
        st = state_ref[h]
        o = _dot(scores.astype(BF16), vh) + _dot((qh * qd_ref[h]).astype(BF16), st.astype(BF16))
        state_ref[h] = st * cd_ref[h, 0:1, :] + _dot_tn((kh * kd_ref[h]).astype(BF16), vh)
        c = o - jnp.mean(o, axis=-1, keepdims=True)
        y = c * lax.rsqrt(jnp.mean(c * c, axis=-1, keepdims=True) + NORM_EPS)
        o_ref[0, :, h * RET_V_DIM:(h + 1) * RET_V_DIM] = (y * (gate * jax.nn.sigmoid(gate))).astype(o_ref.dtype)


def retention_mixer(p, col0, cos, sin):
    b, s, _ = p.shape
    assert col0 % 512 == 0
    H, C = RET_HEADS, min(RET_CHUNK, s)
    log_gamma = jnp.log1p(-jnp.exp2(-5.0 - jnp.arange(H, dtype=F32)))
    n = jnp.arange(C, dtype=F32)
    dist = n[:, None] - n[None, :]
    inner = jnp.where(dist >= 0, jnp.exp(jnp.maximum(dist, 0.0) * log_gamma[:, None, None]), 0.0)
    q_decay = jnp.exp((n + 1.0) * log_gamma[:, None])
    k_decay = jnp.exp((C - 1.0 - n) * log_gamma[:, None])
    chunk_decay = jnp.exp(C * log_gamma)
    lanes = lambda t: jnp.broadcast_to(t[:, :, None], (H, C, 128))
    const = lambda shape: pl.BlockSpec(shape, lambda bi, ci: (0,) * len(shape))
    kw = RET_HEADS * RET_K_DIM
    return pl.pallas_call(
        _retention_kernel,
        grid=(b, s // C),
        in_specs=[pl.BlockSpec((1, C, 512), functools.partial(lambda bi, ci, j: (bi, ci, j), j=col0 // 512 + j))
                  for j in range(4)] + [
                  pl.BlockSpec((C, kw), lambda bi, ci: (ci, 0)), pl.BlockSpec((C, kw), lambda bi, ci: (ci, 0)),
                  const((H, C, C)), const((H, C, 128)), const((H, C, 128)), const((H, 8, 128))],
        out_specs=pl.BlockSpec((1, C, H * RET_V_DIM), lambda bi, ci: (bi, ci, 0)),
        out_shape=jax.ShapeDtypeStruct((b, s, H * RET_V_DIM), BF16),
        scratch_shapes=[pltpu.VMEM((H, 128, 128), F32)],
        compiler_params=_params("parallel", "arbitrary"),
        name="retention",
    )(p, p, p, p, cos, sin, inner, lanes(q_decay), lanes(k_decay), jnp.broadcast_to(chunk_decay[:, None, None], (H, 8, 128)))


def _rot_half_cols(w, half):
    return jnp.concatenate([-w[..., half:2 * half], w[..., :half]], axis=-1)


def _pad_cols(w, width):
    return jnp.pad(w, [(0, 0)] * (w.ndim - 1) + [(0, width - w.shape[-1])])


def _rope_tables(s, inv_freq, width, fill_cos):
    inv_freq = np.asarray(inv_freq, np.float32)
    ang = np.arange(s, dtype=np.float32)[:, None] * inv_freq[None, :]
    cos, sin = np.cos(ang).astype(np.float32), np.sin(ang).astype(np.float32)
    rot = 2 * cos.shape[1]
    cos2 = np.concatenate([cos, cos, np.full((s, width - rot), fill_cos, np.float32)], axis=1)
    sin2 = np.concatenate([sin, sin, np.zeros((s, width - rot), np.float32)], axis=1)
    return cos2, sin2


def _inv_freq(rot_dim, theta):
    return np.float32(theta) ** (-np.arange(0, rot_dim, 2, dtype=np.float32) / np.float32(rot_dim))


def _split_offsets():
    sizes = (MLA_Q_RANK, MLA_KV_RANK, MLA_ROPE_DIM, 512, 128, 128, 128, 128, 128, 128, 12,
             3 * BRANCH_WIDTH + RWKV_DECAY_LORA + RWKV_A_LORA + RWKV_GATE_LORA, 256, 256, 512, 512, N_MIXERS * D_MODEL)
    offs = np.concatenate([[0], np.cumsum(sizes)])
    return [(int(offs[i]), int(offs[i + 1])) for i in range(len(sizes))]


def _pack_in_proj(w):
    sl = [w[:, a:b] for a, b in _split_offsets()]
    (ql, kvl, kr, nq, nkc, nvc, nks, nvs, nkw, nvw, ng, rz, rq, rk, rv, rg, mg) = sl
    w_mla = jnp.concatenate([ql, kvl, _pad_cols(kr, 128), _pad_cols(_rot_half_cols(kr, 32), 128)], axis=1)
    nh = NSA_ROT_DIM // 2
    rot128 = lambda t: _pad_cols(_rot_half_cols(t[:, :NSA_ROT_DIM], nh), 128)
    nq_rot = jnp.concatenate([rot128(nq[:, h * 128:(h + 1) * 128]) for h in range(NSA_HEADS)], axis=1)
    w_nsa = jnp.concatenate([nq, nq_rot, nkc, rot128(nkc), nvc, nks, rot128(nks), nvs, nkw, rot128(nkw), nvw,
                             _pad_cols(ng, 128)], axis=1)
    rh = RET_K_DIM // 2
    ret_rot = lambda t: jnp.concatenate([_rot_half_cols(t[:, h * 64:(h + 1) * 64], rh) for h in range(RET_HEADS)], axis=1)
    w_ret = jnp.concatenate([rq, ret_rot(rq), rk, ret_rot(rk), rv, rg], axis=1)
    c = BRANCH_WIDTH
    w_rwkv = jnp.concatenate([rz[:, :3 * c], _pad_cols(rz[:, 3 * c:3 * c + 96], 128),
                              _pad_cols(rz[:, 3 * c + 96:3 * c + 192], 128), rz[:, 3 * c + 192:]], axis=1)
    bf = lambda t: t.astype(BF16)
    return bf(w_mla), bf(w_nsa), bf(w_ret), bf(w_rwkv), bf(mg)


def _pack_mla_weights(w_uq, w_ukv):
    H, dn, dr, dv = MLA_HEADS, MLA_NOPE_DIM, MLA_ROPE_DIM, MLA_V_DIM
    uq = w_uq.reshape(-1, H, dn + dr)
    nope = uq[:, :, :dn].reshape(-1, H * dn)
    rope = uq[:, :, dn:]
    rope_p = _pad_cols(rope, 128).reshape(-1, H * 128)
    rope_r = _pad_cols(_rot_half_cols(rope, dr // 2), 128).reshape(-1, H * 128)
    ukv = w_ukv.reshape(-1, H, dn + dv)
    wkv = jnp.concatenate([ukv[:, :, :dn].reshape(-1, H * dn), ukv[:, :, dn:].reshape(-1, H * dv)], axis=1)
    return jnp.concatenate([nope, rope_p, rope_r], axis=1).astype(BF16), wkv.astype(BF16)


def _pad_rows(w, rows):
    return jnp.pad(w, ((0, rows - w.shape[0]), (0, 0)))


def kernel(x, w_in, w_branch, w_out, w_up, w_down, norm_gains, mla_g_q, mla_g_kv, mla_w_uq, mla_w_ukv,
           nsa_cmp_pos, nsa_cmp_w1, nsa_cmp_w2, rwkv_mu, rwkv_w0, rwkv_w2, rwkv_a0, rwkv_a2, rwkv_g2,
           rwkv_k_k, rwkv_k_a, rwkv_r_k, rwkv_gn_w, rwkv_gn_b):
    B, S, D = x.shape
    depth = w_in.shape[0]
    T = B * S
    tm, tm_in = min(512, T), min(1024, T)
    mla_cos, mla_sin = _rope_tables(S, _inv_freq(MLA_ROPE_DIM, ROPE_THETA), 128, 0.0)
    nsa_cos, nsa_sin = _rope_tables(S, _inv_freq(NSA_ROT_DIM, ROPE_THETA), 128, 1.0)
    ret_cos, ret_sin = _rope_tables(S, np.float32(RET_THETA) ** (-np.linspace(0.0, 1.0, RET_K_DIM // 2, dtype=np.float32)),
                                    RET_K_DIM, 0.0)
    ret_cos, ret_sin = np.tile(ret_cos, (1, RET_HEADS)), np.tile(ret_sin, (1, RET_HEADS))
    c = BRANCH_WIDTH
    x = x.reshape(T, D)
    for l in range(depth):
        w_mla, w_nsa, w_ret, w_rwkv, w_gate = _pack_in_proj(w_in[l])
        g_pre = norm_gains[l, 0]
        groups = (w_nsa, w_mla, w_gate, w_ret)
        col_nsa, col_mla, col_gate, col_ret = np.cumsum([0] + [w.shape[1] for w in groups[:-1]]).tolist()
        p2 = norm_matmul(x, g_pre, jnp.concatenate(groups, axis=1), BF16, tm_in, 1024)
        p = p2.reshape(B, S, -1)
        p_rwkv = norm_matmul(x, g_pre, w_rwkv, F32, tm_in, 1024)

        wq, wkv = _pack_mla_weights(mla_w_uq[l], mla_w_ukv[l])
        y_mla = mla_mixer(p, col_mla, mla_g_q[l], mla_g_kv[l], wq, wkv, mla_cos, mla_sin)
        y_nsa = nsa_mixer(p, col_nsa, nsa_cmp_pos[l], nsa_cmp_w1[l], nsa_cmp_w2[l], nsa_cos, nsa_sin)
        mu = rwkv_mu[l]
        mu_p = jnp.concatenate([mu[:3 * c], _pad_cols(mu[3 * c:3 * c + 96], 128), _pad_cols(mu[3 * c + 96:3 * c + 192], 128),
                                mu[3 * c + 192:]])
        y_rwkv = rwkv7_mixer(p_rwkv.reshape(B, S, -1), mu_p, rwkv_w0[l], _pad_rows(rwkv_w2[l], 128), rwkv_a0[l],
                             _pad_rows(rwkv_a2[l], 128), rwkv_g2[l], rwkv_k_k[l], rwkv_k_a[l], rwkv_r_k[l].reshape(-1),
                             rwkv_gn_w[l], rwkv_gn_b[l])
        y_ret = retention_mixer(p, col_ret, ret_cos, ret_sin)

        ys = [y.reshape(T, c) for y in (y_mla, y_nsa, y_rwkv, y_ret)]
        merged = merge_branches(ys, p2, col_gate, w_branch[l].astype(BF16), tm, 512)
        x = matmul_norm_residual(merged, w_out[l].astype(BF16), norm_gains[l, 1], x, tm)
        x = ffn_block(x, norm_gains[l, 2], w_up[l].astype(BF16), w_down[l].astype(BF16), norm_gains[l, 3], tm, 1024)
    return x.reshape(B, S, D)
```

```python
import functools

import numpy as np
import jax
import jax.numpy as jnp
from jax import lax
from jax.experimental import pallas as pl
from jax.experimental.pallas import tpu as pltpu

F32 = jnp.float32
BF16 = jnp.bfloat16
HIGHEST = lax.Precision.HIGHEST

V7X_LANES = 128
V7X_VMEM_LIMIT_BYTES = 56 * 1024 * 1024

D_MODEL = 2048
N_MIXERS = 4
BRANCH_WIDTH = D_MODEL // N_MIXERS
D_FF = 4 * D_MODEL
ROPE_THETA = 500000.0
NORM_EPS = 1e-6
MASK_VALUE = -1e30

MLA_NOPE_DIM = 128
MLA_ROPE_DIM = 64
MLA_V_DIM = 128
MLA_HEADS = 4
MLA_Q_RANK = 384
MLA_KV_RANK = 128

NSA_HEAD_DIM = 128
NSA_HEADS = 4
NSA_ROT_DIM = 32
NSA_CMP_LEN = 32
NSA_CMP_STRIDE = 16
NSA_SEL_BLOCK = 64
NSA_TOP_N = 16
NSA_WINDOW = 512
NSA_FORCED_SCORE = 1000.0

RWKV_HEAD_DIM = 64
RWKV_HEADS = 8
RWKV_DECAY_LORA = 96
RWKV_A_LORA = 96
RWKV_GATE_LORA = 256
RWKV_GN_EPS = 64e-5
RWKV_CHUNK = 64
RWKV_CHUNKS_PER_STEP = 4

RET_HEADS = 4
RET_V_DIM = 128
RET_K_DIM = 64
RET_CHUNK = 128
RET_THETA = 10000.0


def _params(*sem):
    return pltpu.CompilerParams(dimension_semantics=sem, vmem_limit_bytes=V7X_VMEM_LIMIT_BYTES)


def _dot(a, b, precision=None):
    return jnp.dot(a, b, preferred_element_type=F32, precision=precision)


def _dot_nt(a, b, precision=None):
    return lax.dot_general(a, b, (((1,), (1,)), ((), ())), preferred_element_type=F32, precision=precision)


def _dot_tn(a, b, precision=None):
    return lax.dot_general(a, b, (((0,), (0,)), ((), ())), preferred_element_type=F32, precision=precision)


def _rms(x, g, eps=NORM_EPS):
    return x * lax.rsqrt(jnp.mean(x * x, axis=-1, keepdims=True) + eps) * g


def _norm_matmul_kernel(x_ref, g_ref, w_ref, o_ref, h_ref):
    @pl.when(pl.program_id(1) == 0)
    def _():
        h_ref[...] = _rms(x_ref[...], g_ref[...]).astype(BF16)

    o_ref[...] = _dot(h_ref[...], w_ref[...]).astype(o_ref.dtype)


def norm_matmul(x, g, w, out_dtype, tm, tn):
    m, k = x.shape
    n = w.shape[1]
    assert m % tm == 0 and n % tn == 0
    return pl.pallas_call(
        _norm_matmul_kernel,
        grid=(m // tm, n // tn),
        in_specs=[pl.BlockSpec((tm, k), lambda i, j: (i, 0)),
                  pl.BlockSpec((1, k), lambda i, j: (0, 0)),
                  pl.BlockSpec((k, tn), lambda i, j: (0, j))],
        out_specs=pl.BlockSpec((tm, tn), lambda i, j: (i, j)),
        out_shape=jax.ShapeDtypeStruct((m, n), out_dtype),
        scratch_shapes=[pltpu.VMEM((tm, k), BF16)],
        compiler_params=_params("parallel", "arbitrary"),
        name="norm_matmul",
    )(x, g.reshape(1, k), w)


def _merge_kernel(y0_ref, y1_ref, y2_ref, y3_ref, g0_ref, g1_ref, g2_ref, g3_ref, wb_ref, o_ref):
    acc = None
    for m, (y_ref, g_ref) in enumerate(((y0_ref, g0_ref), (y1_ref, g1_ref), (y2_ref, g2_ref), (y3_ref, g3_ref))):
        gate = 0.5 * jnp.tanh(0.5 * g_ref[...].astype(F32)) + 0.5
        term = gate * _dot(y_ref[...], wb_ref[m])
        acc = term if acc is None else acc + term
    o_ref[...] = acc.astype(o_ref.dtype)


def merge_branches(ys, gates, col0, wb, tm, tn):
    m, c = ys[0].shape
    d = wb.shape[2]
    nj = d // tn
    assert col0 % tn == 0
    y_spec = pl.BlockSpec((tm, c), lambda i, j: (i, 0))
    gate_specs = [pl.BlockSpec((tm, tn), functools.partial(lambda i, j, mm: (i, col0 // tn + mm * nj + j), mm=mm))
                  for mm in range(N_MIXERS)]
    return pl.pallas_call(
        _merge_kernel,
        grid=(m // tm, nj),
        in_specs=[y_spec] * N_MIXERS + gate_specs + [pl.BlockSpec((N_MIXERS, c, tn), lambda i, j: (0, 0, j))],
        out_specs=pl.BlockSpec((tm, tn), lambda i, j: (i, j)),
        out_shape=jax.ShapeDtypeStruct((m, d), BF16),
        compiler_params=_params("parallel", "arbitrary"),
        name="merge_branches",
    )(*ys, gates, gates, gates, gates, wb)


def _merge_out_kernel(*refs):
    y_refs, g_refs = refs[:N_MIXERS], refs[N_MIXERS:3 * N_MIXERS]
    wb_ref, wo_ref, gn_ref, x_ref, o_ref = refs[3 * N_MIXERS:]
    half = g_refs[0].shape[1]
    merged = []
    for hf in range(2):
        acc = None
        for m in range(N_MIXERS):
            gate = 0.5 * jnp.tanh(0.5 * g_refs[2 * m + hf][...].astype(F32)) + 0.5
            term = gate * _dot(y_refs[m][...], wb_ref[m, :, hf * half:(hf + 1) * half])
            acc = term if acc is None else acc + term
        merged.append(acc.astype(BF16))
    o_ref[...] = x_ref[...] + _rms(_dot(jnp.concatenate(merged, axis=1), wo_ref[...]), gn_ref[...])


def merge_out_residual(ys, gates, col0, wb, wo, gn, x, tm):
    m, c = ys[0].shape
    d = wb.shape[2]
    half = d // 2
    assert col0 % half == 0
    once = pl.Buffered(1)
    return pl.pallas_call(
        _merge_out_kernel,
        grid=(m // tm,),
        in_specs=[pl.BlockSpec((tm, c), lambda i: (i, 0))] * N_MIXERS
        + [pl.BlockSpec((tm, half), functools.partial(lambda i, j: (i, j), j=col0 // half + jj)) for jj in range(2 * N_MIXERS)]
        + [pl.BlockSpec((N_MIXERS, c, d), lambda i: (0, 0, 0), pipeline_mode=once),
           pl.BlockSpec((d, d), lambda i: (0, 0), pipeline_mode=once),
           pl.BlockSpec((1, d), lambda i: (0, 0)),
           pl.BlockSpec((tm, d), lambda i: (i, 0))],
        out_specs=pl.BlockSpec((tm, d), lambda i: (i, 0)),
        out_shape=jax.ShapeDtypeStruct((m, d), F32),
        compiler_params=_params("parallel"),
        name="merge_out_residual",
    )(*ys, *([gates] * (2 * N_MIXERS)), wb, wo, gn.reshape(1, d), x)


def _matmul_norm_res_kernel(a_ref, w_ref, g_ref, x_ref, o_ref):
    y = _dot(a_ref[...], w_ref[...])
    o_ref[...] = x_ref[...] + _rms(y, g_ref[...])


def matmul_norm_residual(a, w, g, x, tm):
    m, k = a.shape
    d = w.shape[1]
    return pl.pallas_call(
        _matmul_norm_res_kernel,
        grid=(m // tm,),
        in_specs=[pl.BlockSpec((tm, k), lambda i: (i, 0)),
                  pl.BlockSpec((k, d), lambda i: (0, 0)),
                  pl.BlockSpec((1, d), lambda i: (0, 0)),
                  pl.BlockSpec((tm, d), lambda i: (i, 0))],
        out_specs=pl.BlockSpec((tm, d), lambda i: (i, 0)),
        out_shape=jax.ShapeDtypeStruct((m, d), F32),
        compiler_params=_params("parallel"),
        name="matmul_norm_residual",
    )(a, w, g.reshape(1, d), x)


def _ffn_kernel(x_ref, g1_ref, wu_ref, wd_ref, g2_ref, o_ref, h_ref, acc_ref):
    f = pl.program_id(1)

    @pl.when(f == 0)
    def _():
        h_ref[...] = _rms(x_ref[...], g1_ref[...]).astype(BF16)
        acc_ref[...] = jnp.zeros_like(acc_ref)

    u = jnp.maximum(_dot(h_ref[...], wu_ref[...]), 0.0)
    acc_ref[...] += _dot((u * u).astype(BF16), wd_ref[...])

    @pl.when(f == pl.num_programs(1) - 1)
    def _():
        o_ref[...] = x_ref[...] + _rms(acc_ref[...], g2_ref[...])


def ffn_block(x, g1, wu, wd, g2, tm, tf):
    m, d = x.shape
    ff = wu.shape[1]
    return pl.pallas_call(
        _ffn_kernel,
        grid=(m // tm, ff // tf),
        in_specs=[pl.BlockSpec((tm, d), lambda i, f: (i, 0)),
                  pl.BlockSpec((1, d), lambda i, f: (0, 0)),
                  pl.BlockSpec((d, tf), lambda i, f: (0, f)),
                  pl.BlockSpec((tf, d), lambda i, f: (f, 0)),
                  pl.BlockSpec((1, d), lambda i, f: (0, 0))],
        out_specs=pl.BlockSpec((tm, d), lambda i, f: (i, 0)),
        out_shape=jax.ShapeDtypeStruct((m, d), F32),
        scratch_shapes=[pltpu.VMEM((tm, d), BF16), pltpu.VMEM((tm, d), F32)],
        compiler_params=_params("parallel", "arbitrary"),
        name="ffn_block",
    )(x, g1.reshape(1, d), wu, wd, g2.reshape(1, d))


def _online_softmax_steps(carries, scores, vt_tiles, masks):
    scores = [s if mask is None else jnp.where(mask, s, -jnp.inf) for s, mask in zip(scores, masks)]
    stats = []
    for (m_prev, l_prev, _), s in zip(carries, scores):
        m_new = jnp.maximum(m_prev, jnp.max(s, axis=0, keepdims=True))
        alpha = jnp.exp(m_prev - m_new)
        p = jnp.exp(s - m_new)
        stats.append((m_new, alpha, alpha * l_prev + jnp.sum(p, axis=0, keepdims=True), p.astype(BF16)))
    return tuple((m_new, l_new, alpha * acc_prev + _dot(vt, p))
                 for (m_new, alpha, l_new, p), (_, _, acc_prev), vt in zip(stats, carries, vt_tiles))


def _softmax_init(cols, dv):
    return (jnp.full((1, cols), MASK_VALUE, F32), jnp.zeros((1, cols), F32), jnp.zeros((dv, cols), F32))


def _softmax_finish(carry):
    _, l, acc = carry
    return acc / jnp.where(l > 0.0, l, 1.0)


def _mla_prep_kernel(p_ref, gq_ref, gkv_ref, wq_ref, wkv_ref, cos_ref, sin_ref,
                     qn_ref, qr_ref, kn_ref, vt_ref, kr_ref, *, scale):
    p = p_ref[0].astype(F32)
    nq = _rms(p[:, :MLA_Q_RANK], gq_ref[...]).astype(BF16)
    nkv = _rms(p[:, MLA_Q_RANK:MLA_Q_RANK + MLA_KV_RANK], gkv_ref[...]).astype(BF16)
    q = _dot(nq, wq_ref[...])
    kv = _dot(nkv, wkv_ref[...])
    cos, sin = cos_ref[...], sin_ref[...]
    hw = MLA_HEADS * MLA_NOPE_DIM
    qn_ref[0] = (q[:, :hw] * scale).astype(BF16)
    for h in range(MLA_HEADS):
        a = q[:, hw + h * 128: hw + (h + 1) * 128]
        b = q[:, 2 * hw + h * 128: 2 * hw + (h + 1) * 128]
        qr_ref[0, :, h * 128:(h + 1) * 128] = ((a * cos + b * sin) * scale).astype(BF16)
    kn_ref[0] = kv[:, :hw].astype(BF16)
    vt_ref[0] = kv[:, hw:].T.astype(BF16)
    kr_ref[0] = (p[:, 512:640] * cos + p[:, 640:768] * sin).astype(BF16)


def _mla_attn_kernel(qn_ref, qr_ref, kn_ref, kr_ref, vt_ref, o_ref, *, tq, tk):
    i = pl.program_id(1)
    hs = [slice(h * 128, (h + 1) * 128) for h in range(MLA_HEADS)]
    qn = [qn_ref[0, :, sl] for sl in hs]
    qr = [qr_ref[0, :, sl] for sl in hs]

    def tile(t, carries, mask):
        k0 = pl.multiple_of(t * tk, tk)
        kr = kr_ref[0, pl.ds(k0, tk), :]
        scores = [_dot_nt(kn_ref[0, pl.ds(k0, tk), sl], qn[h]) + _dot_nt(kr, qr[h]) for h, sl in enumerate(hs)]
        return _online_softmax_steps(carries, scores, [vt_ref[0, sl, pl.ds(k0, tk)] for sl in hs], [mask] * len(hs))

    init = tuple(_softmax_init(tq, MLA_V_DIM) for _ in hs)
    t_diag = (i * tq) // tk
    carries = lax.fori_loop(0, t_diag, lambda t, c: tile(t, c, None), init)
    causal = (t_diag * tk + lax.broadcasted_iota(jnp.int32, (tk, tq), 0)
              <= i * tq + lax.broadcasted_iota(jnp.int32, (tk, tq), 1))
    for sl, carry in zip(hs, tile(t_diag, carries, causal)):
        o_ref[0, :, sl] = _softmax_finish(carry).T.astype(o_ref.dtype)


def mla_mixer(p, col0, g_q, g_kv, wq, wkv, cos, sin, *, tm=512, tq=256, tk=512):
    b, s, _ = p.shape
    tm, tq, tk = min(tm, s), min(tq, s), min(tk, s)
    assert tk % tq == 0 and col0 % 768 == 0
    hw = MLA_HEADS * 128
    scale = (MLA_NOPE_DIM + MLA_ROPE_DIM) ** -0.5
    full = lambda shape: pl.BlockSpec(shape, lambda bi, i: (0,) * len(shape))
    row = lambda w: pl.BlockSpec((1, tm, w), lambda bi, i: (bi, i, 0))
    outs = pl.pallas_call(
        functools.partial(_mla_prep_kernel, scale=scale),
        grid=(b, s // tm),
        in_specs=[pl.BlockSpec((1, tm, 768), lambda bi, i: (bi, i, col0 // 768)),
                  full((1, MLA_Q_RANK)), full((1, MLA_KV_RANK)), full(wq.shape), full(wkv.shape),
                  pl.BlockSpec((tm, 128), lambda bi, i: (i, 0)), pl.BlockSpec((tm, 128), lambda bi, i: (i, 0))],
        out_specs=[row(hw), row(hw), row(hw), pl.BlockSpec((1, hw, tm), lambda bi, i: (bi, 0, i)), row(128)],
        out_shape=[jax.ShapeDtypeStruct((b, s, hw), BF16)] * 3 + [jax.ShapeDtypeStruct((b, hw, s), BF16),
                                                                   jax.ShapeDtypeStruct((b, s, 128), BF16)],
        compiler_params=_params("parallel", "parallel"),
        name="mla_prep",
    )(p, g_q.reshape(1, -1), g_kv.reshape(1, -1), wq, wkv, cos, sin)
    qn, qr, kn, vt, kr = outs
    q_spec = pl.BlockSpec((1, tq, hw), lambda bi, i: (bi, i, 0))
    per_batch = lambda shape: pl.BlockSpec((1,) + shape, lambda bi, i: (bi, 0, 0))
    return pl.pallas_call(
        functools.partial(_mla_attn_kernel, tq=tq, tk=tk),
        grid=(b, s // tq),
        in_specs=[q_spec, q_spec, per_batch((s, hw)), per_batch((s, 128)), per_batch((hw, s))],
        out_specs=q_spec,
        out_shape=jax.ShapeDtypeStruct((b, s, hw), BF16),
        compiler_params=_params("parallel", "arbitrary"),
        name="mla_attn",
    )(qn, qr, kn, kr, vt)


def _nsa_prep_kernel(p_ref, cos_ref, sin_ref, q_ref, kc_ref, vc_ref, ks_ref, kw_ref, vst_ref, vwt_ref, *, scale):
    cos, sin = cos_ref[...], sin_ref[...]
    hw = NSA_HEADS * NSA_HEAD_DIM
    for h in range(NSA_HEADS):
        a = p_ref[0, :, h * 128:(h + 1) * 128].astype(F32)
        b = p_ref[0, :, hw + h * 128: hw + (h + 1) * 128].astype(F32)
        q_ref[0, :, h * 128:(h + 1) * 128] = ((a * cos + b * sin) * scale).astype(BF16)
    base = 2 * hw

    def rot(col):
        a = p_ref[0, :, base + col * 128: base + (col + 1) * 128].astype(F32)
        b = p_ref[0, :, base + (col + 1) * 128: base + (col + 2) * 128].astype(F32)
        return a * cos + b * sin

    kc_ref[0] = rot(0)
    vc_ref[0] = p_ref[0, :, base + 256: base + 384].astype(F32)
    ks_ref[0] = rot(3).astype(BF16)
    kw_ref[0] = rot(6).astype(BF16)
    transposed = lambda col: p_ref[0, :, base + col * 128: base + (col + 1) * 128].astype(F32).T.astype(BF16)
    vst_ref[0] = transposed(5)
    vwt_ref[0] = transposed(8)


def _gelu_tanh(x):
    return 0.5 * x * (1.0 + jnp.tanh(np.sqrt(2.0 / np.pi).astype(np.float32) * (x + 0.044715 * (x * x * x))))


def _split3_bf16(x):
    hi = x.astype(BF16)
    r1 = x - hi.astype(F32)
    mid = r1.astype(BF16)
    return hi, mid, (r1 - mid.astype(F32)).astype(BF16)


def _nsa_compress_kernel(k_ref, v_ref, pos_ref, w1_ref, w2_ref, kc3_ref, vct_ref):
    nrow = k_ref.shape[1]
    row = lax.broadcasted_iota(jnp.int32, (nrow, NSA_HEAD_DIM), 0)
    outs = []
    for z, x_ref in enumerate((k_ref, v_ref)):
        res = _dot(x_ref[0], w1_ref[z], HIGHEST)
        pb = _dot(pos_ref[z], w1_ref[z], HIGHEST)
        bias = pb[0:1, :NSA_HEAD_DIM] + pb[1:2, NSA_HEAD_DIM:]
        nxt = pltpu.roll(res[:, NSA_HEAD_DIM:], nrow - 1, axis=0)
        hid = _gelu_tanh(res[:, :NSA_HEAD_DIM] + nxt + bias)
        outs.append(jnp.where(row < nrow - 1, _dot(hid, w2_ref[z], HIGHEST), 0.0))
    for z, piece in enumerate(_split3_bf16(outs[0])):
        kc3_ref[0, z] = piece
    vct_ref[0] = outs[1].T.astype(BF16)


def _nsa_attn_kernel(q_ref, kc3_ref, vct_ref, covert_ref, ks_ref, vst_ref, kw_ref, vwt_ref, et_ref, g_ref, o_ref,
                     *, tq, tk, n_blk):
    i = pl.program_id(1)
    q0 = i * tq
    cols = NSA_HEADS * tq
    q4 = jnp.concatenate([q_ref[0, :, h * 128:(h + 1) * 128] for h in range(NSA_HEADS)], axis=0)

    ncmp = kc3_ref.shape[2]
    s = sum(_dot_nt(kc3_ref[0, z], q4) for z in range(3))
    n = lax.broadcasted_iota(jnp.int32, (ncmp, cols), 0)
    qpos_c = q0 + (lax.broadcasted_iota(jnp.int32, (ncmp, cols), 1) & (tq - 1))
    s = jnp.where((n * NSA_CMP_STRIDE + (NSA_CMP_LEN - 1) <= qpos_c) & (n < ncmp - 1), s, -jnp.inf)
    e = jnp.exp(s - jnp.maximum(jnp.max(s, axis=0, keepdims=True), MASK_VALUE))
    l = jnp.sum(e, axis=0, keepdims=True)
    p = e / jnp.where(l > 0.0, l, 1.0)
    o_cmp = _dot(vct_ref[0], p.astype(BF16))
    psum = sum(p[:, h * tq:(h + 1) * tq] for h in range(NSA_HEADS))
    imp = sum(_dot(covert_ref[...], piece) for piece in _split3_bf16(psum))

    nb = covert_ref.shape[0]
    jb = lax.broadcasted_iota(jnp.int32, (nb, tq), 0)
    cur = (q0 + lax.broadcasted_iota(jnp.int32, (nb, tq), 1)) // NSA_SEL_BLOCK
    forced = (jb == 0) | (jb == cur) | (jb == cur - 1)
    visible = jb <= cur
    score = jnp.where(visible, imp + jnp.where(forced, NSA_FORCED_SCORE, 0.0), -jnp.inf)
    beaten = jnp.zeros((nb, tq), F32)
    for j in range(n_blk):
        row = score[j:j + 1, :]
        beaten = beaten + jnp.where((row > score) | ((row == score) & (jb > j)), 1.0, 0.0)
    keep = (beaten < float(NSA_TOP_N)) & visible
    sel = jnp.concatenate([jnp.where(keep, 0.0, MASK_VALUE), jnp.full((128 - nb, tq), MASK_VALUE, F32)], axis=0).astype(BF16)

    qh = [q_ref[0, :, h * 128:(h + 1) * 128] for h in range(NSA_HEADS)]
    kio = lax.broadcasted_iota(jnp.int32, (tk, tq), 0)
    qpos = q0 + lax.broadcasted_iota(jnp.int32, (tk, tq), 1)
    init = tuple(_softmax_init(tq, NSA_HEAD_DIM) for _ in qh)

    def tile(t, carries, causal, window):
        k0 = pl.multiple_of(t * tk, tk)
        k = ks_ref[0, pl.ds(k0, tk), :]
        bias = _dot(et_ref[pl.ds(k0, tk), :], sel)
        scores = [_dot_nt(k, q) + bias for q in qh]
        vts = [vst_ref[0, :, pl.ds(k0, tk)]] * NSA_HEADS
        masks = [(k0 + kio <= qpos) if causal else None] * NSA_HEADS
        if window:
            k = kw_ref[0, pl.ds(k0, tk), :]
            dist = qpos - (k0 + kio)
            scores += [_dot_nt(k, q) for q in qh]
            vts += [vwt_ref[0, :, pl.ds(k0, tk)]] * NSA_HEADS
            masks += [(dist >= 0) & (dist < NSA_WINDOW)] * NSA_HEADS
        return _online_softmax_steps(carries, scores, vts, masks)

    t_diag = q0 // tk
    t_lo = jnp.maximum(q0 - NSA_WINDOW + 1, 0) // tk
    o_slc = lax.fori_loop(0, t_lo, lambda t, c: tile(t, c, False, False), init)
    both = lax.fori_loop(t_lo, t_diag, lambda t, c: tile(t, c, False, True), o_slc + init)
    both = tile(t_diag, both, True, True)
    o_slc, o_win = both[:NSA_HEADS], both[NSA_HEADS:]

    g = jax.nn.sigmoid(g_ref[0].astype(F32).T)
    for h in range(NSA_HEADS):
        o = (g[3 * h:3 * h + 1] * o_cmp[:, h * tq:(h + 1) * tq] + g[3 * h + 1:3 * h + 2] * _softmax_finish(o_slc[h])
             + g[3 * h + 2:3 * h + 3] * _softmax_finish(o_win[h]))
        o_ref[0, :, h * 128:(h + 1) * 128] = o.T.astype(o_ref.dtype)


def nsa_mixer(p, col0, pos_emb, w1, w2, cos, sin, *, tm=512, tq=128, tk=256):
    b, s, _ = p.shape
    assert col0 % 2304 == 0
    tm, tk = min(tm, s), min(tk, s)
    hw = NSA_HEADS * NSA_HEAD_DIM
    scale = NSA_HEAD_DIM ** -0.5
    row = lambda w: pl.BlockSpec((1, tm, w), lambda bi, i: (bi, i, 0))
    tab = pl.BlockSpec((tm, 128), lambda bi, i: (i, 0))
    tr_spec = pl.BlockSpec((1, 128, tm), lambda bi, i: (bi, 0, i))
    q, kc_in, vc_in, ks, kw, vst, vwt = pl.pallas_call(
        functools.partial(_nsa_prep_kernel, scale=scale),
        grid=(b, s // tm),
        in_specs=[pl.BlockSpec((1, tm, 2304), lambda bi, i: (bi, i, col0 // 2304)), tab, tab],
        out_specs=[row(hw), row(128), row(128), row(128), row(128), tr_spec, tr_spec],
        out_shape=[jax.ShapeDtypeStruct((b, s, hw), BF16), jax.ShapeDtypeStruct((b, s, 128), F32),
                   jax.ShapeDtypeStruct((b, s, 128), F32), jax.ShapeDtypeStruct((b, s, 128), BF16),
                   jax.ShapeDtypeStruct((b, s, 128), BF16), jax.ShapeDtypeStruct((b, 128, s), BF16),
                   jax.ShapeDtypeStruct((b, 128, s), BF16)],
        compiler_params=_params("parallel", "parallel"),
        name="nsa_prep",
    )(p, cos, sin)

    nrow = s // NSA_CMP_STRIDE
    fw = NSA_CMP_STRIDE * NSA_HEAD_DIM
    half = w1.shape[1] // 2
    w1r = jnp.concatenate([w1[:, :half], w1[:, half:]], axis=2)
    pos2 = jnp.pad(pos_emb.reshape(2, 2, fw), ((0, 0), (0, 6), (0, 0)))
    per_b = lambda shape: pl.BlockSpec((1,) + shape, lambda bi: (bi, 0, 0))
    const3 = lambda shape: pl.BlockSpec(shape, lambda bi: (0, 0, 0))
    kc3, vct = pl.pallas_call(
        _nsa_compress_kernel,
        grid=(b,),
        in_specs=[per_b((nrow, fw)), per_b((nrow, fw)), const3(pos2.shape), const3(w1r.shape), const3(w2.shape)],
        out_specs=[pl.BlockSpec((1, 3, nrow, NSA_HEAD_DIM), lambda bi: (bi, 0, 0, 0)), per_b((NSA_HEAD_DIM, nrow))],
        out_shape=[jax.ShapeDtypeStruct((b, 3, nrow, NSA_HEAD_DIM), BF16), jax.ShapeDtypeStruct((b, NSA_HEAD_DIM, nrow), BF16)],
        compiler_params=_params("parallel"),
        name="nsa_compress",
    )(kc_in.reshape(b, nrow, fw), vc_in.reshape(b, nrow, fw), pos2, w1r, w2)

    n_blk = s // NSA_SEL_BLOCK
    nb = 64
    assert n_blk <= nb and tq == 128 and tk % tq == 0
    cmp_start = np.arange(nrow) * NSA_CMP_STRIDE
    cmp_end = cmp_start + NSA_CMP_LEN - 1
    blk_start = np.arange(nb) * NSA_SEL_BLOCK
    cover_t = ((cmp_start[None, :] <= blk_start[:, None] + NSA_SEL_BLOCK - 1)
               & (cmp_end[None, :] >= blk_start[:, None]) & (np.arange(nb)[:, None] < n_blk)
               & (np.arange(nrow)[None, :] < nrow - 1)).astype(np.float32)
    expand_t = ((np.arange(s)[:, None] // NSA_SEL_BLOCK) == np.arange(128)[None, :]).astype(np.float32)
    qrow = lambda w: pl.BlockSpec((1, tq, w), lambda bi, i: (bi, i, 0))
    k_spec = pl.BlockSpec((1, s, 128), lambda bi, i: (bi, 0, 0))
    vt_spec = pl.BlockSpec((1, 128, s), lambda bi, i: (bi, 0, 0))
    gate_block = (col0 + 2 * hw) // 128 + 9
    return pl.pallas_call(
        functools.partial(_nsa_attn_kernel, tq=tq, tk=tk, n_blk=n_blk),
        grid=(b, s // tq),
        in_specs=[qrow(hw), pl.BlockSpec((1, 3, nrow, NSA_HEAD_DIM), lambda bi, i: (bi, 0, 0, 0)),
                  pl.BlockSpec((1, NSA_HEAD_DIM, nrow), lambda bi, i: (bi, 0, 0)),
                  pl.BlockSpec((nb, nrow), lambda bi, i: (0, 0)),
                  k_spec, vt_spec, k_spec, vt_spec, pl.BlockSpec((s, 128), lambda bi, i: (0, 0)),
                  pl.BlockSpec((1, tq, 128), lambda bi, i: (bi, i, gate_block))],
        out_specs=qrow(hw),
        out_shape=jax.ShapeDtypeStruct((b, s, hw), BF16),
        compiler_params=_params("parallel", "arbitrary"),
        name="nsa_attn",
    )(q, kc3, vct, jnp.asarray(cover_t, dtype=BF16), ks, vst, kw, vwt, jnp.asarray(expand_t, dtype=BF16), p)


def _rwkv_prep_kernel(z_ref, zp_ref, mu_ref, w0_ref, a0_ref, kk_ref, ka_ref, rk_ref, w2_ref, a2_ref, g2_ref, bd_ref,
                      csum_ref, kt_ref, rt_ref, kh_ref, bh_ref, kbar_ref, bbar_ref, v_ref, etot_ref, g_ref, bonus_ref):
    i = pl.program_id(1)
    z = z_ref[0]
    tm = z.shape[0]
    prev_row = jnp.where(i > 0, zp_ref[0, 7:8, :], 0.0)
    rowid = lax.broadcasted_iota(jnp.int32, z.shape, 0)
    z_prev = jnp.where(rowid == 0, prev_row, pltpu.roll(z, 1, axis=0))
    z = z + (z_prev - z) * mu_ref[...]
    c = BRANCH_WIDTH
    r, k, v = z[:, :c], z[:, c:2 * c], z[:, 2 * c:3 * c]
    wd, ad, gd = z[:, 3 * c:3 * c + 128], z[:, 3 * c + 128:3 * c + 256], z[:, 3 * c + 256:3 * c + 512]
    wpre = -(w0_ref[...] + _dot(jnp.tanh(wd), w2_ref[...], HIGHEST))
    w_log = -(jnp.maximum(wpre, 0.0) + jnp.log1p(jnp.exp(-jnp.abs(wpre)))) - 0.5
    lw = -jnp.exp(w_log)
    a = jax.nn.sigmoid(a0_ref[...] + _dot(ad, a2_ref[...], HIGHEST))
    g_ref[0] = _dot(jax.nn.sigmoid(gd), g2_ref[...], HIGHEST)
    kk = k * kk_ref[...]
    head_sum = lambda t: sum(_dot(piece, bd_ref[...]) for piece in _split3_bf16(t))
    kk = kk * lax.rsqrt(jnp.maximum(head_sum(kk * kk), 1e-24))
    k = k * (1.0 + (a - 1.0) * ka_ref[...])
    bonus_ref[0] = head_sum(r * k * rk_ref[...]) * v
    sums = sum(_dot(csum_ref[...], piece) for piece in _split3_bf16(lw))
    cum, total = sums[:tm], sums[tm:]
    dec_out, dec_end = jnp.exp(-cum), jnp.exp(total - cum)
    b = kk * a
    kt_ref[0] = (kk * jnp.exp(cum - lw)).astype(BF16)
    rt_ref[0] = (r * jnp.exp(cum)).astype(BF16)
    kh_ref[0] = (k * dec_out).astype(BF16)
    bh_ref[0] = (b * dec_out).astype(BF16)
    kbar_ref[0] = (k * dec_end).astype(BF16)
    bbar_ref[0] = (b * dec_end).astype(BF16)
    v_ref[0] = v.astype(BF16)
    etot_ref[0] = jnp.exp(total)


def _rwkv_scan_kernel(kt_ref, rt_ref, kh_ref, bh_ref, kbar_ref, bbar_ref, v_ref, etot_ref, g_ref, bonus_ref,
                      gw_ref, gb_ref, y_ref, state_ref, *, L):
    n_chunks = kt_ref.shape[1] // L
    N = RWKV_HEAD_DIM
    n_pairs = kt_ref.shape[2] // 128

    @pl.when(pl.program_id(1) == 0)
    def _():
        state_ref[...] = jnp.zeros_like(state_ref)

    ri = lax.broadcasted_iota(jnp.int32, (L, 2 * L), 0)
    ci = lax.broadcasted_iota(jnp.int32, (L, 2 * L), 1) & (L - 1)
    strict2, incl2 = ci < ri, ci <= ri
    eye = jnp.where(lax.broadcasted_iota(jnp.int32, (L, L), 0) == lax.broadcasted_iota(jnp.int32, (L, L), 1), 1.0, 0.0)
    first2 = lax.broadcasted_iota(jnp.int32, (2 * L, 128), 1) < N
    first = lax.broadcasted_iota(jnp.int32, (L, 128), 1) < N
    bd = (lax.broadcasted_iota(jnp.int32, (128, 128), 0) < N) == (lax.broadcasted_iota(jnp.int32, (128, 128), 1) < N)
    zero = jnp.zeros((), BF16)

    xs_in, a_k, a_r = [], [], []
    for c in range(n_chunks):
        rs = slice(c * L, (c + 1) * L)
        for p in range(n_pairs):
            sl = slice(p * 128, (p + 1) * 128)
            x = jnp.concatenate([kt_ref[0, rs, sl], rt_ref[0, rs, sl]], axis=0)
            y = jnp.concatenate([kh_ref[0, rs, sl], bh_ref[0, rs, sl]], axis=0)
            xs_in.append(x)
            for sub in range(2):
                gm = _dot_nt(jnp.where(first2 if sub == 0 else ~first2, x, zero), y)
                a_k.append(jnp.where(strict2, gm[:L], 0.0))
                a_r.append(jnp.where(incl2, gm[L:], 0.0))
    a_k, a_r = jnp.stack(a_k), jnp.stack(a_r)
    a_kb = a_k[:, :, L:]
    bmm = lambda a, b: jnp.einsum("hij,hjk->hik", a.astype(BF16), b.astype(BF16), preferred_element_type=F32)
    tinv = eye[None] - a_kb
    pw = a_kb
    for _ in range(int(np.log2(L)) - 1):
        pw = bmm(pw, pw)
        tinv = tinv + bmm(tinv, pw)

    pick = lambda t: jnp.where(first, t[:L], t[L:])
    hsum = lambda t: jnp.where(first, jnp.sum(jnp.where(first, t, 0.0), axis=-1, keepdims=True),
                               jnp.sum(jnp.where(first, 0.0, t), axis=-1, keepdims=True))
    states = [state_ref[p] for p in range(n_pairs)]
    for c in range(n_chunks):
        rs = slice(c * L, (c + 1) * L)
        for p in range(n_pairs):
            sl = slice(p * 128, (p + 1) * 128)
            h0 = (c * n_pairs + p) * 2
            h1 = h0 + 1
            st = states[p]
            xs = _dot_nt(xs_in[c * n_pairs + p], st.astype(BF16))
            v = v_ref[0, rs, sl]
            akkv = pick(_dot(jnp.concatenate([a_k[h0, :, :L], a_k[h1, :, :L]], axis=0).astype(BF16), v))
            rhs = -(xs[:L] + akkv)
            u = pick(_dot(jnp.concatenate([tinv[h0], tinv[h1]], axis=0).astype(BF16), rhs.astype(BF16)))
            vu = jnp.concatenate([v, u.astype(BF16)], axis=0)
            o = xs[L:] + pick(_dot(jnp.concatenate([a_r[h0], a_r[h1]], axis=0).astype(BF16), vu))
            kb = jnp.concatenate([kbar_ref[0, rs, sl], bbar_ref[0, rs, sl]], axis=0)
            states[p] = st * etot_ref[0, c * L:c * L + 1, sl] + jnp.where(bd, _dot_tn(vu, kb), 0.0)
            ctr = o - hsum(o) * (1.0 / N)
            on = ctr * lax.rsqrt(hsum(ctr * ctr) * (1.0 / N) + RWKV_GN_EPS)
            y_ref[0, rs, sl] = ((on * gw_ref[:, sl] + gb_ref[:, sl] + bonus_ref[0, rs, sl]) * g_ref[0, rs, sl]).astype(y_ref.dtype)
    for p in range(n_pairs):
        state_ref[p] = states[p]


def rwkv7_mixer(z, mu, w0, w2, a0, a2, g2, k_k, k_a, r_k, gn_w, gn_b, *, tm=256):
    b, s, zw = z.shape
    tm = min(tm, s)
    c, hd, nh = BRANCH_WIDTH, RWKV_HEAD_DIM, RWKV_HEADS
    L = min(RWKV_CHUNK, s)
    assert tm % L == 0 and L & (L - 1) == 0
    bd = jnp.asarray(np.kron(np.eye(nh, dtype=np.float32), np.ones((hd, hd), np.float32)), dtype=BF16)
    chunk = np.arange(tm) // L
    same = chunk[:, None] == chunk[None, :]
    csum = np.concatenate([same & (np.arange(tm)[None, :] <= np.arange(tm)[:, None]), same], axis=0)
    csum = jnp.asarray(csum.astype(np.float32), dtype=BF16)
    row = lambda w: pl.BlockSpec((1, tm, w), lambda bi, i: (bi, i, 0))
    vec = lambda w: pl.BlockSpec((1, w), lambda bi, i: (0, 0))
    mat = lambda shape: pl.BlockSpec(shape, lambda bi, i: (0, 0))
    outs = pl.pallas_call(
        _rwkv_prep_kernel,
        grid=(b, s // tm),
        in_specs=[row(zw), pl.BlockSpec((1, 8, zw), lambda bi, i: (bi, jnp.maximum(i * (tm // 8) - 1, 0), 0)),
                  vec(zw), vec(c), vec(c), vec(c), vec(c), vec(c), mat(w2.shape), mat(a2.shape), mat(g2.shape),
                  mat(bd.shape), mat(csum.shape)],
        out_specs=[row(c)] * 10,
        out_shape=[jax.ShapeDtypeStruct((b, s, c), BF16)] * 7 + [jax.ShapeDtypeStruct((b, s, c), F32)] * 3,
        compiler_params=_params("parallel", "parallel"),
        name="rwkv_prep",
    )(z, z, mu.reshape(1, zw), w0.reshape(1, c), a0.reshape(1, c), k_k.reshape(1, c), k_a.reshape(1, c),
      r_k.reshape(1, c), w2, a2, g2, bd, csum)
    rows = min(RWKV_CHUNKS_PER_STEP * L, s)
    blk = pl.BlockSpec((1, rows, c), lambda bi, ci: (bi, ci, 0))
    gvec = pl.BlockSpec((1, c), lambda bi, ci: (0, 0))
    return pl.pallas_call(
        functools.partial(_rwkv_scan_kernel, L=L),
        grid=(b, s // rows),
        in_specs=[blk] * 10 + [gvec, gvec],
        out_specs=blk,
        out_shape=jax.ShapeDtypeStruct((b, s, c), BF16),
        scratch_shapes=[pltpu.VMEM((c // 128, 128, 128), F32)],
        compiler_params=_params("parallel", "arbitrary"),
        name="rwkv_scan",
    )(*outs, gn_w.reshape(1, c), gn_b.reshape(1, c))


def _retention_kernel(q_ref, k_ref, v_ref, g_ref, cos_ref, sin_ref, inner_ref, qd_ref, kd_ref, cd_ref, o_ref, state_ref):
    @pl.when(pl.program_id(1) == 0)
    def _():
        state_ref[...] = jnp.zeros_like(state_ref)

    C = q_ref.shape[1]
    kw = RET_HEADS * RET_K_DIM
    cos, sin = cos_ref[...], sin_ref[...]
    q = q_ref[0, :, :kw].astype(F32) * cos + q_ref[0, :, kw:].astype(F32) * sin
    k = (k_ref[0, :, :kw].astype(F32) * cos + k_ref[0, :, kw:].astype(F32) * sin) * RET_K_DIM ** -0.5
    lane = lax.broadcasted_iota(jnp.int32, (C, 128), 1)
    for h in range(RET_HEADS):
        pair, sub = h // 2, h % 2
        in_head = (lane // RET_K_DIM) == sub
        qh = jnp.where(in_head, q[:, pair * 128:(pair + 1) * 128], 0.0)
        kh = jnp.where(in_head, k[:, pair * 128:(pair + 1) * 128], 0.0)
        vh = v_ref[0, :, h * RET_V_DIM:(h + 1) * RET_V_DIM]
        gate = g_ref[0, :, h * RET_V_DIM:(h + 1) * RET_V_DIM].astype(F32)
        scores = _dot_nt(qh.astype(BF16), kh.astype(BF16)) * inner_ref[h]
        st = state_ref[h]
        o = _dot(scores.astype(BF16), vh) + _dot((qh * qd_ref[h]).astype(BF16), st.astype(BF16))
        state_ref[h] = st * cd_ref[h, 0:1, :] + _dot_tn((kh * kd_ref[h]).astype(BF16), vh)
        c = o - jnp.mean(o, axis=-1, keepdims=True)
        y = c * lax.rsqrt(jnp.mean(c * c, axis=-1, keepdims=True) + NORM_EPS)
        o_ref[0, :, h * RET_V_DIM:(h + 1) * RET_V_DIM] = (y * (gate * jax.nn.sigmoid(gate))).astype(o_ref.dtype)


def retention_mixer(p, col0, cos, sin):
    b, s, _ = p.shape
    assert col0 % 512 == 0
    H, C = RET_HEADS, min(RET_CHUNK, s)
    log_gamma = jnp.log1p(-jnp.exp2(-5.0 - jnp.arange(H, dtype=F32)))
    n = jnp.arange(C, dtype=F32)
    dist = n[:, None] - n[None, :]
    inner = jnp.where(dist >= 0, jnp.exp(jnp.maximum(dist, 0.0) * log_gamma[:, None, None]), 0.0)
    q_decay = jnp.exp((n + 1.0) * log_gamma[:, None])
    k_decay = jnp.exp((C - 1.0 - n) * log_gamma[:, None])
    chunk_decay = jnp.exp(C * log_gamma)
    lanes = lambda t: jnp.broadcast_to(t[:, :, None], (H, C, 128))
    const = lambda shape: pl.BlockSpec(shape, lambda bi, ci: (0,) * len(shape))
    kw = RET_HEADS * RET_K_DIM
    return pl.pallas_call(
        _retention_kernel,
        grid=(b, s // C),
        in_specs=[pl.BlockSpec((1, C, 512), functools.partial(lambda bi, ci, j: (bi, ci, j), j=col0 // 512 + j))
                  for j in range(4)] + [
                  pl.BlockSpec((C, kw), lambda bi, ci: (ci, 0)), pl.BlockSpec((C, kw), lambda bi, ci: (ci, 0)),
                  const((H, C, C)), const((H, C, 128)), const((H, C, 128)), const((H, 8, 128))],
        out_specs=pl.BlockSpec((1, C, H * RET_V_DIM), lambda bi, ci: (bi, ci, 0)),
        out_shape=jax.ShapeDtypeStruct((b, s, H * RET_V_DIM), BF16),
        scratch_shapes=[pltpu.VMEM((H, 128, 128), F32)],
        compiler_params=_params("parallel", "arbitrary"),
        name="retention",
    )(p, p, p, p, cos, sin, inner, lanes(q_decay), lanes(k_decay), jnp.broadcast_to(chunk_decay[:, None, None], (H, 8, 128)))


def _rot_half_cols(w, half):
    return jnp.concatenate([-w[..., half:2 * half], w[..., :half]], axis=-1)


def _pad_cols(w, width):
    return jnp.pad(w, [(0, 0)] * (w.ndim - 1) + [(0, width - w.shape[-1])])


def _rope_tables(s, inv_freq, width, fill_cos):
    inv_freq = np.asarray(inv_freq, np.float32)
    ang = np.arange(s, dtype=np.float32)[:, None] * inv_freq[None, :]
    cos, sin = np.cos(ang).astype(np.float32), np.sin(ang).astype(np.float32)
    rot = 2 * cos.shape[1]
    cos2 = np.concatenate([cos, cos, np.full((s, width - rot), fill_cos, np.float32)], axis=1)
    sin2 = np.concatenate([sin, sin, np.zeros((s, width - rot), np.float32)], axis=1)
    return cos2, sin2


def _inv_freq(rot_dim, theta):
    return np.float32(theta) ** (-np.arange(0, rot_dim, 2, dtype=np.float32) / np.float32(rot_dim))


def _split_offsets():
    sizes = (MLA_Q_RANK, MLA_KV_RANK, MLA_ROPE_DIM, 512, 128, 128, 128, 128, 128, 128, 12,
             3 * BRANCH_WIDTH + RWKV_DECAY_LORA + RWKV_A_LORA + RWKV_GATE_LORA, 256, 256, 512, 512, N_MIXERS * D_MODEL)
    offs = np.concatenate([[0], np.cumsum(sizes)])
    return [(int(offs[i]), int(offs[i + 1])) for i in range(len(sizes))]


def _pack_in_proj(w):
    w = w.astype(BF16)
    sl = [w[..., a:b] for a, b in _split_offsets()]
    (ql, kvl, kr, nq, nkc, nvc, nks, nvs, nkw, nvw, ng, rz, rq, rk, rv, rg, mg) = sl
    cat = lambda parts: jnp.concatenate(parts, axis=-1)
    w_mla = cat([ql, kvl, _pad_cols(kr, 128), _pad_cols(_rot_half_cols(kr, 32), 128)])
    nh = NSA_ROT_DIM // 2
    rot128 = lambda t: _pad_cols(_rot_half_cols(t[..., :NSA_ROT_DIM], nh), 128)
    nq_rot = cat([rot128(nq[..., h * 128:(h + 1) * 128]) for h in range(NSA_HEADS)])
    w_nsa = cat([nq, nq_rot, nkc, rot128(nkc), nvc, nks, rot128(nks), nvs, nkw, rot128(nkw), nvw, _pad_cols(ng, 128)])
    rh = RET_K_DIM // 2
    ret_rot = lambda t: cat([_rot_half_cols(t[..., h * 64:(h + 1) * 64], rh) for h in range(RET_HEADS)])
    w_ret = cat([rq, ret_rot(rq), rk, ret_rot(rk), rv, rg])
    c = BRANCH_WIDTH
    w_rwkv = cat([rz[..., :3 * c], _pad_cols(rz[..., 3 * c:3 * c + 96], 128),
                  _pad_cols(rz[..., 3 * c + 96:3 * c + 192], 128), rz[..., 3 * c + 192:]])
    return w_mla, w_nsa, w_ret, w_rwkv, mg


def _pack_mla_weights(w_uq, w_ukv):
    H, dn, dr, dv = MLA_HEADS, MLA_NOPE_DIM, MLA_ROPE_DIM, MLA_V_DIM
    uq = w_uq.reshape(-1, H, dn + dr)
    nope = uq[:, :, :dn].reshape(-1, H * dn)
    rope = uq[:, :, dn:]
    rope_p = _pad_cols(rope, 128).reshape(-1, H * 128)
    rope_r = _pad_cols(_rot_half_cols(rope, dr // 2), 128).reshape(-1, H * 128)
    ukv = w_ukv.reshape(-1, H, dn + dv)
    wkv = jnp.concatenate([ukv[:, :, :dn].reshape(-1, H * dn), ukv[:, :, dn:].reshape(-1, H * dv)], axis=1)
    return jnp.concatenate([nope, rope_p, rope_r], axis=1).astype(BF16), wkv.astype(BF16)


def _pad_rows(w, rows):
    return jnp.pad(w, ((0, rows - w.shape[0]), (0, 0)))


def kernel(x, w_in, w_branch, w_out, w_up, w_down, norm_gains, mla_g_q, mla_g_kv, mla_w_uq, mla_w_ukv,
           nsa_cmp_pos, nsa_cmp_w1, nsa_cmp_w2, rwkv_mu, rwkv_w0, rwkv_w2, rwkv_a0, rwkv_a2, rwkv_g2,
           rwkv_k_k, rwkv_k_a, rwkv_r_k, rwkv_gn_w, rwkv_gn_b):
    B, S, D = x.shape
    depth = w_in.shape[0]
    T = B * S
    tm, tm_in = min(512, T), min(1024, T)
    mla_cos, mla_sin = _rope_tables(S, _inv_freq(MLA_ROPE_DIM, ROPE_THETA), 128, 0.0)
    nsa_cos, nsa_sin = _rope_tables(S, _inv_freq(NSA_ROT_DIM, ROPE_THETA), 128, 1.0)
    ret_cos, ret_sin = _rope_tables(S, np.float32(RET_THETA) ** (-np.linspace(0.0, 1.0, RET_K_DIM // 2, dtype=np.float32)),
                                    RET_K_DIM, 0.0)
    ret_cos, ret_sin = np.tile(ret_cos, (1, RET_HEADS)), np.tile(ret_sin, (1, RET_HEADS))
    c = BRANCH_WIDTH
    x = x.reshape(T, D)
    w_mla, w_nsa, w_ret, w_rwkv, w_gate = _pack_in_proj(w_in)
    groups = (w_nsa, w_mla, w_gate, w_ret)
    col_nsa, col_mla, col_gate, col_ret = np.cumsum([0] + [w.shape[-1] for w in groups[:-1]]).tolist()
    w_all = jnp.concatenate(groups, axis=-1)
    w_branch, w_out, w_up, w_down = (w.astype(BF16) for w in (w_branch, w_out, w_up, w_down))
    for l in range(depth):
        g_pre = norm_gains[l, 0]
        p2 = norm_matmul(x, g_pre, w_all[l], BF16, tm_in, 1024)
        p = p2.reshape(B, S, -1)
        p_rwkv = norm_matmul(x, g_pre, w_rwkv[l], F32, tm_in, 1024)

        wq, wkv = _pack_mla_weights(mla_w_uq[l], mla_w_ukv[l])
        y_mla = mla_mixer(p, col_mla, mla_g_q[l], mla_g_kv[l], wq, wkv, mla_cos, mla_sin)
        y_nsa = nsa_mixer(p, col_nsa, nsa_cmp_pos[l], nsa_cmp_w1[l], nsa_cmp_w2[l], nsa_cos, nsa_sin)
        mu = rwkv_mu[l]
        mu_p = jnp.concatenate([mu[:3 * c], _pad_cols(mu[3 * c:3 * c + 96], 128), _pad_cols(mu[3 * c + 96:3 * c + 192], 128),
                                mu[3 * c + 192:]])
        y_rwkv = rwkv7_mixer(p_rwkv.reshape(B, S, -1), mu_p, rwkv_w0[l], _pad_rows(rwkv_w2[l], 128), rwkv_a0[l],
                             _pad_rows(rwkv_a2[l], 128), rwkv_g2[l], rwkv_k_k[l], rwkv_k_a[l], rwkv_r_k[l].reshape(-1),
                             rwkv_gn_w[l], rwkv_gn_b[l])
        y_ret = retention_mixer(p, col_ret, ret_cos, ret_sin)

        ys = [y.reshape(T, c) for y in (y_mla, y_nsa, y_rwkv, y_ret)]
        x = merge_out_residual(ys, p2, col_gate, w_branch[l], w_out[l], norm_gains[l, 1], x, min(256, T))
        x = ffn_block(x, norm_gains[l, 2], w_up[l], w_down[l], norm_gains[l, 3], tm, 1024)
    return x.reshape(B, S, D)
```

```python
import functools

import numpy as np
import jax
import jax.numpy as jnp
from jax import lax
from jax.experimental import pallas as pl
from jax.experimental.pallas import tpu as pltpu

F32 = jnp.float32
BF16 = jnp.bfloat16
HIGHEST = lax.Precision.HIGHEST

V7X_LANES = 128
V7X_VMEM_LIMIT_BYTES = 56 * 1024 * 1024

D_MODEL = 2048
N_MIXERS = 4
BRANCH_WIDTH = D_MODEL // N_MIXERS
D_FF = 4 * D_MODEL
ROPE_THETA = 500000.0
NORM_EPS = 1e-6
MASK_VALUE = -1e30

MLA_NOPE_DIM = 128
MLA_ROPE_DIM = 64
MLA_V_DIM = 128
MLA_HEADS = 4
MLA_Q_RANK = 384
MLA_KV_RANK = 128

NSA_HEAD_DIM = 128
NSA_HEADS = 4
NSA_ROT_DIM = 32
NSA_CMP_LEN = 32
NSA_CMP_STRIDE = 16
NSA_SEL_BLOCK = 64
NSA_TOP_N = 16
NSA_WINDOW = 512
NSA_FORCED_SCORE = 1000.0

RWKV_HEAD_DIM = 64
RWKV_HEADS = 8
RWKV_DECAY_LORA = 96
RWKV_A_LORA = 96
RWKV_GATE_LORA = 256
RWKV_GN_EPS = 64e-5
RWKV_CHUNK = 64
RWKV_CHUNKS_PER_STEP = 4

RET_HEADS = 4
RET_V_DIM = 128
RET_K_DIM = 64
RET_CHUNK = 128
RET_THETA = 10000.0


def _params(*sem):
    return pltpu.CompilerParams(dimension_semantics=sem, vmem_limit_bytes=V7X_VMEM_LIMIT_BYTES)


def _dot(a, b, precision=None):
    return jnp.dot(a, b, preferred_element_type=F32, precision=precision)


def _dot_nt(a, b, precision=None):
    return lax.dot_general(a, b, (((1,), (1,)), ((), ())), preferred_element_type=F32, precision=precision)


def _dot_tn(a, b, precision=None):
    return lax.dot_general(a, b, (((0,), (0,)), ((), ())), preferred_element_type=F32, precision=precision)


def _rms(x, g, eps=NORM_EPS):
    return x * lax.rsqrt(jnp.mean(x * x, axis=-1, keepdims=True) + eps) * g


def _norm_matmul_kernel(x_ref, g_ref, w_ref, o_ref, h_ref):
    @pl.when(pl.program_id(1) == 0)
    def _():
        h_ref[...] = _rms(x_ref[...], g_ref[...]).astype(BF16)

    o_ref[...] = _dot(h_ref[...], w_ref[...]).astype(o_ref.dtype)


def norm_matmul(x, g, w, out_dtype, tm, tn):
    m, k = x.shape
    n = w.shape[1]
    assert m % tm == 0 and n % tn == 0
    return pl.pallas_call(
        _norm_matmul_kernel,
        grid=(m // tm, n // tn),
        in_specs=[pl.BlockSpec((tm, k), lambda i, j: (i, 0)),
                  pl.BlockSpec((1, k), lambda i, j: (0, 0)),
                  pl.BlockSpec((k, tn), lambda i, j: (0, j))],
        out_specs=pl.BlockSpec((tm, tn), lambda i, j: (i, j)),
        out_shape=jax.ShapeDtypeStruct((m, n), out_dtype),
        scratch_shapes=[pltpu.VMEM((tm, k), BF16)],
        compiler_params=_params("parallel", "arbitrary"),
        name="norm_matmul",
    )(x, g.reshape(1, k), w)


def _merge_out_kernel(*refs):
    y_refs, g_refs = refs[:N_MIXERS], refs[N_MIXERS:3 * N_MIXERS]
    wb_ref, wo_ref, gn_ref, x_ref, o_ref = refs[3 * N_MIXERS:]
    half = g_refs[0].shape[1]
    merged = []
    for hf in range(2):
        acc = None
        for m in range(N_MIXERS):
            gate = 0.5 * jnp.tanh(0.5 * g_refs[2 * m + hf][...].astype(F32)) + 0.5
            term = gate * _dot(y_refs[m][...], wb_ref[m, :, hf * half:(hf + 1) * half])
            acc = term if acc is None else acc + term
        merged.append(acc.astype(BF16))
    o_ref[...] = x_ref[...] + _rms(_dot(jnp.concatenate(merged, axis=1), wo_ref[...]), gn_ref[...])


def merge_out_residual(ys, gates, col0, wb, wo, gn, x, tm):
    m, c = ys[0].shape
    d = wb.shape[2]
    half = d // 2
    assert col0 % half == 0
    once = pl.Buffered(1)
    return pl.pallas_call(
        _merge_out_kernel,
        grid=(m // tm,),
        in_specs=[pl.BlockSpec((tm, c), lambda i: (i, 0))] * N_MIXERS
        + [pl.BlockSpec((tm, half), functools.partial(lambda i, j: (i, j), j=col0 // half + jj)) for jj in range(2 * N_MIXERS)]
        + [pl.BlockSpec((N_MIXERS, c, d), lambda i: (0, 0, 0), pipeline_mode=once),
           pl.BlockSpec((d, d), lambda i: (0, 0), pipeline_mode=once),
           pl.BlockSpec((1, d), lambda i: (0, 0)),
           pl.BlockSpec((tm, d), lambda i: (i, 0))],
        out_specs=pl.BlockSpec((tm, d), lambda i: (i, 0)),
        out_shape=jax.ShapeDtypeStruct((m, d), F32),
        compiler_params=_params("parallel"),
        name="merge_out_residual",
    )(*ys, *([gates] * (2 * N_MIXERS)), wb, wo, gn.reshape(1, d), x)


def _ffn_kernel(x_ref, g1_ref, wu_ref, wd_ref, g2_ref, o_ref, h_ref, acc_ref):
    f = pl.program_id(1)

    @pl.when(f == 0)
    def _():
        h_ref[...] = _rms(x_ref[...], g1_ref[...]).astype(BF16)
        acc_ref[...] = jnp.zeros_like(acc_ref)

    u = jnp.maximum(_dot(h_ref[...], wu_ref[...]), 0.0)
    acc_ref[...] += _dot((u * u).astype(BF16), wd_ref[...])

    @pl.when(f == pl.num_programs(1) - 1)
    def _():
        o_ref[...] = x_ref[...] + _rms(acc_ref[...], g2_ref[...])


def ffn_block(x, g1, wu, wd, g2, tm, tf):
    m, d = x.shape
    ff = wu.shape[1]
    return pl.pallas_call(
        _ffn_kernel,
        grid=(m // tm, ff // tf),
        in_specs=[pl.BlockSpec((tm, d), lambda i, f: (i, 0)),
                  pl.BlockSpec((1, d), lambda i, f: (0, 0)),
                  pl.BlockSpec((d, tf), lambda i, f: (0, f)),
                  pl.BlockSpec((tf, d), lambda i, f: (f, 0)),
                  pl.BlockSpec((1, d), lambda i, f: (0, 0))],
        out_specs=pl.BlockSpec((tm, d), lambda i, f: (i, 0)),
        out_shape=jax.ShapeDtypeStruct((m, d), F32),
        scratch_shapes=[pltpu.VMEM((tm, d), BF16), pltpu.VMEM((tm, d), F32)],
        compiler_params=_params("parallel", "arbitrary"),
        name="ffn_block",
    )(x, g1.reshape(1, d), wu, wd, g2.reshape(1, d))


def _online_softmax_steps(carries, scores, vt_tiles, masks):
    scores = [s if mask is None else jnp.where(mask, s, -jnp.inf) for s, mask in zip(scores, masks)]
    stats = []
    for (m_prev, l_prev, _), s in zip(carries, scores):
        m_new = jnp.maximum(m_prev, jnp.max(s, axis=0, keepdims=True))
        alpha = jnp.exp(m_prev - m_new)
        p = jnp.exp(s - m_new)
        stats.append((m_new, alpha, alpha * l_prev + jnp.sum(p, axis=0, keepdims=True), p.astype(BF16)))
    return tuple((m_new, l_new, alpha * acc_prev + _dot(vt, p))
                 for (m_new, alpha, l_new, p), (_, _, acc_prev), vt in zip(stats, carries, vt_tiles))


def _softmax_init(cols, dv):
    return (jnp.full((1, cols), MASK_VALUE, F32), jnp.zeros((1, cols), F32), jnp.zeros((dv, cols), F32))


def _softmax_merge(a, b):
    (ma, la, acca), (mb, lb, accb) = a, b
    m = jnp.maximum(ma, mb)
    wa, wb = jnp.exp(ma - m), jnp.exp(mb - m)
    return m, la * wa + lb * wb, acca * wa + accb * wb


def _softmax_finish(carry):
    _, l, acc = carry
    return acc / jnp.where(l > 0.0, l, 1.0)


def _mla_prep_kernel(p_ref, gq_ref, gkv_ref, wq_ref, wkv_ref, cos_ref, sin_ref,
                     qn_ref, qr_ref, kn_ref, vt_ref, kr_ref, *, scale):
    p = p_ref[0].astype(F32)
    nq = _rms(p[:, :MLA_Q_RANK], gq_ref[...]).astype(BF16)
    nkv = _rms(p[:, MLA_Q_RANK:MLA_Q_RANK + MLA_KV_RANK], gkv_ref[...]).astype(BF16)
    q = _dot(nq, wq_ref[...])
    kv = _dot(nkv, wkv_ref[...])
    cos, sin = cos_ref[...], sin_ref[...]
    hw = MLA_HEADS * MLA_NOPE_DIM
    qn_ref[0] = (q[:, :hw] * scale).astype(BF16)
    for h in range(MLA_HEADS):
        a = q[:, hw + h * 128: hw + (h + 1) * 128]
        b = q[:, 2 * hw + h * 128: 2 * hw + (h + 1) * 128]
        qr_ref[0, :, h * 128:(h + 1) * 128] = ((a * cos + b * sin) * scale).astype(BF16)
    kn_ref[0] = kv[:, :hw].astype(BF16)
    vt_ref[0] = kv[:, hw:].T.astype(BF16)
    kr_ref[0] = (p[:, 512:640] * cos + p[:, 640:768] * sin).astype(BF16)


def _mla_attn_kernel(qn_ref, qr_ref, kn_ref, kr_ref, vt_ref, o_ref, *, tq, tk):
    i = pl.program_id(1)
    hs = [slice(h * 128, (h + 1) * 128) for h in range(MLA_HEADS)]
    q = [jnp.concatenate([qn_ref[0, :, sl], qr_ref[0, :, sl]], axis=1) for sl in hs]

    def tile(t, carries, mask):
        k0 = pl.multiple_of(t * tk, tk)
        kr = kr_ref[0, pl.ds(k0, tk), :]
        scores = [_dot_nt(jnp.concatenate([kn_ref[0, pl.ds(k0, tk), sl], kr], axis=1), q[h])
                  for h, sl in enumerate(hs)]
        return _online_softmax_steps(carries, scores, [vt_ref[0, sl, pl.ds(k0, tk)] for sl in hs], [mask] * len(hs))

    init = tuple(_softmax_init(tq, MLA_V_DIM) for _ in hs)
    t_diag = (i * tq) // tk
    carries = lax.fori_loop(0, t_diag, lambda t, c: tile(t, c, None), init)
    causal = (t_diag * tk + lax.broadcasted_iota(jnp.int32, (tk, tq), 0)
              <= i * tq + lax.broadcasted_iota(jnp.int32, (tk, tq), 1))
    for sl, carry in zip(hs, tile(t_diag, carries, causal)):
        o_ref[0, :, sl] = _softmax_finish(carry).T.astype(o_ref.dtype)


def mla_mixer(p, col0, g_q, g_kv, wq, wkv, cos, sin, *, tm=512, tq=256, tk=512):
    b, s, _ = p.shape
    tm, tq, tk = min(tm, s), min(tq, s), min(tk, s)
    assert tk % tq == 0 and col0 % 768 == 0
    hw = MLA_HEADS * 128
    scale = (MLA_NOPE_DIM + MLA_ROPE_DIM) ** -0.5
    full = lambda shape: pl.BlockSpec(shape, lambda bi, i: (0,) * len(shape))
    row = lambda w: pl.BlockSpec((1, tm, w), lambda bi, i: (bi, i, 0))
    outs = pl.pallas_call(
        functools.partial(_mla_prep_kernel, scale=scale),
        grid=(b, s // tm),
        in_specs=[pl.BlockSpec((1, tm, 768), lambda bi, i: (bi, i, col0 // 768)),
                  full((1, MLA_Q_RANK)), full((1, MLA_KV_RANK)), full(wq.shape), full(wkv.shape),
                  pl.BlockSpec((tm, 128), lambda bi, i: (i, 0)), pl.BlockSpec((tm, 128), lambda bi, i: (i, 0))],
        out_specs=[row(hw), row(hw), row(hw), pl.BlockSpec((1, hw, tm), lambda bi, i: (bi, 0, i)), row(128)],
        out_shape=[jax.ShapeDtypeStruct((b, s, hw), BF16)] * 3 + [jax.ShapeDtypeStruct((b, hw, s), BF16),
                                                                   jax.ShapeDtypeStruct((b, s, 128), BF16)],
        compiler_params=_params("parallel", "parallel"),
        name="mla_prep",
    )(p, g_q.reshape(1, -1), g_kv.reshape(1, -1), wq, wkv, cos, sin)
    qn, qr, kn, vt, kr = outs
    q_spec = pl.BlockSpec((1, tq, hw), lambda bi, i: (bi, i, 0))
    per_batch = lambda shape: pl.BlockSpec((1,) + shape, lambda bi, i: (bi, 0, 0))
    return pl.pallas_call(
        functools.partial(_mla_attn_kernel, tq=tq, tk=tk),
        grid=(b, s // tq),
        in_specs=[q_spec, q_spec, per_batch((s, hw)), per_batch((s, 128)), per_batch((hw, s))],
        out_specs=q_spec,
        out_shape=jax.ShapeDtypeStruct((b, s, hw), BF16),
        compiler_params=_params("parallel", "arbitrary"),
        name="mla_attn",
    )(qn, qr, kn, kr, vt)


def _nsa_prep_kernel(p_ref, cos_ref, sin_ref, q_ref, kc_ref, vc_ref, ks_ref, kw_ref, vst_ref, vwt_ref, *, scale):
    cos, sin = cos_ref[...], sin_ref[...]
    hw = NSA_HEADS * NSA_HEAD_DIM
    for h in range(NSA_HEADS):
        a = p_ref[0, :, h * 128:(h + 1) * 128].astype(F32)
        b = p_ref[0, :, hw + h * 128: hw + (h + 1) * 128].astype(F32)
        q_ref[0, :, h * 128:(h + 1) * 128] = ((a * cos + b * sin) * scale).astype(BF16)
    base = 2 * hw

    def rot(col):
        a = p_ref[0, :, base + col * 128: base + (col + 1) * 128].astype(F32)
        b = p_ref[0, :, base + (col + 1) * 128: base + (col + 2) * 128].astype(F32)
        return a * cos + b * sin

    kc_ref[0] = rot(0)
    vc_ref[0] = p_ref[0, :, base + 256: base + 384].astype(F32)
    ks_ref[0] = rot(3).astype(BF16)
    kw_ref[0] = rot(6).astype(BF16)
    transposed = lambda col: p_ref[0, :, base + col * 128: base + (col + 1) * 128].astype(F32).T.astype(BF16)
    vst_ref[0] = transposed(5)
    vwt_ref[0] = transposed(8)


def _gelu_tanh(x):
    return 0.5 * x * (1.0 + jnp.tanh(np.sqrt(2.0 / np.pi).astype(np.float32) * (x + 0.044715 * (x * x * x))))


def _split3_bf16(x):
    hi = x.astype(BF16)
    r1 = x - hi.astype(F32)
    mid = r1.astype(BF16)
    return hi, mid, (r1 - mid.astype(F32)).astype(BF16)


def _nsa_compress_kernel(k_ref, v_ref, pos_ref, w1_ref, w2_ref, kc3_ref, vct_ref):
    nrow = k_ref.shape[1]
    row = lax.broadcasted_iota(jnp.int32, (nrow, NSA_HEAD_DIM), 0)
    outs = []
    for z, x_ref in enumerate((k_ref, v_ref)):
        res = _dot(x_ref[0], w1_ref[z], HIGHEST)
        pb = _dot(pos_ref[z], w1_ref[z], HIGHEST)
        bias = pb[0:1, :NSA_HEAD_DIM] + pb[1:2, NSA_HEAD_DIM:]
        nxt = pltpu.roll(res[:, NSA_HEAD_DIM:], nrow - 1, axis=0)
        hid = _gelu_tanh(res[:, :NSA_HEAD_DIM] + nxt + bias)
        outs.append(jnp.where(row < nrow - 1, _dot(hid, w2_ref[z], HIGHEST), 0.0))
    for z, piece in enumerate(_split3_bf16(outs[0])):
        kc3_ref[0, z] = piece
    vct_ref[0] = outs[1].T.astype(BF16)


def _nsa_attn_kernel(q_ref, kc3_ref, vct_ref, covert_ref, ks_ref, vst_ref, kw_ref, vwt_ref, et_ref, g_ref, o_ref,
                     *, tq, tk, n_blk):
    i = pl.program_id(1)
    q0 = i * tq
    cols = NSA_HEADS * tq
    q4 = jnp.concatenate([q_ref[0, :, h * 128:(h + 1) * 128] for h in range(NSA_HEADS)], axis=0)

    ncmp = kc3_ref.shape[2]
    s = sum(_dot_nt(kc3_ref[0, z], q4) for z in range(3))
    n = lax.broadcasted_iota(jnp.int32, (ncmp, cols), 0)
    qpos_c = q0 + (lax.broadcasted_iota(jnp.int32, (ncmp, cols), 1) & (tq - 1))
    s = jnp.where((n * NSA_CMP_STRIDE + (NSA_CMP_LEN - 1) <= qpos_c) & (n < ncmp - 1), s, -jnp.inf)
    e = jnp.exp(s - jnp.maximum(jnp.max(s, axis=0, keepdims=True), MASK_VALUE))
    l = jnp.sum(e, axis=0, keepdims=True)
    p = e / jnp.where(l > 0.0, l, 1.0)
    o_cmp = _dot(vct_ref[0], p.astype(BF16))
    psum = sum(p[:, h * tq:(h + 1) * tq] for h in range(NSA_HEADS))
    imp = sum(_dot(covert_ref[...], piece) for piece in _split3_bf16(psum))

    nb = covert_ref.shape[0]
    jb = lax.broadcasted_iota(jnp.int32, (nb, tq), 0)
    cur = (q0 + lax.broadcasted_iota(jnp.int32, (nb, tq), 1)) // NSA_SEL_BLOCK
    forced = (jb == 0) | (jb == cur) | (jb == cur - 1)
    visible = jb <= cur
    score = jnp.where(visible, imp + jnp.where(forced, NSA_FORCED_SCORE, 0.0), -jnp.inf)
    beaten = jnp.zeros((nb, tq), F32)
    for j in range(n_blk):
        row = score[j:j + 1, :]
        beaten = beaten + jnp.where((row > score) | ((row == score) & (jb > j)), 1.0, 0.0)
    keep = (beaten < float(NSA_TOP_N)) & visible
    sel = jnp.concatenate([jnp.where(keep, 0.0, MASK_VALUE), jnp.full((128 - nb, tq), MASK_VALUE, F32)], axis=0).astype(BF16)

    qh = [q_ref[0, :, h * 128:(h + 1) * 128] for h in range(NSA_HEADS)]
    kio = lax.broadcasted_iota(jnp.int32, (tk, tq), 0)
    qpos = q0 + lax.broadcasted_iota(jnp.int32, (tk, tq), 1)
    init = tuple(_softmax_init(tq, NSA_HEAD_DIM) for _ in qh)

    def slc_scores(k0):
        k = ks_ref[0, pl.ds(k0, tk), :]
        bias = _dot(et_ref[pl.ds(k0, tk), :], sel)
        return [_dot_nt(k, q) + bias for q in qh]

    def tile(t, carries, causal, window):
        k0 = pl.multiple_of(t * tk, tk)
        scores = slc_scores(k0)
        vts = [vst_ref[0, :, pl.ds(k0, tk)]] * NSA_HEADS
        masks = [(k0 + kio <= qpos) if causal else None] * NSA_HEADS
        if window:
            k = kw_ref[0, pl.ds(k0, tk), :]
            dist = qpos - (k0 + kio)
            scores += [_dot_nt(k, q) for q in qh]
            vts += [vwt_ref[0, :, pl.ds(k0, tk)]] * NSA_HEADS
            masks += [(dist >= 0) & (dist < NSA_WINDOW)] * NSA_HEADS
        return _online_softmax_steps(carries, scores, vts, masks)

    def tile_pair(gi, carries):
        k0 = pl.multiple_of(gi * (2 * tk), 2 * tk)
        scores, vts = [], []
        for half in range(2):
            scores += slc_scores(k0 + half * tk)
            vts += [vst_ref[0, :, pl.ds(k0 + half * tk, tk)]] * NSA_HEADS
        return _online_softmax_steps(carries, scores, vts, [None] * (2 * NSA_HEADS))

    t_diag = q0 // tk
    t_lo = jnp.maximum(q0 - NSA_WINDOW + 1, 0) // tk
    pairs = lax.fori_loop(0, t_lo // 2, tile_pair, init + init)
    o_slc = tuple(_softmax_merge(a, b) for a, b in zip(pairs[:NSA_HEADS], pairs[NSA_HEADS:]))
    o_slc = lax.fori_loop(2 * (t_lo // 2), t_lo, lambda t, c: tile(t, c, False, False), o_slc)
    both = lax.fori_loop(t_lo, t_diag, lambda t, c: tile(t, c, False, True), o_slc + init)
    both = tile(t_diag, both, True, True)
    o_slc, o_win = both[:NSA_HEADS], both[NSA_HEADS:]

    g = jax.nn.sigmoid(g_ref[0].astype(F32).T)
    for h in range(NSA_HEADS):
        o = (g[3 * h:3 * h + 1] * o_cmp[:, h * tq:(h + 1) * tq] + g[3 * h + 1:3 * h + 2] * _softmax_finish(o_slc[h])
             + g[3 * h + 2:3 * h + 3] * _softmax_finish(o_win[h]))
        o_ref[0, :, h * 128:(h + 1) * 128] = o.T.astype(o_ref.dtype)


def nsa_mixer(p, col0, pos_emb, w1, w2, cos, sin, *, tm=512, tq=128, tk=256):
    b, s, _ = p.shape
    assert col0 % 2304 == 0
    tm, tk = min(tm, s), min(tk, s)
    hw = NSA_HEADS * NSA_HEAD_DIM
    scale = NSA_HEAD_DIM ** -0.5
    row = lambda w: pl.BlockSpec((1, tm, w), lambda bi, i: (bi, i, 0))
    tab = pl.BlockSpec((tm, 128), lambda bi, i: (i, 0))
    tr_spec = pl.BlockSpec((1, 128, tm), lambda bi, i: (bi, 0, i))
    q, kc_in, vc_in, ks, kw, vst, vwt = pl.pallas_call(
        functools.partial(_nsa_prep_kernel, scale=scale),
        grid=(b, s // tm),
        in_specs=[pl.BlockSpec((1, tm, 2304), lambda bi, i: (bi, i, col0 // 2304)), tab, tab],
        out_specs=[row(hw), row(128), row(128), row(128), row(128), tr_spec, tr_spec],
        out_shape=[jax.ShapeDtypeStruct((b, s, hw), BF16), jax.ShapeDtypeStruct((b, s, 128), F32),
                   jax.ShapeDtypeStruct((b, s, 128), F32), jax.ShapeDtypeStruct((b, s, 128), BF16),
                   jax.ShapeDtypeStruct((b, s, 128), BF16), jax.ShapeDtypeStruct((b, 128, s), BF16),
                   jax.ShapeDtypeStruct((b, 128, s), BF16)],
        compiler_params=_params("parallel", "parallel"),
        name="nsa_prep",
    )(p, cos, sin)

    nrow = s // NSA_CMP_STRIDE
    fw = NSA_CMP_STRIDE * NSA_HEAD_DIM
    half = w1.shape[1] // 2
    w1r = jnp.concatenate([w1[:, :half], w1[:, half:]], axis=2)
    pos2 = jnp.pad(pos_emb.reshape(2, 2, fw), ((0, 0), (0, 6), (0, 0)))
    per_b = lambda shape: pl.BlockSpec((1,) + shape, lambda bi: (bi, 0, 0))
    const3 = lambda shape: pl.BlockSpec(shape, lambda bi: (0, 0, 0))
    kc3, vct = pl.pallas_call(
        _nsa_compress_kernel,
        grid=(b,),
        in_specs=[per_b((nrow, fw)), per_b((nrow, fw)), const3(pos2.shape), const3(w1r.shape), const3(w2.shape)],
        out_specs=[pl.BlockSpec((1, 3, nrow, NSA_HEAD_DIM), lambda bi: (bi, 0, 0, 0)), per_b((NSA_HEAD_DIM, nrow))],
        out_shape=[jax.ShapeDtypeStruct((b, 3, nrow, NSA_HEAD_DIM), BF16), jax.ShapeDtypeStruct((b, NSA_HEAD_DIM, nrow), BF16)],
        compiler_params=_params("parallel"),
        name="nsa_compress",
    )(kc_in.reshape(b, nrow, fw), vc_in.reshape(b, nrow, fw), pos2, w1r, w2)

    n_blk = s // NSA_SEL_BLOCK
    nb = 64
    assert n_blk <= nb and tq == 128 and tk % tq == 0
    cmp_start = np.arange(nrow) * NSA_CMP_STRIDE
    cmp_end = cmp_start + NSA_CMP_LEN - 1
    blk_start = np.arange(nb) * NSA_SEL_BLOCK
    cover_t = ((cmp_start[None, :] <= blk_start[:, None] + NSA_SEL_BLOCK - 1)
               & (cmp_end[None, :] >= blk_start[:, None]) & (np.arange(nb)[:, None] < n_blk)
               & (np.arange(nrow)[None, :] < nrow - 1)).astype(np.float32)
    expand_t = ((np.arange(s)[:, None] // NSA_SEL_BLOCK) == np.arange(128)[None, :]).astype(np.float32)
    qrow = lambda w: pl.BlockSpec((1, tq, w), lambda bi, i: (bi, i, 0))
    k_spec = pl.BlockSpec((1, s, 128), lambda bi, i: (bi, 0, 0))
    vt_spec = pl.BlockSpec((1, 128, s), lambda bi, i: (bi, 0, 0))
    gate_block = (col0 + 2 * hw) // 128 + 9
    return pl.pallas_call(
        functools.partial(_nsa_attn_kernel, tq=tq, tk=tk, n_blk=n_blk),
        grid=(b, s // tq),
        in_specs=[qrow(hw), pl.BlockSpec((1, 3, nrow, NSA_HEAD_DIM), lambda bi, i: (bi, 0, 0, 0)),
                  pl.BlockSpec((1, NSA_HEAD_DIM, nrow), lambda bi, i: (bi, 0, 0)),
                  pl.BlockSpec((nb, nrow), lambda bi, i: (0, 0)),
                  k_spec, vt_spec, k_spec, vt_spec, pl.BlockSpec((s, 128), lambda bi, i: (0, 0)),
                  pl.BlockSpec((1, tq, 128), lambda bi, i: (bi, i, gate_block))],
        out_specs=qrow(hw),
        out_shape=jax.ShapeDtypeStruct((b, s, hw), BF16),
        compiler_params=_params("parallel", "arbitrary"),
        name="nsa_attn",
    )(q, kc3, vct, jnp.asarray(cover_t, dtype=BF16), ks, vst, kw, vwt, jnp.asarray(expand_t, dtype=BF16), p)


def _rwkv_prep_kernel(z_ref, zp_ref, mu_ref, w0_ref, a0_ref, kk_ref, ka_ref, rk_ref, w2_ref, a2_ref, g2_ref, bd_ref,
                      csum_ref, kt_ref, rt_ref, kh_ref, bh_ref, kbar_ref, bbar_ref, v_ref, etot_ref, g_ref, bonus_ref):
    i = pl.program_id(1)
    z = z_ref[0]
    tm = z.shape[0]
    prev_row = jnp.where(i > 0, zp_ref[0, 7:8, :], 0.0)
    rowid = lax.broadcasted_iota(jnp.int32, z.shape, 0)
    z_prev = jnp.where(rowid == 0, prev_row, pltpu.roll(z, 1, axis=0))
    z = z + (z_prev - z) * mu_ref[...]
    c = BRANCH_WIDTH
    r, k, v = z[:, :c], z[:, c:2 * c], z[:, 2 * c:3 * c]
    wd, ad, gd = z[:, 3 * c:3 * c + 128], z[:, 3 * c + 128:3 * c + 256], z[:, 3 * c + 256:3 * c + 512]
    wpre = -(w0_ref[...] + _dot(jnp.tanh(wd), w2_ref[...], HIGHEST))
    w_log = -(jnp.maximum(wpre, 0.0) + jnp.log1p(jnp.exp(-jnp.abs(wpre)))) - 0.5
    lw = -jnp.exp(w_log)
    a = jax.nn.sigmoid(a0_ref[...] + _dot(ad, a2_ref[...], HIGHEST))
    g_ref[0] = _dot(jax.nn.sigmoid(gd), g2_ref[...], HIGHEST)
    kk = k * kk_ref[...]
    head_sum = lambda t: sum(_dot(piece, bd_ref[...]) for piece in _split3_bf16(t))
    kk = kk * lax.rsqrt(jnp.maximum(head_sum(kk * kk), 1e-24))
    k = k * (1.0 + (a - 1.0) * ka_ref[...])
    bonus_ref[0] = head_sum(r * k * rk_ref[...]) * v
    sums = sum(_dot(csum_ref[...], piece) for piece in _split3_bf16(lw))
    cum, total = sums[:tm], sums[tm:]
    dec_out, dec_end = jnp.exp(-cum), jnp.exp(total - cum)
    b = kk * a
    kt_ref[0] = (kk * jnp.exp(cum - lw)).astype(BF16)
    rt_ref[0] = (r * jnp.exp(cum)).astype(BF16)
    kh_ref[0] = (k * dec_out).astype(BF16)
    bh_ref[0] = (b * dec_out).astype(BF16)
    kbar_ref[0] = (k * dec_end).astype(BF16)
    bbar_ref[0] = (b * dec_end).astype(BF16)
    v_ref[0] = v.astype(BF16)
    etot_ref[0] = jnp.exp(total)


def _rwkv_scan_kernel(kt_ref, rt_ref, kh_ref, bh_ref, kbar_ref, bbar_ref, v_ref, etot_ref, g_ref, bonus_ref,
                      gw_ref, gb_ref, y_ref, state_ref, *, L):
    n_chunks = kt_ref.shape[1] // L
    N = RWKV_HEAD_DIM
    n_pairs = kt_ref.shape[2] // 128

    @pl.when(pl.program_id(1) == 0)
    def _():
        state_ref[...] = jnp.zeros_like(state_ref)

    ri = lax.broadcasted_iota(jnp.int32, (L, 2 * L), 0)
    ci = lax.broadcasted_iota(jnp.int32, (L, 2 * L), 1) & (L - 1)
    strict2, incl2 = ci < ri, ci <= ri
    eye = jnp.where(lax.broadcasted_iota(jnp.int32, (L, L), 0) == lax.broadcasted_iota(jnp.int32, (L, L), 1), 1.0, 0.0)
    first2 = lax.broadcasted_iota(jnp.int32, (2 * L, 128), 1) < N
    first = lax.broadcasted_iota(jnp.int32, (L, 128), 1) < N
    bd = (lax.broadcasted_iota(jnp.int32, (128, 128), 0) < N) == (lax.broadcasted_iota(jnp.int32, (128, 128), 1) < N)
    zero = jnp.zeros((), BF16)

    xs_in, a_k, a_r = [], [], []
    for c in range(n_chunks):
        rs = slice(c * L, (c + 1) * L)
        for p in range(n_pairs):
            sl = slice(p * 128, (p + 1) * 128)
            x = jnp.concatenate([kt_ref[0, rs, sl], rt_ref[0, rs, sl]], axis=0)
            y = jnp.concatenate([kh_ref[0, rs, sl], bh_ref[0, rs, sl]], axis=0)
            xs_in.append(x)
            for sub in range(2):
                gm = _dot_nt(jnp.where(first2 if sub == 0 else ~first2, x, zero), y)
                a_k.append(jnp.where(strict2, gm[:L], 0.0))
                a_r.append(jnp.where(incl2, gm[L:], 0.0))
    a_k, a_r = jnp.stack(a_k), jnp.stack(a_r)
    a_kb = a_k[:, :, L:]
    bmm = lambda a, b: jnp.einsum("hij,hjk->hik", a.astype(BF16), b.astype(BF16), preferred_element_type=F32)
    tinv = eye[None] - a_kb
    pw = a_kb
    for _ in range(int(np.log2(L)) - 1):
        pw = bmm(pw, pw)
        tinv = tinv + bmm(tinv, pw)

    pick = lambda t: jnp.where(first, t[:L], t[L:])
    hsum = lambda t: jnp.where(first, jnp.sum(jnp.where(first, t, 0.0), axis=-1, keepdims=True),
                               jnp.sum(jnp.where(first, 0.0, t), axis=-1, keepdims=True))
    states = [state_ref[p] for p in range(n_pairs)]
    for c in range(n_chunks):
        rs = slice(c * L, (c + 1) * L)
        for p in range(n_pairs):
            sl = slice(p * 128, (p + 1) * 128)
            h0 = (c * n_pairs + p) * 2
            h1 = h0 + 1
            st = states[p]
            xs = _dot_nt(xs_in[c * n_pairs + p], st.astype(BF16))
            v = v_ref[0, rs, sl]
            akkv = pick(_dot(jnp.concatenate([a_k[h0, :, :L], a_k[h1, :, :L]], axis=0).astype(BF16), v))
            rhs = -(xs[:L] + akkv)
            u = pick(_dot(jnp.concatenate([tinv[h0], tinv[h1]], axis=0).astype(BF16), rhs.astype(BF16)))
            vu = jnp.concatenate([v, u.astype(BF16)], axis=0)
            o = xs[L:] + pick(_dot(jnp.concatenate([a_r[h0], a_r[h1]], axis=0).astype(BF16), vu))
            kb = jnp.concatenate([kbar_ref[0, rs, sl], bbar_ref[0, rs, sl]], axis=0)
            states[p] = st * etot_ref[0, c * L:c * L + 1, sl] + jnp.where(bd, _dot_tn(vu, kb), 0.0)
            ctr = o - hsum(o) * (1.0 / N)
            on = ctr * lax.rsqrt(hsum(ctr * ctr) * (1.0 / N) + RWKV_GN_EPS)
            y_ref[0, rs, sl] = ((on * gw_ref[:, sl] + gb_ref[:, sl] + bonus_ref[0, rs, sl]) * g_ref[0, rs, sl]).astype(y_ref.dtype)
    for p in range(n_pairs):
        state_ref[p] = states[p]


def rwkv7_mixer(z, mu, w0, w2, a0, a2, g2, k_k, k_a, r_k, gn_w, gn_b, *, tm=256):
    b, s, zw = z.shape
    tm = min(tm, s)
    c, hd, nh = BRANCH_WIDTH, RWKV_HEAD_DIM, RWKV_HEADS
    L = min(RWKV_CHUNK, s)
    assert tm % L == 0 and L & (L - 1) == 0
    bd = jnp.asarray(np.kron(np.eye(nh, dtype=np.float32), np.ones((hd, hd), np.float32)), dtype=BF16)
    chunk = np.arange(tm) // L
    same = chunk[:, None] == chunk[None, :]
    csum = np.concatenate([same & (np.arange(tm)[None, :] <= np.arange(tm)[:, None]), same], axis=0)
    csum = jnp.asarray(csum.astype(np.float32), dtype=BF16)
    row = lambda w: pl.BlockSpec((1, tm, w), lambda bi, i: (bi, i, 0))
    vec = lambda w: pl.BlockSpec((1, w), lambda bi, i: (0, 0))
    mat = lambda shape: pl.BlockSpec(shape, lambda bi, i: (0, 0))
    outs = pl.pallas_call(
        _rwkv_prep_kernel,
        grid=(b, s // tm),
        in_specs=[row(zw), pl.BlockSpec((1, 8, zw), lambda bi, i: (bi, jnp.maximum(i * (tm // 8) - 1, 0), 0)),
                  vec(zw), vec(c), vec(c), vec(c), vec(c), vec(c), mat(w2.shape), mat(a2.shape), mat(g2.shape),
                  mat(bd.shape), mat(csum.shape)],
        out_specs=[row(c)] * 10,
        out_shape=[jax.ShapeDtypeStruct((b, s, c), BF16)] * 7 + [jax.ShapeDtypeStruct((b, s, c), F32)] * 3,
        compiler_params=_params("parallel", "parallel"),
        name="rwkv_prep",
    )(z, z, mu.reshape(1, zw), w0.reshape(1, c), a0.reshape(1, c), k_k.reshape(1, c), k_a.reshape(1, c),
      r_k.reshape(1, c), w2, a2, g2, bd, csum)
    rows = min(RWKV_CHUNKS_PER_STEP * L, s)
    blk = pl.BlockSpec((1, rows, c), lambda bi, ci: (bi, ci, 0))
    gvec = pl.BlockSpec((1, c), lambda bi, ci: (0, 0))
    return pl.pallas_call(
        functools.partial(_rwkv_scan_kernel, L=L),
        grid=(b, s // rows),
        in_specs=[blk] * 10 + [gvec, gvec],
        out_specs=blk,
        out_shape=jax.ShapeDtypeStruct((b, s, c), BF16),
        scratch_shapes=[pltpu.VMEM((c // 128, 128, 128), F32)],
        compiler_params=_params("parallel", "arbitrary"),
        name="rwkv_scan",
    )(*outs, gn_w.reshape(1, c), gn_b.reshape(1, c))


def _retention_kernel(q_ref, k_ref, v_ref, g_ref, cos_ref, sin_ref, inner_ref, qd_ref, kd_ref, cd_ref, o_ref, state_ref):
    @pl.when(pl.program_id(1) == 0)
    def _():
        state_ref[...] = jnp.zeros_like(state_ref)

    C = q_ref.shape[1]
    kw = RET_HEADS * RET_K_DIM
    cos, sin = cos_ref[...], sin_ref[...]
    q = q_ref[0, :, :kw].astype(F32) * cos + q_ref[0, :, kw:].astype(F32) * sin
    k = (k_ref[0, :, :kw].astype(F32) * cos + k_ref[0, :, kw:].astype(F32) * sin) * RET_K_DIM ** -0.5
    lane = lax.broadcasted_iota(jnp.int32, (C, 128), 1)
    for h in range(RET_HEADS):
        pair, sub = h // 2, h % 2
        in_head = (lane // RET_K_DIM) == sub
        qh = jnp.where(in_head, q[:, pair * 128:(pair + 1) * 128], 0.0)
        kh = jnp.where(in_head, k[:, pair * 128:(pair + 1) * 128], 0.0)
        vh = v_ref[0, :, h * RET_V_DIM:(h + 1) * RET_V_DIM]
        gate = g_ref[0, :, h * RET_V_DIM:(h + 1) * RET_V_DIM].astype(F32)
        scores = _dot_nt(qh.astype(BF16), kh.astype(BF16)) * inner_ref[h]
        st = state_ref[h]
        o = _dot(scores.astype(BF16), vh) + _dot((qh * qd_ref[h]).astype(BF16), st.astype(BF16))
        state_ref[h] = st * cd_ref[h, 0:1, :] + _dot_tn((kh * kd_ref[h]).astype(BF16), vh)
        c = o - jnp.mean(o, axis=-1, keepdims=True)
        y = c * lax.rsqrt(jnp.mean(c * c, axis=-1, keepdims=True) + NORM_EPS)
        o_ref[0, :, h * RET_V_DIM:(h + 1) * RET_V_DIM] = (y * (gate * jax.nn.sigmoid(gate))).astype(o_ref.dtype)


def retention_mixer(p, col0, cos, sin):
    b, s, _ = p.shape
    assert col0 % 512 == 0
    H, C = RET_HEADS, min(RET_CHUNK, s)
    log_gamma = jnp.log1p(-jnp.exp2(-5.0 - jnp.arange(H, dtype=F32)))
    n = jnp.arange(C, dtype=F32)
    dist = n[:, None] - n[None, :]
    inner = jnp.where(dist >= 0, jnp.exp(jnp.maximum(dist, 0.0) * log_gamma[:, None, None]), 0.0)
    q_decay = jnp.exp((n + 1.0) * log_gamma[:, None])
    k_decay = jnp.exp((C - 1.0 - n) * log_gamma[:, None])
    chunk_decay = jnp.exp(C * log_gamma)
    lanes = lambda t: jnp.broadcast_to(t[:, :, None], (H, C, 128))
    const = lambda shape: pl.BlockSpec(shape, lambda bi, ci: (0,) * len(shape))
    kw = RET_HEADS * RET_K_DIM
    return pl.pallas_call(
        _retention_kernel,
        grid=(b, s // C),
        in_specs=[pl.BlockSpec((1, C, 512), functools.partial(lambda bi, ci, j: (bi, ci, j), j=col0 // 512 + j))
                  for j in range(4)] + [
                  pl.BlockSpec((C, kw), lambda bi, ci: (ci, 0)), pl.BlockSpec((C, kw), lambda bi, ci: (ci, 0)),
                  const((H, C, C)), const((H, C, 128)), const((H, C, 128)), const((H, 8, 128))],
        out_specs=pl.BlockSpec((1, C, H * RET_V_DIM), lambda bi, ci: (bi, ci, 0)),
        out_shape=jax.ShapeDtypeStruct((b, s, H * RET_V_DIM), BF16),
        scratch_shapes=[pltpu.VMEM((H, 128, 128), F32)],
        compiler_params=_params("parallel", "arbitrary"),
        name="retention",
    )(p, p, p, p, cos, sin, inner, lanes(q_decay), lanes(k_decay), jnp.broadcast_to(chunk_decay[:, None, None], (H, 8, 128)))


def _rot_half_cols(w, half):
    return jnp.concatenate([-w[..., half:2 * half], w[..., :half]], axis=-1)


def _pad_cols(w, width):
    return jnp.pad(w, [(0, 0)] * (w.ndim - 1) + [(0, width - w.shape[-1])])


def _rope_tables(s, inv_freq, width, fill_cos):
    inv_freq = np.asarray(inv_freq, np.float32)
    ang = np.arange(s, dtype=np.float32)[:, None] * inv_freq[None, :]
    cos, sin = np.cos(ang).astype(np.float32), np.sin(ang).astype(np.float32)
    rot = 2 * cos.shape[1]
    cos2 = np.concatenate([cos, cos, np.full((s, width - rot), fill_cos, np.float32)], axis=1)
    sin2 = np.concatenate([sin, sin, np.zeros((s, width - rot), np.float32)], axis=1)
    return cos2, sin2


def _inv_freq(rot_dim, theta):
    return np.float32(theta) ** (-np.arange(0, rot_dim, 2, dtype=np.float32) / np.float32(rot_dim))


def _split_offsets():
    sizes = (MLA_Q_RANK, MLA_KV_RANK, MLA_ROPE_DIM, 512, 128, 128, 128, 128, 128, 128, 12,
             3 * BRANCH_WIDTH + RWKV_DECAY_LORA + RWKV_A_LORA + RWKV_GATE_LORA, 256, 256, 512, 512, N_MIXERS * D_MODEL)
    offs = np.concatenate([[0], np.cumsum(sizes)])
    return [(int(offs[i]), int(offs[i + 1])) for i in range(len(sizes))]


def _pack_in_proj(w):
    w = w.astype(BF16)
    sl = [w[..., a:b] for a, b in _split_offsets()]
    (ql, kvl, kr, nq, nkc, nvc, nks, nvs, nkw, nvw, ng, rz, rq, rk, rv, rg, mg) = sl
    cat = lambda parts: jnp.concatenate(parts, axis=-1)
    w_mla = cat([ql, kvl, _pad_cols(kr, 128), _pad_cols(_rot_half_cols(kr, 32), 128)])
    nh = NSA_ROT_DIM // 2
    rot128 = lambda t: _pad_cols(_rot_half_cols(t[..., :NSA_ROT_DIM], nh), 128)
    nq_rot = cat([rot128(nq[..., h * 128:(h + 1) * 128]) for h in range(NSA_HEADS)])
    w_nsa = cat([nq, nq_rot, nkc, rot128(nkc), nvc, nks, rot128(nks), nvs, nkw, rot128(nkw), nvw, _pad_cols(ng, 128)])
    rh = RET_K_DIM // 2
    ret_rot = lambda t: cat([_rot_half_cols(t[..., h * 64:(h + 1) * 64], rh) for h in range(RET_HEADS)])
    w_ret = cat([rq, ret_rot(rq), rk, ret_rot(rk), rv, rg])
    c = BRANCH_WIDTH
    w_rwkv = cat([rz[..., :3 * c], _pad_cols(rz[..., 3 * c:3 * c + 96], 128),
                  _pad_cols(rz[..., 3 * c + 96:3 * c + 192], 128), rz[..., 3 * c + 192:]])
    return w_mla, w_nsa, w_ret, w_rwkv, mg


def _pack_mla_weights(w_uq, w_ukv):
    H, dn, dr, dv = MLA_HEADS, MLA_NOPE_DIM, MLA_ROPE_DIM, MLA_V_DIM
    uq = w_uq.reshape(-1, H, dn + dr)
    nope = uq[:, :, :dn].reshape(-1, H * dn)
    rope = uq[:, :, dn:]
    rope_p = _pad_cols(rope, 128).reshape(-1, H * 128)
    rope_r = _pad_cols(_rot_half_cols(rope, dr // 2), 128).reshape(-1, H * 128)
    ukv = w_ukv.reshape(-1, H, dn + dv)
    wkv = jnp.concatenate([ukv[:, :, :dn].reshape(-1, H * dn), ukv[:, :, dn:].reshape(-1, H * dv)], axis=1)
    return jnp.concatenate([nope, rope_p, rope_r], axis=1).astype(BF16), wkv.astype(BF16)


def _pad_rows(w, rows):
    return jnp.pad(w, ((0, rows - w.shape[0]), (0, 0)))


def kernel(x, w_in, w_branch, w_out, w_up, w_down, norm_gains, mla_g_q, mla_g_kv, mla_w_uq, mla_w_ukv,
           nsa_cmp_pos, nsa_cmp_w1, nsa_cmp_w2, rwkv_mu, rwkv_w0, rwkv_w2, rwkv_a0, rwkv_a2, rwkv_g2,
           rwkv_k_k, rwkv_k_a, rwkv_r_k, rwkv_gn_w, rwkv_gn_b):
    B, S, D = x.shape
    depth = w_in.shape[0]
    T = B * S
    tm, tm_in = min(512, T), min(1024, T)
    mla_cos, mla_sin = _rope_tables(S, _inv_freq(MLA_ROPE_DIM, ROPE_THETA), 128, 0.0)
    nsa_cos, nsa_sin = _rope_tables(S, _inv_freq(NSA_ROT_DIM, ROPE_THETA), 128, 1.0)
    ret_cos, ret_sin = _rope_tables(S, np.float32(RET_THETA) ** (-np.linspace(0.0, 1.0, RET_K_DIM // 2, dtype=np.float32)),
                                    RET_K_DIM, 0.0)
    ret_cos, ret_sin = np.tile(ret_cos, (1, RET_HEADS)), np.tile(ret_sin, (1, RET_HEADS))
    c = BRANCH_WIDTH
    x = x.reshape(T, D)
    w_mla, w_nsa, w_ret, w_rwkv, w_gate = _pack_in_proj(w_in)
    groups = (w_nsa, w_mla, w_gate, w_ret)
    col_nsa, col_mla, col_gate, col_ret = np.cumsum([0] + [w.shape[-1] for w in groups[:-1]]).tolist()
    w_all = jnp.concatenate(groups, axis=-1)
    w_branch, w_out, w_up, w_down = (w.astype(BF16) for w in (w_branch, w_out, w_up, w_down))
    for l in range(depth):
        g_pre = norm_gains[l, 0]
        p2 = norm_matmul(x, g_pre, w_all[l], BF16, tm_in, 1024)
        p = p2.reshape(B, S, -1)
        p_rwkv = norm_matmul(x, g_pre, w_rwkv[l], F32, tm_in, 1024)

        wq, wkv = _pack_mla_weights(mla_w_uq[l], mla_w_ukv[l])
        y_mla = mla_mixer(p, col_mla, mla_g_q[l], mla_g_kv[l], wq, wkv, mla_cos, mla_sin)
        y_nsa = nsa_mixer(p, col_nsa, nsa_cmp_pos[l], nsa_cmp_w1[l], nsa_cmp_w2[l], nsa_cos, nsa_sin)
        mu = rwkv_mu[l]
        mu_p = jnp.concatenate([mu[:3 * c], _pad_cols(mu[3 * c:3 * c + 96], 128), _pad_cols(mu[3 * c + 96:3 * c + 192], 128),
                                mu[3 * c + 192:]])
        y_rwkv = rwkv7_mixer(p_rwkv.reshape(B, S, -1), mu_p, rwkv_w0[l], _pad_rows(rwkv_w2[l], 128), rwkv_a0[l],
                             _pad_rows(rwkv_a2[l], 128), rwkv_g2[l], rwkv_k_k[l], rwkv_k_a[l], rwkv_r_k[l].reshape(-1),
                             rwkv_gn_w[l], rwkv_gn_b[l])
        y_ret = retention_mixer(p, col_ret, ret_cos, ret_sin)

        ys = [y.reshape(T, c) for y in (y_mla, y_nsa, y_rwkv, y_ret)]
        x = merge_out_residual(ys, p2, col_gate, w_branch[l], w_out[l], norm_gains[l, 1], x, min(256, T))
        x = ffn_block(x, norm_gains[l, 2], w_up[l], w_down[l], norm_gains[l, 3], tm, 1024)
    return x.reshape(B, S, D)
```

```python
import functools

import numpy as np
import jax
import jax.numpy as jnp
from jax import lax
from jax.experimental import pallas as pl
from jax.experimental.pallas import tpu as pltpu

F32 = jnp.float32
BF16 = jnp.bfloat16
HIGHEST = lax.Precision.HIGHEST

V7X_LANES = 128
V7X_VMEM_LIMIT_BYTES = 56 * 1024 * 1024

D_MODEL = 2048
N_MIXERS = 4
BRANCH_WIDTH = D_MODEL // N_MIXERS
D_FF = 4 * D_MODEL
ROPE_THETA = 500000.0
NORM_EPS = 1e-6
MASK_VALUE = -1e30
LOG2_E = float(np.log2(np.e))

MLA_NOPE_DIM = 128
MLA_ROPE_DIM = 64
MLA_V_DIM = 128
MLA_HEADS = 4
MLA_Q_RANK = 384
MLA_KV_RANK = 128

NSA_HEAD_DIM = 128
NSA_HEADS = 4
NSA_ROT_DIM = 32
NSA_CMP_LEN = 32
NSA_CMP_STRIDE = 16
NSA_SEL_BLOCK = 64
NSA_TOP_N = 16
NSA_WINDOW = 512
NSA_FORCED_SCORE = 1000.0
NSA_IN = NSA_HEADS * NSA_HEAD_DIM + 7 * 128

RWKV_HEAD_DIM = 64
RWKV_HEADS = 8
RWKV_DECAY_LORA = 96
RWKV_A_LORA = 96
RWKV_GATE_LORA = 256
RWKV_GN_EPS = 64e-5
RWKV_CHUNK = 64
RWKV_CHUNKS_PER_STEP = 4

RET_HEADS = 4
RET_V_DIM = 128
RET_K_DIM = 64
RET_CHUNK = 128
RET_THETA = 10000.0


def _params(*sem):
    return pltpu.CompilerParams(dimension_semantics=sem, vmem_limit_bytes=V7X_VMEM_LIMIT_BYTES)


def _dot(a, b, precision=None):
    return jnp.dot(a, b, preferred_element_type=F32, precision=precision)


def _dot_nt(a, b, precision=None):
    return lax.dot_general(a, b, (((1,), (1,)), ((), ())), preferred_element_type=F32, precision=precision)


def _dot_tn(a, b, precision=None):
    return lax.dot_general(a, b, (((0,), (0,)), ((), ())), preferred_element_type=F32, precision=precision)


def _rotate_half(x, half, period):
    lane = lax.broadcasted_iota(jnp.int32, x.shape, x.ndim - 1) & (period - 1)
    up = pltpu.roll(x, V7X_LANES - half, axis=x.ndim - 1)
    dn = pltpu.roll(x, half, axis=x.ndim - 1)
    return jnp.where(lane < half, -up, jnp.where(lane < 2 * half, dn, 0.0))


def _rope(x, cos, sin, half, period=V7X_LANES):
    return x * cos + _rotate_half(x, half, period) * sin


def _rms(x, g, eps=NORM_EPS):
    return x * lax.rsqrt(jnp.mean(x * x, axis=-1, keepdims=True) + eps) * g


def _norm_matmul_kernel(x_ref, g_ref, w_ref, o_ref, h_ref):
    @pl.when(pl.program_id(1) == 0)
    def _():
        h_ref[...] = _rms(x_ref[...], g_ref[...]).astype(BF16)

    o_ref[...] = _dot(h_ref[...], w_ref[...]).astype(o_ref.dtype)


def norm_matmul(x, g, w, layer, out_dtype, tm, tn):
    m, k = x.shape
    n = w.shape[2]
    assert m % tm == 0 and n % tn == 0
    return pl.pallas_call(
        _norm_matmul_kernel,
        grid=(m // tm, n // tn),
        in_specs=[pl.BlockSpec((tm, k), lambda i, j: (i, 0)),
                  pl.BlockSpec((1, k), lambda i, j: (0, 0)),
                  pl.BlockSpec((None, k, tn), lambda i, j: (layer, 0, j))],
        out_specs=pl.BlockSpec((tm, tn), lambda i, j: (i, j)),
        out_shape=jax.ShapeDtypeStruct((m, n), out_dtype),
        scratch_shapes=[pltpu.VMEM((tm, k), BF16)],
        compiler_params=_params("parallel", "arbitrary"),
        name="norm_matmul",
    )(x, g.reshape(1, k), w)


MERGE_GATE_PIECES = 4


def _merge_out_kernel(*refs):
    n = MERGE_GATE_PIECES
    y_refs, g_refs = refs[:N_MIXERS], refs[N_MIXERS:(1 + n) * N_MIXERS]
    wb_ref, wo_ref, gn_ref, gf_ref, x_ref, o_ref, h_ref = refs[(1 + n) * N_MIXERS:]
    width = g_refs[0].shape[1]
    merged = []
    for piece in range(n):
        acc = None
        for m in range(N_MIXERS):
            gate = 0.5 * jnp.tanh(0.5 * g_refs[n * m + piece][...].astype(F32)) + 0.5
            term = gate * _dot(y_refs[m][...], wb_ref[m, :, piece * width:(piece + 1) * width])
            acc = term if acc is None else acc + term
        merged.append(acc.astype(BF16))
    x_new = x_ref[...] + _rms(_dot(jnp.concatenate(merged, axis=1), wo_ref[...]), gn_ref[...])
    o_ref[...] = x_new
    h_ref[...] = _rms(x_new, gf_ref[...]).astype(BF16)


def merge_out_residual(ys, gates, col0, wb, wo, layer, gn, g_ffn, x, tm):
    m, c = ys[0].shape
    d = wb.shape[3]
    half = d // MERGE_GATE_PIECES
    assert col0 % half == 0
    once = pl.Buffered(1)
    return pl.pallas_call(
        _merge_out_kernel,
        grid=(m // tm,),
        in_specs=[pl.BlockSpec((tm, c), lambda i: (i, 0))] * N_MIXERS
        + [pl.BlockSpec((tm, half), functools.partial(lambda i, j: (i, j), j=col0 // half + jj))
           for jj in range(MERGE_GATE_PIECES * N_MIXERS)]
        + [pl.BlockSpec((None, N_MIXERS, c, d), lambda i: (layer, 0, 0, 0), pipeline_mode=once),
           pl.BlockSpec((None, d, d), lambda i: (layer, 0, 0), pipeline_mode=once),
           pl.BlockSpec((1, d), lambda i: (0, 0)),
           pl.BlockSpec((1, d), lambda i: (0, 0)),
           pl.BlockSpec((tm, d), lambda i: (i, 0))],
        out_specs=[pl.BlockSpec((tm, d), lambda i: (i, 0))] * 2,
        out_shape=[jax.ShapeDtypeStruct((m, d), F32), jax.ShapeDtypeStruct((m, d), BF16)],
        compiler_params=_params("parallel"),
        name="merge_out_residual",
    )(*ys, *([gates] * (MERGE_GATE_PIECES * N_MIXERS)), wb, wo, gn.reshape(1, d), g_ffn.reshape(1, d), x)


def _ffn_kernel(x_ref, h_ref, wu_ref, wd_ref, g2_ref, o_ref, acc_ref):
    f = pl.program_id(1)

    @pl.when(f == 0)
    def _():
        acc_ref[...] = jnp.zeros_like(acc_ref)

    u = jnp.maximum(_dot(h_ref[...], wu_ref[...]), 0.0)
    acc_ref[...] += _dot((u * u).astype(BF16), wd_ref[...])

    @pl.when(f == pl.num_programs(1) - 1)
    def _():
        o_ref[...] = x_ref[...] + _rms(acc_ref[...], g2_ref[...])


def ffn_block(x, h, wu, wd, layer, g2, tm, tf):
    m, d = x.shape
    ff = wu.shape[2]
    return pl.pallas_call(
        _ffn_kernel,
        grid=(m // tm, ff // tf),
        in_specs=[pl.BlockSpec((tm, d), lambda i, f: (i, 0)),
                  pl.BlockSpec((tm, d), lambda i, f: (i, 0)),
                  pl.BlockSpec((None, d, tf), lambda i, f: (layer, 0, f)),
                  pl.BlockSpec((None, tf, d), lambda i, f: (layer, f, 0)),
                  pl.BlockSpec((1, d), lambda i, f: (0, 0))],
        out_specs=pl.BlockSpec((tm, d), lambda i, f: (i, 0)),
        out_shape=jax.ShapeDtypeStruct((m, d), F32),
        scratch_shapes=[pltpu.VMEM((tm, d), F32)],
        compiler_params=_params("parallel", "arbitrary"),
        name="ffn_block",
    )(x, h, wu, wd, g2.reshape(1, d))


def _online_softmax_steps(carries, scores, vt_tiles, masks):
    scores = [s if mask is None else jnp.where(mask, s, -jnp.inf) for s, mask in zip(scores, masks)]
    out = []
    for (m_prev, l_prev, acc_prev), s, vt in zip(carries, scores, vt_tiles):
        m_new = jnp.maximum(m_prev, jnp.max(s, axis=0, keepdims=True))
        alpha = jnp.exp2(m_prev - m_new)
        p = jnp.exp2(s - m_new)
        out.append((m_new, alpha * l_prev + jnp.sum(p, axis=0, keepdims=True), alpha * acc_prev + _dot(vt, p.astype(BF16))))
    return tuple(out)


def _softmax_init(cols, dv):
    return (jnp.full((1, cols), MASK_VALUE, F32), jnp.zeros((1, cols), F32), jnp.zeros((dv, cols), F32))


def _softmax_merge(a, b):
    (ma, la, acca), (mb, lb, accb) = a, b
    m = jnp.maximum(ma, mb)
    wa, wb = jnp.exp2(ma - m), jnp.exp2(mb - m)
    return m, la * wa + lb * wb, acca * wa + accb * wb


def _softmax_finish(carry):
    _, l, acc = carry
    return acc / jnp.where(l > 0.0, l, 1.0)


def _mla_prep_kernel(ql_ref, kvl_ref, krl_ref, gq_ref, gkv_ref, wq_ref, wkv_ref, cos_ref, sin_ref,
                     qn_ref, qr_ref, kn_ref, vt_ref, kr_ref, *, scale):
    nq = _rms(ql_ref[0].astype(F32), gq_ref[...]).astype(BF16)
    nkv = _rms(kvl_ref[0].astype(F32), gkv_ref[...]).astype(BF16)
    q = _dot(nq, wq_ref[...])
    kv = _dot(nkv, wkv_ref[...])
    cos, sin = cos_ref[...], sin_ref[...]
    hw = MLA_HEADS * MLA_NOPE_DIM
    half = MLA_ROPE_DIM // 2
    qn_ref[0] = (q[:, :hw] * scale).astype(BF16)
    for h in range(MLA_HEADS):
        qr_ref[0, :, h * 128:(h + 1) * 128] = (_rope(q[:, hw + h * 128: hw + (h + 1) * 128], cos, sin, half) * scale).astype(BF16)
    kn_ref[0] = kv[:, :hw].astype(BF16)
    vt_ref[0] = kv[:, hw:].T.astype(BF16)
    kr_ref[0] = _rope(krl_ref[0].astype(F32), cos, sin, half).astype(BF16)


def _mla_attn_kernel(qn_ref, qr_ref, kn_ref, kr_ref, vt_ref, o_ref, *, tq, tk):
    i = pl.program_id(1)
    hs = [slice(h * 128, (h + 1) * 128) for h in range(MLA_HEADS)]
    q = [jnp.concatenate([qn_ref[0, :, sl], qr_ref[0, :, sl]], axis=1) for sl in hs]

    def tile(t, carries, mask):
        k0 = pl.multiple_of(t * tk, tk)
        kr = kr_ref[0, pl.ds(k0, tk), :]
        scores = [_dot_nt(jnp.concatenate([kn_ref[0, pl.ds(k0, tk), sl], kr], axis=1), q[h])
                  for h, sl in enumerate(hs)]
        return _online_softmax_steps(carries, scores, [vt_ref[0, sl, pl.ds(k0, tk)] for sl in hs], [mask] * len(hs))

    init = tuple(_softmax_init(tq, MLA_V_DIM) for _ in hs)
    t_diag = (i * tq) // tk
    carries = lax.fori_loop(0, t_diag, lambda t, c: tile(t, c, None), init)
    causal = (t_diag * tk + lax.broadcasted_iota(jnp.int32, (tk, tq), 0)
              <= i * tq + lax.broadcasted_iota(jnp.int32, (tk, tq), 1))
    for sl, carry in zip(hs, tile(t_diag, carries, causal)):
        o_ref[0, :, sl] = _softmax_finish(carry).T.astype(o_ref.dtype)


def mla_mixer(p, cols, g_q, g_kv, wq, wkv, cos, sin, *, tm=512, tq=256, tk=512):
    b, s, _ = p.shape
    tm, tq, tk = min(tm, s), min(tq, s), min(tk, s)
    col_q, col_kv, col_kr = cols
    assert tk % tq == 0 and col_q % MLA_Q_RANK == 0 and col_kv % 128 == 0 and col_kr % 128 == 0
    hw = MLA_HEADS * 128
    scale = (MLA_NOPE_DIM + MLA_ROPE_DIM) ** -0.5 * LOG2_E
    full = lambda shape: pl.BlockSpec(shape, lambda bi, i: (0,) * len(shape))
    row = lambda w: pl.BlockSpec((1, tm, w), lambda bi, i: (bi, i, 0))
    outs = pl.pallas_call(
        functools.partial(_mla_prep_kernel, scale=scale),
        grid=(b, s // tm),
        in_specs=[pl.BlockSpec((1, tm, MLA_Q_RANK), lambda bi, i: (bi, i, col_q // MLA_Q_RANK)),
                  pl.BlockSpec((1, tm, 128), lambda bi, i: (bi, i, col_kv // 128)),
                  pl.BlockSpec((1, tm, 128), lambda bi, i: (bi, i, col_kr // 128)),
                  full((1, MLA_Q_RANK)), full((1, MLA_KV_RANK)), full(wq.shape), full(wkv.shape),
                  pl.BlockSpec((tm, 128), lambda bi, i: (i, 0)), pl.BlockSpec((tm, 128), lambda bi, i: (i, 0))],
        out_specs=[row(hw), row(hw), row(hw), pl.BlockSpec((1, hw, tm), lambda bi, i: (bi, 0, i)), row(128)],
        out_shape=[jax.ShapeDtypeStruct((b, s, hw), BF16)] * 3 + [jax.ShapeDtypeStruct((b, hw, s), BF16),
                                                                   jax.ShapeDtypeStruct((b, s, 128), BF16)],
        compiler_params=_params("parallel", "parallel"),
        name="mla_prep",
    )(p, p, p, g_q.reshape(1, -1), g_kv.reshape(1, -1), wq, wkv, cos, sin)
    qn, qr, kn, vt, kr = outs
    q_spec = pl.BlockSpec((1, tq, hw), lambda bi, i: (bi, i, 0))
    per_batch = lambda shape: pl.BlockSpec((1,) + shape, lambda bi, i: (bi, 0, 0))
    return pl.pallas_call(
        functools.partial(_mla_attn_kernel, tq=tq, tk=tk),
        grid=(b, s // tq),
        in_specs=[q_spec, q_spec, per_batch((s, hw)), per_batch((s, 128)), per_batch((hw, s))],
        out_specs=q_spec,
        out_shape=jax.ShapeDtypeStruct((b, s, hw), BF16),
        compiler_params=_params("parallel", "arbitrary"),
        name="mla_attn",
    )(qn, qr, kn, kr, vt)


def _nsa_prep_kernel(p_ref, cos_ref, sin_ref, q_ref, kc_ref, vc_ref, ks_ref, kw_ref, vst_ref, vwt_ref, *, scale):
    cos, sin = cos_ref[...], sin_ref[...]
    hw = NSA_HEADS * NSA_HEAD_DIM
    half = NSA_ROT_DIM // 2
    tile = lambda col: p_ref[0, :, col * 128:(col + 1) * 128].astype(F32)
    for h in range(NSA_HEADS):
        q_ref[0, :, h * 128:(h + 1) * 128] = (_rope(tile(h), cos, sin, half) * scale).astype(BF16)
    base = hw // 128
    kc_ref[0] = _rope(tile(base), cos, sin, half)
    vc_ref[0] = tile(base + 1)
    ks_ref[0] = _rope(tile(base + 2), cos, sin, half).astype(BF16)
    kw_ref[0] = _rope(tile(base + 4), cos, sin, half).astype(BF16)
    vst_ref[0] = tile(base + 3).T.astype(BF16)
    vwt_ref[0] = tile(base + 5).T.astype(BF16)


def _gelu_tanh(x):
    return 0.5 * x * (1.0 + jnp.tanh(np.sqrt(2.0 / np.pi).astype(np.float32) * (x + 0.044715 * (x * x * x))))


def _split3_bf16(x):
    hi = x.astype(BF16)
    r1 = x - hi.astype(F32)
    mid = r1.astype(BF16)
    return hi, mid, (r1 - mid.astype(F32)).astype(BF16)


def _split2_bf16(x):
    hi = x.astype(BF16)
    return hi, (x - hi.astype(F32)).astype(BF16)


def _dot_split(a, w2_ref):
    hi, lo = _split2_bf16(a)
    return _dot(hi, w2_ref[0]) + (_dot(hi, w2_ref[1]) + _dot(lo, w2_ref[0]))


def _nsa_compress_kernel(k_ref, v_ref, pos_ref, w1_ref, w2_ref, kc3_ref, vct_ref):
    nrow = k_ref.shape[1]
    row = lax.broadcasted_iota(jnp.int32, (nrow, NSA_HEAD_DIM), 0)
    outs = []
    for z, x_ref in enumerate((k_ref, v_ref)):
        res = _dot(x_ref[0], w1_ref[z], HIGHEST)
        pb = _dot(pos_ref[z], w1_ref[z], HIGHEST)
        bias = pb[0:1, :NSA_HEAD_DIM] + pb[1:2, NSA_HEAD_DIM:]
        nxt = pltpu.roll(res[:, NSA_HEAD_DIM:], nrow - 1, axis=0)
        hid = _gelu_tanh(res[:, :NSA_HEAD_DIM] + nxt + bias)
        outs.append(jnp.where(row < nrow - 1, _dot(hid, w2_ref[z], HIGHEST), 0.0))
    for z, piece in enumerate(_split3_bf16(outs[0])):
        kc3_ref[0, z] = piece
    vct_ref[0] = outs[1].T.astype(BF16)


def _nsa_attn_kernel(q_ref, kc3_ref, vct_ref, covert_ref, ks_ref, vst_ref, kw_ref, vwt_ref, et_ref, g_ref, o_ref,
                     *, tq, tk, n_blk):
    i = pl.program_id(1)
    q0 = i * tq
    cols = NSA_HEADS * tq
    q4 = jnp.concatenate([q_ref[0, :, h * 128:(h + 1) * 128] for h in range(NSA_HEADS)], axis=0)

    ncmp = kc3_ref.shape[2]
    s = sum(_dot_nt(kc3_ref[0, z], q4) for z in range(3))
    n = lax.broadcasted_iota(jnp.int32, (ncmp, cols), 0)
    qpos_c = q0 + (lax.broadcasted_iota(jnp.int32, (ncmp, cols), 1) & (tq - 1))
    s = jnp.where((n * NSA_CMP_STRIDE + (NSA_CMP_LEN - 1) <= qpos_c) & (n < ncmp - 1), s, -jnp.inf)
    e = jnp.exp2(s - jnp.maximum(jnp.max(s, axis=0, keepdims=True), MASK_VALUE))
    l = jnp.sum(e, axis=0, keepdims=True)
    p = e / jnp.where(l > 0.0, l, 1.0)
    o_cmp = _dot(vct_ref[0], p.astype(BF16))
    psum = sum(p[:, h * tq:(h + 1) * tq] for h in range(NSA_HEADS))
    imp = sum(_dot(covert_ref[...], piece) for piece in _split3_bf16(psum))

    nb = covert_ref.shape[0]
    jb = lax.broadcasted_iota(jnp.int32, (nb, tq), 0)
    cur = (q0 + lax.broadcasted_iota(jnp.int32, (nb, tq), 1)) // NSA_SEL_BLOCK
    forced = (jb == 0) | (jb == cur) | (jb == cur - 1)
    visible = jb <= cur
    score = jnp.where(visible, imp + jnp.where(forced, NSA_FORCED_SCORE, 0.0), -jnp.inf)
    beaten = jnp.zeros((nb, tq), F32)
    for j in range(n_blk):
        row = score[j:j + 1, :]
        beaten = beaten + jnp.where((row > score) | ((row == score) & (jb > j)), 1.0, 0.0)
    keep = (beaten < float(NSA_TOP_N)) & visible
    sel = jnp.concatenate([jnp.where(keep, 0.0, MASK_VALUE), jnp.full((128 - nb, tq), MASK_VALUE, F32)], axis=0).astype(BF16)

    qh = [q_ref[0, :, h * 128:(h + 1) * 128] for h in range(NSA_HEADS)]
    kio = lax.broadcasted_iota(jnp.int32, (tk, tq), 0)
    qpos = q0 + lax.broadcasted_iota(jnp.int32, (tk, tq), 1)
    init = tuple(_softmax_init(tq, NSA_HEAD_DIM) for _ in qh)

    def slc_scores(k0):
        k = ks_ref[0, pl.ds(k0, tk), :]
        bias = _dot(et_ref[pl.ds(k0, tk), :], sel)
        return [_dot_nt(k, q) + bias for q in qh]

    def tile(t, carries, causal, window):
        k0 = pl.multiple_of(t * tk, tk)
        scores = slc_scores(k0)
        vts = [vst_ref[0, :, pl.ds(k0, tk)]] * NSA_HEADS
        masks = [(k0 + kio <= qpos) if causal else None] * NSA_HEADS
        if window:
            k = kw_ref[0, pl.ds(k0, tk), :]
            dist = qpos - (k0 + kio)
            scores += [_dot_nt(k, q) for q in qh]
            vts += [vwt_ref[0, :, pl.ds(k0, tk)]] * NSA_HEADS
            masks += [(dist >= 0) & (dist < NSA_WINDOW)] * NSA_HEADS
        return _online_softmax_steps(carries, scores, vts, masks)

    def tile_pair(gi, carries):
        k0 = pl.multiple_of(gi * (2 * tk), 2 * tk)
        scores, vts = [], []
        for half in range(2):
            scores += slc_scores(k0 + half * tk)
            vts += [vst_ref[0, :, pl.ds(k0 + half * tk, tk)]] * NSA_HEADS
        return _online_softmax_steps(carries, scores, vts, [None] * (2 * NSA_HEADS))

    t_diag = q0 // tk
    t_lo = jnp.maximum(q0 - NSA_WINDOW + 1, 0) // tk
    pairs = lax.fori_loop(0, t_lo // 2, tile_pair, init + init)
    o_slc = tuple(_softmax_merge(a, b) for a, b in zip(pairs[:NSA_HEADS], pairs[NSA_HEADS:]))
    o_slc = lax.fori_loop(2 * (t_lo // 2), t_lo, lambda t, c: tile(t, c, False, False), o_slc)
    both = lax.fori_loop(t_lo, t_diag, lambda t, c: tile(t, c, False, True), o_slc + init)
    both = tile(t_diag, both, True, True)
    o_slc, o_win = both[:NSA_HEADS], both[NSA_HEADS:]

    g = jax.nn.sigmoid(g_ref[0].astype(F32).T)
    for h in range(NSA_HEADS):
        o = (g[3 * h:3 * h + 1] * o_cmp[:, h * tq:(h + 1) * tq] + g[3 * h + 1:3 * h + 2] * _softmax_finish(o_slc[h])
             + g[3 * h + 2:3 * h + 3] * _softmax_finish(o_win[h]))
        o_ref[0, :, h * 128:(h + 1) * 128] = o.T.astype(o_ref.dtype)


def nsa_mixer(p, col0, pos_emb, w1, w2, cos, sin, *, tm=512, tq=128, tk=256):
    b, s, _ = p.shape
    assert col0 % NSA_IN == 0
    tm, tk = min(tm, s), min(tk, s)
    hw = NSA_HEADS * NSA_HEAD_DIM
    scale = NSA_HEAD_DIM ** -0.5 * LOG2_E
    row = lambda w: pl.BlockSpec((1, tm, w), lambda bi, i: (bi, i, 0))
    tab = pl.BlockSpec((tm, 128), lambda bi, i: (i, 0))
    tr_spec = pl.BlockSpec((1, 128, tm), lambda bi, i: (bi, 0, i))
    q, kc_in, vc_in, ks, kw, vst, vwt = pl.pallas_call(
        functools.partial(_nsa_prep_kernel, scale=scale),
        grid=(b, s // tm),
        in_specs=[pl.BlockSpec((1, tm, NSA_IN), lambda bi, i: (bi, i, col0 // NSA_IN)), tab, tab],
        out_specs=[row(hw), row(128), row(128), row(128), row(128), tr_spec, tr_spec],
        out_shape=[jax.ShapeDtypeStruct((b, s, hw), BF16), jax.ShapeDtypeStruct((b, s, 128), F32),
                   jax.ShapeDtypeStruct((b, s, 128), F32), jax.ShapeDtypeStruct((b, s, 128), BF16),
                   jax.ShapeDtypeStruct((b, s, 128), BF16), jax.ShapeDtypeStruct((b, 128, s), BF16),
                   jax.ShapeDtypeStruct((b, 128, s), BF16)],
        compiler_params=_params("parallel", "parallel"),
        name="nsa_prep",
    )(p, cos, sin)

    nrow = s // NSA_CMP_STRIDE
    fw = NSA_CMP_STRIDE * NSA_HEAD_DIM
    half = w1.shape[1] // 2
    w1r = jnp.concatenate([w1[:, :half], w1[:, half:]], axis=2)
    pos2 = jnp.pad(pos_emb.reshape(2, 2, fw), ((0, 0), (0, 6), (0, 0)))
    per_b = lambda shape: pl.BlockSpec((1,) + shape, lambda bi: (bi, 0, 0))
    const3 = lambda shape: pl.BlockSpec(shape, lambda bi: (0, 0, 0))
    kc3, vct = pl.pallas_call(
        _nsa_compress_kernel,
        grid=(b,),
        in_specs=[per_b((nrow, fw)), per_b((nrow, fw)), const3(pos2.shape), const3(w1r.shape), const3(w2.shape)],
        out_specs=[pl.BlockSpec((1, 3, nrow, NSA_HEAD_DIM), lambda bi: (bi, 0, 0, 0)), per_b((NSA_HEAD_DIM, nrow))],
        out_shape=[jax.ShapeDtypeStruct((b, 3, nrow, NSA_HEAD_DIM), BF16), jax.ShapeDtypeStruct((b, NSA_HEAD_DIM, nrow), BF16)],
        compiler_params=_params("parallel"),
        name="nsa_compress",
    )(kc_in.reshape(b, nrow, fw), vc_in.reshape(b, nrow, fw), pos2, w1r, w2)

    n_blk = s // NSA_SEL_BLOCK
    nb = 64
    assert n_blk <= nb and tq == 128 and tk % tq == 0
    cmp_start = np.arange(nrow) * NSA_CMP_STRIDE
    cmp_end = cmp_start + NSA_CMP_LEN - 1
    blk_start = np.arange(nb) * NSA_SEL_BLOCK
    cover_t = ((cmp_start[None, :] <= blk_start[:, None] + NSA_SEL_BLOCK - 1)
               & (cmp_end[None, :] >= blk_start[:, None]) & (np.arange(nb)[:, None] < n_blk)
               & (np.arange(nrow)[None, :] < nrow - 1)).astype(np.float32)
    expand_t = ((np.arange(s)[:, None] // NSA_SEL_BLOCK) == np.arange(128)[None, :]).astype(np.float32)
    qrow = lambda w: pl.BlockSpec((1, tq, w), lambda bi, i: (bi, i, 0))
    k_spec = pl.BlockSpec((1, s, 128), lambda bi, i: (bi, 0, 0))
    vt_spec = pl.BlockSpec((1, 128, s), lambda bi, i: (bi, 0, 0))
    gate_block = (col0 + NSA_IN) // 128 - 1
    return pl.pallas_call(
        functools.partial(_nsa_attn_kernel, tq=tq, tk=tk, n_blk=n_blk),
        grid=(b, s // tq),
        in_specs=[qrow(hw), pl.BlockSpec((1, 3, nrow, NSA_HEAD_DIM), lambda bi, i: (bi, 0, 0, 0)),
                  pl.BlockSpec((1, NSA_HEAD_DIM, nrow), lambda bi, i: (bi, 0, 0)),
                  pl.BlockSpec((nb, nrow), lambda bi, i: (0, 0)),
                  k_spec, vt_spec, k_spec, vt_spec, pl.BlockSpec((s, 128), lambda bi, i: (0, 0)),
                  pl.BlockSpec((1, tq, 128), lambda bi, i: (bi, i, gate_block))],
        out_specs=qrow(hw),
        out_shape=jax.ShapeDtypeStruct((b, s, hw), BF16),
        compiler_params=_params("parallel", "arbitrary"),
        name="nsa_attn",
    )(q, kc3, vct, jnp.asarray(cover_t, dtype=BF16), ks, vst, kw, vwt, jnp.asarray(expand_t, dtype=BF16), p)


RWKV_Z_PIECES = 4


def _rwkv_prep_kernel(*refs):
    z_refs, zp_refs = refs[:RWKV_Z_PIECES], refs[RWKV_Z_PIECES:2 * RWKV_Z_PIECES]
    (mu_ref, w0_ref, a0_ref, kk_ref, ka_ref, rk_ref, w2_ref, a2_ref, g2_ref, bd_ref, csum_ref, kt_ref, rt_ref, kh_ref,
     bh_ref, kbar_ref, bbar_ref, v_ref, etot_ref, g_ref, bonus_ref) = refs[2 * RWKV_Z_PIECES:]
    i = pl.program_id(1)
    z = jnp.concatenate([r[0] for r in z_refs], axis=1).astype(F32)
    tm = z.shape[0]
    last = zp_refs[0].shape[1] - 1
    prev_row = jnp.concatenate([r[0, last:, :] for r in zp_refs], axis=1).astype(F32)
    prev_row = jnp.where(i > 0, prev_row, 0.0)
    rowid = lax.broadcasted_iota(jnp.int32, z.shape, 0)
    z_prev = jnp.where(rowid == 0, prev_row, pltpu.roll(z, 1, axis=0))
    z = z + (z_prev - z) * mu_ref[...]
    c = BRANCH_WIDTH
    r, k, v = z[:, :c], z[:, c:2 * c], z[:, 2 * c:3 * c]
    wd, ad, gd = z[:, 3 * c:3 * c + 128], z[:, 3 * c + 128:3 * c + 256], z[:, 3 * c + 256:3 * c + 512]
    wpre = -(w0_ref[...] + _dot_split(jnp.tanh(wd), w2_ref))
    w_log = -(jnp.maximum(wpre, 0.0) + jnp.log1p(jnp.exp(-jnp.abs(wpre)))) - 0.5
    lw = -jnp.exp(w_log)
    a = jax.nn.sigmoid(a0_ref[...] + _dot_split(ad, a2_ref))
    g_ref[0] = _dot_split(jax.nn.sigmoid(gd), g2_ref)
    kk = k * kk_ref[...]
    head_sum = lambda t: sum(_dot(piece, bd_ref[...]) for piece in _split2_bf16(t))
    kk = kk * lax.rsqrt(jnp.maximum(head_sum(kk * kk), 1e-24))
    k = k * (1.0 + (a - 1.0) * ka_ref[...])
    bonus_ref[0] = head_sum(r * k * rk_ref[...]) * v
    sums = sum(_dot(csum_ref[...], piece) for piece in _split3_bf16(lw))
    cum, total = sums[:tm], sums[tm:]
    dec_out, dec_end = jnp.exp(-cum), jnp.exp(total - cum)
    b = kk * a
    kt_ref[0] = (kk * jnp.exp(cum - lw)).astype(BF16)
    rt_ref[0] = (r * jnp.exp(cum)).astype(BF16)
    kh_ref[0] = (k * dec_out).astype(BF16)
    bh_ref[0] = (b * dec_out).astype(BF16)
    kbar_ref[0] = (k * dec_end).astype(BF16)
    bbar_ref[0] = (b * dec_end).astype(BF16)
    v_ref[0] = v.astype(BF16)
    etot_ref[0] = jnp.exp(total)


def _rwkv_scan_kernel(kt_ref, rt_ref, kh_ref, bh_ref, kbar_ref, bbar_ref, v_ref, etot_ref, g_ref, bonus_ref,
                      gw_ref, gb_ref, y_ref, state_ref, *, L):
    n_chunks = kt_ref.shape[1] // L
    N = RWKV_HEAD_DIM
    n_pairs = kt_ref.shape[2] // 128

    @pl.when(pl.program_id(1) == 0)
    def _():
        state_ref[...] = jnp.zeros_like(state_ref)

    ri = lax.broadcasted_iota(jnp.int32, (L, 2 * L), 0)
    ci = lax.broadcasted_iota(jnp.int32, (L, 2 * L), 1) & (L - 1)
    strict2, incl2 = ci < ri, ci <= ri
    eye = jnp.where(lax.broadcasted_iota(jnp.int32, (L, L), 0) == lax.broadcasted_iota(jnp.int32, (L, L), 1), 1.0, 0.0)
    first2 = lax.broadcasted_iota(jnp.int32, (2 * L, 128), 1) < N
    first = lax.broadcasted_iota(jnp.int32, (L, 128), 1) < N
    bd = (lax.broadcasted_iota(jnp.int32, (128, 128), 0) < N) == (lax.broadcasted_iota(jnp.int32, (128, 128), 1) < N)
    zero = jnp.zeros((), BF16)

    items = range(n_chunks * n_pairs)
    block = lambda ref, i: ref[0, (i // n_pairs) * L:(i // n_pairs + 1) * L, (i % n_pairs) * 128:(i % n_pairs + 1) * 128]
    xs_in = [jnp.concatenate([block(kt_ref, i), block(rt_ref, i)], axis=0) for i in items]
    ys_in = [jnp.concatenate([block(kh_ref, i), block(bh_ref, i)], axis=0) for i in items]
    gms = [_dot_nt(jnp.where(first2 if sub == 0 else ~first2, xs_in[i], zero), ys_in[i])
           for i in items for sub in range(2)]
    a_k = jnp.stack([jnp.where(strict2, gm[:L], 0.0) for gm in gms])
    a_r = jnp.stack([jnp.where(incl2, gm[L:], 0.0) for gm in gms])
    a_kb = a_k[:, :, L:]
    bmm = lambda a, b: jnp.einsum("hij,hjk->hik", a.astype(BF16), b.astype(BF16), preferred_element_type=F32)
    tinv = eye[None] - a_kb
    pw = bmm(a_kb, a_kb)
    for _ in range(int(np.log2(L)) - 2):
        both = bmm(jnp.concatenate([tinv, pw], axis=1), pw)
        tinv = tinv + both[:, :L]
        pw = both[:, L:]
    tinv = tinv + bmm(tinv, pw)

    pick = lambda t: jnp.where(first, t[:L], t[L:])
    hsum = lambda t: jnp.where(first, jnp.sum(jnp.where(first, t, 0.0), axis=-1, keepdims=True),
                               jnp.sum(jnp.where(first, 0.0, t), axis=-1, keepdims=True))
    states = [state_ref[p] for p in range(n_pairs)]
    sls = [slice(p * 128, (p + 1) * 128) for p in range(n_pairs)]
    pairs = range(n_pairs)
    stack2 = lambda t, i: jnp.concatenate([t[2 * i], t[2 * i + 1]], axis=0).astype(BF16)
    vs = [block(v_ref, i) for i in items]
    akkv = [pick(_dot(stack2(a_k[:, :, :L], i), vs[i])) for i in items]
    tk = [_dot(stack2(tinv, i), jnp.concatenate([xs_in[i][:L], akkv[i].astype(BF16)], axis=1)) for i in items]
    xm = [jnp.concatenate([pick(tk[i][:, :128]).astype(BF16), xs_in[i][L:]], axis=0) for i in items]
    u0 = [-pick(tk[i][:, 128:]) for i in items]
    for c in range(n_chunks):
        rs = slice(c * L, (c + 1) * L)
        i0 = c * n_pairs
        v = vs[i0:i0 + n_pairs]
        xs = [_dot_nt(xm[i0 + p], states[p].astype(BF16)) for p in pairs]
        vu = [jnp.concatenate([v[p], (u0[i0 + p] - xs[p][:L]).astype(BF16)], axis=0) for p in pairs]
        o = [xs[p][L:] + pick(_dot(stack2(a_r, i0 + p), vu[p])) for p in pairs]
        for p, sl in enumerate(sls):
            kb = jnp.concatenate([kbar_ref[0, rs, sl], bbar_ref[0, rs, sl]], axis=0)
            states[p] = states[p] * etot_ref[0, c * L:c * L + 1, sl] + jnp.where(bd, _dot_tn(vu[p], kb), 0.0)
        for p, sl in enumerate(sls):
            ctr = o[p] - hsum(o[p]) * (1.0 / N)
            on = ctr * lax.rsqrt(hsum(ctr * ctr) * (1.0 / N) + RWKV_GN_EPS)
            y_ref[0, rs, sl] = ((on * gw_ref[:, sl] + gb_ref[:, sl] + bonus_ref[0, rs, sl]) * g_ref[0, rs, sl]).astype(y_ref.dtype)
    for p in pairs:
        state_ref[p] = states[p]


def rwkv7_mixer(p, col0, mu, w0, w2, a0, a2, g2, k_k, k_a, r_k, gn_w, gn_b, *, tm=256):
    b, s, _ = p.shape
    zw = mu.shape[0]
    zh = zw // RWKV_Z_PIECES
    prev_rows = 16 if p.dtype == BF16 else 8
    assert col0 % zh == 0
    tm = min(tm, s)
    c, hd, nh = BRANCH_WIDTH, RWKV_HEAD_DIM, RWKV_HEADS
    L = min(RWKV_CHUNK, s)
    assert tm % L == 0 and L & (L - 1) == 0
    bd = jnp.asarray(np.kron(np.eye(nh, dtype=np.float32), np.ones((hd, hd), np.float32)), dtype=BF16)
    chunk = np.arange(tm) // L
    same = chunk[:, None] == chunk[None, :]
    csum = np.concatenate([same & (np.arange(tm)[None, :] <= np.arange(tm)[:, None]), same], axis=0)
    csum = jnp.asarray(csum.astype(np.float32), dtype=BF16)
    row = lambda w: pl.BlockSpec((1, tm, w), lambda bi, i: (bi, i, 0))
    vec = lambda w: pl.BlockSpec((1, w), lambda bi, i: (0, 0))
    mat = lambda shape: pl.BlockSpec(shape, lambda bi, i: (0,) * len(shape))
    w2, a2, g2 = (jnp.stack(_split2_bf16(w)) for w in (w2, a2, g2))
    outs = pl.pallas_call(
        _rwkv_prep_kernel,
        grid=(b, s // tm),
        in_specs=[pl.BlockSpec((1, tm, zh), functools.partial(lambda bi, i, j: (bi, i, j), j=col0 // zh + j))
                  for j in range(RWKV_Z_PIECES)]
        + [pl.BlockSpec((1, prev_rows, zh),
                        functools.partial(lambda bi, i, j: (bi, jnp.maximum(i * (tm // prev_rows) - 1, 0), j), j=col0 // zh + j))
           for j in range(RWKV_Z_PIECES)]
        + [vec(zw), vec(c), vec(c), vec(c), vec(c), vec(c), mat(w2.shape), mat(a2.shape), mat(g2.shape),
                  mat(bd.shape), mat(csum.shape)],
        out_specs=[row(c)] * 10,
        out_shape=[jax.ShapeDtypeStruct((b, s, c), BF16)] * 7 + [jax.ShapeDtypeStruct((b, s, c), F32)] * 3,
        compiler_params=_params("parallel", "parallel"),
        name="rwkv_prep",
    )(*([p] * (2 * RWKV_Z_PIECES)), mu.reshape(1, zw), w0.reshape(1, c), a0.reshape(1, c), k_k.reshape(1, c), k_a.reshape(1, c),
      r_k.reshape(1, c), w2, a2, g2, bd, csum)
    rows = min(RWKV_CHUNKS_PER_STEP * L, s)
    blk = pl.BlockSpec((1, rows, c), lambda bi, ci: (bi, ci, 0))
    gvec = pl.BlockSpec((1, c), lambda bi, ci: (0, 0))
    return pl.pallas_call(
        functools.partial(_rwkv_scan_kernel, L=L),
        grid=(b, s // rows),
        in_specs=[blk] * 10 + [gvec, gvec],
        out_specs=blk,
        out_shape=jax.ShapeDtypeStruct((b, s, c), BF16),
        scratch_shapes=[pltpu.VMEM((c // 128, 128, 128), F32)],
        compiler_params=_params("parallel", "arbitrary"),
        name="rwkv_scan",
    )(*outs, gn_w.reshape(1, c), gn_b.reshape(1, c))


def _retention_kernel(qk_ref, v_ref, g_ref, cos_ref, sin_ref, inner_ref, qd_ref, kd_ref, cd_ref, o_ref, state_ref):
    @pl.when(pl.program_id(1) == 0)
    def _():
        state_ref[...] = jnp.zeros_like(state_ref)

    C = qk_ref.shape[1]
    kw = RET_HEADS * RET_K_DIM
    cos, sin = cos_ref[...], sin_ref[...]
    rope = lambda t: jnp.concatenate(
        [_rope(t[:, j:j + 128], cos[:, j:j + 128], sin[:, j:j + 128], RET_K_DIM // 2, RET_K_DIM) for j in range(0, kw, 128)], axis=1)
    q = rope(qk_ref[0, :, :kw].astype(F32))
    k = rope(qk_ref[0, :, kw:].astype(F32)) * RET_K_DIM ** -0.5
    lane = lax.broadcasted_iota(jnp.int32, (C, 128), 1)
    heads = range(RET_HEADS)
    vsl = [slice(h * RET_V_DIM, (h + 1) * RET_V_DIM) for h in heads]
    own = lambda t, h: jnp.where((lane // RET_K_DIM) == h % 2, t[:, (h // 2) * 128:(h // 2 + 1) * 128], 0.0)
    qh = [own(q, h) for h in heads]
    kh = [own(k, h) for h in heads]
    vh = [v_ref[0, :, sl] for sl in vsl]
    st = [state_ref[h] for h in heads]
    scores = [_dot_nt(qh[h].astype(BF16), kh[h].astype(BF16)) * inner_ref[h] for h in heads]
    cross = [_dot((qh[h] * qd_ref[h]).astype(BF16), st[h].astype(BF16)) for h in heads]
    o = [_dot(scores[h].astype(BF16), vh[h]) + cross[h] for h in heads]
    for h in heads:
        state_ref[h] = st[h] * cd_ref[h, 0:1, :] + _dot_tn((kh[h] * kd_ref[h]).astype(BF16), vh[h])
    for h, sl in enumerate(vsl):
        gate = g_ref[0, :, sl].astype(F32)
        c = o[h] - jnp.mean(o[h], axis=-1, keepdims=True)
        y = c * lax.rsqrt(jnp.mean(c * c, axis=-1, keepdims=True) + NORM_EPS)
        o_ref[0, :, sl] = (y * (gate * jax.nn.sigmoid(gate))).astype(o_ref.dtype)


def retention_mixer(p, col0, cos, sin):
    b, s, _ = p.shape
    assert col0 % 512 == 0
    H, C = RET_HEADS, min(RET_CHUNK, s)
    log_gamma = jnp.log1p(-jnp.exp2(-5.0 - jnp.arange(H, dtype=F32)))
    n = jnp.arange(C, dtype=F32)
    dist = n[:, None] - n[None, :]
    inner = jnp.where(dist >= 0, jnp.exp(jnp.maximum(dist, 0.0) * log_gamma[:, None, None]), 0.0)
    q_decay = jnp.exp((n + 1.0) * log_gamma[:, None])
    k_decay = jnp.exp((C - 1.0 - n) * log_gamma[:, None])
    chunk_decay = jnp.exp(C * log_gamma)
    lanes = lambda t: jnp.broadcast_to(t[:, :, None], (H, C, 128))
    const = lambda shape: pl.BlockSpec(shape, lambda bi, ci: (0,) * len(shape))
    kw = RET_HEADS * RET_K_DIM
    return pl.pallas_call(
        _retention_kernel,
        grid=(b, s // C),
        in_specs=[pl.BlockSpec((1, C, 512), functools.partial(lambda bi, ci, j: (bi, ci, j), j=col0 // 512 + j))
                  for j in range(3)] + [
                  pl.BlockSpec((C, kw), lambda bi, ci: (ci, 0)), pl.BlockSpec((C, kw), lambda bi, ci: (ci, 0)),
                  const((H, C, C)), const((H, C, 128)), const((H, C, 128)), const((H, 8, 128))],
        out_specs=pl.BlockSpec((1, C, H * RET_V_DIM), lambda bi, ci: (bi, ci, 0)),
        out_shape=jax.ShapeDtypeStruct((b, s, H * RET_V_DIM), BF16),
        scratch_shapes=[pltpu.VMEM((H, 128, 128), F32)],
        compiler_params=_params("parallel", "arbitrary"),
        name="retention",
    )(p, p, p, cos, sin, inner, lanes(q_decay), lanes(k_decay), jnp.broadcast_to(chunk_decay[:, None, None], (H, 8, 128)))


def _pad_cols(w, width):
    return jnp.pad(w, [(0, 0)] * (w.ndim - 1) + [(0, width - w.shape[-1])])


def _rope_tables(s, inv_freq, width, fill_cos):
    inv_freq = np.asarray(inv_freq, np.float32)
    ang = np.arange(s, dtype=np.float32)[:, None] * inv_freq[None, :]
    cos, sin = np.cos(ang).astype(np.float32), np.sin(ang).astype(np.float32)
    rot = 2 * cos.shape[1]
    cos2 = np.concatenate([cos, cos, np.full((s, width - rot), fill_cos, np.float32)], axis=1)
    sin2 = np.concatenate([sin, sin, np.zeros((s, width - rot), np.float32)], axis=1)
    return cos2, sin2


def _inv_freq(rot_dim, theta):
    return np.float32(theta) ** (-np.arange(0, rot_dim, 2, dtype=np.float32) / np.float32(rot_dim))


def _split_offsets():
    sizes = (MLA_Q_RANK, MLA_KV_RANK, MLA_ROPE_DIM, 512, 128, 128, 128, 128, 128, 128, 12,
             3 * BRANCH_WIDTH + RWKV_DECAY_LORA + RWKV_A_LORA + RWKV_GATE_LORA, 256, 256, 512, 512, N_MIXERS * D_MODEL)
    offs = np.concatenate([[0], np.cumsum(sizes)])
    return [(int(offs[i]), int(offs[i + 1])) for i in range(len(sizes))]


def _pack_in_proj(w):
    w = w.astype(BF16)
    sl = [w[..., a:b] for a, b in _split_offsets()]
    (ql, kvl, kr, nq, nkc, nvc, nks, nvs, nkw, nvw, ng, rz, rq, rk, rv, rg, mg) = sl
    c = BRANCH_WIDTH
    w_rwkv = [rz[..., :3 * c], _pad_cols(rz[..., 3 * c:3 * c + 96], 128), _pad_cols(rz[..., 3 * c + 96:3 * c + 192], 128),
              rz[..., 3 * c + 192:]]
    groups = {"nsa": [nq, nkc, nvc, nks, nvs, nkw, nvw, _pad_cols(ng, 128)], "mla_kv": [kvl], "mla_q": [ql],
              "mla_kr": [_pad_cols(kr, 128)], "ret": [rq, rk, rv, rg], "gate": [mg], "rwkv": w_rwkv}
    cols, parts, off = {}, [], 0
    for name, ws in groups.items():
        cols[name] = off
        parts += ws
        off += sum(t.shape[-1] for t in ws)
    return jnp.concatenate(parts, axis=-1), cols


def _pack_mla_weights(w_uq, w_ukv):
    H, dn, dr, dv = MLA_HEADS, MLA_NOPE_DIM, MLA_ROPE_DIM, MLA_V_DIM
    uq = w_uq.reshape(-1, H, dn + dr)
    nope = uq[:, :, :dn].reshape(-1, H * dn)
    rope = uq[:, :, dn:]
    rope_p = _pad_cols(rope, 128).reshape(-1, H * 128)
    ukv = w_ukv.reshape(-1, H, dn + dv)
    wkv = jnp.concatenate([ukv[:, :, :dn].reshape(-1, H * dn), ukv[:, :, dn:].reshape(-1, H * dv)], axis=1)
    return jnp.concatenate([nope, rope_p], axis=1).astype(BF16), wkv.astype(BF16)


def _pad_rows(w, rows):
    return jnp.pad(w, ((0, rows - w.shape[0]), (0, 0)))


def kernel(x, w_in, w_branch, w_out, w_up, w_down, norm_gains, mla_g_q, mla_g_kv, mla_w_uq, mla_w_ukv,
           nsa_cmp_pos, nsa_cmp_w1, nsa_cmp_w2, rwkv_mu, rwkv_w0, rwkv_w2, rwkv_a0, rwkv_a2, rwkv_g2,
           rwkv_k_k, rwkv_k_a, rwkv_r_k, rwkv_gn_w, rwkv_gn_b):
    B, S, D = x.shape
    depth = w_in.shape[0]
    T = B * S
    tm, tm_in = min(512, T), min(1024, T)
    mla_cos, mla_sin = _rope_tables(S, _inv_freq(MLA_ROPE_DIM, ROPE_THETA), 128, 0.0)
    nsa_cos, nsa_sin = _rope_tables(S, _inv_freq(NSA_ROT_DIM, ROPE_THETA), 128, 1.0)
    ret_cos, ret_sin = _rope_tables(S, np.float32(RET_THETA) ** (-np.linspace(0.0, 1.0, RET_K_DIM // 2, dtype=np.float32)),
                                    RET_K_DIM, 0.0)
    ret_cos, ret_sin = np.tile(ret_cos, (1, RET_HEADS)), np.tile(ret_sin, (1, RET_HEADS))
    c = BRANCH_WIDTH
    x = x.reshape(T, D)
    w_all, cols = _pack_in_proj(w_in)
    tn_in = w_all.shape[-1] // 9
    assert tn_in % 256 == 0 and tn_in * 9 == w_all.shape[-1]
    w_branch, w_out, w_up, w_down = (w.astype(BF16) for w in (w_branch, w_out, w_up, w_down))
    for l in range(depth):
        g_pre = norm_gains[l, 0]
        p2 = norm_matmul(x, g_pre, w_all, l, BF16, tm_in, tn_in)
        p = p2.reshape(B, S, -1)

        wq, wkv = _pack_mla_weights(mla_w_uq[l], mla_w_ukv[l])
        y_mla = mla_mixer(p, (cols["mla_q"], cols["mla_kv"], cols["mla_kr"]), mla_g_q[l], mla_g_kv[l], wq, wkv, mla_cos, mla_sin)
        y_nsa = nsa_mixer(p, cols["nsa"], nsa_cmp_pos[l], nsa_cmp_w1[l], nsa_cmp_w2[l], nsa_cos, nsa_sin)
        mu = rwkv_mu[l]
        mu_p = jnp.concatenate([mu[:3 * c], _pad_cols(mu[3 * c:3 * c + 96], 128), _pad_cols(mu[3 * c + 96:3 * c + 192], 128),
                                mu[3 * c + 192:]])
        y_rwkv = rwkv7_mixer(p, cols["rwkv"], mu_p, rwkv_w0[l], _pad_rows(rwkv_w2[l], 128), rwkv_a0[l],
                             _pad_rows(rwkv_a2[l], 128), rwkv_g2[l], rwkv_k_k[l], rwkv_k_a[l], rwkv_r_k[l].reshape(-1),
                             rwkv_gn_w[l], rwkv_gn_b[l])
        y_ret = retention_mixer(p, cols["ret"], ret_cos, ret_sin)

        ys = [y.reshape(T, c) for y in (y_mla, y_nsa, y_rwkv, y_ret)]
        x, h = merge_out_residual(ys, p2, cols["gate"], w_branch, w_out, l, norm_gains[l, 1], norm_gains[l, 2], x, min(256, T))
        x = ffn_block(x, h, w_up, w_down, l, norm_gains[l, 3], tm, 1024)
    return x.reshape(B, S, D)
```

```python
import functools

import numpy as np
import jax
import jax.numpy as jnp
from jax import lax
from jax.experimental import pallas as pl
from jax.experimental.pallas import tpu as pltpu

F32 = jnp.float32
BF16 = jnp.bfloat16
HIGHEST = lax.Precision.HIGHEST

V7X_LANES = 128
V7X_VMEM_LIMIT_BYTES = 56 * 1024 * 1024

D_MODEL = 2048
N_MIXERS = 4
BRANCH_WIDTH = D_MODEL // N_MIXERS
ROPE_THETA = 500000.0
NORM_EPS = 1e-6
MASK_VALUE = -1e30
LOG2_E = float(np.log2(np.e))

MLA_NOPE_DIM = 128
MLA_ROPE_DIM = 64
MLA_V_DIM = 128
MLA_HEADS = 4
MLA_Q_RANK = 384
MLA_KV_RANK = 128

NSA_HEAD_DIM = 128
NSA_HEADS = 4
NSA_ROT_DIM = 32
NSA_CMP_LEN = 32
NSA_CMP_STRIDE = 16
NSA_SEL_BLOCK = 64
NSA_TOP_N = 16
NSA_WINDOW = 512
NSA_FORCED_SCORE = 1000.0
NSA_IN = NSA_HEADS * NSA_HEAD_DIM + 7 * 128

RWKV_HEAD_DIM = 64
RWKV_HEADS = 8
RWKV_DECAY_LORA = 96
RWKV_A_LORA = 96
RWKV_GATE_LORA = 256
RWKV_GN_EPS = 64e-5
RWKV_CHUNK = 64
RWKV_CHUNKS_PER_STEP = 4

RET_HEADS = 4
RET_V_DIM = 128
RET_K_DIM = 64
RET_CHUNK = 128
RET_THETA = 10000.0


def _params(*sem):
    return pltpu.CompilerParams(dimension_semantics=sem, vmem_limit_bytes=V7X_VMEM_LIMIT_BYTES)


def _dot(a, b, precision=None):
    return jnp.dot(a, b, preferred_element_type=F32, precision=precision)


def _dot_nt(a, b, precision=None):
    return lax.dot_general(a, b, (((1,), (1,)), ((), ())), preferred_element_type=F32, precision=precision)


def _dot_tn(a, b, precision=None):
    return lax.dot_general(a, b, (((0,), (0,)), ((), ())), preferred_element_type=F32, precision=precision)


def _rotate_half(x, half, period):
    lane = lax.broadcasted_iota(jnp.int32, x.shape, x.ndim - 1) & (period - 1)
    up = pltpu.roll(x, V7X_LANES - half, axis=x.ndim - 1)
    dn = pltpu.roll(x, half, axis=x.ndim - 1)
    return jnp.where(lane < half, -up, jnp.where(lane < 2 * half, dn, 0.0))


def _rope(x, cos, sin, half, period=V7X_LANES):
    return x * cos + _rotate_half(x, half, period) * sin


def _rms(x, g, eps=NORM_EPS):
    return x * lax.rsqrt(jnp.mean(x * x, axis=-1, keepdims=True) + eps) * g


def _norm_matmul_kernel(x_ref, g_ref, w_ref, o_ref, h_ref):
    @pl.when(pl.program_id(1) == 0)
    def _():
        h_ref[...] = _rms(x_ref[...], g_ref[...]).astype(BF16)

    o_ref[...] = _dot(h_ref[...], w_ref[...]).astype(o_ref.dtype)


def norm_matmul(x, g, w, layer, out_dtype, tm, tn):
    m, k = x.shape
    n = w.shape[2]
    assert m % tm == 0 and n % tn == 0
    return pl.pallas_call(
        _norm_matmul_kernel,
        grid=(m // tm, n // tn),
        in_specs=[pl.BlockSpec((tm, k), lambda i, j: (i, 0)),
                  pl.BlockSpec((1, k), lambda i, j: (0, 0)),
                  pl.BlockSpec((None, k, tn), lambda i, j: (layer, 0, j))],
        out_specs=pl.BlockSpec((tm, tn), lambda i, j: (i, j)),
        out_shape=jax.ShapeDtypeStruct((m, n), out_dtype),
        scratch_shapes=[pltpu.VMEM((tm, k), BF16)],
        compiler_params=_params("parallel", "arbitrary"),
        name="norm_matmul",
    )(x, g.reshape(1, k), w)


MERGE_GATE_PIECES = 4


def _merge_out_kernel(*refs):
    n = MERGE_GATE_PIECES
    y_refs, g_refs = refs[:N_MIXERS], refs[N_MIXERS:(1 + n) * N_MIXERS]
    wb_ref, wo_ref, gn_ref, gf_ref, x_ref, o_ref, h_ref = refs[(1 + n) * N_MIXERS:]
    width = g_refs[0].shape[1]
    merged = []
    for piece in range(n):
        acc = None
        for m in range(N_MIXERS):
            gate = 0.5 * jnp.tanh(0.5 * g_refs[n * m + piece][...].astype(F32)) + 0.5
            term = gate * _dot(y_refs[m][...], wb_ref[m, :, piece * width:(piece + 1) * width])
            acc = term if acc is None else acc + term
        merged.append(acc.astype(BF16))
    x_new = x_ref[...] + _rms(_dot(jnp.concatenate(merged, axis=1), wo_ref[...]), gn_ref[...])
    o_ref[...] = x_new
    h_ref[...] = _rms(x_new, gf_ref[...]).astype(BF16)


def merge_out_residual(ys, gates, col0, wb, wo, layer, gn, g_ffn, x, tm):
    m, c = ys[0].shape
    d = wb.shape[3]
    half = d // MERGE_GATE_PIECES
    assert col0 % half == 0
    once = pl.Buffered(1)
    return pl.pallas_call(
        _merge_out_kernel,
        grid=(m // tm,),
        in_specs=[pl.BlockSpec((tm, c), lambda i: (i, 0))] * N_MIXERS
        + [pl.BlockSpec((tm, half), functools.partial(lambda i, j: (i, j), j=col0 // half + jj))
           for jj in range(MERGE_GATE_PIECES * N_MIXERS)]
        + [pl.BlockSpec((None, N_MIXERS, c, d), lambda i: (layer, 0, 0, 0), pipeline_mode=once),
           pl.BlockSpec((None, d, d), lambda i: (layer, 0, 0), pipeline_mode=once),
           pl.BlockSpec((1, d), lambda i: (0, 0)),
           pl.BlockSpec((1, d), lambda i: (0, 0)),
           pl.BlockSpec((tm, d), lambda i: (i, 0))],
        out_specs=[pl.BlockSpec((tm, d), lambda i: (i, 0))] * 2,
        out_shape=[jax.ShapeDtypeStruct((m, d), F32), jax.ShapeDtypeStruct((m, d), BF16)],
        compiler_params=_params("parallel"),
        name="merge_out_residual",
    )(*ys, *([gates] * (MERGE_GATE_PIECES * N_MIXERS)), wb, wo, gn.reshape(1, d), g_ffn.reshape(1, d), x)


def _ffn_kernel(x_ref, h_ref, wu_ref, wd_ref, g2_ref, o_ref, acc_ref):
    f = pl.program_id(1)

    @pl.when(f == 0)
    def _():
        acc_ref[...] = jnp.zeros_like(acc_ref)

    u = jnp.maximum(_dot(h_ref[...], wu_ref[...]), 0.0)
    acc_ref[...] += _dot((u * u).astype(BF16), wd_ref[...])

    @pl.when(f == pl.num_programs(1) - 1)
    def _():
        o_ref[...] = x_ref[...] + _rms(acc_ref[...], g2_ref[...])


def ffn_block(x, h, wu, wd, layer, g2, tm, tf):
    m, d = x.shape
    ff = wu.shape[2]
    return pl.pallas_call(
        _ffn_kernel,
        grid=(m // tm, ff // tf),
        in_specs=[pl.BlockSpec((tm, d), lambda i, f: (i, 0)),
                  pl.BlockSpec((tm, d), lambda i, f: (i, 0)),
                  pl.BlockSpec((None, d, tf), lambda i, f: (layer, 0, f)),
                  pl.BlockSpec((None, tf, d), lambda i, f: (layer, f, 0)),
                  pl.BlockSpec((1, d), lambda i, f: (0, 0))],
        out_specs=pl.BlockSpec((tm, d), lambda i, f: (i, 0)),
        out_shape=jax.ShapeDtypeStruct((m, d), F32),
        scratch_shapes=[pltpu.VMEM((tm, d), F32)],
        compiler_params=_params("parallel", "arbitrary"),
        name="ffn_block",
    )(x, h, wu, wd, g2.reshape(1, d))


def _online_softmax_steps(carries, scores, vt_tiles, masks):
    scores = [s if mask is None else jnp.where(mask, s, -jnp.inf) for s, mask in zip(scores, masks)]
    out = []
    for (m_prev, l_prev, acc_prev), s, vt in zip(carries, scores, vt_tiles):
        m_new = jnp.maximum(m_prev, jnp.max(s, axis=0, keepdims=True))
        alpha = jnp.exp2(m_prev - m_new)
        p = jnp.exp2(s - m_new)
        out.append((m_new, alpha * l_prev + jnp.sum(p, axis=0, keepdims=True), alpha * acc_prev + _dot(vt, p.astype(BF16))))
    return tuple(out)


def _softmax_init(cols, dv):
    return (jnp.full((1, cols), MASK_VALUE, F32), jnp.zeros((1, cols), F32), jnp.zeros((dv, cols), F32))


def _softmax_merge(a, b):
    (ma, la, acca), (mb, lb, accb) = a, b
    m = jnp.maximum(ma, mb)
    wa, wb = jnp.exp2(ma - m), jnp.exp2(mb - m)
    return m, la * wa + lb * wb, acca * wa + accb * wb


def _softmax_finish(carry):
    _, l, acc = carry
    return acc / jnp.where(l > 0.0, l, 1.0)


def _mla_prep_kernel(ql_ref, kvl_ref, krl_ref, gq_ref, gkv_ref, wq_ref, wkv_ref, cos_ref, sin_ref,
                     qn_ref, qr_ref, kn_ref, vt_ref, kr_ref, *, scale):
    nq = _rms(ql_ref[0].astype(F32), gq_ref[...]).astype(BF16)
    nkv = _rms(kvl_ref[0].astype(F32), gkv_ref[...]).astype(BF16)
    q = _dot(nq, wq_ref[...])
    kv = _dot(nkv, wkv_ref[...])
    cos, sin = cos_ref[...], sin_ref[...]
    hw = MLA_HEADS * MLA_NOPE_DIM
    half = MLA_ROPE_DIM // 2
    qn_ref[0] = (q[:, :hw] * scale).astype(BF16)
    for h in range(MLA_HEADS):
        qr_ref[0, :, h * 128:(h + 1) * 128] = (_rope(q[:, hw + h * 128: hw + (h + 1) * 128], cos, sin, half) * scale).astype(BF16)
    kn_ref[0] = kv[:, :hw].astype(BF16)
    vt_ref[0] = kv[:, hw:].T.astype(BF16)
    kr_ref[0] = _rope(krl_ref[0].astype(F32), cos, sin, half).astype(BF16)


def _mla_attn_kernel(qn_ref, qr_ref, kn_ref, kr_ref, vt_ref, o_ref, *, tq, tk):
    i = pl.program_id(1)
    hs = [slice(h * 128, (h + 1) * 128) for h in range(MLA_HEADS)]
    q = [jnp.concatenate([qn_ref[0, :, sl], qr_ref[0, :, sl]], axis=1) for sl in hs]

    def tile(t, carries, mask):
        k0 = pl.multiple_of(t * tk, tk)
        kr = kr_ref[0, pl.ds(k0, tk), :]
        scores = [_dot_nt(jnp.concatenate([kn_ref[0, pl.ds(k0, tk), sl], kr], axis=1), q[h])
                  for h, sl in enumerate(hs)]
        return _online_softmax_steps(carries, scores, [vt_ref[0, sl, pl.ds(k0, tk)] for sl in hs], [mask] * len(hs))

    init = tuple(_softmax_init(tq, MLA_V_DIM) for _ in hs)
    t_diag = (i * tq) // tk
    carries = lax.fori_loop(0, t_diag, lambda t, c: tile(t, c, None), init)
    causal = (t_diag * tk + lax.broadcasted_iota(jnp.int32, (tk, tq), 0)
              <= i * tq + lax.broadcasted_iota(jnp.int32, (tk, tq), 1))
    for sl, carry in zip(hs, tile(t_diag, carries, causal)):
        o_ref[0, :, sl] = _softmax_finish(carry).T.astype(o_ref.dtype)


def mla_mixer(p, cols, g_q, g_kv, wq, wkv, cos, sin, *, tm=512, tq=512, tk=512):
    b, s, _ = p.shape
    tm, tq, tk = min(tm, s), min(tq, s), min(tk, s)
    col_q, col_kv, col_kr = cols
    assert tk % tq == 0 and col_q % MLA_Q_RANK == 0 and col_kv % 128 == 0 and col_kr % 128 == 0
    hw = MLA_HEADS * 128
    scale = (MLA_NOPE_DIM + MLA_ROPE_DIM) ** -0.5 * LOG2_E
    full = lambda shape: pl.BlockSpec(shape, lambda bi, i: (0,) * len(shape))
    row = lambda w: pl.BlockSpec((1, tm, w), lambda bi, i: (bi, i, 0))
    outs = pl.pallas_call(
        functools.partial(_mla_prep_kernel, scale=scale),
        grid=(b, s // tm),
        in_specs=[pl.BlockSpec((1, tm, MLA_Q_RANK), lambda bi, i: (bi, i, col_q // MLA_Q_RANK)),
                  pl.BlockSpec((1, tm, 128), lambda bi, i: (bi, i, col_kv // 128)),
                  pl.BlockSpec((1, tm, 128), lambda bi, i: (bi, i, col_kr // 128)),
                  full((1, MLA_Q_RANK)), full((1, MLA_KV_RANK)), full(wq.shape), full(wkv.shape),
                  pl.BlockSpec((tm, 128), lambda bi, i: (i, 0)), pl.BlockSpec((tm, 128), lambda bi, i: (i, 0))],
        out_specs=[row(hw), row(hw), row(hw), pl.BlockSpec((1, hw, tm), lambda bi, i: (bi, 0, i)), row(128)],
        out_shape=[jax.ShapeDtypeStruct((b, s, hw), BF16)] * 3 + [jax.ShapeDtypeStruct((b, hw, s), BF16),
                                                                   jax.ShapeDtypeStruct((b, s, 128), BF16)],
        compiler_params=_params("parallel", "parallel"),
        name="mla_prep",
    )(p, p, p, g_q.reshape(1, -1), g_kv.reshape(1, -1), wq, wkv, cos, sin)
    qn, qr, kn, vt, kr = outs
    q_spec = pl.BlockSpec((1, tq, hw), lambda bi, i: (bi, i, 0))
    per_batch = lambda shape: pl.BlockSpec((1,) + shape, lambda bi, i: (bi, 0, 0))
    return pl.pallas_call(
        functools.partial(_mla_attn_kernel, tq=tq, tk=tk),
        grid=(b, s // tq),
        in_specs=[q_spec, q_spec, per_batch((s, hw)), per_batch((s, 128)), per_batch((hw, s))],
        out_specs=q_spec,
        out_shape=jax.ShapeDtypeStruct((b, s, hw), BF16),
        compiler_params=_params("parallel", "arbitrary"),
        name="mla_attn",
    )(qn, qr, kn, kr, vt)


def _nsa_prep_kernel(p_ref, cos_ref, sin_ref, q_ref, kc_ref, vc_ref, ks_ref, kw_ref, vst_ref, vwt_ref, *, scale):
    cos, sin = cos_ref[...], sin_ref[...]
    hw = NSA_HEADS * NSA_HEAD_DIM
    half = NSA_ROT_DIM // 2
    tile = lambda col: p_ref[0, :, col * 128:(col + 1) * 128].astype(F32)
    for h in range(NSA_HEADS):
        q_ref[0, :, h * 128:(h + 1) * 128] = (_rope(tile(h), cos, sin, half) * scale).astype(BF16)
    base = hw // 128
    kc_ref[0] = _rope(tile(base), cos, sin, half)
    vc_ref[0] = tile(base + 1)
    ks_ref[0] = _rope(tile(base + 2), cos, sin, half).astype(BF16)
    kw_ref[0] = _rope(tile(base + 4), cos, sin, half).astype(BF16)
    vst_ref[0] = tile(base + 3).T.astype(BF16)
    vwt_ref[0] = tile(base + 5).T.astype(BF16)


def _gelu_tanh(x):
    return 0.5 * x * (1.0 + jnp.tanh(np.sqrt(2.0 / np.pi).astype(np.float32) * (x + 0.044715 * (x * x * x))))


def _split3_bf16(x):
    hi = x.astype(BF16)
    r1 = x - hi.astype(F32)
    mid = r1.astype(BF16)
    return hi, mid, (r1 - mid.astype(F32)).astype(BF16)


def _split2_bf16(x):
    hi = x.astype(BF16)
    return hi, (x - hi.astype(F32)).astype(BF16)


def _dot_split(a, w2_ref):
    hi, lo = _split2_bf16(a)
    return _dot(hi, w2_ref[0]) + (_dot(hi, w2_ref[1]) + _dot(lo, w2_ref[0]))


def _nsa_compress_kernel(k_ref, v_ref, pos_ref, w1_ref, w2_ref, kc3_ref, vct_ref):
    nrow = k_ref.shape[1]
    row = lax.broadcasted_iota(jnp.int32, (nrow, NSA_HEAD_DIM), 0)
    outs = []
    for z, x_ref in enumerate((k_ref, v_ref)):
        res = _dot(x_ref[0], w1_ref[z], HIGHEST)
        pb = _dot(pos_ref[z], w1_ref[z], HIGHEST)
        bias = pb[0:1, :NSA_HEAD_DIM] + pb[1:2, NSA_HEAD_DIM:]
        nxt = pltpu.roll(res[:, NSA_HEAD_DIM:], nrow - 1, axis=0)
        hid = _gelu_tanh(res[:, :NSA_HEAD_DIM] + nxt + bias)
        outs.append(jnp.where(row < nrow - 1, _dot(hid, w2_ref[z], HIGHEST), 0.0))
    for z, piece in enumerate(_split3_bf16(outs[0])):
        kc3_ref[0, z] = piece
    vct_ref[0] = outs[1].T.astype(BF16)


def _nsa_attn_kernel(q_ref, kc3_ref, vct_ref, covert_ref, ks_ref, vst_ref, kw_ref, vwt_ref, et_ref, g_ref, o_ref,
                     *, tq, tk, n_blk):
    i = pl.program_id(1)
    q0 = i * tq
    cols = NSA_HEADS * tq
    q4 = jnp.concatenate([q_ref[0, :, h * 128:(h + 1) * 128] for h in range(NSA_HEADS)], axis=0)

    ncmp = kc3_ref.shape[2]
    s = sum(_dot_nt(kc3_ref[0, z], q4) for z in range(3))
    n = lax.broadcasted_iota(jnp.int32, (ncmp, cols), 0)
    qpos_c = q0 + (lax.broadcasted_iota(jnp.int32, (ncmp, cols), 1) & (tq - 1))
    s = jnp.where((n * NSA_CMP_STRIDE + (NSA_CMP_LEN - 1) <= qpos_c) & (n < ncmp - 1), s, -jnp.inf)
    e = jnp.exp2(s - jnp.maximum(jnp.max(s, axis=0, keepdims=True), MASK_VALUE))
    l = jnp.sum(e, axis=0, keepdims=True)
    p = e / jnp.where(l > 0.0, l, 1.0)
    o_cmp = _dot(vct_ref[0], p.astype(BF16))
    psum = sum(p[:, h * tq:(h + 1) * tq] for h in range(NSA_HEADS))
    imp = sum(_dot(covert_ref[...], piece) for piece in _split3_bf16(psum))

    nb = covert_ref.shape[0]
    jb = lax.broadcasted_iota(jnp.int32, (nb, tq), 0)
    cur = (q0 + lax.broadcasted_iota(jnp.int32, (nb, tq), 1)) // NSA_SEL_BLOCK
    forced = (jb == 0) | (jb == cur) | (jb == cur - 1)
    visible = jb <= cur
    score = jnp.where(visible, imp + jnp.where(forced, NSA_FORCED_SCORE, 0.0), -jnp.inf)
    beaten = jnp.zeros((nb, tq), F32)
    for j in range(n_blk):
        row = score[j:j + 1, :]
        beaten = beaten + jnp.where((row > score) | ((row == score) & (jb > j)), 1.0, 0.0)
    keep = (beaten < float(NSA_TOP_N)) & visible
    sel = jnp.concatenate([jnp.where(keep, 0.0, MASK_VALUE), jnp.full((128 - nb, tq), MASK_VALUE, F32)], axis=0).astype(BF16)

    qh = [q_ref[0, :, h * 128:(h + 1) * 128] for h in range(NSA_HEADS)]
    kio = lax.broadcasted_iota(jnp.int32, (tk, tq), 0)
    qpos = q0 + lax.broadcasted_iota(jnp.int32, (tk, tq), 1)
    init = tuple(_softmax_init(tq, NSA_HEAD_DIM) for _ in qh)

    def slc_scores(k0):
        k = ks_ref[0, pl.ds(k0, tk), :]
        bias = _dot(et_ref[pl.ds(k0, tk), :], sel)
        return [_dot_nt(k, q) + bias for q in qh]

    def tile(t, carries, causal, window):
        k0 = pl.multiple_of(t * tk, tk)
        scores = slc_scores(k0)
        vts = [vst_ref[0, :, pl.ds(k0, tk)]] * NSA_HEADS
        masks = [(k0 + kio <= qpos) if causal else None] * NSA_HEADS
        if window:
            k = kw_ref[0, pl.ds(k0, tk), :]
            dist = qpos - (k0 + kio)
            scores += [_dot_nt(k, q) for q in qh]
            vts += [vwt_ref[0, :, pl.ds(k0, tk)]] * NSA_HEADS
            masks += [(dist >= 0) & (dist < NSA_WINDOW)] * NSA_HEADS
        return _online_softmax_steps(carries, scores, vts, masks)

    def tile_pair(gi, carries):
        k0 = pl.multiple_of(gi * (2 * tk), 2 * tk)
        scores, vts = [], []
        for half in range(2):
            scores += slc_scores(k0 + half * tk)
            vts += [vst_ref[0, :, pl.ds(k0 + half * tk, tk)]] * NSA_HEADS
        return _online_softmax_steps(carries, scores, vts, [None] * (2 * NSA_HEADS))

    t_diag = q0 // tk
    t_lo = jnp.maximum(q0 - NSA_WINDOW + 1, 0) // tk
    pairs = lax.fori_loop(0, t_lo // 2, tile_pair, init + init)
    o_slc = tuple(_softmax_merge(a, b) for a, b in zip(pairs[:NSA_HEADS], pairs[NSA_HEADS:]))
    o_slc = lax.fori_loop(2 * (t_lo // 2), t_lo, lambda t, c: tile(t, c, False, False), o_slc)
    both = lax.fori_loop(t_lo, t_diag, lambda t, c: tile(t, c, False, True), o_slc + init)
    both = tile(t_diag, both, True, True)
    o_slc, o_win = both[:NSA_HEADS], both[NSA_HEADS:]

    g = jax.nn.sigmoid(g_ref[0].astype(F32).T)
    for h in range(NSA_HEADS):
        o = (g[3 * h:3 * h + 1] * o_cmp[:, h * tq:(h + 1) * tq] + g[3 * h + 1:3 * h + 2] * _softmax_finish(o_slc[h])
             + g[3 * h + 2:3 * h + 3] * _softmax_finish(o_win[h]))
        o_ref[0, :, h * 128:(h + 1) * 128] = o.T.astype(o_ref.dtype)


def nsa_mixer(p, col0, pos_emb, w1, w2, cos, sin, *, tm=512, tq=128, tk=256):
    b, s, _ = p.shape
    assert col0 % NSA_IN == 0
    tm, tk = min(tm, s), min(tk, s)
    hw = NSA_HEADS * NSA_HEAD_DIM
    scale = NSA_HEAD_DIM ** -0.5 * LOG2_E
    row = lambda w: pl.BlockSpec((1, tm, w), lambda bi, i: (bi, i, 0))
    tab = pl.BlockSpec((tm, 128), lambda bi, i: (i, 0))
    tr_spec = pl.BlockSpec((1, 128, tm), lambda bi, i: (bi, 0, i))
    q, kc_in, vc_in, ks, kw, vst, vwt = pl.pallas_call(
        functools.partial(_nsa_prep_kernel, scale=scale),
        grid=(b, s // tm),
        in_specs=[pl.BlockSpec((1, tm, NSA_IN), lambda bi, i: (bi, i, col0 // NSA_IN)), tab, tab],
        out_specs=[row(hw), row(128), row(128), row(128), row(128), tr_spec, tr_spec],
        out_shape=[jax.ShapeDtypeStruct((b, s, hw), BF16), jax.ShapeDtypeStruct((b, s, 128), F32),
                   jax.ShapeDtypeStruct((b, s, 128), F32), jax.ShapeDtypeStruct((b, s, 128), BF16),
                   jax.ShapeDtypeStruct((b, s, 128), BF16), jax.ShapeDtypeStruct((b, 128, s), BF16),
                   jax.ShapeDtypeStruct((b, 128, s), BF16)],
        compiler_params=_params("parallel", "parallel"),
        name="nsa_prep",
    )(p, cos, sin)

    nrow = s // NSA_CMP_STRIDE
    fw = NSA_CMP_STRIDE * NSA_HEAD_DIM
    half = w1.shape[1] // 2
    w1r = jnp.concatenate([w1[:, :half], w1[:, half:]], axis=2)
    pos2 = jnp.pad(pos_emb.reshape(2, 2, fw), ((0, 0), (0, 6), (0, 0)))
    per_b = lambda shape: pl.BlockSpec((1,) + shape, lambda bi: (bi, 0, 0))
    const3 = lambda shape: pl.BlockSpec(shape, lambda bi: (0, 0, 0))
    kc3, vct = pl.pallas_call(
        _nsa_compress_kernel,
        grid=(b,),
        in_specs=[per_b((nrow, fw)), per_b((nrow, fw)), const3(pos2.shape), const3(w1r.shape), const3(w2.shape)],
        out_specs=[pl.BlockSpec((1, 3, nrow, NSA_HEAD_DIM), lambda bi: (bi, 0, 0, 0)), per_b((NSA_HEAD_DIM, nrow))],
        out_shape=[jax.ShapeDtypeStruct((b, 3, nrow, NSA_HEAD_DIM), BF16), jax.ShapeDtypeStruct((b, NSA_HEAD_DIM, nrow), BF16)],
        compiler_params=_params("parallel"),
        name="nsa_compress",
    )(kc_in.reshape(b, nrow, fw), vc_in.reshape(b, nrow, fw), pos2, w1r, w2)

    n_blk = s // NSA_SEL_BLOCK
    nb = 64
    assert n_blk <= nb and tq == 128 and tk % tq == 0
    cmp_start = np.arange(nrow) * NSA_CMP_STRIDE
    cmp_end = cmp_start + NSA_CMP_LEN - 1
    blk_start = np.arange(nb) * NSA_SEL_BLOCK
    cover_t = ((cmp_start[None, :] <= blk_start[:, None] + NSA_SEL_BLOCK - 1)
               & (cmp_end[None, :] >= blk_start[:, None]) & (np.arange(nb)[:, None] < n_blk)
               & (np.arange(nrow)[None, :] < nrow - 1)).astype(np.float32)
    expand_t = ((np.arange(s)[:, None] // NSA_SEL_BLOCK) == np.arange(128)[None, :]).astype(np.float32)
    qrow = lambda w: pl.BlockSpec((1, tq, w), lambda bi, i: (bi, i, 0))
    k_spec = pl.BlockSpec((1, s, 128), lambda bi, i: (bi, 0, 0))
    vt_spec = pl.BlockSpec((1, 128, s), lambda bi, i: (bi, 0, 0))
    gate_block = (col0 + NSA_IN) // 128 - 1
    return pl.pallas_call(
        functools.partial(_nsa_attn_kernel, tq=tq, tk=tk, n_blk=n_blk),
        grid=(b, s // tq),
        in_specs=[qrow(hw), pl.BlockSpec((1, 3, nrow, NSA_HEAD_DIM), lambda bi, i: (bi, 0, 0, 0)),
                  pl.BlockSpec((1, NSA_HEAD_DIM, nrow), lambda bi, i: (bi, 0, 0)),
                  pl.BlockSpec((nb, nrow), lambda bi, i: (0, 0)),
                  k_spec, vt_spec, k_spec, vt_spec, pl.BlockSpec((s, 128), lambda bi, i: (0, 0)),
                  pl.BlockSpec((1, tq, 128), lambda bi, i: (bi, i, gate_block))],
        out_specs=qrow(hw),
        out_shape=jax.ShapeDtypeStruct((b, s, hw), BF16),
        compiler_params=_params("parallel", "arbitrary"),
        name="nsa_attn",
    )(q, kc3, vct, jnp.asarray(cover_t, dtype=BF16), ks, vst, kw, vwt, jnp.asarray(expand_t, dtype=BF16), p)


RWKV_Z_PIECES = 4


def _rwkv_prep_kernel(*refs):
    z_refs, zp_refs = refs[:RWKV_Z_PIECES], refs[RWKV_Z_PIECES:2 * RWKV_Z_PIECES]
    (mu_ref, w0_ref, a0_ref, kk_ref, ka_ref, rk_ref, w2_ref, a2_ref, g2_ref, bd_ref, csum_ref, kt_ref, rt_ref, kh_ref,
     bh_ref, kbar_ref, bbar_ref, v_ref, etot_ref, g_ref, bonus_ref) = refs[2 * RWKV_Z_PIECES:]
    i = pl.program_id(1)
    z = jnp.concatenate([r[0] for r in z_refs], axis=1).astype(F32)
    tm = z.shape[0]
    last = zp_refs[0].shape[1] - 1
    prev_row = jnp.concatenate([r[0, last:, :] for r in zp_refs], axis=1).astype(F32)
    prev_row = jnp.where(i > 0, prev_row, 0.0)
    rowid = lax.broadcasted_iota(jnp.int32, z.shape, 0)
    z_prev = jnp.where(rowid == 0, prev_row, pltpu.roll(z, 1, axis=0))
    z = z + (z_prev - z) * mu_ref[...]
    c = BRANCH_WIDTH
    r, k, v = z[:, :c], z[:, c:2 * c], z[:, 2 * c:3 * c]
    wd, ad, gd = z[:, 3 * c:3 * c + 128], z[:, 3 * c + 128:3 * c + 256], z[:, 3 * c + 256:3 * c + 512]
    wpre = -(w0_ref[...] + _dot_split(jnp.tanh(wd), w2_ref))
    w_log = -(jnp.maximum(wpre, 0.0) + jnp.log1p(jnp.exp(-jnp.abs(wpre)))) - 0.5
    lw = -jnp.exp(w_log)
    a = jax.nn.sigmoid(a0_ref[...] + _dot_split(ad, a2_ref))
    g_ref[0] = _dot_split(jax.nn.sigmoid(gd), g2_ref)
    kk = k * kk_ref[...]
    head_sum = lambda t: sum(_dot(piece, bd_ref[...]) for piece in _split2_bf16(t))
    kk = kk * lax.rsqrt(jnp.maximum(head_sum(kk * kk), 1e-24))
    k = k * (1.0 + (a - 1.0) * ka_ref[...])
    bonus_ref[0] = head_sum(r * k * rk_ref[...]) * v
    sums = sum(_dot(csum_ref[...], piece) for piece in _split3_bf16(lw))
    cum, total = sums[:tm], sums[tm:]
    dec_out, dec_end = jnp.exp(-cum), jnp.exp(total - cum)
    b = kk * a
    kt_ref[0] = (kk * jnp.exp(cum - lw)).astype(BF16)
    rt_ref[0] = (r * jnp.exp(cum)).astype(BF16)
    kh_ref[0] = (k * dec_out).astype(BF16)
    bh_ref[0] = (b * dec_out).astype(BF16)
    kbar_ref[0] = (k * dec_end).astype(BF16)
    bbar_ref[0] = (b * dec_end).astype(BF16)
    v_ref[0] = v.astype(BF16)
    etot_ref[0] = jnp.exp(total)


def _rwkv_scan_kernel(kt_ref, rt_ref, kh_ref, bh_ref, kbar_ref, bbar_ref, v_ref, etot_ref, g_ref, bonus_ref,
                      gw_ref, gb_ref, y_ref, state_ref, *, L):
    n_chunks = kt_ref.shape[1] // L
    N = RWKV_HEAD_DIM
    n_pairs = kt_ref.shape[2] // 128

    @pl.when(pl.program_id(1) == 0)
    def _():
        state_ref[...] = jnp.zeros_like(state_ref)

    ri = lax.broadcasted_iota(jnp.int32, (L, 2 * L), 0)
    ci = lax.broadcasted_iota(jnp.int32, (L, 2 * L), 1) & (L - 1)
    strict2, incl2 = ci < ri, ci <= ri
    eye = jnp.where(lax.broadcasted_iota(jnp.int32, (L, L), 0) == lax.broadcasted_iota(jnp.int32, (L, L), 1), 1.0, 0.0)
    first2 = lax.broadcasted_iota(jnp.int32, (2 * L, 128), 1) < N
    first = lax.broadcasted_iota(jnp.int32, (L, 128), 1) < N
    bd = (lax.broadcasted_iota(jnp.int32, (128, 128), 0) < N) == (lax.broadcasted_iota(jnp.int32, (128, 128), 1) < N)
    zero = jnp.zeros((), BF16)

    items = range(n_chunks * n_pairs)
    block = lambda ref, i: ref[0, (i // n_pairs) * L:(i // n_pairs + 1) * L, (i % n_pairs) * 128:(i % n_pairs + 1) * 128]
    xs_in = [jnp.concatenate([block(kt_ref, i), block(rt_ref, i)], axis=0) for i in items]
    ys_in = [jnp.concatenate([block(kh_ref, i), block(bh_ref, i)], axis=0) for i in items]
    gms = [_dot_nt(jnp.where(first2 if sub == 0 else ~first2, xs_in[i], zero), ys_in[i])
           for i in items for sub in range(2)]
    a_k = jnp.stack([jnp.where(strict2, gm[:L], 0.0) for gm in gms])
    a_r = jnp.stack([jnp.where(incl2, gm[L:], 0.0) for gm in gms])
    a_kb = a_k[:, :, L:]
    bmm = lambda a, b: jnp.einsum("hij,hjk->hik", a.astype(BF16), b.astype(BF16), preferred_element_type=F32)
    tinv = eye[None] - a_kb
    pw = bmm(a_kb, a_kb)
    for _ in range(int(np.log2(L)) - 2):
        both = bmm(jnp.concatenate([tinv, pw], axis=1), pw)
        tinv = tinv + both[:, :L]
        pw = both[:, L:]
    tinv = tinv + bmm(tinv, pw)

    pick = lambda t: jnp.where(first, t[:L], t[L:])
    hsum = lambda t: jnp.where(first, jnp.sum(jnp.where(first, t, 0.0), axis=-1, keepdims=True),
                               jnp.sum(jnp.where(first, 0.0, t), axis=-1, keepdims=True))
    states = [state_ref[p] for p in range(n_pairs)]
    sls = [slice(p * 128, (p + 1) * 128) for p in range(n_pairs)]
    pairs = range(n_pairs)
    stack2 = lambda t, i: jnp.concatenate([t[2 * i], t[2 * i + 1]], axis=0).astype(BF16)
    vs = [block(v_ref, i) for i in items]
    akkv = [pick(_dot(stack2(a_k[:, :, :L], i), vs[i])) for i in items]
    tk = [_dot(stack2(tinv, i), jnp.concatenate([xs_in[i][:L], akkv[i].astype(BF16)], axis=1)) for i in items]
    xm = [jnp.concatenate([pick(tk[i][:, :128]).astype(BF16), xs_in[i][L:]], axis=0) for i in items]
    u0 = [-pick(tk[i][:, 128:]) for i in items]
    for c in range(n_chunks):
        rs = slice(c * L, (c + 1) * L)
        i0 = c * n_pairs
        v = vs[i0:i0 + n_pairs]
        xs = [_dot_nt(xm[i0 + p], states[p].astype(BF16)) for p in pairs]
        vu = [jnp.concatenate([v[p], (u0[i0 + p] - xs[p][:L]).astype(BF16)], axis=0) for p in pairs]
        o = [xs[p][L:] + pick(_dot(stack2(a_r, i0 + p), vu[p])) for p in pairs]
        for p, sl in enumerate(sls):
            kb = jnp.concatenate([kbar_ref[0, rs, sl], bbar_ref[0, rs, sl]], axis=0)
            states[p] = states[p] * etot_ref[0, c * L:c * L + 1, sl] + jnp.where(bd, _dot_tn(vu[p], kb), 0.0)
        for p, sl in enumerate(sls):
            ctr = o[p] - hsum(o[p]) * (1.0 / N)
            on = ctr * lax.rsqrt(hsum(ctr * ctr) * (1.0 / N) + RWKV_GN_EPS)
            y_ref[0, rs, sl] = ((on * gw_ref[:, sl] + gb_ref[:, sl] + bonus_ref[0, rs, sl]) * g_ref[0, rs, sl]).astype(y_ref.dtype)
    for p in pairs:
        state_ref[p] = states[p]


def rwkv7_mixer(p, col0, mu, w0, w2, a0, a2, g2, k_k, k_a, r_k, gn_w, gn_b, *, tm=256):
    b, s, _ = p.shape
    zw = mu.shape[0]
    zh = zw // RWKV_Z_PIECES
    prev_rows = 16 if p.dtype == BF16 else 8
    assert col0 % zh == 0
    tm = min(tm, s)
    c, hd, nh = BRANCH_WIDTH, RWKV_HEAD_DIM, RWKV_HEADS
    L = min(RWKV_CHUNK, s)
    assert tm % L == 0 and L & (L - 1) == 0
    bd = jnp.asarray(np.kron(np.eye(nh, dtype=np.float32), np.ones((hd, hd), np.float32)), dtype=BF16)
    chunk = np.arange(tm) // L
    same = chunk[:, None] == chunk[None, :]
    csum = np.concatenate([same & (np.arange(tm)[None, :] <= np.arange(tm)[:, None]), same], axis=0)
    csum = jnp.asarray(csum.astype(np.float32), dtype=BF16)
    row = lambda w: pl.BlockSpec((1, tm, w), lambda bi, i: (bi, i, 0))
    vec = lambda w: pl.BlockSpec((1, w), lambda bi, i: (0, 0))
    mat = lambda shape: pl.BlockSpec(shape, lambda bi, i: (0,) * len(shape))
    w2, a2, g2 = (jnp.stack(_split2_bf16(w)) for w in (w2, a2, g2))
    outs = pl.pallas_call(
        _rwkv_prep_kernel,
        grid=(b, s // tm),
        in_specs=[pl.BlockSpec((1, tm, zh), functools.partial(lambda bi, i, j: (bi, i, j), j=col0 // zh + j))
                  for j in range(RWKV_Z_PIECES)]
        + [pl.BlockSpec((1, prev_rows, zh),
                        functools.partial(lambda bi, i, j: (bi, jnp.maximum(i * (tm // prev_rows) - 1, 0), j), j=col0 // zh + j))
           for j in range(RWKV_Z_PIECES)]
        + [vec(zw), vec(c), vec(c), vec(c), vec(c), vec(c), mat(w2.shape), mat(a2.shape), mat(g2.shape),
                  mat(bd.shape), mat(csum.shape)],
        out_specs=[row(c)] * 10,
        out_shape=[jax.ShapeDtypeStruct((b, s, c), BF16)] * 7 + [jax.ShapeDtypeStruct((b, s, c), F32)] * 3,
        compiler_params=_params("parallel", "parallel"),
        name="rwkv_prep",
    )(*([p] * (2 * RWKV_Z_PIECES)), mu.reshape(1, zw), w0.reshape(1, c), a0.reshape(1, c), k_k.reshape(1, c), k_a.reshape(1, c),
      r_k.reshape(1, c), w2, a2, g2, bd, csum)
    rows = min(RWKV_CHUNKS_PER_STEP * L, s)
    blk = pl.BlockSpec((1, rows, c), lambda bi, ci: (bi, ci, 0))
    gvec = pl.BlockSpec((1, c), lambda bi, ci: (0, 0))
    return pl.pallas_call(
        functools.partial(_rwkv_scan_kernel, L=L),
        grid=(b, s // rows),
        in_specs=[blk] * 10 + [gvec, gvec],
        out_specs=blk,
        out_shape=jax.ShapeDtypeStruct((b, s, c), BF16),
        scratch_shapes=[pltpu.VMEM((c // 128, 128, 128), F32)],
        compiler_params=_params("parallel", "arbitrary"),
        name="rwkv_scan",
    )(*outs, gn_w.reshape(1, c), gn_b.reshape(1, c))


def _retention_kernel(qk_ref, v_ref, g_ref, cos_ref, sin_ref, inner_ref, qd_ref, kd_ref, cd_ref, o_ref, state_ref):
    @pl.when(pl.program_id(1) == 0)
    def _():
        state_ref[...] = jnp.zeros_like(state_ref)

    C = qk_ref.shape[1]
    kw = RET_HEADS * RET_K_DIM
    cos, sin = cos_ref[...], sin_ref[...]
    rope = lambda t: jnp.concatenate(
        [_rope(t[:, j:j + 128], cos[:, j:j + 128], sin[:, j:j + 128], RET_K_DIM // 2, RET_K_DIM) for j in range(0, kw, 128)], axis=1)
    q = rope(qk_ref[0, :, :kw].astype(F32))
    k = rope(qk_ref[0, :, kw:].astype(F32)) * RET_K_DIM ** -0.5
    lane = lax.broadcasted_iota(jnp.int32, (C, 128), 1)
    heads = range(RET_HEADS)
    vsl = [slice(h * RET_V_DIM, (h + 1) * RET_V_DIM) for h in heads]
    own = lambda t, h: jnp.where((lane // RET_K_DIM) == h % 2, t[:, (h // 2) * 128:(h // 2 + 1) * 128], 0.0)
    qh = [own(q, h) for h in heads]
    kh = [own(k, h) for h in heads]
    vh = [v_ref[0, :, sl] for sl in vsl]
    st = [state_ref[h] for h in heads]
    scores = [_dot_nt(qh[h].astype(BF16), kh[h].astype(BF16)) * inner_ref[h] for h in heads]
    cross = [_dot((qh[h] * qd_ref[h]).astype(BF16), st[h].astype(BF16)) for h in heads]
    o = [_dot(scores[h].astype(BF16), vh[h]) + cross[h] for h in heads]
    for h in heads:
        state_ref[h] = st[h] * cd_ref[h, 0:1, :] + _dot_tn((kh[h] * kd_ref[h]).astype(BF16), vh[h])
    for h, sl in enumerate(vsl):
        gate = g_ref[0, :, sl].astype(F32)
        c = o[h] - jnp.mean(o[h], axis=-1, keepdims=True)
        y = c * lax.rsqrt(jnp.mean(c * c, axis=-1, keepdims=True) + NORM_EPS)
        o_ref[0, :, sl] = (y * (gate * jax.nn.sigmoid(gate))).astype(o_ref.dtype)


def retention_mixer(p, col0, cos, sin):
    b, s, _ = p.shape
    assert col0 % 512 == 0
    H, C = RET_HEADS, min(RET_CHUNK, s)
    log_gamma = jnp.log1p(-jnp.exp2(-5.0 - jnp.arange(H, dtype=F32)))
    n = jnp.arange(C, dtype=F32)
    dist = n[:, None] - n[None, :]
    inner = jnp.where(dist >= 0, jnp.exp(jnp.maximum(dist, 0.0) * log_gamma[:, None, None]), 0.0)
    q_decay = jnp.exp((n + 1.0) * log_gamma[:, None])
    k_decay = jnp.exp((C - 1.0 - n) * log_gamma[:, None])
    chunk_decay = jnp.exp(C * log_gamma)
    lanes = lambda t: jnp.broadcast_to(t[:, :, None], (H, C, 128))
    const = lambda shape: pl.BlockSpec(shape, lambda bi, ci: (0,) * len(shape))
    kw = RET_HEADS * RET_K_DIM
    return pl.pallas_call(
        _retention_kernel,
        grid=(b, s // C),
        in_specs=[pl.BlockSpec((1, C, 512), functools.partial(lambda bi, ci, j: (bi, ci, j), j=col0 // 512 + j))
                  for j in range(3)] + [
                  pl.BlockSpec((C, kw), lambda bi, ci: (ci, 0)), pl.BlockSpec((C, kw), lambda bi, ci: (ci, 0)),
                  const((H, C, C)), const((H, C, 128)), const((H, C, 128)), const((H, 8, 128))],
        out_specs=pl.BlockSpec((1, C, H * RET_V_DIM), lambda bi, ci: (bi, ci, 0)),
        out_shape=jax.ShapeDtypeStruct((b, s, H * RET_V_DIM), BF16),
        scratch_shapes=[pltpu.VMEM((H, 128, 128), F32)],
        compiler_params=_params("parallel", "arbitrary"),
        name="retention",
    )(p, p, p, cos, sin, inner, lanes(q_decay), lanes(k_decay), jnp.broadcast_to(chunk_decay[:, None, None], (H, 8, 128)))


def _pad_cols(w, width):
    return jnp.pad(w, [(0, 0)] * (w.ndim - 1) + [(0, width - w.shape[-1])])


def _rope_tables(s, inv_freq, width, fill_cos):
    inv_freq = np.asarray(inv_freq, np.float32)
    ang = np.arange(s, dtype=np.float32)[:, None] * inv_freq[None, :]
    cos, sin = np.cos(ang).astype(np.float32), np.sin(ang).astype(np.float32)
    rot = 2 * cos.shape[1]
    cos2 = np.concatenate([cos, cos, np.full((s, width - rot), fill_cos, np.float32)], axis=1)
    sin2 = np.concatenate([sin, sin, np.zeros((s, width - rot), np.float32)], axis=1)
    return cos2, sin2


def _inv_freq(rot_dim, theta):
    return np.float32(theta) ** (-np.arange(0, rot_dim, 2, dtype=np.float32) / np.float32(rot_dim))


def _split_offsets():
    sizes = (MLA_Q_RANK, MLA_KV_RANK, MLA_ROPE_DIM, 512, 128, 128, 128, 128, 128, 128, 12,
             3 * BRANCH_WIDTH + RWKV_DECAY_LORA + RWKV_A_LORA + RWKV_GATE_LORA, 256, 256, 512, 512, N_MIXERS * D_MODEL)
    offs = np.concatenate([[0], np.cumsum(sizes)])
    return [(int(offs[i]), int(offs[i + 1])) for i in range(len(sizes))]


def _pack_in_proj(w):
    w = w.astype(BF16)
    sl = [w[..., a:b] for a, b in _split_offsets()]
    (ql, kvl, kr, nq, nkc, nvc, nks, nvs, nkw, nvw, ng, rz, rq, rk, rv, rg, mg) = sl
    c = BRANCH_WIDTH
    w_rwkv = [rz[..., :3 * c], _pad_cols(rz[..., 3 * c:3 * c + 96], 128), _pad_cols(rz[..., 3 * c + 96:3 * c + 192], 128),
              rz[..., 3 * c + 192:]]
    groups = {"nsa": [nq, nkc, nvc, nks, nvs, nkw, nvw, _pad_cols(ng, 128)], "mla_kv": [kvl], "mla_q": [ql],
              "mla_kr": [_pad_cols(kr, 128)], "ret": [rq, rk, rv, rg], "gate": [mg], "rwkv": w_rwkv}
    cols, parts, off = {}, [], 0
    for name, ws in groups.items():
        cols[name] = off
        parts += ws
        off += sum(t.shape[-1] for t in ws)
    return jnp.concatenate(parts, axis=-1), cols


def _pack_mla_weights(w_uq, w_ukv):
    H, dn, dr, dv = MLA_HEADS, MLA_NOPE_DIM, MLA_ROPE_DIM, MLA_V_DIM
    uq = w_uq.reshape(-1, H, dn + dr)
    nope = uq[:, :, :dn].reshape(-1, H * dn)
    rope = uq[:, :, dn:]
    rope_p = _pad_cols(rope, 128).reshape(-1, H * 128)
    ukv = w_ukv.reshape(-1, H, dn + dv)
    wkv = jnp.concatenate([ukv[:, :, :dn].reshape(-1, H * dn), ukv[:, :, dn:].reshape(-1, H * dv)], axis=1)
    return jnp.concatenate([nope, rope_p], axis=1).astype(BF16), wkv.astype(BF16)


def _pad_rows(w, rows):
    return jnp.pad(w, ((0, rows - w.shape[0]), (0, 0)))


def kernel(x, w_in, w_branch, w_out, w_up, w_down, norm_gains, mla_g_q, mla_g_kv, mla_w_uq, mla_w_ukv,
           nsa_cmp_pos, nsa_cmp_w1, nsa_cmp_w2, rwkv_mu, rwkv_w0, rwkv_w2, rwkv_a0, rwkv_a2, rwkv_g2,
           rwkv_k_k, rwkv_k_a, rwkv_r_k, rwkv_gn_w, rwkv_gn_b):
    B, S, D = x.shape
    depth = w_in.shape[0]
    T = B * S
    tm, tm_in = min(512, T), min(1024, T)
    mla_cos, mla_sin = _rope_tables(S, _inv_freq(MLA_ROPE_DIM, ROPE_THETA), 128, 0.0)
    nsa_cos, nsa_sin = _rope_tables(S, _inv_freq(NSA_ROT_DIM, ROPE_THETA), 128, 1.0)
    ret_cos, ret_sin = _rope_tables(S, np.float32(RET_THETA) ** (-np.linspace(0.0, 1.0, RET_K_DIM // 2, dtype=np.float32)),
                                    RET_K_DIM, 0.0)
    ret_cos, ret_sin = np.tile(ret_cos, (1, RET_HEADS)), np.tile(ret_sin, (1, RET_HEADS))
    c = BRANCH_WIDTH
    x = x.reshape(T, D)
    w_all, cols = _pack_in_proj(w_in)
    tn_in = w_all.shape[-1] // 9
    assert tn_in % 256 == 0 and tn_in * 9 == w_all.shape[-1]
    w_branch, w_out, w_up, w_down = (w.astype(BF16) for w in (w_branch, w_out, w_up, w_down))
    for l in range(depth):
        g_pre = norm_gains[l, 0]
        p2 = norm_matmul(x, g_pre, w_all, l, BF16, tm_in, tn_in)
        p = p2.reshape(B, S, -1)

        wq, wkv = _pack_mla_weights(mla_w_uq[l], mla_w_ukv[l])
        y_mla = mla_mixer(p, (cols["mla_q"], cols["mla_kv"], cols["mla_kr"]), mla_g_q[l], mla_g_kv[l], wq, wkv, mla_cos, mla_sin)
        y_nsa = nsa_mixer(p, cols["nsa"], nsa_cmp_pos[l], nsa_cmp_w1[l], nsa_cmp_w2[l], nsa_cos, nsa_sin)
        mu = rwkv_mu[l]
        mu_p = jnp.concatenate([mu[:3 * c], _pad_cols(mu[3 * c:3 * c + 96], 128), _pad_cols(mu[3 * c + 96:3 * c + 192], 128),
                                mu[3 * c + 192:]])
        y_rwkv = rwkv7_mixer(p, cols["rwkv"], mu_p, rwkv_w0[l], _pad_rows(rwkv_w2[l], 128), rwkv_a0[l],
                             _pad_rows(rwkv_a2[l], 128), rwkv_g2[l], rwkv_k_k[l], rwkv_k_a[l], rwkv_r_k[l].reshape(-1),
                             rwkv_gn_w[l], rwkv_gn_b[l])
        y_ret = retention_mixer(p, cols["ret"], ret_cos, ret_sin)

        ys = [y.reshape(T, c) for y in (y_mla, y_nsa, y_rwkv, y_ret)]
        x, h = merge_out_residual(ys, p2, cols["gate"], w_branch, w_out, l, norm_gains[l, 1], norm_gains[l, 2], x, min(256, T))
        x = ffn_block(x, h, w_up, w_down, l, norm_gains[l, 3], tm, 1024)
    return x.reshape(B, S, D)
```

```python
import functools

import numpy as np
import jax
import jax.numpy as jnp
from jax import lax
from jax.experimental import pallas as pl
from jax.experimental.pallas import tpu as pltpu

F32 = jnp.float32
BF16 = jnp.bfloat16
HIGHEST = lax.Precision.HIGHEST

V7X_LANES = 128
V7X_VMEM_LIMIT_BYTES = 56 * 1024 * 1024

D_MODEL = 2048
N_MIXERS = 4
BRANCH_WIDTH = D_MODEL // N_MIXERS
ROPE_THETA = 500000.0
NORM_EPS = 1e-6
MASK_VALUE = -1e30
LOG2_E = float(np.log2(np.e))

MLA_NOPE_DIM = 128
MLA_ROPE_DIM = 64
MLA_V_DIM = 128
MLA_HEADS = 4
MLA_Q_RANK = 384
MLA_KV_RANK = 128

NSA_HEAD_DIM = 128
NSA_HEADS = 4
NSA_ROT_DIM = 32
NSA_CMP_LEN = 32
NSA_CMP_STRIDE = 16
NSA_SEL_BLOCK = 64
NSA_TOP_N = 16
NSA_WINDOW = 512
NSA_FORCED_SCORE = 1000.0
NSA_IN = NSA_HEADS * NSA_HEAD_DIM + 7 * 128

RWKV_HEAD_DIM = 64
RWKV_HEADS = 8
RWKV_DECAY_LORA = 96
RWKV_A_LORA = 96
RWKV_GATE_LORA = 256
RWKV_GN_EPS = 64e-5
RWKV_CHUNK = 64
RWKV_CHUNKS_PER_STEP = 4

RET_HEADS = 4
RET_V_DIM = 128
RET_K_DIM = 64
RET_CHUNK = 128
RET_THETA = 10000.0


def _params(*sem):
    return pltpu.CompilerParams(dimension_semantics=sem, vmem_limit_bytes=V7X_VMEM_LIMIT_BYTES)


def _dot(a, b, precision=None):
    return jnp.dot(a, b, preferred_element_type=F32, precision=precision)


def _dot_nt(a, b, precision=None):
    return lax.dot_general(a, b, (((1,), (1,)), ((), ())), preferred_element_type=F32, precision=precision)


def _dot_tn(a, b, precision=None):
    return lax.dot_general(a, b, (((0,), (0,)), ((), ())), preferred_element_type=F32, precision=precision)


def _rotate_half(x, half, period):
    lane = lax.broadcasted_iota(jnp.int32, x.shape, x.ndim - 1) & (period - 1)
    up = pltpu.roll(x, V7X_LANES - half, axis=x.ndim - 1)
    dn = pltpu.roll(x, half, axis=x.ndim - 1)
    return jnp.where(lane < half, -up, jnp.where(lane < 2 * half, dn, 0.0))


def _rope(x, cos, sin, half, period=V7X_LANES):
    return x * cos + _rotate_half(x, half, period) * sin


def _rms(x, g, eps=NORM_EPS):
    return x * lax.rsqrt(jnp.mean(x * x, axis=-1, keepdims=True) + eps) * g


def _norm_matmul_kernel(x_ref, g_ref, w_ref, o_ref, h_ref):
    @pl.when(pl.program_id(1) == 0)
    def _():
        h_ref[...] = _rms(x_ref[...], g_ref[...]).astype(BF16)

    o_ref[...] = _dot(h_ref[...], w_ref[...]).astype(o_ref.dtype)


def norm_matmul(x, g, w, layer, out_dtype, tm, tn):
    m, k = x.shape
    n = w.shape[2]
    assert m % tm == 0 and n % tn == 0
    return pl.pallas_call(
        _norm_matmul_kernel,
        grid=(m // tm, n // tn),
        in_specs=[pl.BlockSpec((tm, k), lambda i, j: (i, 0)),
                  pl.BlockSpec((1, k), lambda i, j: (0, 0)),
                  pl.BlockSpec((None, k, tn), lambda i, j: (layer, 0, j))],
        out_specs=pl.BlockSpec((tm, tn), lambda i, j: (i, j)),
        out_shape=jax.ShapeDtypeStruct((m, n), out_dtype),
        scratch_shapes=[pltpu.VMEM((tm, k), BF16)],
        compiler_params=_params("parallel", "arbitrary"),
        name="norm_matmul",
    )(x, g.reshape(1, k), w)


MERGE_GATE_PIECES = 4


def _merge_out_kernel(*refs):
    n = MERGE_GATE_PIECES
    y_refs, g_refs = refs[:N_MIXERS], refs[N_MIXERS:(1 + n) * N_MIXERS]
    wb_ref, wo_ref, gn_ref, gf_ref, x_ref, o_ref, h_ref = refs[(1 + n) * N_MIXERS:]
    width = g_refs[0].shape[1]
    merged = []
    for piece in range(n):
        acc = None
        for m in range(N_MIXERS):
            gate = 0.5 * jnp.tanh(0.5 * g_refs[n * m + piece][...].astype(F32)) + 0.5
            term = gate * _dot(y_refs[m][...], wb_ref[m, :, piece * width:(piece + 1) * width])
            acc = term if acc is None else acc + term
        merged.append(acc.astype(BF16))
    x_new = x_ref[...] + _rms(_dot(jnp.concatenate(merged, axis=1), wo_ref[...]), gn_ref[...])
    o_ref[...] = x_new
    h_ref[...] = _rms(x_new, gf_ref[...]).astype(BF16)


def merge_out_residual(ys, gates, col0, wb, wo, layer, gn, g_ffn, x, tm):
    m, c = ys[0].shape
    d = wb.shape[3]
    half = d // MERGE_GATE_PIECES
    assert col0 % half == 0
    once = pl.Buffered(1)
    return pl.pallas_call(
        _merge_out_kernel,
        grid=(m // tm,),
        in_specs=[pl.BlockSpec((tm, c), lambda i: (i, 0))] * N_MIXERS
        + [pl.BlockSpec((tm, half), functools.partial(lambda i, j: (i, j), j=col0 // half + jj))
           for jj in range(MERGE_GATE_PIECES * N_MIXERS)]
        + [pl.BlockSpec((None, N_MIXERS, c, d), lambda i: (layer, 0, 0, 0), pipeline_mode=once),
           pl.BlockSpec((None, d, d), lambda i: (layer, 0, 0), pipeline_mode=once),
           pl.BlockSpec((1, d), lambda i: (0, 0)),
           pl.BlockSpec((1, d), lambda i: (0, 0)),
           pl.BlockSpec((tm, d), lambda i: (i, 0))],
        out_specs=[pl.BlockSpec((tm, d), lambda i: (i, 0))] * 2,
        out_shape=[jax.ShapeDtypeStruct((m, d), F32), jax.ShapeDtypeStruct((m, d), BF16)],
        compiler_params=_params("parallel"),
        name="merge_out_residual",
    )(*ys, *([gates] * (MERGE_GATE_PIECES * N_MIXERS)), wb, wo, gn.reshape(1, d), g_ffn.reshape(1, d), x)


def _ffn_kernel(x_ref, h_ref, wu_ref, wd_ref, g2_ref, o_ref, acc_ref):
    f = pl.program_id(1)

    @pl.when(f == 0)
    def _():
        acc_ref[...] = jnp.zeros_like(acc_ref)

    u = jnp.maximum(_dot(h_ref[...], wu_ref[...]), 0.0)
    acc_ref[...] += _dot((u * u).astype(BF16), wd_ref[...])

    @pl.when(f == pl.num_programs(1) - 1)
    def _():
        o_ref[...] = x_ref[...] + _rms(acc_ref[...], g2_ref[...])


def ffn_block(x, h, wu, wd, layer, g2, tm, tf):
    m, d = x.shape
    ff = wu.shape[2]
    return pl.pallas_call(
        _ffn_kernel,
        grid=(m // tm, ff // tf),
        in_specs=[pl.BlockSpec((tm, d), lambda i, f: (i, 0)),
                  pl.BlockSpec((tm, d), lambda i, f: (i, 0)),
                  pl.BlockSpec((None, d, tf), lambda i, f: (layer, 0, f)),
                  pl.BlockSpec((None, tf, d), lambda i, f: (layer, f, 0)),
                  pl.BlockSpec((1, d), lambda i, f: (0, 0))],
        out_specs=pl.BlockSpec((tm, d), lambda i, f: (i, 0)),
        out_shape=jax.ShapeDtypeStruct((m, d), F32),
        scratch_shapes=[pltpu.VMEM((tm, d), F32)],
        compiler_params=_params("parallel", "arbitrary"),
        name="ffn_block",
    )(x, h, wu, wd, g2.reshape(1, d))


def _online_softmax_steps(carries, scores, vt_tiles, masks):
    scores = [s if mask is None else jnp.where(mask, s, -jnp.inf) for s, mask in zip(scores, masks)]
    out = []
    for (m_prev, l_prev, acc_prev), s, vt in zip(carries, scores, vt_tiles):
        m_new = jnp.maximum(m_prev, jnp.max(s, axis=0, keepdims=True))
        alpha = jnp.exp2(m_prev - m_new)
        p = jnp.exp2(s - m_new)
        out.append((m_new, alpha * l_prev + jnp.sum(p, axis=0, keepdims=True), alpha * acc_prev + _dot(vt, p.astype(BF16))))
    return tuple(out)


def _softmax_init(cols, dv):
    return (jnp.full((1, cols), MASK_VALUE, F32), jnp.zeros((1, cols), F32), jnp.zeros((dv, cols), F32))


def _softmax_merge(a, b):
    (ma, la, acca), (mb, lb, accb) = a, b
    m = jnp.maximum(ma, mb)
    wa, wb = jnp.exp2(ma - m), jnp.exp2(mb - m)
    return m, la * wa + lb * wb, acca * wa + accb * wb


def _softmax_finish(carry):
    _, l, acc = carry
    return acc / jnp.where(l > 0.0, l, 1.0)


def _mla_prep_kernel(ql_ref, kvl_ref, krl_ref, gq_ref, gkv_ref, wq_ref, wkv_ref, cos_ref, sin_ref,
                     qn_ref, qr_ref, kn_ref, vt_ref, kr_ref, *, scale):
    nq = _rms(ql_ref[0].astype(F32), gq_ref[...]).astype(BF16)
    nkv = _rms(kvl_ref[0].astype(F32), gkv_ref[...]).astype(BF16)
    q = _dot(nq, wq_ref[...])
    kv = _dot(nkv, wkv_ref[...])
    cos, sin = cos_ref[...], sin_ref[...]
    hw = MLA_HEADS * MLA_NOPE_DIM
    half = MLA_ROPE_DIM // 2
    qn_ref[0] = (q[:, :hw] * scale).astype(BF16)
    for h in range(MLA_HEADS):
        qr_ref[0, :, h * 128:(h + 1) * 128] = (_rope(q[:, hw + h * 128: hw + (h + 1) * 128], cos, sin, half) * scale).astype(BF16)
    kn_ref[0] = kv[:, :hw].astype(BF16)
    vt_ref[0] = kv[:, hw:].T.astype(BF16)
    kr_ref[0] = _rope(krl_ref[0].astype(F32), cos, sin, half).astype(BF16)


def _mla_attn_kernel(qn_ref, qr_ref, kn_ref, kr_ref, vt_ref, o_ref, *, tq, tk):
    i = pl.program_id(1)
    hs = [slice(h * 128, (h + 1) * 128) for h in range(MLA_HEADS)]
    q = [jnp.concatenate([qn_ref[0, :, sl], qr_ref[0, :, sl]], axis=1) for sl in hs]

    def tile(t, carries, mask):
        k0 = pl.multiple_of(t * tk, tk)
        kr = kr_ref[0, pl.ds(k0, tk), :]
        scores = [_dot_nt(jnp.concatenate([kn_ref[0, pl.ds(k0, tk), sl], kr], axis=1), q[h])
                  for h, sl in enumerate(hs)]
        return _online_softmax_steps(carries, scores, [vt_ref[0, sl, pl.ds(k0, tk)] for sl in hs], [mask] * len(hs))

    init = tuple(_softmax_init(tq, MLA_V_DIM) for _ in hs)
    t_diag = (i * tq) // tk
    carries = lax.fori_loop(0, t_diag, lambda t, c: tile(t, c, None), init)
    causal = (t_diag * tk + lax.broadcasted_iota(jnp.int32, (tk, tq), 0)
              <= i * tq + lax.broadcasted_iota(jnp.int32, (tk, tq), 1))
    for sl, carry in zip(hs, tile(t_diag, carries, causal)):
        o_ref[0, :, sl] = _softmax_finish(carry).T.astype(o_ref.dtype)


def mla_mixer(p, cols, g_q, g_kv, wq, wkv, cos, sin, *, tm=512, tq=512, tk=512):
    b, s, _ = p.shape
    tm, tq, tk = min(tm, s), min(tq, s), min(tk, s)
    col_q, col_kv, col_kr = cols
    assert tk % tq == 0 and col_q % MLA_Q_RANK == 0 and col_kv % 128 == 0 and col_kr % 128 == 0
    hw = MLA_HEADS * 128
    scale = (MLA_NOPE_DIM + MLA_ROPE_DIM) ** -0.5 * LOG2_E
    full = lambda shape: pl.BlockSpec(shape, lambda bi, i: (0,) * len(shape))
    row = lambda w: pl.BlockSpec((1, tm, w), lambda bi, i: (bi, i, 0))
    outs = pl.pallas_call(
        functools.partial(_mla_prep_kernel, scale=scale),
        grid=(b, s // tm),
        in_specs=[pl.BlockSpec((1, tm, MLA_Q_RANK), lambda bi, i: (bi, i, col_q // MLA_Q_RANK)),
                  pl.BlockSpec((1, tm, 128), lambda bi, i: (bi, i, col_kv // 128)),
                  pl.BlockSpec((1, tm, 128), lambda bi, i: (bi, i, col_kr // 128)),
                  full((1, MLA_Q_RANK)), full((1, MLA_KV_RANK)), full(wq.shape), full(wkv.shape),
                  pl.BlockSpec((tm, 128), lambda bi, i: (i, 0)), pl.BlockSpec((tm, 128), lambda bi, i: (i, 0))],
        out_specs=[row(hw), row(hw), row(hw), pl.BlockSpec((1, hw, tm), lambda bi, i: (bi, 0, i)), row(128)],
        out_shape=[jax.ShapeDtypeStruct((b, s, hw), BF16)] * 3 + [jax.ShapeDtypeStruct((b, hw, s), BF16),
                                                                   jax.ShapeDtypeStruct((b, s, 128), BF16)],
        compiler_params=_params("parallel", "parallel"),
        name="mla_prep",
    )(p, p, p, g_q.reshape(1, -1), g_kv.reshape(1, -1), wq, wkv, cos, sin)
    qn, qr, kn, vt, kr = outs
    q_spec = pl.BlockSpec((1, tq, hw), lambda bi, i: (bi, i, 0))
    per_batch = lambda shape: pl.BlockSpec((1,) + shape, lambda bi, i: (bi, 0, 0))
    return pl.pallas_call(
        functools.partial(_mla_attn_kernel, tq=tq, tk=tk),
        grid=(b, s // tq),
        in_specs=[q_spec, q_spec, per_batch((s, hw)), per_batch((s, 128)), per_batch((hw, s))],
        out_specs=q_spec,
        out_shape=jax.ShapeDtypeStruct((b, s, hw), BF16),
        compiler_params=_params("parallel", "arbitrary"),
        name="mla_attn",
    )(qn, qr, kn, kr, vt)


def _nsa_prep_kernel(p_ref, cos_ref, sin_ref, q_ref, kc_ref, vc_ref, ks_ref, kw_ref, vst_ref, vwt_ref, *, scale):
    cos, sin = cos_ref[...], sin_ref[...]
    hw = NSA_HEADS * NSA_HEAD_DIM
    half = NSA_ROT_DIM // 2
    tile = lambda col: p_ref[0, :, col * 128:(col + 1) * 128].astype(F32)
    for h in range(NSA_HEADS):
        q_ref[0, :, h * 128:(h + 1) * 128] = (_rope(tile(h), cos, sin, half) * scale).astype(BF16)
    base = hw // 128
    kc_ref[0] = _rope(tile(base), cos, sin, half)
    vc_ref[0] = tile(base + 1)
    ks_ref[0] = _rope(tile(base + 2), cos, sin, half).astype(BF16)
    kw_ref[0] = _rope(tile(base + 4), cos, sin, half).astype(BF16)
    vst_ref[0] = tile(base + 3).T.astype(BF16)
    vwt_ref[0] = tile(base + 5).T.astype(BF16)


def _gelu_tanh(x):
    return 0.5 * x * (1.0 + jnp.tanh(np.sqrt(2.0 / np.pi).astype(np.float32) * (x + 0.044715 * (x * x * x))))


def _split3_bf16(x):
    hi = x.astype(BF16)
    r1 = x - hi.astype(F32)
    mid = r1.astype(BF16)
    return hi, mid, (r1 - mid.astype(F32)).astype(BF16)


def _split2_bf16(x):
    hi = x.astype(BF16)
    return hi, (x - hi.astype(F32)).astype(BF16)


def _dot_split(a, w2_ref):
    hi, lo = _split2_bf16(a)
    return _dot(hi, w2_ref[0]) + (_dot(hi, w2_ref[1]) + _dot(lo, w2_ref[0]))


def _nsa_compress_kernel(k_ref, v_ref, pos_ref, w1_ref, w2_ref, kc3_ref, vct_ref):
    nrow = k_ref.shape[1]
    row = lax.broadcasted_iota(jnp.int32, (nrow, NSA_HEAD_DIM), 0)
    outs = []
    for z, x_ref in enumerate((k_ref, v_ref)):
        res = _dot(x_ref[0], w1_ref[z], HIGHEST)
        pb = _dot(pos_ref[z], w1_ref[z], HIGHEST)
        bias = pb[0:1, :NSA_HEAD_DIM] + pb[1:2, NSA_HEAD_DIM:]
        nxt = pltpu.roll(res[:, NSA_HEAD_DIM:], nrow - 1, axis=0)
        hid = _gelu_tanh(res[:, :NSA_HEAD_DIM] + nxt + bias)
        outs.append(jnp.where(row < nrow - 1, _dot(hid, w2_ref[z], HIGHEST), 0.0))
    for z, piece in enumerate(_split3_bf16(outs[0])):
        kc3_ref[0, z] = piece
    vct_ref[0] = outs[1].T.astype(BF16)


def _nsa_attn_kernel(q_ref, kc3_ref, vct_ref, covert_ref, ks_ref, vst_ref, kw_ref, vwt_ref, et_ref, g_ref, o_ref,
                     *, tq, tk, n_blk):
    i = pl.program_id(1)
    q0 = i * tq
    cols = NSA_HEADS * tq
    q4 = jnp.concatenate([q_ref[0, :, h * 128:(h + 1) * 128] for h in range(NSA_HEADS)], axis=0)

    ncmp = kc3_ref.shape[2]
    s = sum(_dot_nt(kc3_ref[0, z], q4) for z in range(3))
    n = lax.broadcasted_iota(jnp.int32, (ncmp, cols), 0)
    qpos_c = q0 + (lax.broadcasted_iota(jnp.int32, (ncmp, cols), 1) & (tq - 1))
    s = jnp.where((n * NSA_CMP_STRIDE + (NSA_CMP_LEN - 1) <= qpos_c) & (n < ncmp - 1), s, -jnp.inf)
    e = jnp.exp2(s - jnp.maximum(jnp.max(s, axis=0, keepdims=True), MASK_VALUE))
    l = jnp.sum(e, axis=0, keepdims=True)
    p = e / jnp.where(l > 0.0, l, 1.0)
    o_cmp = _dot(vct_ref[0], p.astype(BF16))
    psum = sum(p[:, h * tq:(h + 1) * tq] for h in range(NSA_HEADS))
    imp = sum(_dot(covert_ref[...], piece) for piece in _split3_bf16(psum))

    nb = covert_ref.shape[0]
    jb = lax.broadcasted_iota(jnp.int32, (nb, tq), 0)
    cur = (q0 + lax.broadcasted_iota(jnp.int32, (nb, tq), 1)) // NSA_SEL_BLOCK
    forced = (jb == 0) | (jb == cur) | (jb == cur - 1)
    visible = jb <= cur
    score = jnp.where(visible, imp + jnp.where(forced, NSA_FORCED_SCORE, 0.0), -jnp.inf)
    beaten = jnp.zeros((nb, tq), F32)
    for j in range(n_blk):
        row = score[j:j + 1, :]
        beaten = beaten + jnp.where((row > score) | ((row == score) & (jb > j)), 1.0, 0.0)
    keep = (beaten < float(NSA_TOP_N)) & visible
    sel = jnp.concatenate([jnp.where(keep, 0.0, MASK_VALUE), jnp.full((128 - nb, tq), MASK_VALUE, F32)], axis=0).astype(BF16)

    qh = [q_ref[0, :, h * 128:(h + 1) * 128] for h in range(NSA_HEADS)]
    kio = lax.broadcasted_iota(jnp.int32, (tk, tq), 0)
    qpos = q0 + lax.broadcasted_iota(jnp.int32, (tk, tq), 1)
    init = tuple(_softmax_init(tq, NSA_HEAD_DIM) for _ in qh)

    def slc_scores(k0):
        k = ks_ref[0, pl.ds(k0, tk), :]
        bias = _dot(et_ref[pl.ds(k0, tk), :], sel)
        return [_dot_nt(k, q) + bias for q in qh]

    def tile(t, carries, causal, window):
        k0 = pl.multiple_of(t * tk, tk)
        scores = slc_scores(k0)
        vts = [vst_ref[0, :, pl.ds(k0, tk)]] * NSA_HEADS
        masks = [(k0 + kio <= qpos) if causal else None] * NSA_HEADS
        if window:
            k = kw_ref[0, pl.ds(k0, tk), :]
            dist = qpos - (k0 + kio)
            scores += [_dot_nt(k, q) for q in qh]
            vts += [vwt_ref[0, :, pl.ds(k0, tk)]] * NSA_HEADS
            masks += [(dist >= 0) & (dist < NSA_WINDOW)] * NSA_HEADS
        return _online_softmax_steps(carries, scores, vts, masks)

    def tile_pair(gi, carries):
        k0 = pl.multiple_of(gi * (2 * tk), 2 * tk)
        scores, vts = [], []
        for half in range(2):
            scores += slc_scores(k0 + half * tk)
            vts += [vst_ref[0, :, pl.ds(k0 + half * tk, tk)]] * NSA_HEADS
        return _online_softmax_steps(carries, scores, vts, [None] * (2 * NSA_HEADS))

    t_diag = q0 // tk
    t_lo = jnp.maximum(q0 - NSA_WINDOW + 1, 0) // tk
    pairs = lax.fori_loop(0, t_lo // 2, tile_pair, init + init)
    o_slc = tuple(_softmax_merge(a, b) for a, b in zip(pairs[:NSA_HEADS], pairs[NSA_HEADS:]))
    o_slc = lax.fori_loop(2 * (t_lo // 2), t_lo, lambda t, c: tile(t, c, False, False), o_slc)
    both = lax.fori_loop(t_lo, t_diag, lambda t, c: tile(t, c, False, True), o_slc + init)
    both = tile(t_diag, both, True, True)
    o_slc, o_win = both[:NSA_HEADS], both[NSA_HEADS:]

    g = jax.nn.sigmoid(g_ref[0].astype(F32).T)
    for h in range(NSA_HEADS):
        o = (g[3 * h:3 * h + 1] * o_cmp[:, h * tq:(h + 1) * tq] + g[3 * h + 1:3 * h + 2] * _softmax_finish(o_slc[h])
             + g[3 * h + 2:3 * h + 3] * _softmax_finish(o_win[h]))
        o_ref[0, :, h * 128:(h + 1) * 128] = o.T.astype(o_ref.dtype)


def nsa_mixer(p, col0, pos_emb, w1, w2, cos, sin, *, tm=512, tq=256, tk=256):
    b, s, _ = p.shape
    assert col0 % NSA_IN == 0
    tm, tk = min(tm, s), min(tk, s)
    hw = NSA_HEADS * NSA_HEAD_DIM
    scale = NSA_HEAD_DIM ** -0.5 * LOG2_E
    row = lambda w: pl.BlockSpec((1, tm, w), lambda bi, i: (bi, i, 0))
    tab = pl.BlockSpec((tm, 128), lambda bi, i: (i, 0))
    tr_spec = pl.BlockSpec((1, 128, tm), lambda bi, i: (bi, 0, i))
    q, kc_in, vc_in, ks, kw, vst, vwt = pl.pallas_call(
        functools.partial(_nsa_prep_kernel, scale=scale),
        grid=(b, s // tm),
        in_specs=[pl.BlockSpec((1, tm, NSA_IN), lambda bi, i: (bi, i, col0 // NSA_IN)), tab, tab],
        out_specs=[row(hw), row(128), row(128), row(128), row(128), tr_spec, tr_spec],
        out_shape=[jax.ShapeDtypeStruct((b, s, hw), BF16), jax.ShapeDtypeStruct((b, s, 128), F32),
                   jax.ShapeDtypeStruct((b, s, 128), F32), jax.ShapeDtypeStruct((b, s, 128), BF16),
                   jax.ShapeDtypeStruct((b, s, 128), BF16), jax.ShapeDtypeStruct((b, 128, s), BF16),
                   jax.ShapeDtypeStruct((b, 128, s), BF16)],
        compiler_params=_params("parallel", "parallel"),
        name="nsa_prep",
    )(p, cos, sin)

    nrow = s // NSA_CMP_STRIDE
    fw = NSA_CMP_STRIDE * NSA_HEAD_DIM
    half = w1.shape[1] // 2
    w1r = jnp.concatenate([w1[:, :half], w1[:, half:]], axis=2)
    pos2 = jnp.pad(pos_emb.reshape(2, 2, fw), ((0, 0), (0, 6), (0, 0)))
    per_b = lambda shape: pl.BlockSpec((1,) + shape, lambda bi: (bi, 0, 0))
    const3 = lambda shape: pl.BlockSpec(shape, lambda bi: (0, 0, 0))
    kc3, vct = pl.pallas_call(
        _nsa_compress_kernel,
        grid=(b,),
        in_specs=[per_b((nrow, fw)), per_b((nrow, fw)), const3(pos2.shape), const3(w1r.shape), const3(w2.shape)],
        out_specs=[pl.BlockSpec((1, 3, nrow, NSA_HEAD_DIM), lambda bi: (bi, 0, 0, 0)), per_b((NSA_HEAD_DIM, nrow))],
        out_shape=[jax.ShapeDtypeStruct((b, 3, nrow, NSA_HEAD_DIM), BF16), jax.ShapeDtypeStruct((b, NSA_HEAD_DIM, nrow), BF16)],
        compiler_params=_params("parallel"),
        name="nsa_compress",
    )(kc_in.reshape(b, nrow, fw), vc_in.reshape(b, nrow, fw), pos2, w1r, w2)

    n_blk = s // NSA_SEL_BLOCK
    nb = 64
    assert n_blk <= nb and tq % 128 == 0 and tq & (tq - 1) == 0 and tk % tq == 0
    cmp_start = np.arange(nrow) * NSA_CMP_STRIDE
    cmp_end = cmp_start + NSA_CMP_LEN - 1
    blk_start = np.arange(nb) * NSA_SEL_BLOCK
    cover_t = ((cmp_start[None, :] <= blk_start[:, None] + NSA_SEL_BLOCK - 1)
               & (cmp_end[None, :] >= blk_start[:, None]) & (np.arange(nb)[:, None] < n_blk)
               & (np.arange(nrow)[None, :] < nrow - 1)).astype(np.float32)
    expand_t = ((np.arange(s)[:, None] // NSA_SEL_BLOCK) == np.arange(128)[None, :]).astype(np.float32)
    qrow = lambda w: pl.BlockSpec((1, tq, w), lambda bi, i: (bi, i, 0))
    k_spec = pl.BlockSpec((1, s, 128), lambda bi, i: (bi, 0, 0))
    vt_spec = pl.BlockSpec((1, 128, s), lambda bi, i: (bi, 0, 0))
    gate_block = (col0 + NSA_IN) // 128 - 1
    return pl.pallas_call(
        functools.partial(_nsa_attn_kernel, tq=tq, tk=tk, n_blk=n_blk),
        grid=(b, s // tq),
        in_specs=[qrow(hw), pl.BlockSpec((1, 3, nrow, NSA_HEAD_DIM), lambda bi, i: (bi, 0, 0, 0)),
                  pl.BlockSpec((1, NSA_HEAD_DIM, nrow), lambda bi, i: (bi, 0, 0)),
                  pl.BlockSpec((nb, nrow), lambda bi, i: (0, 0)),
                  k_spec, vt_spec, k_spec, vt_spec, pl.BlockSpec((s, 128), lambda bi, i: (0, 0)),
                  pl.BlockSpec((1, tq, 128), lambda bi, i: (bi, i, gate_block))],
        out_specs=qrow(hw),
        out_shape=jax.ShapeDtypeStruct((b, s, hw), BF16),
        compiler_params=_params("parallel", "arbitrary"),
        name="nsa_attn",
    )(q, kc3, vct, jnp.asarray(cover_t, dtype=BF16), ks, vst, kw, vwt, jnp.asarray(expand_t, dtype=BF16), p)


RWKV_Z_PIECES = 4


def _rwkv_prep_kernel(*refs):
    z_refs, zp_refs = refs[:RWKV_Z_PIECES], refs[RWKV_Z_PIECES:2 * RWKV_Z_PIECES]
    (mu_ref, w0_ref, a0_ref, kk_ref, ka_ref, rk_ref, w2_ref, a2_ref, g2_ref, bd_ref, csum_ref, kt_ref, rt_ref, kh_ref,
     bh_ref, kbar_ref, bbar_ref, v_ref, etot_ref, g_ref, bonus_ref) = refs[2 * RWKV_Z_PIECES:]
    i = pl.program_id(1)
    z = jnp.concatenate([r[0] for r in z_refs], axis=1).astype(F32)
    tm = z.shape[0]
    last = zp_refs[0].shape[1] - 1
    prev_row = jnp.concatenate([r[0, last:, :] for r in zp_refs], axis=1).astype(F32)
    prev_row = jnp.where(i > 0, prev_row, 0.0)
    rowid = lax.broadcasted_iota(jnp.int32, z.shape, 0)
    z_prev = jnp.where(rowid == 0, prev_row, pltpu.roll(z, 1, axis=0))
    z = z + (z_prev - z) * mu_ref[...]
    c = BRANCH_WIDTH
    r, k, v = z[:, :c], z[:, c:2 * c], z[:, 2 * c:3 * c]
    wd, ad, gd = z[:, 3 * c:3 * c + 128], z[:, 3 * c + 128:3 * c + 256], z[:, 3 * c + 256:3 * c + 512]
    wpre = -(w0_ref[...] + _dot_split(jnp.tanh(wd), w2_ref))
    w_log = -(jnp.maximum(wpre, 0.0) + jnp.log1p(jnp.exp(-jnp.abs(wpre)))) - 0.5
    lw = -jnp.exp(w_log)
    a = jax.nn.sigmoid(a0_ref[...] + _dot_split(ad, a2_ref))
    g_ref[0] = _dot_split(jax.nn.sigmoid(gd), g2_ref)
    kk = k * kk_ref[...]
    head_sum = lambda t: sum(_dot(piece, bd_ref[...]) for piece in _split2_bf16(t))
    kk = kk * lax.rsqrt(jnp.maximum(head_sum(kk * kk), 1e-24))
    k = k * (1.0 + (a - 1.0) * ka_ref[...])
    bonus_ref[0] = head_sum(r * k * rk_ref[...]) * v
    sums = sum(_dot(csum_ref[...], piece) for piece in _split3_bf16(lw))
    cum, total = sums[:tm], sums[tm:]
    dec_out, dec_end = jnp.exp(-cum), jnp.exp(total - cum)
    b = kk * a
    kt_ref[0] = (kk * jnp.exp(cum - lw)).astype(BF16)
    rt_ref[0] = (r * jnp.exp(cum)).astype(BF16)
    kh_ref[0] = (k * dec_out).astype(BF16)
    bh_ref[0] = (b * dec_out).astype(BF16)
    kbar_ref[0] = (k * dec_end).astype(BF16)
    bbar_ref[0] = (b * dec_end).astype(BF16)
    v_ref[0] = v.astype(BF16)
    etot_ref[0] = jnp.exp(total)


def _rwkv_scan_kernel(kt_ref, rt_ref, kh_ref, bh_ref, kbar_ref, bbar_ref, v_ref, etot_ref, g_ref, bonus_ref,
                      gw_ref, gb_ref, y_ref, state_ref, *, L):
    n_chunks = kt_ref.shape[1] // L
    N = RWKV_HEAD_DIM
    n_pairs = kt_ref.shape[2] // 128

    @pl.when(pl.program_id(1) == 0)
    def _():
        state_ref[...] = jnp.zeros_like(state_ref)

    ri = lax.broadcasted_iota(jnp.int32, (L, 2 * L), 0)
    ci = lax.broadcasted_iota(jnp.int32, (L, 2 * L), 1) & (L - 1)
    strict2, incl2 = ci < ri, ci <= ri
    eye = jnp.where(lax.broadcasted_iota(jnp.int32, (L, L), 0) == lax.broadcasted_iota(jnp.int32, (L, L), 1), 1.0, 0.0)
    first2 = lax.broadcasted_iota(jnp.int32, (2 * L, 128), 1) < N
    first = lax.broadcasted_iota(jnp.int32, (L, 128), 1) < N
    bd = (lax.broadcasted_iota(jnp.int32, (128, 128), 0) < N) == (lax.broadcasted_iota(jnp.int32, (128, 128), 1) < N)
    zero = jnp.zeros((), BF16)

    items = range(n_chunks * n_pairs)
    block = lambda ref, i: ref[0, (i // n_pairs) * L:(i // n_pairs + 1) * L, (i % n_pairs) * 128:(i % n_pairs + 1) * 128]
    xs_in = [jnp.concatenate([block(kt_ref, i), block(rt_ref, i)], axis=0) for i in items]
    ys_in = [jnp.concatenate([block(kh_ref, i), block(bh_ref, i)], axis=0) for i in items]
    gms = [_dot_nt(jnp.where(first2 if sub == 0 else ~first2, xs_in[i], zero), ys_in[i])
           for i in items for sub in range(2)]
    a_k = jnp.stack([jnp.where(strict2, gm[:L], 0.0) for gm in gms])
    a_r = jnp.stack([jnp.where(incl2, gm[L:], 0.0) for gm in gms])
    a_kb = a_k[:, :, L:]
    bmm = lambda a, b: jnp.einsum("hij,hjk->hik", a.astype(BF16), b.astype(BF16), preferred_element_type=F32)
    tinv = eye[None] - a_kb
    pw = bmm(a_kb, a_kb)
    for _ in range(int(np.log2(L)) - 2):
        both = bmm(jnp.concatenate([tinv, pw], axis=1), pw)
        tinv = tinv + both[:, :L]
        pw = both[:, L:]
    tinv = tinv + bmm(tinv, pw)

    pick = lambda t: jnp.where(first, t[:L], t[L:])
    hsum = lambda t: jnp.where(first, jnp.sum(jnp.where(first, t, 0.0), axis=-1, keepdims=True),
                               jnp.sum(jnp.where(first, 0.0, t), axis=-1, keepdims=True))
    states = [state_ref[p] for p in range(n_pairs)]
    sls = [slice(p * 128, (p + 1) * 128) for p in range(n_pairs)]
    pairs = range(n_pairs)
    stack2 = lambda t, i: jnp.concatenate([t[2 * i], t[2 * i + 1]], axis=0).astype(BF16)
    vs = [block(v_ref, i) for i in items]
    akkv = [pick(_dot(stack2(a_k[:, :, :L], i), vs[i])) for i in items]
    tk = [_dot(stack2(tinv, i), jnp.concatenate([xs_in[i][:L], akkv[i].astype(BF16)], axis=1)) for i in items]
    xm = [jnp.concatenate([pick(tk[i][:, :128]).astype(BF16), xs_in[i][L:]], axis=0) for i in items]
    u0 = [-pick(tk[i][:, 128:]) for i in items]
    for c in range(n_chunks):
        rs = slice(c * L, (c + 1) * L)
        i0 = c * n_pairs
        v = vs[i0:i0 + n_pairs]
        xs = [_dot_nt(xm[i0 + p], states[p].astype(BF16)) for p in pairs]
        vu = [jnp.concatenate([v[p], (u0[i0 + p] - xs[p][:L]).astype(BF16)], axis=0) for p in pairs]
        o = [xs[p][L:] + pick(_dot(stack2(a_r, i0 + p), vu[p])) for p in pairs]
        for p, sl in enumerate(sls):
            kb = jnp.concatenate([kbar_ref[0, rs, sl], bbar_ref[0, rs, sl]], axis=0)
            states[p] = states[p] * etot_ref[0, c * L:c * L + 1, sl] + jnp.where(bd, _dot_tn(vu[p], kb), 0.0)
        for p, sl in enumerate(sls):
            ctr = o[p] - hsum(o[p]) * (1.0 / N)
            on = ctr * lax.rsqrt(hsum(ctr * ctr) * (1.0 / N) + RWKV_GN_EPS)
            y_ref[0, rs, sl] = ((on * gw_ref[:, sl] + gb_ref[:, sl] + bonus_ref[0, rs, sl]) * g_ref[0, rs, sl]).astype(y_ref.dtype)
    for p in pairs:
        state_ref[p] = states[p]


def rwkv7_mixer(p, col0, mu, w0, w2, a0, a2, g2, k_k, k_a, r_k, gn_w, gn_b, *, tm=256):
    b, s, _ = p.shape
    zw = mu.shape[0]
    zh = zw // RWKV_Z_PIECES
    prev_rows = 16 if p.dtype == BF16 else 8
    assert col0 % zh == 0
    tm = min(tm, s)
    c, hd, nh = BRANCH_WIDTH, RWKV_HEAD_DIM, RWKV_HEADS
    L = min(RWKV_CHUNK, s)
    assert tm % L == 0 and L & (L - 1) == 0
    bd = jnp.asarray(np.kron(np.eye(nh, dtype=np.float32), np.ones((hd, hd), np.float32)), dtype=BF16)
    chunk = np.arange(tm) // L
    same = chunk[:, None] == chunk[None, :]
    csum = np.concatenate([same & (np.arange(tm)[None, :] <= np.arange(tm)[:, None]), same], axis=0)
    csum = jnp.asarray(csum.astype(np.float32), dtype=BF16)
    row = lambda w: pl.BlockSpec((1, tm, w), lambda bi, i: (bi, i, 0))
    vec = lambda w: pl.BlockSpec((1, w), lambda bi, i: (0, 0))
    mat = lambda shape: pl.BlockSpec(shape, lambda bi, i: (0,) * len(shape))
    w2, a2, g2 = (jnp.stack(_split2_bf16(w)) for w in (w2, a2, g2))
    outs = pl.pallas_call(
        _rwkv_prep_kernel,
        grid=(b, s // tm),
        in_specs=[pl.BlockSpec((1, tm, zh), functools.partial(lambda bi, i, j: (bi, i, j), j=col0 // zh + j))
                  for j in range(RWKV_Z_PIECES)]
        + [pl.BlockSpec((1, prev_rows, zh),
                        functools.partial(lambda bi, i, j: (bi, jnp.maximum(i * (tm // prev_rows) - 1, 0), j), j=col0 // zh + j))
           for j in range(RWKV_Z_PIECES)]
        + [vec(zw), vec(c), vec(c), vec(c), vec(c), vec(c), mat(w2.shape), mat(a2.shape), mat(g2.shape),
                  mat(bd.shape), mat(csum.shape)],
        out_specs=[row(c)] * 10,
        out_shape=[jax.ShapeDtypeStruct((b, s, c), BF16)] * 7 + [jax.ShapeDtypeStruct((b, s, c), F32)] * 3,
        compiler_params=_params("parallel", "parallel"),
        name="rwkv_prep",
    )(*([p] * (2 * RWKV_Z_PIECES)), mu.reshape(1, zw), w0.reshape(1, c), a0.reshape(1, c), k_k.reshape(1, c), k_a.reshape(1, c),
      r_k.reshape(1, c), w2, a2, g2, bd, csum)
    rows = min(RWKV_CHUNKS_PER_STEP * L, s)
    blk = pl.BlockSpec((1, rows, c), lambda bi, ci: (bi, ci, 0))
    gvec = pl.BlockSpec((1, c), lambda bi, ci: (0, 0))
    return pl.pallas_call(
        functools.partial(_rwkv_scan_kernel, L=L),
        grid=(b, s // rows),
        in_specs=[blk] * 10 + [gvec, gvec],
        out_specs=blk,
        out_shape=jax.ShapeDtypeStruct((b, s, c), BF16),
        scratch_shapes=[pltpu.VMEM((c // 128, 128, 128), F32)],
        compiler_params=_params("parallel", "arbitrary"),
        name="rwkv_scan",
    )(*outs, gn_w.reshape(1, c), gn_b.reshape(1, c))


def _retention_kernel(qk_ref, v_ref, g_ref, cos_ref, sin_ref, inner_ref, qd_ref, kd_ref, cd_ref, o_ref, state_ref):
    @pl.when(pl.program_id(1) == 0)
    def _():
        state_ref[...] = jnp.zeros_like(state_ref)

    C = qk_ref.shape[1]
    kw = RET_HEADS * RET_K_DIM
    cos, sin = cos_ref[...], sin_ref[...]
    rope = lambda t: jnp.concatenate(
        [_rope(t[:, j:j + 128], cos[:, j:j + 128], sin[:, j:j + 128], RET_K_DIM // 2, RET_K_DIM) for j in range(0, kw, 128)], axis=1)
    q = rope(qk_ref[0, :, :kw].astype(F32))
    k = rope(qk_ref[0, :, kw:].astype(F32)) * RET_K_DIM ** -0.5
    lane = lax.broadcasted_iota(jnp.int32, (C, 128), 1)
    heads = range(RET_HEADS)
    vsl = [slice(h * RET_V_DIM, (h + 1) * RET_V_DIM) for h in heads]
    own = lambda t, h: jnp.where((lane // RET_K_DIM) == h % 2, t[:, (h // 2) * 128:(h // 2 + 1) * 128], 0.0)
    qh = [own(q, h) for h in heads]
    kh = [own(k, h) for h in heads]
    vh = [v_ref[0, :, sl] for sl in vsl]
    st = [state_ref[h] for h in heads]
    scores = [_dot_nt(qh[h].astype(BF16), kh[h].astype(BF16)) * inner_ref[h] for h in heads]
    cross = [_dot((qh[h] * qd_ref[h]).astype(BF16), st[h].astype(BF16)) for h in heads]
    o = [_dot(scores[h].astype(BF16), vh[h]) + cross[h] for h in heads]
    for h in heads:
        state_ref[h] = st[h] * cd_ref[h, 0:1, :] + _dot_tn((kh[h] * kd_ref[h]).astype(BF16), vh[h])
    for h, sl in enumerate(vsl):
        gate = g_ref[0, :, sl].astype(F32)
        c = o[h] - jnp.mean(o[h], axis=-1, keepdims=True)
        y = c * lax.rsqrt(jnp.mean(c * c, axis=-1, keepdims=True) + NORM_EPS)
        o_ref[0, :, sl] = (y * (gate * jax.nn.sigmoid(gate))).astype(o_ref.dtype)


def retention_mixer(p, col0, cos, sin):
    b, s, _ = p.shape
    assert col0 % 512 == 0
    H, C = RET_HEADS, min(RET_CHUNK, s)
    log_gamma = jnp.log1p(-jnp.exp2(-5.0 - jnp.arange(H, dtype=F32)))
    n = jnp.arange(C, dtype=F32)
    dist = n[:, None] - n[None, :]
    inner = jnp.where(dist >= 0, jnp.exp(jnp.maximum(dist, 0.0) * log_gamma[:, None, None]), 0.0)
    q_decay = jnp.exp((n + 1.0) * log_gamma[:, None])
    k_decay = jnp.exp((C - 1.0 - n) * log_gamma[:, None])
    chunk_decay = jnp.exp(C * log_gamma)
    lanes = lambda t: jnp.broadcast_to(t[:, :, None], (H, C, 128))
    const = lambda shape: pl.BlockSpec(shape, lambda bi, ci: (0,) * len(shape))
    kw = RET_HEADS * RET_K_DIM
    return pl.pallas_call(
        _retention_kernel,
        grid=(b, s // C),
        in_specs=[pl.BlockSpec((1, C, 512), functools.partial(lambda bi, ci, j: (bi, ci, j), j=col0 // 512 + j))
                  for j in range(3)] + [
                  pl.BlockSpec((C, kw), lambda bi, ci: (ci, 0)), pl.BlockSpec((C, kw), lambda bi, ci: (ci, 0)),
                  const((H, C, C)), const((H, C, 128)), const((H, C, 128)), const((H, 8, 128))],
        out_specs=pl.BlockSpec((1, C, H * RET_V_DIM), lambda bi, ci: (bi, ci, 0)),
        out_shape=jax.ShapeDtypeStruct((b, s, H * RET_V_DIM), BF16),
        scratch_shapes=[pltpu.VMEM((H, 128, 128), F32)],
        compiler_params=_params("parallel", "arbitrary"),
        name="retention",
    )(p, p, p, cos, sin, inner, lanes(q_decay), lanes(k_decay), jnp.broadcast_to(chunk_decay[:, None, None], (H, 8, 128)))


def _pad_cols(w, width):
    return jnp.pad(w, [(0, 0)] * (w.ndim - 1) + [(0, width - w.shape[-1])])


def _rope_tables(s, inv_freq, width, fill_cos):
    inv_freq = np.asarray(inv_freq, np.float32)
    ang = np.arange(s, dtype=np.float32)[:, None] * inv_freq[None, :]
    cos, sin = np.cos(ang).astype(np.float32), np.sin(ang).astype(np.float32)
    rot = 2 * cos.shape[1]
    cos2 = np.concatenate([cos, cos, np.full((s, width - rot), fill_cos, np.float32)], axis=1)
    sin2 = np.concatenate([sin, sin, np.zeros((s, width - rot), np.float32)], axis=1)
    return cos2, sin2


def _inv_freq(rot_dim, theta):
    return np.float32(theta) ** (-np.arange(0, rot_dim, 2, dtype=np.float32) / np.float32(rot_dim))


def _split_offsets():
    sizes = (MLA_Q_RANK, MLA_KV_RANK, MLA_ROPE_DIM, 512, 128, 128, 128, 128, 128, 128, 12,
             3 * BRANCH_WIDTH + RWKV_DECAY_LORA + RWKV_A_LORA + RWKV_GATE_LORA, 256, 256, 512, 512, N_MIXERS * D_MODEL)
    offs = np.concatenate([[0], np.cumsum(sizes)])
    return [(int(offs[i]), int(offs[i + 1])) for i in range(len(sizes))]


def _pack_in_proj(w):
    w = w.astype(BF16)
    sl = [w[..., a:b] for a, b in _split_offsets()]
    (ql, kvl, kr, nq, nkc, nvc, nks, nvs, nkw, nvw, ng, rz, rq, rk, rv, rg, mg) = sl
    c = BRANCH_WIDTH
    w_rwkv = [rz[..., :3 * c], _pad_cols(rz[..., 3 * c:3 * c + 96], 128), _pad_cols(rz[..., 3 * c + 96:3 * c + 192], 128),
              rz[..., 3 * c + 192:]]
    groups = {"nsa": [nq, nkc, nvc, nks, nvs, nkw, nvw, _pad_cols(ng, 128)], "mla_kv": [kvl], "mla_q": [ql],
              "mla_kr": [_pad_cols(kr, 128)], "ret": [rq, rk, rv, rg], "gate": [mg], "rwkv": w_rwkv}
    cols, parts, off = {}, [], 0
    for name, ws in groups.items():
        cols[name] = off
        parts += ws
        off += sum(t.shape[-1] for t in ws)
    return jnp.concatenate(parts, axis=-1), cols


def _pack_mla_weights(w_uq, w_ukv):
    H, dn, dr, dv = MLA_HEADS, MLA_NOPE_DIM, MLA_ROPE_DIM, MLA_V_DIM
    uq = w_uq.reshape(-1, H, dn + dr)
    nope = uq[:, :, :dn].reshape(-1, H * dn)
    rope = uq[:, :, dn:]
    rope_p = _pad_cols(rope, 128).reshape(-1, H * 128)
    ukv = w_ukv.reshape(-1, H, dn + dv)
    wkv = jnp.concatenate([ukv[:, :, :dn].reshape(-1, H * dn), ukv[:, :, dn:].reshape(-1, H * dv)], axis=1)
    return jnp.concatenate([nope, rope_p], axis=1).astype(BF16), wkv.astype(BF16)


def _pad_rows(w, rows):
    return jnp.pad(w, ((0, rows - w.shape[0]), (0, 0)))


def kernel(x, w_in, w_branch, w_out, w_up, w_down, norm_gains, mla_g_q, mla_g_kv, mla_w_uq, mla_w_ukv,
           nsa_cmp_pos, nsa_cmp_w1, nsa_cmp_w2, rwkv_mu, rwkv_w0, rwkv_w2, rwkv_a0, rwkv_a2, rwkv_g2,
           rwkv_k_k, rwkv_k_a, rwkv_r_k, rwkv_gn_w, rwkv_gn_b):
    B, S, D = x.shape
    depth = w_in.shape[0]
    T = B * S
    tm, tm_in = min(512, T), min(1024, T)
    mla_cos, mla_sin = _rope_tables(S, _inv_freq(MLA_ROPE_DIM, ROPE_THETA), 128, 0.0)
    nsa_cos, nsa_sin = _rope_tables(S, _inv_freq(NSA_ROT_DIM, ROPE_THETA), 128, 1.0)
    ret_cos, ret_sin = _rope_tables(S, np.float32(RET_THETA) ** (-np.linspace(0.0, 1.0, RET_K_DIM // 2, dtype=np.float32)),
                                    RET_K_DIM, 0.0)
    ret_cos, ret_sin = np.tile(ret_cos, (1, RET_HEADS)), np.tile(ret_sin, (1, RET_HEADS))
    c = BRANCH_WIDTH
    x = x.reshape(T, D)
    w_all, cols = _pack_in_proj(w_in)
    tn_in = w_all.shape[-1] // 9
    assert tn_in % 256 == 0 and tn_in * 9 == w_all.shape[-1]
    w_branch, w_out, w_up, w_down = (w.astype(BF16) for w in (w_branch, w_out, w_up, w_down))
    for l in range(depth):
        g_pre = norm_gains[l, 0]
        p2 = norm_matmul(x, g_pre, w_all, l, BF16, tm_in, tn_in)
        p = p2.reshape(B, S, -1)

        wq, wkv = _pack_mla_weights(mla_w_uq[l], mla_w_ukv[l])
        y_mla = mla_mixer(p, (cols["mla_q"], cols["mla_kv"], cols["mla_kr"]), mla_g_q[l], mla_g_kv[l], wq, wkv, mla_cos, mla_sin)
        y_nsa = nsa_mixer(p, cols["nsa"], nsa_cmp_pos[l], nsa_cmp_w1[l], nsa_cmp_w2[l], nsa_cos, nsa_sin)
        mu = rwkv_mu[l]
        mu_p = jnp.concatenate([mu[:3 * c], _pad_cols(mu[3 * c:3 * c + 96], 128), _pad_cols(mu[3 * c + 96:3 * c + 192], 128),
                                mu[3 * c + 192:]])
        y_rwkv = rwkv7_mixer(p, cols["rwkv"], mu_p, rwkv_w0[l], _pad_rows(rwkv_w2[l], 128), rwkv_a0[l],
                             _pad_rows(rwkv_a2[l], 128), rwkv_g2[l], rwkv_k_k[l], rwkv_k_a[l], rwkv_r_k[l].reshape(-1),
                             rwkv_gn_w[l], rwkv_gn_b[l])
        y_ret = retention_mixer(p, cols["ret"], ret_cos, ret_sin)

        ys = [y.reshape(T, c) for y in (y_mla, y_nsa, y_rwkv, y_ret)]
        x, h = merge_out_residual(ys, p2, cols["gate"], w_branch, w_out, l, norm_gains[l, 1], norm_gains[l, 2], x, min(256, T))
        x = ffn_block(x, h, w_up, w_down, l, norm_gains[l, 3], tm, 1024)
    return x.reshape(B, S, D)
```

```python
import functools

import numpy as np
import jax
import jax.numpy as jnp
from jax import lax
from jax.experimental import pallas as pl
from jax.experimental.pallas import tpu as pltpu

F32 = jnp.float32
BF16 = jnp.bfloat16
HIGHEST = lax.Precision.HIGHEST

V7X_LANES = 128
V7X_VMEM_LIMIT_BYTES = 56 * 1024 * 1024

D_MODEL = 2048
N_MIXERS = 4
BRANCH_WIDTH = D_MODEL // N_MIXERS
ROPE_THETA = 500000.0
NORM_EPS = 1e-6
MASK_VALUE = -1e30
LOG2_E = float(np.log2(np.e))

MLA_NOPE_DIM = 128
MLA_ROPE_DIM = 64
MLA_V_DIM = 128
MLA_HEADS = 4
MLA_Q_RANK = 384
MLA_KV_RANK = 128

NSA_HEAD_DIM = 128
NSA_HEADS = 4
NSA_ROT_DIM = 32
NSA_CMP_LEN = 32
NSA_CMP_STRIDE = 16
NSA_SEL_BLOCK = 64
NSA_TOP_N = 16
NSA_WINDOW = 512
NSA_FORCED_SCORE = 1000.0
NSA_IN = NSA_HEADS * NSA_HEAD_DIM + 7 * 128

RWKV_HEAD_DIM = 64
RWKV_HEADS = 8
RWKV_DECAY_LORA = 96
RWKV_A_LORA = 96
RWKV_GATE_LORA = 256
RWKV_GN_EPS = 64e-5
RWKV_CHUNK = 64
RWKV_CHUNKS_PER_STEP = 4

RET_HEADS = 4
RET_V_DIM = 128
RET_K_DIM = 64
RET_CHUNK = 128
RET_THETA = 10000.0


def _params(*sem):
    return pltpu.CompilerParams(dimension_semantics=sem, vmem_limit_bytes=V7X_VMEM_LIMIT_BYTES)


def _dot(a, b, precision=None):
    return jnp.dot(a, b, preferred_element_type=F32, precision=precision)


def _dot_nt(a, b, precision=None):
    return lax.dot_general(a, b, (((1,), (1,)), ((), ())), preferred_element_type=F32, precision=precision)


def _dot_tn(a, b, precision=None):
    return lax.dot_general(a, b, (((0,), (0,)), ((), ())), preferred_element_type=F32, precision=precision)


def _rotate_half(x, half, period):
    lane = lax.broadcasted_iota(jnp.int32, x.shape, x.ndim - 1) & (period - 1)
    up = pltpu.roll(x, V7X_LANES - half, axis=x.ndim - 1)
    dn = pltpu.roll(x, half, axis=x.ndim - 1)
    return jnp.where(lane < half, -up, jnp.where(lane < 2 * half, dn, 0.0))


def _rope(x, cos, sin, half, period=V7X_LANES):
    return x * cos + _rotate_half(x, half, period) * sin


def _rms(x, g, eps=NORM_EPS):
    return x * lax.rsqrt(jnp.mean(x * x, axis=-1, keepdims=True) + eps) * g


def _norm_matmul_kernel(x_ref, g_ref, w_ref, o_ref, h_ref):
    @pl.when(pl.program_id(1) == 0)
    def _():
        h_ref[...] = _rms(x_ref[...], g_ref[...]).astype(BF16)

    o_ref[...] = _dot(h_ref[...], w_ref[...]).astype(o_ref.dtype)


def norm_matmul(x, g, w, layer, out_dtype, tm, tn):
    m, k = x.shape
    n = w.shape[2]
    assert m % tm == 0 and n % tn == 0
    return pl.pallas_call(
        _norm_matmul_kernel,
        grid=(m // tm, n // tn),
        in_specs=[pl.BlockSpec((tm, k), lambda i, j: (i, 0)),
                  pl.BlockSpec((1, k), lambda i, j: (0, 0)),
                  pl.BlockSpec((None, k, tn), lambda i, j: (layer, 0, j))],
        out_specs=pl.BlockSpec((tm, tn), lambda i, j: (i, j)),
        out_shape=jax.ShapeDtypeStruct((m, n), out_dtype),
        scratch_shapes=[pltpu.VMEM((tm, k), BF16)],
        compiler_params=_params("parallel", "arbitrary"),
        name="norm_matmul",
    )(x, g.reshape(1, k), w)


MERGE_GATE_PIECES = 4


def _merge_out_kernel(*refs):
    n = MERGE_GATE_PIECES
    y_refs, g_refs = refs[:N_MIXERS], refs[N_MIXERS:(1 + n) * N_MIXERS]
    wb_ref, wo_ref, gn_ref, gf_ref, x_ref, o_ref, h_ref = refs[(1 + n) * N_MIXERS:]
    width = g_refs[0].shape[1]
    merged = []
    for piece in range(n):
        acc = None
        for m in range(N_MIXERS):
            gate = 0.5 * jnp.tanh(0.5 * g_refs[n * m + piece][...].astype(F32)) + 0.5
            term = gate * _dot(y_refs[m][...], wb_ref[m, :, piece * width:(piece + 1) * width])
            acc = term if acc is None else acc + term
        merged.append(acc.astype(BF16))
    x_new = x_ref[...] + _rms(_dot(jnp.concatenate(merged, axis=1), wo_ref[...]), gn_ref[...])
    o_ref[...] = x_new
    h_ref[...] = _rms(x_new, gf_ref[...]).astype(BF16)


def merge_out_residual(ys, gates, col0, wb, wo, layer, gn, g_ffn, x, tm):
    m, c = ys[0].shape
    d = wb.shape[3]
    half = d // MERGE_GATE_PIECES
    assert col0 % half == 0
    once = pl.Buffered(1)
    return pl.pallas_call(
        _merge_out_kernel,
        grid=(m // tm,),
        in_specs=[pl.BlockSpec((tm, c), lambda i: (i, 0))] * N_MIXERS
        + [pl.BlockSpec((tm, half), functools.partial(lambda i, j: (i, j), j=col0 // half + jj))
           for jj in range(MERGE_GATE_PIECES * N_MIXERS)]
        + [pl.BlockSpec((None, N_MIXERS, c, d), lambda i: (layer, 0, 0, 0), pipeline_mode=once),
           pl.BlockSpec((None, d, d), lambda i: (layer, 0, 0), pipeline_mode=once),
           pl.BlockSpec((1, d), lambda i: (0, 0)),
           pl.BlockSpec((1, d), lambda i: (0, 0)),
           pl.BlockSpec((tm, d), lambda i: (i, 0))],
        out_specs=[pl.BlockSpec((tm, d), lambda i: (i, 0))] * 2,
        out_shape=[jax.ShapeDtypeStruct((m, d), F32), jax.ShapeDtypeStruct((m, d), BF16)],
        compiler_params=_params("parallel"),
        name="merge_out_residual",
    )(*ys, *([gates] * (MERGE_GATE_PIECES * N_MIXERS)), wb, wo, gn.reshape(1, d), g_ffn.reshape(1, d), x)


def _ffn_kernel(x_ref, h_ref, wu_ref, wd_ref, g2_ref, o_ref, acc_ref):
    f = pl.program_id(1)

    @pl.when(f == 0)
    def _():
        acc_ref[...] = jnp.zeros_like(acc_ref)

    u = jnp.maximum(_dot(h_ref[...], wu_ref[...]), 0.0)
    acc_ref[...] += _dot((u * u).astype(BF16), wd_ref[...])

    @pl.when(f == pl.num_programs(1) - 1)
    def _():
        o_ref[...] = x_ref[...] + _rms(acc_ref[...], g2_ref[...])


def ffn_block(x, h, wu, wd, layer, g2, tm, tf):
    m, d = x.shape
    ff = wu.shape[2]
    return pl.pallas_call(
        _ffn_kernel,
        grid=(m // tm, ff // tf),
        in_specs=[pl.BlockSpec((tm, d), lambda i, f: (i, 0)),
                  pl.BlockSpec((tm, d), lambda i, f: (i, 0)),
                  pl.BlockSpec((None, d, tf), lambda i, f: (layer, 0, f)),
                  pl.BlockSpec((None, tf, d), lambda i, f: (layer, f, 0)),
                  pl.BlockSpec((1, d), lambda i, f: (0, 0))],
        out_specs=pl.BlockSpec((tm, d), lambda i, f: (i, 0)),
        out_shape=jax.ShapeDtypeStruct((m, d), F32),
        scratch_shapes=[pltpu.VMEM((tm, d), F32)],
        compiler_params=_params("parallel", "arbitrary"),
        name="ffn_block",
    )(x, h, wu, wd, g2.reshape(1, d))


def _online_softmax_steps(carries, scores, vt_tiles, masks):
    scores = [s if mask is None else jnp.where(mask, s, -jnp.inf) for s, mask in zip(scores, masks)]
    out = []
    for (m_prev, l_prev, acc_prev), s, vt in zip(carries, scores, vt_tiles):
        m_new = jnp.maximum(m_prev, jnp.max(s, axis=0, keepdims=True))
        alpha = jnp.exp2(m_prev - m_new)
        p = jnp.exp2(s - m_new)
        out.append((m_new, alpha * l_prev + jnp.sum(p, axis=0, keepdims=True), alpha * acc_prev + _dot(vt, p.astype(BF16))))
    return tuple(out)


def _softmax_init(cols, dv):
    return (jnp.full((1, cols), MASK_VALUE, F32), jnp.zeros((1, cols), F32), jnp.zeros((dv, cols), F32))


def _softmax_merge(a, b):
    (ma, la, acca), (mb, lb, accb) = a, b
    m = jnp.maximum(ma, mb)
    wa, wb = jnp.exp2(ma - m), jnp.exp2(mb - m)
    return m, la * wa + lb * wb, acca * wa + accb * wb


def _softmax_finish(carry):
    _, l, acc = carry
    return acc / jnp.where(l > 0.0, l, 1.0)


def _mla_prep_kernel(ql_ref, kvl_ref, krl_ref, gq_ref, gkv_ref, wq_ref, wkv_ref, cos_ref, sin_ref,
                     qn_ref, qr_ref, kn_ref, vt_ref, kr_ref, *, scale):
    nq = _rms(ql_ref[0].astype(F32), gq_ref[...]).astype(BF16)
    nkv = _rms(kvl_ref[0].astype(F32), gkv_ref[...]).astype(BF16)
    q = _dot(nq, wq_ref[...])
    kv = _dot(nkv, wkv_ref[...])
    cos, sin = cos_ref[...], sin_ref[...]
    hw = MLA_HEADS * MLA_NOPE_DIM
    half = MLA_ROPE_DIM // 2
    qn_ref[0] = (q[:, :hw] * scale).astype(BF16)
    for h in range(MLA_HEADS):
        qr_ref[0, :, h * 128:(h + 1) * 128] = (_rope(q[:, hw + h * 128: hw + (h + 1) * 128], cos, sin, half) * scale).astype(BF16)
    kn_ref[0] = kv[:, :hw].astype(BF16)
    vt_ref[0] = kv[:, hw:].T.astype(BF16)
    kr_ref[0] = _rope(krl_ref[0].astype(F32), cos, sin, half).astype(BF16)


def _mla_attn_kernel(qn_ref, qr_ref, kn_ref, kr_ref, vt_ref, o_ref, *, tq, tk):
    i = pl.program_id(1)
    hs = [slice(h * 128, (h + 1) * 128) for h in range(MLA_HEADS)]
    q = [jnp.concatenate([qn_ref[0, :, sl], qr_ref[0, :, sl]], axis=1) for sl in hs]

    def tile(t, carries, mask):
        k0 = pl.multiple_of(t * tk, tk)
        kr = kr_ref[0, pl.ds(k0, tk), :]
        scores = [_dot_nt(jnp.concatenate([kn_ref[0, pl.ds(k0, tk), sl], kr], axis=1), q[h])
                  for h, sl in enumerate(hs)]
        return _online_softmax_steps(carries, scores, [vt_ref[0, sl, pl.ds(k0, tk)] for sl in hs], [mask] * len(hs))

    init = tuple(_softmax_init(tq, MLA_V_DIM) for _ in hs)
    t_diag = (i * tq) // tk
    carries = lax.fori_loop(0, t_diag, lambda t, c: tile(t, c, None), init)
    causal = (t_diag * tk + lax.broadcasted_iota(jnp.int32, (tk, tq), 0)
              <= i * tq + lax.broadcasted_iota(jnp.int32, (tk, tq), 1))
    for sl, carry in zip(hs, tile(t_diag, carries, causal)):
        o_ref[0, :, sl] = _softmax_finish(carry).T.astype(o_ref.dtype)


def mla_mixer(p, cols, g_q, g_kv, wq, wkv, cos, sin, *, tm=512, tq=512, tk=512):
    b, s, _ = p.shape
    tm, tq, tk = min(tm, s), min(tq, s), min(tk, s)
    col_q, col_kv, col_kr = cols
    assert tk % tq == 0 and col_q % MLA_Q_RANK == 0 and col_kv % 128 == 0 and col_kr % 128 == 0
    hw = MLA_HEADS * 128
    scale = (MLA_NOPE_DIM + MLA_ROPE_DIM) ** -0.5 * LOG2_E
    full = lambda shape: pl.BlockSpec(shape, lambda bi, i: (0,) * len(shape))
    row = lambda w: pl.BlockSpec((1, tm, w), lambda bi, i: (bi, i, 0))
    outs = pl.pallas_call(
        functools.partial(_mla_prep_kernel, scale=scale),
        grid=(b, s // tm),
        in_specs=[pl.BlockSpec((1, tm, MLA_Q_RANK), lambda bi, i: (bi, i, col_q // MLA_Q_RANK)),
                  pl.BlockSpec((1, tm, 128), lambda bi, i: (bi, i, col_kv // 128)),
                  pl.BlockSpec((1, tm, 128), lambda bi, i: (bi, i, col_kr // 128)),
                  full((1, MLA_Q_RANK)), full((1, MLA_KV_RANK)), full(wq.shape), full(wkv.shape),
                  pl.BlockSpec((tm, 128), lambda bi, i: (i, 0)), pl.BlockSpec((tm, 128), lambda bi, i: (i, 0))],
        out_specs=[row(hw), row(hw), row(hw), pl.BlockSpec((1, hw, tm), lambda bi, i: (bi, 0, i)), row(128)],
        out_shape=[jax.ShapeDtypeStruct((b, s, hw), BF16)] * 3 + [jax.ShapeDtypeStruct((b, hw, s), BF16),
                                                                   jax.ShapeDtypeStruct((b, s, 128), BF16)],
        compiler_params=_params("parallel", "parallel"),
        name="mla_prep",
    )(p, p, p, g_q.reshape(1, -1), g_kv.reshape(1, -1), wq, wkv, cos, sin)
    qn, qr, kn, vt, kr = outs
    q_spec = pl.BlockSpec((1, tq, hw), lambda bi, i: (bi, i, 0))
    per_batch = lambda shape: pl.BlockSpec((1,) + shape, lambda bi, i: (bi, 0, 0))
    return pl.pallas_call(
        functools.partial(_mla_attn_kernel, tq=tq, tk=tk),
        grid=(b, s // tq),
        in_specs=[q_spec, q_spec, per_batch((s, hw)), per_batch((s, 128)), per_batch((hw, s))],
        out_specs=q_spec,
        out_shape=jax.ShapeDtypeStruct((b, s, hw), BF16),
        compiler_params=_params("parallel", "arbitrary"),
        name="mla_attn",
    )(qn, qr, kn, kr, vt)


def _nsa_prep_kernel(p_ref, cos_ref, sin_ref, q_ref, kc_ref, vc_ref, ks_ref, kw_ref, vst_ref, vwt_ref, *, scale):
    cos, sin = cos_ref[...], sin_ref[...]
    hw = NSA_HEADS * NSA_HEAD_DIM
    half = NSA_ROT_DIM // 2
    tile = lambda col: p_ref[0, :, col * 128:(col + 1) * 128].astype(F32)
    for h in range(NSA_HEADS):
        q_ref[0, :, h * 128:(h + 1) * 128] = (_rope(tile(h), cos, sin, half) * scale).astype(BF16)
    base = hw // 128
    kc_ref[0] = _rope(tile(base), cos, sin, half)
    vc_ref[0] = tile(base + 1)
    ks_ref[0] = _rope(tile(base + 2), cos, sin, half).astype(BF16)
    kw_ref[0] = _rope(tile(base + 4), cos, sin, half).astype(BF16)
    vst_ref[0] = tile(base + 3).T.astype(BF16)
    vwt_ref[0] = tile(base + 5).T.astype(BF16)


def _gelu_tanh(x):
    return 0.5 * x * (1.0 + jnp.tanh(np.sqrt(2.0 / np.pi).astype(np.float32) * (x + 0.044715 * (x * x * x))))


def _split3_bf16(x):
    hi = x.astype(BF16)
    r1 = x - hi.astype(F32)
    mid = r1.astype(BF16)
    return hi, mid, (r1 - mid.astype(F32)).astype(BF16)


def _split2_bf16(x):
    hi = x.astype(BF16)
    return hi, (x - hi.astype(F32)).astype(BF16)


def _dot_split(a, w2_ref):
    hi, lo = _split2_bf16(a)
    return _dot(hi, w2_ref[0]) + (_dot(hi, w2_ref[1]) + _dot(lo, w2_ref[0]))


def _nsa_compress_kernel(k_ref, v_ref, pos_ref, w1_ref, w2_ref, kc3_ref, vct_ref):
    nrow = k_ref.shape[1]
    row = lax.broadcasted_iota(jnp.int32, (nrow, NSA_HEAD_DIM), 0)
    outs = []
    for z, x_ref in enumerate((k_ref, v_ref)):
        res = _dot(x_ref[0], w1_ref[z], HIGHEST)
        pb = _dot(pos_ref[z], w1_ref[z], HIGHEST)
        bias = pb[0:1, :NSA_HEAD_DIM] + pb[1:2, NSA_HEAD_DIM:]
        nxt = pltpu.roll(res[:, NSA_HEAD_DIM:], nrow - 1, axis=0)
        hid = _gelu_tanh(res[:, :NSA_HEAD_DIM] + nxt + bias)
        outs.append(jnp.where(row < nrow - 1, _dot(hid, w2_ref[z], HIGHEST), 0.0))
    for z, piece in enumerate(_split3_bf16(outs[0])):
        kc3_ref[0, z] = piece
    vct_ref[0] = outs[1].T.astype(BF16)


def _nsa_attn_kernel(q_ref, kc3_ref, vct_ref, covert_ref, ks_ref, vst_ref, kw_ref, vwt_ref, et_ref, g_ref, o_ref,
                     *, tq, tk, n_blk):
    i = pl.program_id(1)
    q0 = i * tq
    cols = NSA_HEADS * tq
    q4 = jnp.concatenate([q_ref[0, :, h * 128:(h + 1) * 128] for h in range(NSA_HEADS)], axis=0)

    ncmp = kc3_ref.shape[2]
    s = sum(_dot_nt(kc3_ref[0, z], q4) for z in range(3))
    n = lax.broadcasted_iota(jnp.int32, (ncmp, cols), 0)
    qpos_c = q0 + (lax.broadcasted_iota(jnp.int32, (ncmp, cols), 1) & (tq - 1))
    s = jnp.where((n * NSA_CMP_STRIDE + (NSA_CMP_LEN - 1) <= qpos_c) & (n < ncmp - 1), s, -jnp.inf)
    e = jnp.exp2(s - jnp.maximum(jnp.max(s, axis=0, keepdims=True), MASK_VALUE))
    l = jnp.sum(e, axis=0, keepdims=True)
    p = e / jnp.where(l > 0.0, l, 1.0)
    o_cmp = _dot(vct_ref[0], p.astype(BF16))
    psum = sum(p[:, h * tq:(h + 1) * tq] for h in range(NSA_HEADS))
    imp = sum(_dot(covert_ref[...], piece) for piece in _split3_bf16(psum))

    nb = covert_ref.shape[0]
    jb = lax.broadcasted_iota(jnp.int32, (nb, tq), 0)
    cur = (q0 + lax.broadcasted_iota(jnp.int32, (nb, tq), 1)) // NSA_SEL_BLOCK
    forced = (jb == 0) | (jb == cur) | (jb == cur - 1)
    visible = jb <= cur
    score = jnp.where(visible, imp + jnp.where(forced, NSA_FORCED_SCORE, 0.0), -jnp.inf)
    beaten = jnp.zeros((nb, tq), F32)
    for j in range(n_blk):
        row = score[j:j + 1, :]
        beaten = beaten + jnp.where((row > score) | ((row == score) & (jb > j)), 1.0, 0.0)
    keep = (beaten < float(NSA_TOP_N)) & visible
    sel = jnp.concatenate([jnp.where(keep, 0.0, MASK_VALUE), jnp.full((128 - nb, tq), MASK_VALUE, F32)], axis=0).astype(BF16)

    qh = [q_ref[0, :, h * 128:(h + 1) * 128] for h in range(NSA_HEADS)]
    kio = lax.broadcasted_iota(jnp.int32, (tk, tq), 0)
    qpos = q0 + lax.broadcasted_iota(jnp.int32, (tk, tq), 1)
    init = tuple(_softmax_init(tq, NSA_HEAD_DIM) for _ in qh)

    def slc_scores(k0):
        k = ks_ref[0, pl.ds(k0, tk), :]
        bias = _dot(et_ref[pl.ds(k0, tk), :], sel)
        return [_dot_nt(k, q) + bias for q in qh]

    def tile(t, carries, causal, window):
        k0 = pl.multiple_of(t * tk, tk)
        scores = slc_scores(k0)
        vts = [vst_ref[0, :, pl.ds(k0, tk)]] * NSA_HEADS
        masks = [(k0 + kio <= qpos) if causal else None] * NSA_HEADS
        if window:
            k = kw_ref[0, pl.ds(k0, tk), :]
            dist = qpos - (k0 + kio)
            scores += [_dot_nt(k, q) for q in qh]
            vts += [vwt_ref[0, :, pl.ds(k0, tk)]] * NSA_HEADS
            masks += [(dist >= 0) & (dist < NSA_WINDOW)] * NSA_HEADS
        return _online_softmax_steps(carries, scores, vts, masks)

    def tile_pair(gi, carries):
        k0 = pl.multiple_of(gi * (2 * tk), 2 * tk)
        scores, vts = [], []
        for half in range(2):
            scores += slc_scores(k0 + half * tk)
            vts += [vst_ref[0, :, pl.ds(k0 + half * tk, tk)]] * NSA_HEADS
        return _online_softmax_steps(carries, scores, vts, [None] * (2 * NSA_HEADS))

    t_diag = q0 // tk
    t_lo = jnp.maximum(q0 - NSA_WINDOW + 1, 0) // tk
    pairs = lax.fori_loop(0, t_lo // 2, tile_pair, init + init)
    o_slc = tuple(_softmax_merge(a, b) for a, b in zip(pairs[:NSA_HEADS], pairs[NSA_HEADS:]))
    o_slc = lax.fori_loop(2 * (t_lo // 2), t_lo, lambda t, c: tile(t, c, False, False), o_slc)
    both = lax.fori_loop(t_lo, t_diag, lambda t, c: tile(t, c, False, True), o_slc + init)
    both = tile(t_diag, both, True, True)
    o_slc, o_win = both[:NSA_HEADS], both[NSA_HEADS:]

    g = jax.nn.sigmoid(g_ref[0].astype(F32).T)
    for h in range(NSA_HEADS):
        o = (g[3 * h:3 * h + 1] * o_cmp[:, h * tq:(h + 1) * tq] + g[3 * h + 1:3 * h + 2] * _softmax_finish(o_slc[h])
             + g[3 * h + 2:3 * h + 3] * _softmax_finish(o_win[h]))
        o_ref[0, :, h * 128:(h + 1) * 128] = o.T.astype(o_ref.dtype)


def nsa_mixer(p, col0, pos_emb, w1, w2, cos, sin, *, tm=512, tq=512, tk=512):
    b, s, _ = p.shape
    assert col0 % NSA_IN == 0
    tm, tk = min(tm, s), min(tk, s)
    hw = NSA_HEADS * NSA_HEAD_DIM
    scale = NSA_HEAD_DIM ** -0.5 * LOG2_E
    row = lambda w: pl.BlockSpec((1, tm, w), lambda bi, i: (bi, i, 0))
    tab = pl.BlockSpec((tm, 128), lambda bi, i: (i, 0))
    tr_spec = pl.BlockSpec((1, 128, tm), lambda bi, i: (bi, 0, i))
    q, kc_in, vc_in, ks, kw, vst, vwt = pl.pallas_call(
        functools.partial(_nsa_prep_kernel, scale=scale),
        grid=(b, s // tm),
        in_specs=[pl.BlockSpec((1, tm, NSA_IN), lambda bi, i: (bi, i, col0 // NSA_IN)), tab, tab],
        out_specs=[row(hw), row(128), row(128), row(128), row(128), tr_spec, tr_spec],
        out_shape=[jax.ShapeDtypeStruct((b, s, hw), BF16), jax.ShapeDtypeStruct((b, s, 128), F32),
                   jax.ShapeDtypeStruct((b, s, 128), F32), jax.ShapeDtypeStruct((b, s, 128), BF16),
                   jax.ShapeDtypeStruct((b, s, 128), BF16), jax.ShapeDtypeStruct((b, 128, s), BF16),
                   jax.ShapeDtypeStruct((b, 128, s), BF16)],
        compiler_params=_params("parallel", "parallel"),
        name="nsa_prep",
    )(p, cos, sin)

    nrow = s // NSA_CMP_STRIDE
    fw = NSA_CMP_STRIDE * NSA_HEAD_DIM
    half = w1.shape[1] // 2
    w1r = jnp.concatenate([w1[:, :half], w1[:, half:]], axis=2)
    pos2 = jnp.pad(pos_emb.reshape(2, 2, fw), ((0, 0), (0, 6), (0, 0)))
    per_b = lambda shape: pl.BlockSpec((1,) + shape, lambda bi: (bi, 0, 0))
    const3 = lambda shape: pl.BlockSpec(shape, lambda bi: (0, 0, 0))
    kc3, vct = pl.pallas_call(
        _nsa_compress_kernel,
        grid=(b,),
        in_specs=[per_b((nrow, fw)), per_b((nrow, fw)), const3(pos2.shape), const3(w1r.shape), const3(w2.shape)],
        out_specs=[pl.BlockSpec((1, 3, nrow, NSA_HEAD_DIM), lambda bi: (bi, 0, 0, 0)), per_b((NSA_HEAD_DIM, nrow))],
        out_shape=[jax.ShapeDtypeStruct((b, 3, nrow, NSA_HEAD_DIM), BF16), jax.ShapeDtypeStruct((b, NSA_HEAD_DIM, nrow), BF16)],
        compiler_params=_params("parallel"),
        name="nsa_compress",
    )(kc_in.reshape(b, nrow, fw), vc_in.reshape(b, nrow, fw), pos2, w1r, w2)

    n_blk = s // NSA_SEL_BLOCK
    nb = 64
    assert n_blk <= nb and tq % 128 == 0 and tq & (tq - 1) == 0 and tk % tq == 0
    cmp_start = np.arange(nrow) * NSA_CMP_STRIDE
    cmp_end = cmp_start + NSA_CMP_LEN - 1
    blk_start = np.arange(nb) * NSA_SEL_BLOCK
    cover_t = ((cmp_start[None, :] <= blk_start[:, None] + NSA_SEL_BLOCK - 1)
               & (cmp_end[None, :] >= blk_start[:, None]) & (np.arange(nb)[:, None] < n_blk)
               & (np.arange(nrow)[None, :] < nrow - 1)).astype(np.float32)
    expand_t = ((np.arange(s)[:, None] // NSA_SEL_BLOCK) == np.arange(128)[None, :]).astype(np.float32)
    qrow = lambda w: pl.BlockSpec((1, tq, w), lambda bi, i: (bi, i, 0))
    k_spec = pl.BlockSpec((1, s, 128), lambda bi, i: (bi, 0, 0))
    vt_spec = pl.BlockSpec((1, 128, s), lambda bi, i: (bi, 0, 0))
    gate_block = (col0 + NSA_IN) // 128 - 1
    return pl.pallas_call(
        functools.partial(_nsa_attn_kernel, tq=tq, tk=tk, n_blk=n_blk),
        grid=(b, s // tq),
        in_specs=[qrow(hw), pl.BlockSpec((1, 3, nrow, NSA_HEAD_DIM), lambda bi, i: (bi, 0, 0, 0)),
                  pl.BlockSpec((1, NSA_HEAD_DIM, nrow), lambda bi, i: (bi, 0, 0)),
                  pl.BlockSpec((nb, nrow), lambda bi, i: (0, 0)),
                  k_spec, vt_spec, k_spec, vt_spec, pl.BlockSpec((s, 128), lambda bi, i: (0, 0)),
                  pl.BlockSpec((1, tq, 128), lambda bi, i: (bi, i, gate_block))],
        out_specs=qrow(hw),
        out_shape=jax.ShapeDtypeStruct((b, s, hw), BF16),
        compiler_params=_params("parallel", "arbitrary"),
        name="nsa_attn",
    )(q, kc3, vct, jnp.asarray(cover_t, dtype=BF16), ks, vst, kw, vwt, jnp.asarray(expand_t, dtype=BF16), p)


RWKV_Z_PIECES = 4


def _rwkv_prep_kernel(*refs):
    z_refs, zp_refs = refs[:RWKV_Z_PIECES], refs[RWKV_Z_PIECES:2 * RWKV_Z_PIECES]
    (mu_ref, w0_ref, a0_ref, kk_ref, ka_ref, rk_ref, w2_ref, a2_ref, g2_ref, bd_ref, csum_ref, kt_ref, rt_ref, kh_ref,
     bh_ref, kbar_ref, bbar_ref, v_ref, etot_ref, g_ref, bonus_ref) = refs[2 * RWKV_Z_PIECES:]
    i = pl.program_id(1)
    z = jnp.concatenate([r[0] for r in z_refs], axis=1).astype(F32)
    tm = z.shape[0]
    last = zp_refs[0].shape[1] - 1
    prev_row = jnp.concatenate([r[0, last:, :] for r in zp_refs], axis=1).astype(F32)
    prev_row = jnp.where(i > 0, prev_row, 0.0)
    rowid = lax.broadcasted_iota(jnp.int32, z.shape, 0)
    z_prev = jnp.where(rowid == 0, prev_row, pltpu.roll(z, 1, axis=0))
    z = z + (z_prev - z) * mu_ref[...]
    c = BRANCH_WIDTH
    r, k, v = z[:, :c], z[:, c:2 * c], z[:, 2 * c:3 * c]
    wd, ad, gd = z[:, 3 * c:3 * c + 128], z[:, 3 * c + 128:3 * c + 256], z[:, 3 * c + 256:3 * c + 512]
    wpre = -(w0_ref[...] + _dot_split(jnp.tanh(wd), w2_ref))
    w_log = -(jnp.maximum(wpre, 0.0) + jnp.log1p(jnp.exp(-jnp.abs(wpre)))) - 0.5
    lw = -jnp.exp(w_log)
    a = jax.nn.sigmoid(a0_ref[...] + _dot_split(ad, a2_ref))
    g_ref[0] = _dot_split(jax.nn.sigmoid(gd), g2_ref)
    kk = k * kk_ref[...]
    head_sum = lambda t: sum(_dot(piece, bd_ref[...]) for piece in _split2_bf16(t))
    kk = kk * lax.rsqrt(jnp.maximum(head_sum(kk * kk), 1e-24))
    k = k * (1.0 + (a - 1.0) * ka_ref[...])
    bonus_ref[0] = head_sum(r * k * rk_ref[...]) * v
    sums = sum(_dot(csum_ref[...], piece) for piece in _split3_bf16(lw))
    cum, total = sums[:tm], sums[tm:]
    dec_out, dec_end = jnp.exp(-cum), jnp.exp(total - cum)
    b = kk * a
    kt_ref[0] = (kk * jnp.exp(cum - lw)).astype(BF16)
    rt_ref[0] = (r * jnp.exp(cum)).astype(BF16)
    kh_ref[0] = (k * dec_out).astype(BF16)
    bh_ref[0] = (b * dec_out).astype(BF16)
    kbar_ref[0] = (k * dec_end).astype(BF16)
    bbar_ref[0] = (b * dec_end).astype(BF16)
    v_ref[0] = v.astype(BF16)
    etot_ref[0] = jnp.exp(total)


def _rwkv_scan_kernel(kt_ref, rt_ref, kh_ref, bh_ref, kbar_ref, bbar_ref, v_ref, etot_ref, g_ref, bonus_ref,
                      gw_ref, gb_ref, y_ref, state_ref, *, L):
    n_chunks = kt_ref.shape[1] // L
    N = RWKV_HEAD_DIM
    n_pairs = kt_ref.shape[2] // 128

    @pl.when(pl.program_id(1) == 0)
    def _():
        state_ref[...] = jnp.zeros_like(state_ref)

    ri = lax.broadcasted_iota(jnp.int32, (L, 2 * L), 0)
    ci = lax.broadcasted_iota(jnp.int32, (L, 2 * L), 1) & (L - 1)
    strict2, incl2 = ci < ri, ci <= ri
    eye = jnp.where(lax.broadcasted_iota(jnp.int32, (L, L), 0) == lax.broadcasted_iota(jnp.int32, (L, L), 1), 1.0, 0.0)
    first2 = lax.broadcasted_iota(jnp.int32, (2 * L, 128), 1) < N
    first = lax.broadcasted_iota(jnp.int32, (L, 128), 1) < N
    bd = (lax.broadcasted_iota(jnp.int32, (128, 128), 0) < N) == (lax.broadcasted_iota(jnp.int32, (128, 128), 1) < N)
    zero = jnp.zeros((), BF16)

    items = range(n_chunks * n_pairs)
    block = lambda ref, i: ref[0, (i // n_pairs) * L:(i // n_pairs + 1) * L, (i % n_pairs) * 128:(i % n_pairs + 1) * 128]
    xs_in = [jnp.concatenate([block(kt_ref, i), block(rt_ref, i)], axis=0) for i in items]
    ys_in = [jnp.concatenate([block(kh_ref, i), block(bh_ref, i)], axis=0) for i in items]
    gms = [_dot_nt(jnp.where(first2 if sub == 0 else ~first2, xs_in[i], zero), ys_in[i])
           for i in items for sub in range(2)]
    a_k = jnp.stack([jnp.where(strict2, gm[:L], 0.0) for gm in gms])
    a_r = jnp.stack([jnp.where(incl2, gm[L:], 0.0) for gm in gms])
    a_kb = a_k[:, :, L:]
    bmm = lambda a, b: jnp.einsum("hij,hjk->hik", a.astype(BF16), b.astype(BF16), preferred_element_type=F32)
    tinv = eye[None] - a_kb
    pw = bmm(a_kb, a_kb)
    for _ in range(int(np.log2(L)) - 2):
        both = bmm(jnp.concatenate([tinv, pw], axis=1), pw)
        tinv = tinv + both[:, :L]
        pw = both[:, L:]
    tinv = tinv + bmm(tinv, pw)

    pick = lambda t: jnp.where(first, t[:L], t[L:])
    hsum = lambda t: jnp.where(first, jnp.sum(jnp.where(first, t, 0.0), axis=-1, keepdims=True),
                               jnp.sum(jnp.where(first, 0.0, t), axis=-1, keepdims=True))
    states = [state_ref[p] for p in range(n_pairs)]
    sls = [slice(p * 128, (p + 1) * 128) for p in range(n_pairs)]
    pairs = range(n_pairs)
    stack2 = lambda t, i: jnp.concatenate([t[2 * i], t[2 * i + 1]], axis=0).astype(BF16)
    vs = [block(v_ref, i) for i in items]
    akkv = [pick(_dot(stack2(a_k[:, :, :L], i), vs[i])) for i in items]
    tk = [_dot(stack2(tinv, i), jnp.concatenate([xs_in[i][:L], akkv[i].astype(BF16)], axis=1)) for i in items]
    xm = [jnp.concatenate([pick(tk[i][:, :128]).astype(BF16), xs_in[i][L:]], axis=0) for i in items]
    u0 = [-pick(tk[i][:, 128:]) for i in items]
    for c in range(n_chunks):
        rs = slice(c * L, (c + 1) * L)
        i0 = c * n_pairs
        v = vs[i0:i0 + n_pairs]
        xs = [_dot_nt(xm[i0 + p], states[p].astype(BF16)) for p in pairs]
        vu = [jnp.concatenate([v[p], (u0[i0 + p] - xs[p][:L]).astype(BF16)], axis=0) for p in pairs]
        o = [xs[p][L:] + pick(_dot(stack2(a_r, i0 + p), vu[p])) for p in pairs]
        for p, sl in enumerate(sls):
            kb = jnp.concatenate([kbar_ref[0, rs, sl], bbar_ref[0, rs, sl]], axis=0)
            states[p] = states[p] * etot_ref[0, c * L:c * L + 1, sl] + jnp.where(bd, _dot_tn(vu[p], kb), 0.0)
        for p, sl in enumerate(sls):
            ctr = o[p] - hsum(o[p]) * (1.0 / N)
            on = ctr * lax.rsqrt(hsum(ctr * ctr) * (1.0 / N) + RWKV_GN_EPS)
            y_ref[0, rs, sl] = ((on * gw_ref[:, sl] + gb_ref[:, sl] + bonus_ref[0, rs, sl]) * g_ref[0, rs, sl]).astype(y_ref.dtype)
    for p in pairs:
        state_ref[p] = states[p]


def rwkv7_mixer(p, col0, mu, w0, w2, a0, a2, g2, k_k, k_a, r_k, gn_w, gn_b, *, tm=256):
    b, s, _ = p.shape
    zw = mu.shape[0]
    zh = zw // RWKV_Z_PIECES
    prev_rows = 16 if p.dtype == BF16 else 8
    assert col0 % zh == 0
    tm = min(tm, s)
    c, hd, nh = BRANCH_WIDTH, RWKV_HEAD_DIM, RWKV_HEADS
    L = min(RWKV_CHUNK, s)
    assert tm % L == 0 and L & (L - 1) == 0
    bd = jnp.asarray(np.kron(np.eye(nh, dtype=np.float32), np.ones((hd, hd), np.float32)), dtype=BF16)
    chunk = np.arange(tm) // L
    same = chunk[:, None] == chunk[None, :]
    csum = np.concatenate([same & (np.arange(tm)[None, :] <= np.arange(tm)[:, None]), same], axis=0)
    csum = jnp.asarray(csum.astype(np.float32), dtype=BF16)
    row = lambda w: pl.BlockSpec((1, tm, w), lambda bi, i: (bi, i, 0))
    vec = lambda w: pl.BlockSpec((1, w), lambda bi, i: (0, 0))
    mat = lambda shape: pl.BlockSpec(shape, lambda bi, i: (0,) * len(shape))
    w2, a2, g2 = (jnp.stack(_split2_bf16(w)) for w in (w2, a2, g2))
    outs = pl.pallas_call(
        _rwkv_prep_kernel,
        grid=(b, s // tm),
        in_specs=[pl.BlockSpec((1, tm, zh), functools.partial(lambda bi, i, j: (bi, i, j), j=col0 // zh + j))
                  for j in range(RWKV_Z_PIECES)]
        + [pl.BlockSpec((1, prev_rows, zh),
                        functools.partial(lambda bi, i, j: (bi, jnp.maximum(i * (tm // prev_rows) - 1, 0), j), j=col0 // zh + j))
           for j in range(RWKV_Z_PIECES)]
        + [vec(zw), vec(c), vec(c), vec(c), vec(c), vec(c), mat(w2.shape), mat(a2.shape), mat(g2.shape),
                  mat(bd.shape), mat(csum.shape)],
        out_specs=[row(c)] * 10,
        out_shape=[jax.ShapeDtypeStruct((b, s, c), BF16)] * 7 + [jax.ShapeDtypeStruct((b, s, c), F32)] * 3,
        compiler_params=_params("parallel", "parallel"),
        name="rwkv_prep",
    )(*([p] * (2 * RWKV_Z_PIECES)), mu.reshape(1, zw), w0.reshape(1, c), a0.reshape(1, c), k_k.reshape(1, c), k_a.reshape(1, c),
      r_k.reshape(1, c), w2, a2, g2, bd, csum)
    rows = min(RWKV_CHUNKS_PER_STEP * L, s)
    blk = pl.BlockSpec((1, rows, c), lambda bi, ci: (bi, ci, 0))
    gvec = pl.BlockSpec((1, c), lambda bi, ci: (0, 0))
    return pl.pallas_call(
        functools.partial(_rwkv_scan_kernel, L=L),
        grid=(b, s // rows),
        in_specs=[blk] * 10 + [gvec, gvec],
        out_specs=blk,
        out_shape=jax.ShapeDtypeStruct((b, s, c), BF16),
        scratch_shapes=[pltpu.VMEM((c // 128, 128, 128), F32)],
        compiler_params=_params("parallel", "arbitrary"),
        name="rwkv_scan",
    )(*outs, gn_w.reshape(1, c), gn_b.reshape(1, c))


def _retention_kernel(qk_ref, v_ref, g_ref, cos_ref, sin_ref, inner_ref, qd_ref, kd_ref, cd_ref, o_ref, state_ref):
    @pl.when(pl.program_id(1) == 0)
    def _():
        state_ref[...] = jnp.zeros_like(state_ref)

    C = qk_ref.shape[1]
    kw = RET_HEADS * RET_K_DIM
    cos, sin = cos_ref[...], sin_ref[...]
    rope = lambda t: jnp.concatenate(
        [_rope(t[:, j:j + 128], cos[:, j:j + 128], sin[:, j:j + 128], RET_K_DIM // 2, RET_K_DIM) for j in range(0, kw, 128)], axis=1)
    q = rope(qk_ref[0, :, :kw].astype(F32))
    k = rope(qk_ref[0, :, kw:].astype(F32)) * RET_K_DIM ** -0.5
    lane = lax.broadcasted_iota(jnp.int32, (C, 128), 1)
    heads = range(RET_HEADS)
    vsl = [slice(h * RET_V_DIM, (h + 1) * RET_V_DIM) for h in heads]
    own = lambda t, h: jnp.where((lane // RET_K_DIM) == h % 2, t[:, (h // 2) * 128:(h // 2 + 1) * 128], 0.0)
    qh = [own(q, h) for h in heads]
    kh = [own(k, h) for h in heads]
    vh = [v_ref[0, :, sl] for sl in vsl]
    st = [state_ref[h] for h in heads]
    scores = [_dot_nt(qh[h].astype(BF16), kh[h].astype(BF16)) * inner_ref[h] for h in heads]
    cross = [_dot((qh[h] * qd_ref[h]).astype(BF16), st[h].astype(BF16)) for h in heads]
    o = [_dot(scores[h].astype(BF16), vh[h]) + cross[h] for h in heads]
    for h in heads:
        state_ref[h] = st[h] * cd_ref[h, 0:1, :] + _dot_tn((kh[h] * kd_ref[h]).astype(BF16), vh[h])
    for h, sl in enumerate(vsl):
        gate = g_ref[0, :, sl].astype(F32)
        c = o[h] - jnp.mean(o[h], axis=-1, keepdims=True)
        y = c * lax.rsqrt(jnp.mean(c * c, axis=-1, keepdims=True) + NORM_EPS)
        o_ref[0, :, sl] = (y * (gate * jax.nn.sigmoid(gate))).astype(o_ref.dtype)


def retention_mixer(p, col0, cos, sin):
    b, s, _ = p.shape
    assert col0 % 512 == 0
    H, C = RET_HEADS, min(RET_CHUNK, s)
    log_gamma = jnp.log1p(-jnp.exp2(-5.0 - jnp.arange(H, dtype=F32)))
    n = jnp.arange(C, dtype=F32)
    dist = n[:, None] - n[None, :]
    inner = jnp.where(dist >= 0, jnp.exp(jnp.maximum(dist, 0.0) * log_gamma[:, None, None]), 0.0)
    q_decay = jnp.exp((n + 1.0) * log_gamma[:, None])
    k_decay = jnp.exp((C - 1.0 - n) * log_gamma[:, None])
    chunk_decay = jnp.exp(C * log_gamma)
    lanes = lambda t: jnp.broadcast_to(t[:, :, None], (H, C, 128))
    const = lambda shape: pl.BlockSpec(shape, lambda bi, ci: (0,) * len(shape))
    kw = RET_HEADS * RET_K_DIM
    return pl.pallas_call(
        _retention_kernel,
        grid=(b, s // C),
        in_specs=[pl.BlockSpec((1, C, 512), functools.partial(lambda bi, ci, j: (bi, ci, j), j=col0 // 512 + j))
                  for j in range(3)] + [
                  pl.BlockSpec((C, kw), lambda bi, ci: (ci, 0)), pl.BlockSpec((C, kw), lambda bi, ci: (ci, 0)),
                  const((H, C, C)), const((H, C, 128)), const((H, C, 128)), const((H, 8, 128))],
        out_specs=pl.BlockSpec((1, C, H * RET_V_DIM), lambda bi, ci: (bi, ci, 0)),
        out_shape=jax.ShapeDtypeStruct((b, s, H * RET_V_DIM), BF16),
        scratch_shapes=[pltpu.VMEM((H, 128, 128), F32)],
        compiler_params=_params("parallel", "arbitrary"),
        name="retention",
    )(p, p, p, cos, sin, inner, lanes(q_decay), lanes(k_decay), jnp.broadcast_to(chunk_decay[:, None, None], (H, 8, 128)))


def _pad_cols(w, width):
    return jnp.pad(w, [(0, 0)] * (w.ndim - 1) + [(0, width - w.shape[-1])])


def _rope_tables(s, inv_freq, width, fill_cos):
    inv_freq = np.asarray(inv_freq, np.float32)
    ang = np.arange(s, dtype=np.float32)[:, None] * inv_freq[None, :]
    cos, sin = np.cos(ang).astype(np.float32), np.sin(ang).astype(np.float32)
    rot = 2 * cos.shape[1]
    cos2 = np.concatenate([cos, cos, np.full((s, width - rot), fill_cos, np.float32)], axis=1)
    sin2 = np.concatenate([sin, sin, np.zeros((s, width - rot), np.float32)], axis=1)
    return cos2, sin2


def _inv_freq(rot_dim, theta):
    return np.float32(theta) ** (-np.arange(0, rot_dim, 2, dtype=np.float32) / np.float32(rot_dim))


def _split_offsets():
    sizes = (MLA_Q_RANK, MLA_KV_RANK, MLA_ROPE_DIM, 512, 128, 128, 128, 128, 128, 128, 12,
             3 * BRANCH_WIDTH + RWKV_DECAY_LORA + RWKV_A_LORA + RWKV_GATE_LORA, 256, 256, 512, 512, N_MIXERS * D_MODEL)
    offs = np.concatenate([[0], np.cumsum(sizes)])
    return [(int(offs[i]), int(offs[i + 1])) for i in range(len(sizes))]


def _pack_in_proj(w):
    w = w.astype(BF16)
    sl = [w[..., a:b] for a, b in _split_offsets()]
    (ql, kvl, kr, nq, nkc, nvc, nks, nvs, nkw, nvw, ng, rz, rq, rk, rv, rg, mg) = sl
    c = BRANCH_WIDTH
    w_rwkv = [rz[..., :3 * c], _pad_cols(rz[..., 3 * c:3 * c + 96], 128), _pad_cols(rz[..., 3 * c + 96:3 * c + 192], 128),
              rz[..., 3 * c + 192:]]
    groups = {"nsa": [nq, nkc, nvc, nks, nvs, nkw, nvw, _pad_cols(ng, 128)], "mla_kv": [kvl], "mla_q": [ql],
              "mla_kr": [_pad_cols(kr, 128)], "ret": [rq, rk, rv, rg], "gate": [mg], "rwkv": w_rwkv}
    cols, parts, off = {}, [], 0
    for name, ws in groups.items():
        cols[name] = off
        parts += ws
        off += sum(t.shape[-1] for t in ws)
    return jnp.concatenate(parts, axis=-1), cols


def _pack_mla_weights(w_uq, w_ukv):
    H, dn, dr, dv = MLA_HEADS, MLA_NOPE_DIM, MLA_ROPE_DIM, MLA_V_DIM
    uq = w_uq.reshape(-1, H, dn + dr)
    nope = uq[:, :, :dn].reshape(-1, H * dn)
    rope = uq[:, :, dn:]
    rope_p = _pad_cols(rope, 128).reshape(-1, H * 128)
    ukv = w_ukv.reshape(-1, H, dn + dv)
    wkv = jnp.concatenate([ukv[:, :, :dn].reshape(-1, H * dn), ukv[:, :, dn:].reshape(-1, H * dv)], axis=1)
    return jnp.concatenate([nope, rope_p], axis=1).astype(BF16), wkv.astype(BF16)


def _pad_rows(w, rows):
    return jnp.pad(w, ((0, rows - w.shape[0]), (0, 0)))


def kernel(x, w_in, w_branch, w_out, w_up, w_down, norm_gains, mla_g_q, mla_g_kv, mla_w_uq, mla_w_ukv,
           nsa_cmp_pos, nsa_cmp_w1, nsa_cmp_w2, rwkv_mu, rwkv_w0, rwkv_w2, rwkv_a0, rwkv_a2, rwkv_g2,
           rwkv_k_k, rwkv_k_a, rwkv_r_k, rwkv_gn_w, rwkv_gn_b):
    B, S, D = x.shape
    depth = w_in.shape[0]
    T = B * S
    tm, tm_in = min(512, T), min(1024, T)
    mla_cos, mla_sin = _rope_tables(S, _inv_freq(MLA_ROPE_DIM, ROPE_THETA), 128, 0.0)
    nsa_cos, nsa_sin = _rope_tables(S, _inv_freq(NSA_ROT_DIM, ROPE_THETA), 128, 1.0)
    ret_cos, ret_sin = _rope_tables(S, np.float32(RET_THETA) ** (-np.linspace(0.0, 1.0, RET_K_DIM // 2, dtype=np.float32)),
                                    RET_K_DIM, 0.0)
    ret_cos, ret_sin = np.tile(ret_cos, (1, RET_HEADS)), np.tile(ret_sin, (1, RET_HEADS))
    c = BRANCH_WIDTH
    x = x.reshape(T, D)
    w_all, cols = _pack_in_proj(w_in)
    tn_in = w_all.shape[-1] // 9
    assert tn_in % 256 == 0 and tn_in * 9 == w_all.shape[-1]
    w_branch, w_out, w_up, w_down = (w.astype(BF16) for w in (w_branch, w_out, w_up, w_down))
    for l in range(depth):
        g_pre = norm_gains[l, 0]
        p2 = norm_matmul(x, g_pre, w_all, l, BF16, tm_in, tn_in)
        p = p2.reshape(B, S, -1)

        wq, wkv = _pack_mla_weights(mla_w_uq[l], mla_w_ukv[l])
        y_mla = mla_mixer(p, (cols["mla_q"], cols["mla_kv"], cols["mla_kr"]), mla_g_q[l], mla_g_kv[l], wq, wkv, mla_cos, mla_sin)
        y_nsa = nsa_mixer(p, cols["nsa"], nsa_cmp_pos[l], nsa_cmp_w1[l], nsa_cmp_w2[l], nsa_cos, nsa_sin)
        mu = rwkv_mu[l]
        mu_p = jnp.concatenate([mu[:3 * c], _pad_cols(mu[3 * c:3 * c + 96], 128), _pad_cols(mu[3 * c + 96:3 * c + 192], 128),
                                mu[3 * c + 192:]])
        y_rwkv = rwkv7_mixer(p, cols["rwkv"], mu_p, rwkv_w0[l], _pad_rows(rwkv_w2[l], 128), rwkv_a0[l],
                             _pad_rows(rwkv_a2[l], 128), rwkv_g2[l], rwkv_k_k[l], rwkv_k_a[l], rwkv_r_k[l].reshape(-1),
                             rwkv_gn_w[l], rwkv_gn_b[l])
        y_ret = retention_mixer(p, cols["ret"], ret_cos, ret_sin)

        ys = [y.reshape(T, c) for y in (y_mla, y_nsa, y_rwkv, y_ret)]
        x, h = merge_out_residual(ys, p2, cols["gate"], w_branch, w_out, l, norm_gains[l, 1], norm_gains[l, 2], x, min(256, T))
        x = ffn_block(x, h, w_up, w_down, l, norm_gains[l, 3], tm, 1024)
    return x.reshape(B, S, D)
```

```python
import functools

import numpy as np
import jax
import jax.numpy as jnp
from jax import lax
from jax.experimental import pallas as pl
from jax.experimental.pallas import tpu as pltpu

F32 = jnp.float32
BF16 = jnp.bfloat16
HIGHEST = lax.Precision.HIGHEST

V7X_LANES = 128
V7X_VMEM_LIMIT_BYTES = 56 * 1024 * 1024

D_MODEL = 2048
N_MIXERS = 4
BRANCH_WIDTH = D_MODEL // N_MIXERS
ROPE_THETA = 500000.0
NORM_EPS = 1e-6
MASK_VALUE = -1e30
LOG2_E = float(np.log2(np.e))

MLA_NOPE_DIM = 128
MLA_ROPE_DIM = 64
MLA_V_DIM = 128
MLA_HEADS = 4
MLA_Q_RANK = 384
MLA_KV_RANK = 128

NSA_HEAD_DIM = 128
NSA_HEADS = 4
NSA_ROT_DIM = 32
NSA_CMP_LEN = 32
NSA_CMP_STRIDE = 16
NSA_SEL_BLOCK = 64
NSA_TOP_N = 16
NSA_WINDOW = 512
NSA_FORCED_SCORE = 1000.0
NSA_IN = NSA_HEADS * NSA_HEAD_DIM + 7 * 128

RWKV_HEAD_DIM = 64
RWKV_HEADS = 8
RWKV_DECAY_LORA = 96
RWKV_A_LORA = 96
RWKV_GATE_LORA = 256
RWKV_GN_EPS = 64e-5
RWKV_CHUNK = 64
RWKV_CHUNKS_PER_STEP = 4

RET_HEADS = 4
RET_V_DIM = 128
RET_K_DIM = 64
RET_CHUNK = 256
RET_THETA = 10000.0


def _params(*sem):
    return pltpu.CompilerParams(dimension_semantics=sem, vmem_limit_bytes=V7X_VMEM_LIMIT_BYTES)


def _dot(a, b, precision=None):
    return jnp.dot(a, b, preferred_element_type=F32, precision=precision)


def _dot_nt(a, b, precision=None):
    return lax.dot_general(a, b, (((1,), (1,)), ((), ())), preferred_element_type=F32, precision=precision)


def _dot_tn(a, b, precision=None):
    return lax.dot_general(a, b, (((0,), (0,)), ((), ())), preferred_element_type=F32, precision=precision)


def _rotate_half(x, half, period):
    lane = lax.broadcasted_iota(jnp.int32, x.shape, x.ndim - 1) & (period - 1)
    up = pltpu.roll(x, V7X_LANES - half, axis=x.ndim - 1)
    dn = pltpu.roll(x, half, axis=x.ndim - 1)
    return jnp.where(lane < half, -up, jnp.where(lane < 2 * half, dn, 0.0))


def _rope(x, cos, sin, half, period=V7X_LANES):
    return x * cos + _rotate_half(x, half, period) * sin


def _rms(x, g, eps=NORM_EPS):
    return x * lax.rsqrt(jnp.mean(x * x, axis=-1, keepdims=True) + eps) * g


def _norm_matmul_kernel(x_ref, g_ref, w_ref, o_ref, h_ref):
    @pl.when(pl.program_id(1) == 0)
    def _():
        h_ref[...] = _rms(x_ref[...], g_ref[...]).astype(BF16)

    o_ref[...] = _dot(h_ref[...], w_ref[...]).astype(o_ref.dtype)


def norm_matmul(x, g, w, layer, out_dtype, tm, tn):
    m, k = x.shape
    n = w.shape[2]
    assert m % tm == 0 and n % tn == 0
    return pl.pallas_call(
        _norm_matmul_kernel,
        grid=(m // tm, n // tn),
        in_specs=[pl.BlockSpec((tm, k), lambda i, j: (i, 0)),
                  pl.BlockSpec((1, k), lambda i, j: (0, 0)),
                  pl.BlockSpec((None, k, tn), lambda i, j: (layer, 0, j))],
        out_specs=pl.BlockSpec((tm, tn), lambda i, j: (i, j)),
        out_shape=jax.ShapeDtypeStruct((m, n), out_dtype),
        scratch_shapes=[pltpu.VMEM((tm, k), BF16)],
        compiler_params=_params("parallel", "arbitrary"),
        name="norm_matmul",
    )(x, g.reshape(1, k), w)


MERGE_GATE_PIECES = 4


def _merge_out_kernel(*refs):
    n = MERGE_GATE_PIECES
    y_refs, g_refs = refs[:N_MIXERS], refs[N_MIXERS:(1 + n) * N_MIXERS]
    wb_ref, wo_ref, gn_ref, gf_ref, x_ref, o_ref, h_ref = refs[(1 + n) * N_MIXERS:]
    width = g_refs[0].shape[1]
    merged = []
    for piece in range(n):
        acc = None
        for m in range(N_MIXERS):
            gate = 0.5 * jnp.tanh(0.5 * g_refs[n * m + piece][...].astype(F32)) + 0.5
            term = gate * _dot(y_refs[m][...], wb_ref[m, :, piece * width:(piece + 1) * width])
            acc = term if acc is None else acc + term
        merged.append(acc.astype(BF16))
    x_new = x_ref[...] + _rms(_dot(jnp.concatenate(merged, axis=1), wo_ref[...]), gn_ref[...])
    o_ref[...] = x_new
    h_ref[...] = _rms(x_new, gf_ref[...]).astype(BF16)


def merge_out_residual(ys, gates, col0, wb, wo, layer, gn, g_ffn, x, tm):
    m, c = ys[0].shape
    d = wb.shape[3]
    half = d // MERGE_GATE_PIECES
    assert col0 % half == 0
    once = pl.Buffered(1)
    return pl.pallas_call(
        _merge_out_kernel,
        grid=(m // tm,),
        in_specs=[pl.BlockSpec((tm, c), lambda i: (i, 0))] * N_MIXERS
        + [pl.BlockSpec((tm, half), functools.partial(lambda i, j: (i, j), j=col0 // half + jj))
           for jj in range(MERGE_GATE_PIECES * N_MIXERS)]
        + [pl.BlockSpec((None, N_MIXERS, c, d), lambda i: (layer, 0, 0, 0), pipeline_mode=once),
           pl.BlockSpec((None, d, d), lambda i: (layer, 0, 0), pipeline_mode=once),
           pl.BlockSpec((1, d), lambda i: (0, 0)),
           pl.BlockSpec((1, d), lambda i: (0, 0)),
           pl.BlockSpec((tm, d), lambda i: (i, 0))],
        out_specs=[pl.BlockSpec((tm, d), lambda i: (i, 0))] * 2,
        out_shape=[jax.ShapeDtypeStruct((m, d), F32), jax.ShapeDtypeStruct((m, d), BF16)],
        compiler_params=_params("parallel"),
        name="merge_out_residual",
    )(*ys, *([gates] * (MERGE_GATE_PIECES * N_MIXERS)), wb, wo, gn.reshape(1, d), g_ffn.reshape(1, d), x)


def _ffn_kernel(x_ref, h_ref, wu_ref, wd_ref, g2_ref, o_ref, acc_ref):
    f = pl.program_id(1)

    @pl.when(f == 0)
    def _():
        acc_ref[...] = jnp.zeros_like(acc_ref)

    u = jnp.maximum(_dot(h_ref[...], wu_ref[...]), 0.0)
    acc_ref[...] += _dot((u * u).astype(BF16), wd_ref[...])

    @pl.when(f == pl.num_programs(1) - 1)
    def _():
        o_ref[...] = x_ref[...] + _rms(acc_ref[...], g2_ref[...])


def ffn_block(x, h, wu, wd, layer, g2, tm, tf):
    m, d = x.shape
    ff = wu.shape[2]
    return pl.pallas_call(
        _ffn_kernel,
        grid=(m // tm, ff // tf),
        in_specs=[pl.BlockSpec((tm, d), lambda i, f: (i, 0)),
                  pl.BlockSpec((tm, d), lambda i, f: (i, 0)),
                  pl.BlockSpec((None, d, tf), lambda i, f: (layer, 0, f)),
                  pl.BlockSpec((None, tf, d), lambda i, f: (layer, f, 0)),
                  pl.BlockSpec((1, d), lambda i, f: (0, 0))],
        out_specs=pl.BlockSpec((tm, d), lambda i, f: (i, 0)),
        out_shape=jax.ShapeDtypeStruct((m, d), F32),
        scratch_shapes=[pltpu.VMEM((tm, d), F32)],
        compiler_params=_params("parallel", "arbitrary"),
        name="ffn_block",
    )(x, h, wu, wd, g2.reshape(1, d))


def _online_softmax_steps(carries, scores, vt_tiles, masks):
    scores = [s if mask is None else jnp.where(mask, s, -jnp.inf) for s, mask in zip(scores, masks)]
    out = []
    for (m_prev, l_prev, acc_prev), s, vt in zip(carries, scores, vt_tiles):
        m_new = jnp.maximum(m_prev, jnp.max(s, axis=0, keepdims=True))
        alpha = jnp.exp2(m_prev - m_new)
        p = jnp.exp2(s - m_new)
        out.append((m_new, alpha * l_prev + jnp.sum(p, axis=0, keepdims=True), alpha * acc_prev + _dot(vt, p.astype(BF16))))
    return tuple(out)


def _softmax_init(cols, dv):
    return (jnp.full((1, cols), MASK_VALUE, F32), jnp.zeros((1, cols), F32), jnp.zeros((dv, cols), F32))


def _softmax_merge(a, b):
    (ma, la, acca), (mb, lb, accb) = a, b
    m = jnp.maximum(ma, mb)
    wa, wb = jnp.exp2(ma - m), jnp.exp2(mb - m)
    return m, la * wa + lb * wb, acca * wa + accb * wb


def _softmax_finish(carry):
    _, l, acc = carry
    return acc / jnp.where(l > 0.0, l, 1.0)


def _mla_prep_kernel(ql_ref, kvl_ref, krl_ref, gq_ref, gkv_ref, wq_ref, wkv_ref, cos_ref, sin_ref,
                     qn_ref, qr_ref, kn_ref, vt_ref, kr_ref, *, scale):
    nq = _rms(ql_ref[0].astype(F32), gq_ref[...]).astype(BF16)
    nkv = _rms(kvl_ref[0].astype(F32), gkv_ref[...]).astype(BF16)
    q = _dot(nq, wq_ref[...])
    kv = _dot(nkv, wkv_ref[...])
    cos, sin = cos_ref[...], sin_ref[...]
    hw = MLA_HEADS * MLA_NOPE_DIM
    half = MLA_ROPE_DIM // 2
    qn_ref[0] = (q[:, :hw] * scale).astype(BF16)
    for h in range(MLA_HEADS):
        qr_ref[0, :, h * 128:(h + 1) * 128] = (_rope(q[:, hw + h * 128: hw + (h + 1) * 128], cos, sin, half) * scale).astype(BF16)
    kn_ref[0] = kv[:, :hw].astype(BF16)
    vt_ref[0] = kv[:, hw:].T.astype(BF16)
    kr_ref[0] = _rope(krl_ref[0].astype(F32), cos, sin, half).astype(BF16)


def _mla_attn_kernel(qn_ref, qr_ref, kn_ref, kr_ref, vt_ref, o_ref, *, tq, tk):
    i = pl.program_id(1)
    hs = [slice(h * 128, (h + 1) * 128) for h in range(MLA_HEADS)]
    q = [jnp.concatenate([qn_ref[0, :, sl], qr_ref[0, :, sl]], axis=1) for sl in hs]

    def tile(t, carries, mask):
        k0 = pl.multiple_of(t * tk, tk)
        kr = kr_ref[0, pl.ds(k0, tk), :]
        scores = [_dot_nt(jnp.concatenate([kn_ref[0, pl.ds(k0, tk), sl], kr], axis=1), q[h])
                  for h, sl in enumerate(hs)]
        return _online_softmax_steps(carries, scores, [vt_ref[0, sl, pl.ds(k0, tk)] for sl in hs], [mask] * len(hs))

    init = tuple(_softmax_init(tq, MLA_V_DIM) for _ in hs)
    t_diag = (i * tq) // tk
    carries = lax.fori_loop(0, t_diag, lambda t, c: tile(t, c, None), init)
    causal = (t_diag * tk + lax.broadcasted_iota(jnp.int32, (tk, tq), 0)
              <= i * tq + lax.broadcasted_iota(jnp.int32, (tk, tq), 1))
    for sl, carry in zip(hs, tile(t_diag, carries, causal)):
        o_ref[0, :, sl] = _softmax_finish(carry).T.astype(o_ref.dtype)


def mla_mixer(p, cols, g_q, g_kv, wq, wkv, cos, sin, *, tm=512, tq=512, tk=512):
    b, s, _ = p.shape
    tm, tq, tk = min(tm, s), min(tq, s), min(tk, s)
    col_q, col_kv, col_kr = cols
    assert tk % tq == 0 and col_q % MLA_Q_RANK == 0 and col_kv % 128 == 0 and col_kr % 128 == 0
    hw = MLA_HEADS * 128
    scale = (MLA_NOPE_DIM + MLA_ROPE_DIM) ** -0.5 * LOG2_E
    full = lambda shape: pl.BlockSpec(shape, lambda bi, i: (0,) * len(shape))
    row = lambda w: pl.BlockSpec((1, tm, w), lambda bi, i: (bi, i, 0))
    outs = pl.pallas_call(
        functools.partial(_mla_prep_kernel, scale=scale),
        grid=(b, s // tm),
        in_specs=[pl.BlockSpec((1, tm, MLA_Q_RANK), lambda bi, i: (bi, i, col_q // MLA_Q_RANK)),
                  pl.BlockSpec((1, tm, 128), lambda bi, i: (bi, i, col_kv // 128)),
                  pl.BlockSpec((1, tm, 128), lambda bi, i: (bi, i, col_kr // 128)),
                  full((1, MLA_Q_RANK)), full((1, MLA_KV_RANK)), full(wq.shape), full(wkv.shape),
                  pl.BlockSpec((tm, 128), lambda bi, i: (i, 0)), pl.BlockSpec((tm, 128), lambda bi, i: (i, 0))],
        out_specs=[row(hw), row(hw), row(hw), pl.BlockSpec((1, hw, tm), lambda bi, i: (bi, 0, i)), row(128)],
        out_shape=[jax.ShapeDtypeStruct((b, s, hw), BF16)] * 3 + [jax.ShapeDtypeStruct((b, hw, s), BF16),
                                                                   jax.ShapeDtypeStruct((b, s, 128), BF16)],
        compiler_params=_params("parallel", "parallel"),
        name="mla_prep",
    )(p, p, p, g_q.reshape(1, -1), g_kv.reshape(1, -1), wq, wkv, cos, sin)
    qn, qr, kn, vt, kr = outs
    q_spec = pl.BlockSpec((1, tq, hw), lambda bi, i: (bi, i, 0))
    per_batch = lambda shape: pl.BlockSpec((1,) + shape, lambda bi, i: (bi, 0, 0))
    return pl.pallas_call(
        functools.partial(_mla_attn_kernel, tq=tq, tk=tk),
        grid=(b, s // tq),
        in_specs=[q_spec, q_spec, per_batch((s, hw)), per_batch((s, 128)), per_batch((hw, s))],
        out_specs=q_spec,
        out_shape=jax.ShapeDtypeStruct((b, s, hw), BF16),
        compiler_params=_params("parallel", "arbitrary"),
        name="mla_attn",
    )(qn, qr, kn, kr, vt)


def _nsa_prep_kernel(p_ref, cos_ref, sin_ref, q_ref, kc_ref, vc_ref, ks_ref, kw_ref, vst_ref, vwt_ref, *, scale):
    cos, sin = cos_ref[...], sin_ref[...]
    hw = NSA_HEADS * NSA_HEAD_DIM
    half = NSA_ROT_DIM // 2
    tile = lambda col: p_ref[0, :, col * 128:(col + 1) * 128].astype(F32)
    for h in range(NSA_HEADS):
        q_ref[0, :, h * 128:(h + 1) * 128] = (_rope(tile(h), cos, sin, half) * scale).astype(BF16)
    base = hw // 128
    kc_ref[0] = _rope(tile(base), cos, sin, half)
    vc_ref[0] = tile(base + 1)
    ks_ref[0] = _rope(tile(base + 2), cos, sin, half).astype(BF16)
    kw_ref[0] = _rope(tile(base + 4), cos, sin, half).astype(BF16)
    vst_ref[0] = tile(base + 3).T.astype(BF16)
    vwt_ref[0] = tile(base + 5).T.astype(BF16)


def _gelu_tanh(x):
    return 0.5 * x * (1.0 + jnp.tanh(np.sqrt(2.0 / np.pi).astype(np.float32) * (x + 0.044715 * (x * x * x))))


def _split3_bf16(x):
    hi = x.astype(BF16)
    r1 = x - hi.astype(F32)
    mid = r1.astype(BF16)
    return hi, mid, (r1 - mid.astype(F32)).astype(BF16)


def _split2_bf16(x):
    hi = x.astype(BF16)
    return hi, (x - hi.astype(F32)).astype(BF16)


def _dot_split(a, w2_ref):
    hi, lo = _split2_bf16(a)
    return _dot(hi, w2_ref[0]) + (_dot(hi, w2_ref[1]) + _dot(lo, w2_ref[0]))


def _nsa_compress_kernel(k_ref, v_ref, pos_ref, w1_ref, w2_ref, kc3_ref, vct_ref):
    nrow = k_ref.shape[1]
    row = lax.broadcasted_iota(jnp.int32, (nrow, NSA_HEAD_DIM), 0)
    outs = []
    for z, x_ref in enumerate((k_ref, v_ref)):
        res = _dot(x_ref[0], w1_ref[z], HIGHEST)
        pb = _dot(pos_ref[z], w1_ref[z], HIGHEST)
        bias = pb[0:1, :NSA_HEAD_DIM] + pb[1:2, NSA_HEAD_DIM:]
        nxt = pltpu.roll(res[:, NSA_HEAD_DIM:], nrow - 1, axis=0)
        hid = _gelu_tanh(res[:, :NSA_HEAD_DIM] + nxt + bias)
        outs.append(jnp.where(row < nrow - 1, _dot(hid, w2_ref[z], HIGHEST), 0.0))
    for z, piece in enumerate(_split3_bf16(outs[0])):
        kc3_ref[0, z] = piece
    vct_ref[0] = outs[1].T.astype(BF16)


def _nsa_attn_kernel(q_ref, kc3_ref, vct_ref, covert_ref, ks_ref, vst_ref, kw_ref, vwt_ref, et_ref, g_ref, o_ref,
                     *, tq, tk, n_blk):
    i = pl.program_id(1)
    q0 = i * tq
    cols = NSA_HEADS * tq
    q4 = jnp.concatenate([q_ref[0, :, h * 128:(h + 1) * 128] for h in range(NSA_HEADS)], axis=0)

    ncmp = kc3_ref.shape[2]
    s = sum(_dot_nt(kc3_ref[0, z], q4) for z in range(3))
    n = lax.broadcasted_iota(jnp.int32, (ncmp, cols), 0)
    qpos_c = q0 + (lax.broadcasted_iota(jnp.int32, (ncmp, cols), 1) & (tq - 1))
    s = jnp.where((n * NSA_CMP_STRIDE + (NSA_CMP_LEN - 1) <= qpos_c) & (n < ncmp - 1), s, -jnp.inf)
    e = jnp.exp2(s - jnp.maximum(jnp.max(s, axis=0, keepdims=True), MASK_VALUE))
    l = jnp.sum(e, axis=0, keepdims=True)
    p = e / jnp.where(l > 0.0, l, 1.0)
    o_cmp = _dot(vct_ref[0], p.astype(BF16))
    psum = sum(p[:, h * tq:(h + 1) * tq] for h in range(NSA_HEADS))
    imp = sum(_dot(covert_ref[...], piece) for piece in _split3_bf16(psum))

    nb = covert_ref.shape[0]
    jb = lax.broadcasted_iota(jnp.int32, (nb, tq), 0)
    cur = (q0 + lax.broadcasted_iota(jnp.int32, (nb, tq), 1)) // NSA_SEL_BLOCK
    forced = (jb == 0) | (jb == cur) | (jb == cur - 1)
    visible = jb <= cur
    score = jnp.where(visible, imp + jnp.where(forced, NSA_FORCED_SCORE, 0.0), -jnp.inf)
    beaten = jnp.zeros((nb, tq), F32)
    for j in range(n_blk):
        row = score[j:j + 1, :]
        beaten = beaten + jnp.where((row > score) | ((row == score) & (jb > j)), 1.0, 0.0)
    keep = (beaten < float(NSA_TOP_N)) & visible
    sel = jnp.concatenate([jnp.where(keep, 0.0, MASK_VALUE), jnp.full((128 - nb, tq), MASK_VALUE, F32)], axis=0).astype(BF16)

    qh = [q_ref[0, :, h * 128:(h + 1) * 128] for h in range(NSA_HEADS)]
    kio = lax.broadcasted_iota(jnp.int32, (tk, tq), 0)
    qpos = q0 + lax.broadcasted_iota(jnp.int32, (tk, tq), 1)
    init = tuple(_softmax_init(tq, NSA_HEAD_DIM) for _ in qh)

    def slc_scores(k0):
        k = ks_ref[0, pl.ds(k0, tk), :]
        bias = _dot(et_ref[pl.ds(k0, tk), :], sel)
        return [_dot_nt(k, q) + bias for q in qh]

    def tile(t, carries, causal, window):
        k0 = pl.multiple_of(t * tk, tk)
        scores = slc_scores(k0)
        vts = [vst_ref[0, :, pl.ds(k0, tk)]] * NSA_HEADS
        masks = [(k0 + kio <= qpos) if causal else None] * NSA_HEADS
        if window:
            k = kw_ref[0, pl.ds(k0, tk), :]
            dist = qpos - (k0 + kio)
            scores += [_dot_nt(k, q) for q in qh]
            vts += [vwt_ref[0, :, pl.ds(k0, tk)]] * NSA_HEADS
            masks += [(dist >= 0) & (dist < NSA_WINDOW)] * NSA_HEADS
        return _online_softmax_steps(carries, scores, vts, masks)

    def tile_pair(gi, carries):
        k0 = pl.multiple_of(gi * (2 * tk), 2 * tk)
        scores, vts = [], []
        for half in range(2):
            scores += slc_scores(k0 + half * tk)
            vts += [vst_ref[0, :, pl.ds(k0 + half * tk, tk)]] * NSA_HEADS
        return _online_softmax_steps(carries, scores, vts, [None] * (2 * NSA_HEADS))

    t_diag = q0 // tk
    t_lo = jnp.maximum(q0 - NSA_WINDOW + 1, 0) // tk
    pairs = lax.fori_loop(0, t_lo // 2, tile_pair, init + init)
    o_slc = tuple(_softmax_merge(a, b) for a, b in zip(pairs[:NSA_HEADS], pairs[NSA_HEADS:]))
    o_slc = lax.fori_loop(2 * (t_lo // 2), t_lo, lambda t, c: tile(t, c, False, False), o_slc)
    both = lax.fori_loop(t_lo, t_diag, lambda t, c: tile(t, c, False, True), o_slc + init)
    both = tile(t_diag, both, True, True)
    o_slc, o_win = both[:NSA_HEADS], both[NSA_HEADS:]

    g = jax.nn.sigmoid(g_ref[0].astype(F32).T)
    for h in range(NSA_HEADS):
        o = (g[3 * h:3 * h + 1] * o_cmp[:, h * tq:(h + 1) * tq] + g[3 * h + 1:3 * h + 2] * _softmax_finish(o_slc[h])
             + g[3 * h + 2:3 * h + 3] * _softmax_finish(o_win[h]))
        o_ref[0, :, h * 128:(h + 1) * 128] = o.T.astype(o_ref.dtype)


def nsa_mixer(p, col0, pos_emb, w1, w2, cos, sin, *, tm=512, tq=512, tk=512):
    b, s, _ = p.shape
    assert col0 % NSA_IN == 0
    tm, tk = min(tm, s), min(tk, s)
    hw = NSA_HEADS * NSA_HEAD_DIM
    scale = NSA_HEAD_DIM ** -0.5 * LOG2_E
    row = lambda w: pl.BlockSpec((1, tm, w), lambda bi, i: (bi, i, 0))
    tab = pl.BlockSpec((tm, 128), lambda bi, i: (i, 0))
    tr_spec = pl.BlockSpec((1, 128, tm), lambda bi, i: (bi, 0, i))
    q, kc_in, vc_in, ks, kw, vst, vwt = pl.pallas_call(
        functools.partial(_nsa_prep_kernel, scale=scale),
        grid=(b, s // tm),
        in_specs=[pl.BlockSpec((1, tm, NSA_IN), lambda bi, i: (bi, i, col0 // NSA_IN)), tab, tab],
        out_specs=[row(hw), row(128), row(128), row(128), row(128), tr_spec, tr_spec],
        out_shape=[jax.ShapeDtypeStruct((b, s, hw), BF16), jax.ShapeDtypeStruct((b, s, 128), F32),
                   jax.ShapeDtypeStruct((b, s, 128), F32), jax.ShapeDtypeStruct((b, s, 128), BF16),
                   jax.ShapeDtypeStruct((b, s, 128), BF16), jax.ShapeDtypeStruct((b, 128, s), BF16),
                   jax.ShapeDtypeStruct((b, 128, s), BF16)],
        compiler_params=_params("parallel", "parallel"),
        name="nsa_prep",
    )(p, cos, sin)

    nrow = s // NSA_CMP_STRIDE
    fw = NSA_CMP_STRIDE * NSA_HEAD_DIM
    half = w1.shape[1] // 2
    w1r = jnp.concatenate([w1[:, :half], w1[:, half:]], axis=2)
    pos2 = jnp.pad(pos_emb.reshape(2, 2, fw), ((0, 0), (0, 6), (0, 0)))
    per_b = lambda shape: pl.BlockSpec((1,) + shape, lambda bi: (bi, 0, 0))
    const3 = lambda shape: pl.BlockSpec(shape, lambda bi: (0, 0, 0))
    kc3, vct = pl.pallas_call(
        _nsa_compress_kernel,
        grid=(b,),
        in_specs=[per_b((nrow, fw)), per_b((nrow, fw)), const3(pos2.shape), const3(w1r.shape), const3(w2.shape)],
        out_specs=[pl.BlockSpec((1, 3, nrow, NSA_HEAD_DIM), lambda bi: (bi, 0, 0, 0)), per_b((NSA_HEAD_DIM, nrow))],
        out_shape=[jax.ShapeDtypeStruct((b, 3, nrow, NSA_HEAD_DIM), BF16), jax.ShapeDtypeStruct((b, NSA_HEAD_DIM, nrow), BF16)],
        compiler_params=_params("parallel"),
        name="nsa_compress",
    )(kc_in.reshape(b, nrow, fw), vc_in.reshape(b, nrow, fw), pos2, w1r, w2)

    n_blk = s // NSA_SEL_BLOCK
    nb = 64
    assert n_blk <= nb and tq % 128 == 0 and tq & (tq - 1) == 0 and tk % tq == 0
    cmp_start = np.arange(nrow) * NSA_CMP_STRIDE
    cmp_end = cmp_start + NSA_CMP_LEN - 1
    blk_start = np.arange(nb) * NSA_SEL_BLOCK
    cover_t = ((cmp_start[None, :] <= blk_start[:, None] + NSA_SEL_BLOCK - 1)
               & (cmp_end[None, :] >= blk_start[:, None]) & (np.arange(nb)[:, None] < n_blk)
               & (np.arange(nrow)[None, :] < nrow - 1)).astype(np.float32)
    expand_t = ((np.arange(s)[:, None] // NSA_SEL_BLOCK) == np.arange(128)[None, :]).astype(np.float32)
    qrow = lambda w: pl.BlockSpec((1, tq, w), lambda bi, i: (bi, i, 0))
    k_spec = pl.BlockSpec((1, s, 128), lambda bi, i: (bi, 0, 0))
    vt_spec = pl.BlockSpec((1, 128, s), lambda bi, i: (bi, 0, 0))
    gate_block = (col0 + NSA_IN) // 128 - 1
    return pl.pallas_call(
        functools.partial(_nsa_attn_kernel, tq=tq, tk=tk, n_blk=n_blk),
        grid=(b, s // tq),
        in_specs=[qrow(hw), pl.BlockSpec((1, 3, nrow, NSA_HEAD_DIM), lambda bi, i: (bi, 0, 0, 0)),
                  pl.BlockSpec((1, NSA_HEAD_DIM, nrow), lambda bi, i: (bi, 0, 0)),
                  pl.BlockSpec((nb, nrow), lambda bi, i: (0, 0)),
                  k_spec, vt_spec, k_spec, vt_spec, pl.BlockSpec((s, 128), lambda bi, i: (0, 0)),
                  pl.BlockSpec((1, tq, 128), lambda bi, i: (bi, i, gate_block))],
        out_specs=qrow(hw),
        out_shape=jax.ShapeDtypeStruct((b, s, hw), BF16),
        compiler_params=_params("parallel", "arbitrary"),
        name="nsa_attn",
    )(q, kc3, vct, jnp.asarray(cover_t, dtype=BF16), ks, vst, kw, vwt, jnp.asarray(expand_t, dtype=BF16), p)


RWKV_Z_PIECES = 4


def _rwkv_prep_kernel(*refs):
    z_refs, zp_refs = refs[:RWKV_Z_PIECES], refs[RWKV_Z_PIECES:2 * RWKV_Z_PIECES]
    (mu_ref, w0_ref, a0_ref, kk_ref, ka_ref, rk_ref, w2_ref, a2_ref, g2_ref, bd_ref, csum_ref, kt_ref, rt_ref, kh_ref,
     bh_ref, kbar_ref, bbar_ref, v_ref, etot_ref, g_ref, bonus_ref) = refs[2 * RWKV_Z_PIECES:]
    i = pl.program_id(1)
    z = jnp.concatenate([r[0] for r in z_refs], axis=1).astype(F32)
    tm = z.shape[0]
    last = zp_refs[0].shape[1] - 1
    prev_row = jnp.concatenate([r[0, last:, :] for r in zp_refs], axis=1).astype(F32)
    prev_row = jnp.where(i > 0, prev_row, 0.0)
    rowid = lax.broadcasted_iota(jnp.int32, z.shape, 0)
    z_prev = jnp.where(rowid == 0, prev_row, pltpu.roll(z, 1, axis=0))
    z = z + (z_prev - z) * mu_ref[...]
    c = BRANCH_WIDTH
    r, k, v = z[:, :c], z[:, c:2 * c], z[:, 2 * c:3 * c]
    wd, ad, gd = z[:, 3 * c:3 * c + 128], z[:, 3 * c + 128:3 * c + 256], z[:, 3 * c + 256:3 * c + 512]
    wpre = -(w0_ref[...] + _dot_split(jnp.tanh(wd), w2_ref))
    w_log = -(jnp.maximum(wpre, 0.0) + jnp.log1p(jnp.exp(-jnp.abs(wpre)))) - 0.5
    lw = -jnp.exp(w_log)
    a = jax.nn.sigmoid(a0_ref[...] + _dot_split(ad, a2_ref))
    g_ref[0] = _dot_split(jax.nn.sigmoid(gd), g2_ref)
    kk = k * kk_ref[...]
    head_sum = lambda t: sum(_dot(piece, bd_ref[...]) for piece in _split2_bf16(t))
    kk = kk * lax.rsqrt(jnp.maximum(head_sum(kk * kk), 1e-24))
    k = k * (1.0 + (a - 1.0) * ka_ref[...])
    bonus_ref[0] = head_sum(r * k * rk_ref[...]) * v
    sums = sum(_dot(csum_ref[...], piece) for piece in _split3_bf16(lw))
    cum, total = sums[:tm], sums[tm:]
    dec_out, dec_end = jnp.exp(-cum), jnp.exp(total - cum)
    b = kk * a
    kt_ref[0] = (kk * jnp.exp(cum - lw)).astype(BF16)
    rt_ref[0] = (r * jnp.exp(cum)).astype(BF16)
    kh_ref[0] = (k * dec_out).astype(BF16)
    bh_ref[0] = (b * dec_out).astype(BF16)
    kbar_ref[0] = (k * dec_end).astype(BF16)
    bbar_ref[0] = (b * dec_end).astype(BF16)
    v_ref[0] = v.astype(BF16)
    etot_ref[0] = jnp.exp(total)


def _rwkv_scan_kernel(kt_ref, rt_ref, kh_ref, bh_ref, kbar_ref, bbar_ref, v_ref, etot_ref, g_ref, bonus_ref,
                      gw_ref, gb_ref, y_ref, state_ref, *, L):
    n_chunks = kt_ref.shape[1] // L
    N = RWKV_HEAD_DIM
    n_pairs = kt_ref.shape[2] // 128

    @pl.when(pl.program_id(1) == 0)
    def _():
        state_ref[...] = jnp.zeros_like(state_ref)

    ri = lax.broadcasted_iota(jnp.int32, (L, 2 * L), 0)
    ci = lax.broadcasted_iota(jnp.int32, (L, 2 * L), 1) & (L - 1)
    strict2, incl2 = ci < ri, ci <= ri
    eye = jnp.where(lax.broadcasted_iota(jnp.int32, (L, L), 0) == lax.broadcasted_iota(jnp.int32, (L, L), 1), 1.0, 0.0)
    first2 = lax.broadcasted_iota(jnp.int32, (2 * L, 128), 1) < N
    first = lax.broadcasted_iota(jnp.int32, (L, 128), 1) < N
    bd = (lax.broadcasted_iota(jnp.int32, (128, 128), 0) < N) == (lax.broadcasted_iota(jnp.int32, (128, 128), 1) < N)
    zero = jnp.zeros((), BF16)

    items = range(n_chunks * n_pairs)
    block = lambda ref, i: ref[0, (i // n_pairs) * L:(i // n_pairs + 1) * L, (i % n_pairs) * 128:(i % n_pairs + 1) * 128]
    xs_in = [jnp.concatenate([block(kt_ref, i), block(rt_ref, i)], axis=0) for i in items]
    ys_in = [jnp.concatenate([block(kh_ref, i), block(bh_ref, i)], axis=0) for i in items]
    gms = [_dot_nt(jnp.where(first2 if sub == 0 else ~first2, xs_in[i], zero), ys_in[i])
           for i in items for sub in range(2)]
    a_k = jnp.stack([jnp.where(strict2, gm[:L], 0.0) for gm in gms])
    a_r = jnp.stack([jnp.where(incl2, gm[L:], 0.0) for gm in gms])
    a_kb = a_k[:, :, L:]
    bmm = lambda a, b: jnp.einsum("hij,hjk->hik", a.astype(BF16), b.astype(BF16), preferred_element_type=F32)
    tinv = eye[None] - a_kb
    pw = bmm(a_kb, a_kb)
    for _ in range(int(np.log2(L)) - 2):
        both = bmm(jnp.concatenate([tinv, pw], axis=1), pw)
        tinv = tinv + both[:, :L]
        pw = both[:, L:]
    tinv = tinv + bmm(tinv, pw)

    pick = lambda t: jnp.where(first, t[:L], t[L:])
    hsum = lambda t: jnp.where(first, jnp.sum(jnp.where(first, t, 0.0), axis=-1, keepdims=True),
                               jnp.sum(jnp.where(first, 0.0, t), axis=-1, keepdims=True))
    states = [state_ref[p] for p in range(n_pairs)]
    sls = [slice(p * 128, (p + 1) * 128) for p in range(n_pairs)]
    pairs = range(n_pairs)
    stack2 = lambda t, i: jnp.concatenate([t[2 * i], t[2 * i + 1]], axis=0).astype(BF16)
    vs = [block(v_ref, i) for i in items]
    akkv = [pick(_dot(stack2(a_k[:, :, :L], i), vs[i])) for i in items]
    tk = [_dot(stack2(tinv, i), jnp.concatenate([xs_in[i][:L], akkv[i].astype(BF16)], axis=1)) for i in items]
    xm = [jnp.concatenate([pick(tk[i][:, :128]).astype(BF16), xs_in[i][L:]], axis=0) for i in items]
    u0 = [-pick(tk[i][:, 128:]) for i in items]
    for c in range(n_chunks):
        rs = slice(c * L, (c + 1) * L)
        i0 = c * n_pairs
        v = vs[i0:i0 + n_pairs]
        xs = [_dot_nt(xm[i0 + p], states[p].astype(BF16)) for p in pairs]
        vu = [jnp.concatenate([v[p], (u0[i0 + p] - xs[p][:L]).astype(BF16)], axis=0) for p in pairs]
        o = [xs[p][L:] + pick(_dot(stack2(a_r, i0 + p), vu[p])) for p in pairs]
        for p, sl in enumerate(sls):
            kb = jnp.concatenate([kbar_ref[0, rs, sl], bbar_ref[0, rs, sl]], axis=0)
            states[p] = states[p] * etot_ref[0, c * L:c * L + 1, sl] + jnp.where(bd, _dot_tn(vu[p], kb), 0.0)
        for p, sl in enumerate(sls):
            ctr = o[p] - hsum(o[p]) * (1.0 / N)
            on = ctr * lax.rsqrt(hsum(ctr * ctr) * (1.0 / N) + RWKV_GN_EPS)
            y_ref[0, rs, sl] = ((on * gw_ref[:, sl] + gb_ref[:, sl] + bonus_ref[0, rs, sl]) * g_ref[0, rs, sl]).astype(y_ref.dtype)
    for p in pairs:
        state_ref[p] = states[p]


def rwkv7_mixer(p, col0, mu, w0, w2, a0, a2, g2, k_k, k_a, r_k, gn_w, gn_b, *, tm=256):
    b, s, _ = p.shape
    zw = mu.shape[0]
    zh = zw // RWKV_Z_PIECES
    prev_rows = 16 if p.dtype == BF16 else 8
    assert col0 % zh == 0
    tm = min(tm, s)
    c, hd, nh = BRANCH_WIDTH, RWKV_HEAD_DIM, RWKV_HEADS
    L = min(RWKV_CHUNK, s)
    assert tm % L == 0 and L & (L - 1) == 0
    bd = jnp.asarray(np.kron(np.eye(nh, dtype=np.float32), np.ones((hd, hd), np.float32)), dtype=BF16)
    chunk = np.arange(tm) // L
    same = chunk[:, None] == chunk[None, :]
    csum = np.concatenate([same & (np.arange(tm)[None, :] <= np.arange(tm)[:, None]), same], axis=0)
    csum = jnp.asarray(csum.astype(np.float32), dtype=BF16)
    row = lambda w: pl.BlockSpec((1, tm, w), lambda bi, i: (bi, i, 0))
    vec = lambda w: pl.BlockSpec((1, w), lambda bi, i: (0, 0))
    mat = lambda shape: pl.BlockSpec(shape, lambda bi, i: (0,) * len(shape))
    w2, a2, g2 = (jnp.stack(_split2_bf16(w)) for w in (w2, a2, g2))
    outs = pl.pallas_call(
        _rwkv_prep_kernel,
        grid=(b, s // tm),
        in_specs=[pl.BlockSpec((1, tm, zh), functools.partial(lambda bi, i, j: (bi, i, j), j=col0 // zh + j))
                  for j in range(RWKV_Z_PIECES)]
        + [pl.BlockSpec((1, prev_rows, zh),
                        functools.partial(lambda bi, i, j: (bi, jnp.maximum(i * (tm // prev_rows) - 1, 0), j), j=col0 // zh + j))
           for j in range(RWKV_Z_PIECES)]
        + [vec(zw), vec(c), vec(c), vec(c), vec(c), vec(c), mat(w2.shape), mat(a2.shape), mat(g2.shape),
                  mat(bd.shape), mat(csum.shape)],
        out_specs=[row(c)] * 10,
        out_shape=[jax.ShapeDtypeStruct((b, s, c), BF16)] * 7 + [jax.ShapeDtypeStruct((b, s, c), F32)] * 3,
        compiler_params=_params("parallel", "parallel"),
        name="rwkv_prep",
    )(*([p] * (2 * RWKV_Z_PIECES)), mu.reshape(1, zw), w0.reshape(1, c), a0.reshape(1, c), k_k.reshape(1, c), k_a.reshape(1, c),
      r_k.reshape(1, c), w2, a2, g2, bd, csum)
    rows = min(RWKV_CHUNKS_PER_STEP * L, s)
    blk = pl.BlockSpec((1, rows, c), lambda bi, ci: (bi, ci, 0))
    gvec = pl.BlockSpec((1, c), lambda bi, ci: (0, 0))
    return pl.pallas_call(
        functools.partial(_rwkv_scan_kernel, L=L),
        grid=(b, s // rows),
        in_specs=[blk] * 10 + [gvec, gvec],
        out_specs=blk,
        out_shape=jax.ShapeDtypeStruct((b, s, c), BF16),
        scratch_shapes=[pltpu.VMEM((c // 128, 128, 128), F32)],
        compiler_params=_params("parallel", "arbitrary"),
        name="rwkv_scan",
    )(*outs, gn_w.reshape(1, c), gn_b.reshape(1, c))


def _retention_kernel(qk_ref, v_ref, g_ref, cos_ref, sin_ref, inner_ref, qd_ref, kd_ref, cd_ref, o_ref, state_ref):
    @pl.when(pl.program_id(1) == 0)
    def _():
        state_ref[...] = jnp.zeros_like(state_ref)

    C = qk_ref.shape[1]
    kw = RET_HEADS * RET_K_DIM
    cos, sin = cos_ref[...], sin_ref[...]
    rope = lambda t: jnp.concatenate(
        [_rope(t[:, j:j + 128], cos[:, j:j + 128], sin[:, j:j + 128], RET_K_DIM // 2, RET_K_DIM) for j in range(0, kw, 128)], axis=1)
    q = rope(qk_ref[0, :, :kw].astype(F32))
    k = rope(qk_ref[0, :, kw:].astype(F32)) * RET_K_DIM ** -0.5
    lane = lax.broadcasted_iota(jnp.int32, (C, 128), 1)
    heads = range(RET_HEADS)
    vsl = [slice(h * RET_V_DIM, (h + 1) * RET_V_DIM) for h in heads]
    own = lambda t, h: jnp.where((lane // RET_K_DIM) == h % 2, t[:, (h // 2) * 128:(h // 2 + 1) * 128], 0.0)
    qh = [own(q, h) for h in heads]
    kh = [own(k, h) for h in heads]
    vh = [v_ref[0, :, sl] for sl in vsl]
    st = [state_ref[h] for h in heads]
    scores = [_dot_nt(qh[h].astype(BF16), kh[h].astype(BF16)) * inner_ref[h] for h in heads]
    cross = [_dot((qh[h] * qd_ref[h]).astype(BF16), st[h].astype(BF16)) for h in heads]
    o = [_dot(scores[h].astype(BF16), vh[h]) + cross[h] for h in heads]
    for h in heads:
        state_ref[h] = st[h] * cd_ref[h, 0:1, :] + _dot_tn((kh[h] * kd_ref[h]).astype(BF16), vh[h])
    for h, sl in enumerate(vsl):
        gate = g_ref[0, :, sl].astype(F32)
        c = o[h] - jnp.mean(o[h], axis=-1, keepdims=True)
        y = c * lax.rsqrt(jnp.mean(c * c, axis=-1, keepdims=True) + NORM_EPS)
        o_ref[0, :, sl] = (y * (gate * jax.nn.sigmoid(gate))).astype(o_ref.dtype)


def retention_mixer(p, col0, cos, sin):
    b, s, _ = p.shape
    assert col0 % 512 == 0
    H, C = RET_HEADS, min(RET_CHUNK, s)
    log_gamma = jnp.log1p(-jnp.exp2(-5.0 - jnp.arange(H, dtype=F32)))
    n = jnp.arange(C, dtype=F32)
    dist = n[:, None] - n[None, :]
    inner = jnp.where(dist >= 0, jnp.exp(jnp.maximum(dist, 0.0) * log_gamma[:, None, None]), 0.0)
    q_decay = jnp.exp((n + 1.0) * log_gamma[:, None])
    k_decay = jnp.exp((C - 1.0 - n) * log_gamma[:, None])
    chunk_decay = jnp.exp(C * log_gamma)
    lanes = lambda t: jnp.broadcast_to(t[:, :, None], (H, C, 128))
    const = lambda shape: pl.BlockSpec(shape, lambda bi, ci: (0,) * len(shape))
    kw = RET_HEADS * RET_K_DIM
    return pl.pallas_call(
        _retention_kernel,
        grid=(b, s // C),
        in_specs=[pl.BlockSpec((1, C, 512), functools.partial(lambda bi, ci, j: (bi, ci, j), j=col0 // 512 + j))
                  for j in range(3)] + [
                  pl.BlockSpec((C, kw), lambda bi, ci: (ci, 0)), pl.BlockSpec((C, kw), lambda bi, ci: (ci, 0)),
                  const((H, C, C)), const((H, C, 128)), const((H, C, 128)), const((H, 8, 128))],
        out_specs=pl.BlockSpec((1, C, H * RET_V_DIM), lambda bi, ci: (bi, ci, 0)),
        out_shape=jax.ShapeDtypeStruct((b, s, H * RET_V_DIM), BF16),
        scratch_shapes=[pltpu.VMEM((H, 128, 128), F32)],
        compiler_params=_params("parallel", "arbitrary"),
        name="retention",
    )(p, p, p, cos, sin, inner, lanes(q_decay), lanes(k_decay), jnp.broadcast_to(chunk_decay[:, None, None], (H, 8, 128)))


def _pad_cols(w, width):
    return jnp.pad(w, [(0, 0)] * (w.ndim - 1) + [(0, width - w.shape[-1])])


def _rope_tables(s, inv_freq, width, fill_cos):
    inv_freq = np.asarray(inv_freq, np.float32)
    ang = np.arange(s, dtype=np.float32)[:, None] * inv_freq[None, :]
    cos, sin = np.cos(ang).astype(np.float32), np.sin(ang).astype(np.float32)
    rot = 2 * cos.shape[1]
    cos2 = np.concatenate([cos, cos, np.full((s, width - rot), fill_cos, np.float32)], axis=1)
    sin2 = np.concatenate([sin, sin, np.zeros((s, width - rot), np.float32)], axis=1)
    return cos2, sin2


def _inv_freq(rot_dim, theta):
    return np.float32(theta) ** (-np.arange(0, rot_dim, 2, dtype=np.float32) / np.float32(rot_dim))


def _split_offsets():
    sizes = (MLA_Q_RANK, MLA_KV_RANK, MLA_ROPE_DIM, 512, 128, 128, 128, 128, 128, 128, 12,
             3 * BRANCH_WIDTH + RWKV_DECAY_LORA + RWKV_A_LORA + RWKV_GATE_LORA, 256, 256, 512, 512, N_MIXERS * D_MODEL)
    offs = np.concatenate([[0], np.cumsum(sizes)])
    return [(int(offs[i]), int(offs[i + 1])) for i in range(len(sizes))]


def _pack_in_proj(w):
    w = w.astype(BF16)
    sl = [w[..., a:b] for a, b in _split_offsets()]
    (ql, kvl, kr, nq, nkc, nvc, nks, nvs, nkw, nvw, ng, rz, rq, rk, rv, rg, mg) = sl
    c = BRANCH_WIDTH
    w_rwkv = [rz[..., :3 * c], _pad_cols(rz[..., 3 * c:3 * c + 96], 128), _pad_cols(rz[..., 3 * c + 96:3 * c + 192], 128),
              rz[..., 3 * c + 192:]]
    groups = {"nsa": [nq, nkc, nvc, nks, nvs, nkw, nvw, _pad_cols(ng, 128)], "mla_kv": [kvl], "mla_q": [ql],
              "mla_kr": [_pad_cols(kr, 128)], "ret": [rq, rk, rv, rg], "gate": [mg], "rwkv": w_rwkv}
    cols, parts, off = {}, [], 0
    for name, ws in groups.items():
        cols[name] = off
        parts += ws
        off += sum(t.shape[-1] for t in ws)
    return jnp.concatenate(parts, axis=-1), cols


def _pack_mla_weights(w_uq, w_ukv):
    H, dn, dr, dv = MLA_HEADS, MLA_NOPE_DIM, MLA_ROPE_DIM, MLA_V_DIM
    uq = w_uq.reshape(-1, H, dn + dr)
    nope = uq[:, :, :dn].reshape(-1, H * dn)
    rope = uq[:, :, dn:]
    rope_p = _pad_cols(rope, 128).reshape(-1, H * 128)
    ukv = w_ukv.reshape(-1, H, dn + dv)
    wkv = jnp.concatenate([ukv[:, :, :dn].reshape(-1, H * dn), ukv[:, :, dn:].reshape(-1, H * dv)], axis=1)
    return jnp.concatenate([nope, rope_p], axis=1).astype(BF16), wkv.astype(BF16)


def _pad_rows(w, rows):
    return jnp.pad(w, ((0, rows - w.shape[0]), (0, 0)))


def kernel(x, w_in, w_branch, w_out, w_up, w_down, norm_gains, mla_g_q, mla_g_kv, mla_w_uq, mla_w_ukv,
           nsa_cmp_pos, nsa_cmp_w1, nsa_cmp_w2, rwkv_mu, rwkv_w0, rwkv_w2, rwkv_a0, rwkv_a2, rwkv_g2,
           rwkv_k_k, rwkv_k_a, rwkv_r_k, rwkv_gn_w, rwkv_gn_b):
    B, S, D = x.shape
    depth = w_in.shape[0]
    T = B * S
    tm, tm_in = min(512, T), min(1024, T)
    mla_cos, mla_sin = _rope_tables(S, _inv_freq(MLA_ROPE_DIM, ROPE_THETA), 128, 0.0)
    nsa_cos, nsa_sin = _rope_tables(S, _inv_freq(NSA_ROT_DIM, ROPE_THETA), 128, 1.0)
    ret_cos, ret_sin = _rope_tables(S, np.float32(RET_THETA) ** (-np.linspace(0.0, 1.0, RET_K_DIM // 2, dtype=np.float32)),
                                    RET_K_DIM, 0.0)
    ret_cos, ret_sin = np.tile(ret_cos, (1, RET_HEADS)), np.tile(ret_sin, (1, RET_HEADS))
    c = BRANCH_WIDTH
    x = x.reshape(T, D)
    w_all, cols = _pack_in_proj(w_in)
    tn_in = w_all.shape[-1] // 9
    assert tn_in % 256 == 0 and tn_in * 9 == w_all.shape[-1]
    w_branch, w_out, w_up, w_down = (w.astype(BF16) for w in (w_branch, w_out, w_up, w_down))
    for l in range(depth):
        g_pre = norm_gains[l, 0]
        p2 = norm_matmul(x, g_pre, w_all, l, BF16, tm_in, tn_in)
        p = p2.reshape(B, S, -1)

        wq, wkv = _pack_mla_weights(mla_w_uq[l], mla_w_ukv[l])
        y_mla = mla_mixer(p, (cols["mla_q"], cols["mla_kv"], cols["mla_kr"]), mla_g_q[l], mla_g_kv[l], wq, wkv, mla_cos, mla_sin)
        y_nsa = nsa_mixer(p, cols["nsa"], nsa_cmp_pos[l], nsa_cmp_w1[l], nsa_cmp_w2[l], nsa_cos, nsa_sin)
        mu = rwkv_mu[l]
        mu_p = jnp.concatenate([mu[:3 * c], _pad_cols(mu[3 * c:3 * c + 96], 128), _pad_cols(mu[3 * c + 96:3 * c + 192], 128),
                                mu[3 * c + 192:]])
        y_rwkv = rwkv7_mixer(p, cols["rwkv"], mu_p, rwkv_w0[l], _pad_rows(rwkv_w2[l], 128), rwkv_a0[l],
                             _pad_rows(rwkv_a2[l], 128), rwkv_g2[l], rwkv_k_k[l], rwkv_k_a[l], rwkv_r_k[l].reshape(-1),
                             rwkv_gn_w[l], rwkv_gn_b[l])
        y_ret = retention_mixer(p, cols["ret"], ret_cos, ret_sin)

        ys = [y.reshape(T, c) for y in (y_mla, y_nsa, y_rwkv, y_ret)]
        x, h = merge_out_residual(ys, p2, cols["gate"], w_branch, w_out, l, norm_gains[l, 1], norm_gains[l, 2], x, min(256, T))
        x = ffn_block(x, h, w_up, w_down, l, norm_gains[l, 3], tm, 1024)
    return x.reshape(B, S, D)
```

```python
import functools

import numpy as np
import jax
import jax.numpy as jnp
from jax import lax
from jax.experimental import pallas as pl
from jax.experimental.pallas import tpu as pltpu

F32 = jnp.float32
BF16 = jnp.bfloat16
HIGHEST = lax.Precision.HIGHEST

V7X_LANES = 128
V7X_VMEM_LIMIT_BYTES = 56 * 1024 * 1024

D_MODEL = 2048
N_MIXERS = 4
BRANCH_WIDTH = D_MODEL // N_MIXERS
ROPE_THETA = 500000.0
NORM_EPS = 1e-6
MASK_VALUE = -1e30
LOG2_E = float(np.log2(np.e))

MLA_NOPE_DIM = 128
MLA_ROPE_DIM = 64
MLA_V_DIM = 128
MLA_HEADS = 4
MLA_Q_RANK = 384
MLA_KV_RANK = 128

NSA_HEAD_DIM = 128
NSA_HEADS = 4
NSA_ROT_DIM = 32
NSA_CMP_LEN = 32
NSA_CMP_STRIDE = 16
NSA_SEL_BLOCK = 64
NSA_TOP_N = 16
NSA_WINDOW = 512
NSA_FORCED_SCORE = 1000.0
NSA_IN = NSA_HEADS * NSA_HEAD_DIM + 7 * 128

RWKV_HEAD_DIM = 64
RWKV_HEADS = 8
RWKV_DECAY_LORA = 96
RWKV_A_LORA = 96
RWKV_GATE_LORA = 256
RWKV_GN_EPS = 64e-5
RWKV_CHUNK = 64
RWKV_CHUNKS_PER_STEP = 4

RET_HEADS = 4
RET_V_DIM = 128
RET_K_DIM = 64
RET_CHUNK = 256
RET_THETA = 10000.0


def _params(*sem):
    return pltpu.CompilerParams(dimension_semantics=sem, vmem_limit_bytes=V7X_VMEM_LIMIT_BYTES)


def _dot(a, b, precision=None):
    return jnp.dot(a, b, preferred_element_type=F32, precision=precision)


def _dot_nt(a, b, precision=None):
    return lax.dot_general(a, b, (((1,), (1,)), ((), ())), preferred_element_type=F32, precision=precision)


def _dot_tn(a, b, precision=None):
    return lax.dot_general(a, b, (((0,), (0,)), ((), ())), preferred_element_type=F32, precision=precision)


def _rotate_half(x, half, period):
    lane = lax.broadcasted_iota(jnp.int32, x.shape, x.ndim - 1) & (period - 1)
    up = pltpu.roll(x, V7X_LANES - half, axis=x.ndim - 1)
    dn = pltpu.roll(x, half, axis=x.ndim - 1)
    return jnp.where(lane < half, -up, jnp.where(lane < 2 * half, dn, 0.0))


def _rope(x, cos, sin, half, period=V7X_LANES):
    return x * cos + _rotate_half(x, half, period) * sin


def _rms(x, g, eps=NORM_EPS):
    return x * lax.rsqrt(jnp.mean(x * x, axis=-1, keepdims=True) + eps) * g


def _norm_matmul_kernel(x_ref, g_ref, w_ref, o_ref, h_ref):
    @pl.when(pl.program_id(1) == 0)
    def _():
        h_ref[...] = _rms(x_ref[...], g_ref[...]).astype(BF16)

    o_ref[...] = _dot(h_ref[...], w_ref[...]).astype(o_ref.dtype)


def norm_matmul(x, g, w, layer, out_dtype, tm, tn):
    m, k = x.shape
    n = w.shape[2]
    assert m % tm == 0 and n % tn == 0
    return pl.pallas_call(
        _norm_matmul_kernel,
        grid=(m // tm, n // tn),
        in_specs=[pl.BlockSpec((tm, k), lambda i, j: (i, 0)),
                  pl.BlockSpec((1, k), lambda i, j: (0, 0)),
                  pl.BlockSpec((None, k, tn), lambda i, j: (layer, 0, j))],
        out_specs=pl.BlockSpec((tm, tn), lambda i, j: (i, j)),
        out_shape=jax.ShapeDtypeStruct((m, n), out_dtype),
        scratch_shapes=[pltpu.VMEM((tm, k), BF16)],
        compiler_params=_params("parallel", "arbitrary"),
        name="norm_matmul",
    )(x, g.reshape(1, k), w)


MERGE_GATE_PIECES = 4


def _merge_out_kernel(*refs):
    n = MERGE_GATE_PIECES
    y_refs, g_refs = refs[:N_MIXERS], refs[N_MIXERS:(1 + n) * N_MIXERS]
    wb_ref, wo_ref, gn_ref, gf_ref, x_ref, o_ref, h_ref = refs[(1 + n) * N_MIXERS:]
    width = g_refs[0].shape[1]
    merged = []
    for piece in range(n):
        acc = None
        for m in range(N_MIXERS):
            gate = 0.5 * jnp.tanh(0.5 * g_refs[n * m + piece][...].astype(F32)) + 0.5
            term = gate * _dot(y_refs[m][...], wb_ref[m, :, piece * width:(piece + 1) * width])
            acc = term if acc is None else acc + term
        merged.append(acc.astype(BF16))
    x_new = x_ref[...] + _rms(_dot(jnp.concatenate(merged, axis=1), wo_ref[...]), gn_ref[...])
    o_ref[...] = x_new
    h_ref[...] = _rms(x_new, gf_ref[...]).astype(BF16)


def merge_out_residual(ys, gates, col0, wb, wo, layer, gn, g_ffn, x, tm):
    m, c = ys[0].shape
    d = wb.shape[3]
    half = d // MERGE_GATE_PIECES
    assert col0 % half == 0
    once = pl.Buffered(1)
    return pl.pallas_call(
        _merge_out_kernel,
        grid=(m // tm,),
        in_specs=[pl.BlockSpec((tm, c), lambda i: (i, 0))] * N_MIXERS
        + [pl.BlockSpec((tm, half), functools.partial(lambda i, j: (i, j), j=col0 // half + jj))
           for jj in range(MERGE_GATE_PIECES * N_MIXERS)]
        + [pl.BlockSpec((None, N_MIXERS, c, d), lambda i: (layer, 0, 0, 0), pipeline_mode=once),
           pl.BlockSpec((None, d, d), lambda i: (layer, 0, 0), pipeline_mode=once),
           pl.BlockSpec((1, d), lambda i: (0, 0)),
           pl.BlockSpec((1, d), lambda i: (0, 0)),
           pl.BlockSpec((tm, d), lambda i: (i, 0))],
        out_specs=[pl.BlockSpec((tm, d), lambda i: (i, 0))] * 2,
        out_shape=[jax.ShapeDtypeStruct((m, d), F32), jax.ShapeDtypeStruct((m, d), BF16)],
        compiler_params=_params("parallel"),
        name="merge_out_residual",
    )(*ys, *([gates] * (MERGE_GATE_PIECES * N_MIXERS)), wb, wo, gn.reshape(1, d), g_ffn.reshape(1, d), x)


def _ffn_kernel(x_ref, h_ref, wu_ref, wd_ref, g2_ref, o_ref, acc_ref):
    f = pl.program_id(1)

    @pl.when(f == 0)
    def _():
        acc_ref[...] = jnp.zeros_like(acc_ref)

    u = jnp.maximum(_dot(h_ref[...], wu_ref[...]), 0.0)
    acc_ref[...] += _dot((u * u).astype(BF16), wd_ref[...])

    @pl.when(f == pl.num_programs(1) - 1)
    def _():
        o_ref[...] = x_ref[...] + _rms(acc_ref[...], g2_ref[...])


def ffn_block(x, h, wu, wd, layer, g2, tm, tf):
    m, d = x.shape
    ff = wu.shape[2]
    return pl.pallas_call(
        _ffn_kernel,
        grid=(m // tm, ff // tf),
        in_specs=[pl.BlockSpec((tm, d), lambda i, f: (i, 0)),
                  pl.BlockSpec((tm, d), lambda i, f: (i, 0)),
                  pl.BlockSpec((None, d, tf), lambda i, f: (layer, 0, f)),
                  pl.BlockSpec((None, tf, d), lambda i, f: (layer, f, 0)),
                  pl.BlockSpec((1, d), lambda i, f: (0, 0))],
        out_specs=pl.BlockSpec((tm, d), lambda i, f: (i, 0)),
        out_shape=jax.ShapeDtypeStruct((m, d), F32),
        scratch_shapes=[pltpu.VMEM((tm, d), F32)],
        compiler_params=_params("parallel", "arbitrary"),
        name="ffn_block",
    )(x, h, wu, wd, g2.reshape(1, d))


def _online_softmax_steps(carries, scores, vt_tiles, masks):
    scores = [s if mask is None else jnp.where(mask, s, -jnp.inf) for s, mask in zip(scores, masks)]
    out = []
    for (m_prev, l_prev, acc_prev), s, vt in zip(carries, scores, vt_tiles):
        m_new = jnp.maximum(m_prev, jnp.max(s, axis=0, keepdims=True))
        alpha = jnp.exp2(m_prev - m_new)
        p = jnp.exp2(s - m_new)
        out.append((m_new, alpha * l_prev + jnp.sum(p, axis=0, keepdims=True), alpha * acc_prev + _dot(vt, p.astype(BF16))))
    return tuple(out)


def _softmax_init(cols, dv):
    return (jnp.full((1, cols), MASK_VALUE, F32), jnp.zeros((1, cols), F32), jnp.zeros((dv, cols), F32))


def _softmax_merge(a, b):
    (ma, la, acca), (mb, lb, accb) = a, b
    m = jnp.maximum(ma, mb)
    wa, wb = jnp.exp2(ma - m), jnp.exp2(mb - m)
    return m, la * wa + lb * wb, acca * wa + accb * wb


def _softmax_finish(carry):
    _, l, acc = carry
    return acc / jnp.where(l > 0.0, l, 1.0)


def _mla_prep_kernel(ql_ref, kvl_ref, krl_ref, gq_ref, gkv_ref, wq_ref, wk_ref, wvt_ref, cos_ref, sin_ref,
                     qn_ref, qr_ref, kn_ref, vt_ref, kr_ref, *, scale):
    nq = _rms(ql_ref[0].astype(F32), gq_ref[...]).astype(BF16)
    nkv = _rms(kvl_ref[0].astype(F32), gkv_ref[...]).astype(BF16)
    q = _dot(nq, wq_ref[...])
    cos, sin = cos_ref[...], sin_ref[...]
    hw = MLA_HEADS * MLA_NOPE_DIM
    half = MLA_ROPE_DIM // 2
    qn_ref[0] = (q[:, :hw] * scale).astype(BF16)
    for h in range(MLA_HEADS):
        qr_ref[0, :, h * 128:(h + 1) * 128] = (_rope(q[:, hw + h * 128: hw + (h + 1) * 128], cos, sin, half) * scale).astype(BF16)
    kn_ref[0] = _dot(nkv, wk_ref[...]).astype(BF16)
    vt_ref[0] = _dot_nt(wvt_ref[...], nkv).astype(BF16)
    kr_ref[0] = _rope(krl_ref[0].astype(F32), cos, sin, half).astype(BF16)


def _mla_attn_kernel(qn_ref, qr_ref, kn_ref, kr_ref, vt_ref, o_ref, *, tq, tk):
    i = pl.program_id(1)
    hs = [slice(h * 128, (h + 1) * 128) for h in range(MLA_HEADS)]
    q = [jnp.concatenate([qn_ref[0, :, sl], qr_ref[0, :, sl]], axis=1) for sl in hs]

    def tile(t, carries, mask):
        k0 = pl.multiple_of(t * tk, tk)
        kr = kr_ref[0, pl.ds(k0, tk), :]
        scores = [_dot_nt(jnp.concatenate([kn_ref[0, pl.ds(k0, tk), sl], kr], axis=1), q[h])
                  for h, sl in enumerate(hs)]
        return _online_softmax_steps(carries, scores, [vt_ref[0, sl, pl.ds(k0, tk)] for sl in hs], [mask] * len(hs))

    init = tuple(_softmax_init(tq, MLA_V_DIM) for _ in hs)
    t_diag = (i * tq) // tk
    carries = lax.fori_loop(0, t_diag, lambda t, c: tile(t, c, None), init)
    causal = (t_diag * tk + lax.broadcasted_iota(jnp.int32, (tk, tq), 0)
              <= i * tq + lax.broadcasted_iota(jnp.int32, (tk, tq), 1))
    for sl, carry in zip(hs, tile(t_diag, carries, causal)):
        o_ref[0, :, sl] = _softmax_finish(carry).T.astype(o_ref.dtype)


def mla_mixer(p, cols, g_q, g_kv, wq, wkv, cos, sin, *, tm=512, tq=512, tk=512):
    b, s, _ = p.shape
    tm, tq, tk = min(tm, s), min(tq, s), min(tk, s)
    col_q, col_kv, col_kr = cols
    assert tk % tq == 0 and col_q % MLA_Q_RANK == 0 and col_kv % 128 == 0 and col_kr % 128 == 0
    hw = MLA_HEADS * 128
    scale = (MLA_NOPE_DIM + MLA_ROPE_DIM) ** -0.5 * LOG2_E
    full = lambda shape: pl.BlockSpec(shape, lambda bi, i: (0,) * len(shape))
    row = lambda w: pl.BlockSpec((1, tm, w), lambda bi, i: (bi, i, 0))
    outs = pl.pallas_call(
        functools.partial(_mla_prep_kernel, scale=scale),
        grid=(b, s // tm),
        in_specs=[pl.BlockSpec((1, tm, MLA_Q_RANK), lambda bi, i: (bi, i, col_q // MLA_Q_RANK)),
                  pl.BlockSpec((1, tm, 128), lambda bi, i: (bi, i, col_kv // 128)),
                  pl.BlockSpec((1, tm, 128), lambda bi, i: (bi, i, col_kr // 128)),
                  full((1, MLA_Q_RANK)), full((1, MLA_KV_RANK)), full(wq.shape), full(wkv[0].shape), full(wkv[1].shape),
                  pl.BlockSpec((tm, 128), lambda bi, i: (i, 0)), pl.BlockSpec((tm, 128), lambda bi, i: (i, 0))],
        out_specs=[row(hw), row(hw), row(hw), pl.BlockSpec((1, hw, tm), lambda bi, i: (bi, 0, i)), row(128)],
        out_shape=[jax.ShapeDtypeStruct((b, s, hw), BF16)] * 3 + [jax.ShapeDtypeStruct((b, hw, s), BF16),
                                                                   jax.ShapeDtypeStruct((b, s, 128), BF16)],
        compiler_params=_params("parallel", "parallel"),
        name="mla_prep",
    )(p, p, p, g_q.reshape(1, -1), g_kv.reshape(1, -1), wq, wkv[0], wkv[1], cos, sin)
    qn, qr, kn, vt, kr = outs
    q_spec = pl.BlockSpec((1, tq, hw), lambda bi, i: (bi, i, 0))
    per_batch = lambda shape: pl.BlockSpec((1,) + shape, lambda bi, i: (bi, 0, 0))
    return pl.pallas_call(
        functools.partial(_mla_attn_kernel, tq=tq, tk=tk),
        grid=(b, s // tq),
        in_specs=[q_spec, q_spec, per_batch((s, hw)), per_batch((s, 128)), per_batch((hw, s))],
        out_specs=q_spec,
        out_shape=jax.ShapeDtypeStruct((b, s, hw), BF16),
        compiler_params=_params("parallel", "arbitrary"),
        name="mla_attn",
    )(qn, qr, kn, kr, vt)


def _nsa_prep_kernel(p_ref, cos_ref, sin_ref, q_ref, kc_ref, vc_ref, ks_ref, kw_ref, vst_ref, vwt_ref, *, scale):
    cos, sin = cos_ref[...], sin_ref[...]
    hw = NSA_HEADS * NSA_HEAD_DIM
    half = NSA_ROT_DIM // 2
    tile = lambda col: p_ref[0, :, col * 128:(col + 1) * 128].astype(F32)
    for h in range(NSA_HEADS):
        q_ref[0, :, h * 128:(h + 1) * 128] = (_rope(tile(h), cos, sin, half) * scale).astype(BF16)
    base = hw // 128
    kc_ref[0] = _rope(tile(base), cos, sin, half)
    vc_ref[0] = tile(base + 1)
    ks_ref[0] = _rope(tile(base + 2), cos, sin, half).astype(BF16)
    kw_ref[0] = _rope(tile(base + 4), cos, sin, half).astype(BF16)
    eye = (lax.broadcasted_iota(jnp.int32, (128, 128), 0) == lax.broadcasted_iota(jnp.int32, (128, 128), 1)).astype(BF16)
    vst_ref[0] = _dot_nt(eye, p_ref[0, :, (base + 3) * 128:(base + 4) * 128]).astype(BF16)
    vwt_ref[0] = _dot_nt(eye, p_ref[0, :, (base + 5) * 128:(base + 6) * 128]).astype(BF16)


def _gelu_tanh(x):
    return 0.5 * x * (1.0 + jnp.tanh(np.sqrt(2.0 / np.pi).astype(np.float32) * (x + 0.044715 * (x * x * x))))


def _split3_bf16(x):
    hi = x.astype(BF16)
    r1 = x - hi.astype(F32)
    mid = r1.astype(BF16)
    return hi, mid, (r1 - mid.astype(F32)).astype(BF16)


def _split2_bf16(x):
    hi = x.astype(BF16)
    return hi, (x - hi.astype(F32)).astype(BF16)


def _dot_split(a, w2_ref):
    hi, lo = _split2_bf16(a)
    return _dot(hi, w2_ref[0]) + (_dot(hi, w2_ref[1]) + _dot(lo, w2_ref[0]))


def _nsa_compress_kernel(k_ref, v_ref, pos_ref, w1_ref, w2_ref, kc3_ref, vct_ref):
    nrow = k_ref.shape[1]
    row = lax.broadcasted_iota(jnp.int32, (nrow, NSA_HEAD_DIM), 0)
    outs = []
    for z, x_ref in enumerate((k_ref, v_ref)):
        res = _dot(x_ref[0], w1_ref[z], HIGHEST)
        pb = _dot(pos_ref[z], w1_ref[z], HIGHEST)
        bias = pb[0:1, :NSA_HEAD_DIM] + pb[1:2, NSA_HEAD_DIM:]
        nxt = pltpu.roll(res[:, NSA_HEAD_DIM:], nrow - 1, axis=0)
        hid = _gelu_tanh(res[:, :NSA_HEAD_DIM] + nxt + bias)
        outs.append(jnp.where(row < nrow - 1, _dot(hid, w2_ref[z], HIGHEST), 0.0))
    for z, piece in enumerate(_split3_bf16(outs[0])):
        kc3_ref[0, z] = piece
    vct_ref[0] = outs[1].T.astype(BF16)


def _nsa_attn_kernel(q_ref, kc3_ref, vct_ref, covert_ref, ks_ref, vst_ref, kw_ref, vwt_ref, et_ref, g_ref, o_ref,
                     *, tq, tk, n_blk):
    i = pl.program_id(1)
    q0 = i * tq
    cols = NSA_HEADS * tq
    q4 = jnp.concatenate([q_ref[0, :, h * 128:(h + 1) * 128] for h in range(NSA_HEADS)], axis=0)

    ncmp = kc3_ref.shape[2]
    s = sum(_dot_nt(kc3_ref[0, z], q4) for z in range(3))
    n = lax.broadcasted_iota(jnp.int32, (ncmp, cols), 0)
    qpos_c = q0 + (lax.broadcasted_iota(jnp.int32, (ncmp, cols), 1) & (tq - 1))
    s = jnp.where((n * NSA_CMP_STRIDE + (NSA_CMP_LEN - 1) <= qpos_c) & (n < ncmp - 1), s, -jnp.inf)
    e = jnp.exp2(s - jnp.maximum(jnp.max(s, axis=0, keepdims=True), MASK_VALUE))
    l = jnp.sum(e, axis=0, keepdims=True)
    p = e / jnp.where(l > 0.0, l, 1.0)
    o_cmp = _dot(vct_ref[0], p.astype(BF16))
    psum = sum(p[:, h * tq:(h + 1) * tq] for h in range(NSA_HEADS))
    imp = sum(_dot(covert_ref[...], piece) for piece in _split3_bf16(psum))

    nb = covert_ref.shape[0]
    jb = lax.broadcasted_iota(jnp.int32, (nb, tq), 0)
    cur = (q0 + lax.broadcasted_iota(jnp.int32, (nb, tq), 1)) // NSA_SEL_BLOCK
    forced = (jb == 0) | (jb == cur) | (jb == cur - 1)
    visible = jb <= cur
    score = jnp.where(visible, imp + jnp.where(forced, NSA_FORCED_SCORE, 0.0), -jnp.inf)
    beaten = jnp.zeros((nb, tq), F32)
    for j in range(n_blk):
        row = score[j:j + 1, :]
        beaten = beaten + jnp.where((row > score) | ((row == score) & (jb > j)), 1.0, 0.0)
    keep = (beaten < float(NSA_TOP_N)) & visible
    sel = jnp.concatenate([jnp.where(keep, 0.0, MASK_VALUE), jnp.full((128 - nb, tq), MASK_VALUE, F32)], axis=0).astype(BF16)

    qh = [q_ref[0, :, h * 128:(h + 1) * 128] for h in range(NSA_HEADS)]
    kio = lax.broadcasted_iota(jnp.int32, (tk, tq), 0)
    qpos = q0 + lax.broadcasted_iota(jnp.int32, (tk, tq), 1)
    init = tuple(_softmax_init(tq, NSA_HEAD_DIM) for _ in qh)

    def slc_scores(k0):
        k = ks_ref[0, pl.ds(k0, tk), :]
        bias = _dot(et_ref[pl.ds(k0, tk), :], sel)
        return [_dot_nt(k, q) + bias for q in qh]

    def tile(t, carries, causal, window):
        k0 = pl.multiple_of(t * tk, tk)
        scores = slc_scores(k0)
        vts = [vst_ref[0, :, pl.ds(k0, tk)]] * NSA_HEADS
        masks = [(k0 + kio <= qpos) if causal else None] * NSA_HEADS
        if window:
            k = kw_ref[0, pl.ds(k0, tk), :]
            dist = qpos - (k0 + kio)
            scores += [_dot_nt(k, q) for q in qh]
            vts += [vwt_ref[0, :, pl.ds(k0, tk)]] * NSA_HEADS
            masks += [(dist >= 0) & (dist < NSA_WINDOW)] * NSA_HEADS
        return _online_softmax_steps(carries, scores, vts, masks)

    def tile_pair(gi, carries):
        k0 = pl.multiple_of(gi * (2 * tk), 2 * tk)
        scores, vts = [], []
        for half in range(2):
            scores += slc_scores(k0 + half * tk)
            vts += [vst_ref[0, :, pl.ds(k0 + half * tk, tk)]] * NSA_HEADS
        return _online_softmax_steps(carries, scores, vts, [None] * (2 * NSA_HEADS))

    t_diag = q0 // tk
    t_lo = jnp.maximum(q0 - NSA_WINDOW + 1, 0) // tk
    pairs = lax.fori_loop(0, t_lo // 2, tile_pair, init + init)
    o_slc = tuple(_softmax_merge(a, b) for a, b in zip(pairs[:NSA_HEADS], pairs[NSA_HEADS:]))
    o_slc = lax.fori_loop(2 * (t_lo // 2), t_lo, lambda t, c: tile(t, c, False, False), o_slc)
    both = lax.fori_loop(t_lo, t_diag, lambda t, c: tile(t, c, False, True), o_slc + init)
    both = tile(t_diag, both, True, True)
    o_slc, o_win = both[:NSA_HEADS], both[NSA_HEADS:]

    g = jax.nn.sigmoid(g_ref[0].astype(F32).T)
    for h in range(NSA_HEADS):
        o = (g[3 * h:3 * h + 1] * o_cmp[:, h * tq:(h + 1) * tq] + g[3 * h + 1:3 * h + 2] * _softmax_finish(o_slc[h])
             + g[3 * h + 2:3 * h + 3] * _softmax_finish(o_win[h]))
        o_ref[0, :, h * 128:(h + 1) * 128] = o.T.astype(o_ref.dtype)


def nsa_mixer(p, col0, pos_emb, w1, w2, cos, sin, *, tm=512, tq=512, tk=512):
    b, s, _ = p.shape
    assert col0 % NSA_IN == 0
    tm, tk = min(tm, s), min(tk, s)
    hw = NSA_HEADS * NSA_HEAD_DIM
    scale = NSA_HEAD_DIM ** -0.5 * LOG2_E
    row = lambda w: pl.BlockSpec((1, tm, w), lambda bi, i: (bi, i, 0))
    tab = pl.BlockSpec((tm, 128), lambda bi, i: (i, 0))
    tr_spec = pl.BlockSpec((1, 128, tm), lambda bi, i: (bi, 0, i))
    q, kc_in, vc_in, ks, kw, vst, vwt = pl.pallas_call(
        functools.partial(_nsa_prep_kernel, scale=scale),
        grid=(b, s // tm),
        in_specs=[pl.BlockSpec((1, tm, NSA_IN), lambda bi, i: (bi, i, col0 // NSA_IN)), tab, tab],
        out_specs=[row(hw), row(128), row(128), row(128), row(128), tr_spec, tr_spec],
        out_shape=[jax.ShapeDtypeStruct((b, s, hw), BF16), jax.ShapeDtypeStruct((b, s, 128), F32),
                   jax.ShapeDtypeStruct((b, s, 128), F32), jax.ShapeDtypeStruct((b, s, 128), BF16),
                   jax.ShapeDtypeStruct((b, s, 128), BF16), jax.ShapeDtypeStruct((b, 128, s), BF16),
                   jax.ShapeDtypeStruct((b, 128, s), BF16)],
        compiler_params=_params("parallel", "parallel"),
        name="nsa_prep",
    )(p, cos, sin)

    nrow = s // NSA_CMP_STRIDE
    fw = NSA_CMP_STRIDE * NSA_HEAD_DIM
    half = w1.shape[1] // 2
    w1r = jnp.concatenate([w1[:, :half], w1[:, half:]], axis=2)
    pos2 = jnp.pad(pos_emb.reshape(2, 2, fw), ((0, 0), (0, 6), (0, 0)))
    per_b = lambda shape: pl.BlockSpec((1,) + shape, lambda bi: (bi, 0, 0))
    const3 = lambda shape: pl.BlockSpec(shape, lambda bi: (0, 0, 0))
    kc3, vct = pl.pallas_call(
        _nsa_compress_kernel,
        grid=(b,),
        in_specs=[per_b((nrow, fw)), per_b((nrow, fw)), const3(pos2.shape), const3(w1r.shape), const3(w2.shape)],
        out_specs=[pl.BlockSpec((1, 3, nrow, NSA_HEAD_DIM), lambda bi: (bi, 0, 0, 0)), per_b((NSA_HEAD_DIM, nrow))],
        out_shape=[jax.ShapeDtypeStruct((b, 3, nrow, NSA_HEAD_DIM), BF16), jax.ShapeDtypeStruct((b, NSA_HEAD_DIM, nrow), BF16)],
        compiler_params=_params("parallel"),
        name="nsa_compress",
    )(kc_in.reshape(b, nrow, fw), vc_in.reshape(b, nrow, fw), pos2, w1r, w2)

    n_blk = s // NSA_SEL_BLOCK
    nb = 64
    assert n_blk <= nb and tq % 128 == 0 and tq & (tq - 1) == 0 and tk % tq == 0
    cmp_start = np.arange(nrow) * NSA_CMP_STRIDE
    cmp_end = cmp_start + NSA_CMP_LEN - 1
    blk_start = np.arange(nb) * NSA_SEL_BLOCK
    cover_t = ((cmp_start[None, :] <= blk_start[:, None] + NSA_SEL_BLOCK - 1)
               & (cmp_end[None, :] >= blk_start[:, None]) & (np.arange(nb)[:, None] < n_blk)
               & (np.arange(nrow)[None, :] < nrow - 1)).astype(np.float32)
    expand_t = ((np.arange(s)[:, None] // NSA_SEL_BLOCK) == np.arange(128)[None, :]).astype(np.float32)
    qrow = lambda w: pl.BlockSpec((1, tq, w), lambda bi, i: (bi, i, 0))
    k_spec = pl.BlockSpec((1, s, 128), lambda bi, i: (bi, 0, 0))
    vt_spec = pl.BlockSpec((1, 128, s), lambda bi, i: (bi, 0, 0))
    gate_block = (col0 + NSA_IN) // 128 - 1
    return pl.pallas_call(
        functools.partial(_nsa_attn_kernel, tq=tq, tk=tk, n_blk=n_blk),
        grid=(b, s // tq),
        in_specs=[qrow(hw), pl.BlockSpec((1, 3, nrow, NSA_HEAD_DIM), lambda bi, i: (bi, 0, 0, 0)),
                  pl.BlockSpec((1, NSA_HEAD_DIM, nrow), lambda bi, i: (bi, 0, 0)),
                  pl.BlockSpec((nb, nrow), lambda bi, i: (0, 0)),
                  k_spec, vt_spec, k_spec, vt_spec, pl.BlockSpec((s, 128), lambda bi, i: (0, 0)),
                  pl.BlockSpec((1, tq, 128), lambda bi, i: (bi, i, gate_block))],
        out_specs=qrow(hw),
        out_shape=jax.ShapeDtypeStruct((b, s, hw), BF16),
        compiler_params=_params("parallel", "arbitrary"),
        name="nsa_attn",
    )(q, kc3, vct, jnp.asarray(cover_t, dtype=BF16), ks, vst, kw, vwt, jnp.asarray(expand_t, dtype=BF16), p)


RWKV_Z_PIECES = 4


def _rwkv_prep_kernel(*refs):
    z_refs, zp_refs = refs[:RWKV_Z_PIECES], refs[RWKV_Z_PIECES:2 * RWKV_Z_PIECES]
    (mu_ref, w0_ref, a0_ref, kk_ref, ka_ref, rk_ref, w2_ref, a2_ref, g2_ref, bd_ref, csum_ref, kt_ref, rt_ref, kh_ref,
     bh_ref, kbar_ref, bbar_ref, v_ref, etot_ref, g_ref, bonus_ref) = refs[2 * RWKV_Z_PIECES:]
    i = pl.program_id(1)
    z = jnp.concatenate([r[0] for r in z_refs], axis=1).astype(F32)
    tm = z.shape[0]
    last = zp_refs[0].shape[1] - 1
    prev_row = jnp.concatenate([r[0, last:, :] for r in zp_refs], axis=1).astype(F32)
    prev_row = jnp.where(i > 0, prev_row, 0.0)
    rowid = lax.broadcasted_iota(jnp.int32, z.shape, 0)
    z_prev = jnp.where(rowid == 0, prev_row, pltpu.roll(z, 1, axis=0))
    z = z + (z_prev - z) * mu_ref[...]
    c = BRANCH_WIDTH
    r, k, v = z[:, :c], z[:, c:2 * c], z[:, 2 * c:3 * c]
    wd, ad, gd = z[:, 3 * c:3 * c + 128], z[:, 3 * c + 128:3 * c + 256], z[:, 3 * c + 256:3 * c + 512]
    wpre = -(w0_ref[...] + _dot_split(jnp.tanh(wd), w2_ref))
    w_log = -(jnp.maximum(wpre, 0.0) + jnp.log1p(jnp.exp(-jnp.abs(wpre)))) - 0.5
    lw = -jnp.exp(w_log)
    a = jax.nn.sigmoid(a0_ref[...] + _dot_split(ad, a2_ref))
    g_ref[0] = _dot_split(jax.nn.sigmoid(gd), g2_ref)
    kk = k * kk_ref[...]
    head_sum = lambda t: sum(_dot(piece, bd_ref[...]) for piece in _split2_bf16(t))
    kk = kk * lax.rsqrt(jnp.maximum(head_sum(kk * kk), 1e-24))
    k = k * (1.0 + (a - 1.0) * ka_ref[...])
    bonus_ref[0] = head_sum(r * k * rk_ref[...]) * v
    sums = sum(_dot(csum_ref[...], piece) for piece in _split3_bf16(lw))
    cum, total = sums[:tm], sums[tm:]
    dec_out, dec_end = jnp.exp(-cum), jnp.exp(total - cum)
    b = kk * a
    kt_ref[0] = (kk * jnp.exp(cum - lw)).astype(BF16)
    rt_ref[0] = (r * jnp.exp(cum)).astype(BF16)
    kh_ref[0] = (k * dec_out).astype(BF16)
    bh_ref[0] = (b * dec_out).astype(BF16)
    kbar_ref[0] = (k * dec_end).astype(BF16)
    bbar_ref[0] = (b * dec_end).astype(BF16)
    v_ref[0] = v.astype(BF16)
    etot_ref[0] = jnp.exp(total)


def _rwkv_scan_kernel(kt_ref, rt_ref, kh_ref, bh_ref, kbar_ref, bbar_ref, v_ref, etot_ref, g_ref, bonus_ref,
                      gw_ref, gb_ref, y_ref, state_ref, *, L):
    n_chunks = kt_ref.shape[1] // L
    N = RWKV_HEAD_DIM
    n_pairs = kt_ref.shape[2] // 128

    @pl.when(pl.program_id(1) == 0)
    def _():
        state_ref[...] = jnp.zeros_like(state_ref)

    ri = lax.broadcasted_iota(jnp.int32, (L, 2 * L), 0)
    ci = lax.broadcasted_iota(jnp.int32, (L, 2 * L), 1) & (L - 1)
    strict2, incl2 = ci < ri, ci <= ri
    eye = jnp.where(lax.broadcasted_iota(jnp.int32, (L, L), 0) == lax.broadcasted_iota(jnp.int32, (L, L), 1), 1.0, 0.0)
    first2 = lax.broadcasted_iota(jnp.int32, (2 * L, 128), 1) < N
    first = lax.broadcasted_iota(jnp.int32, (L, 128), 1) < N
    bd = (lax.broadcasted_iota(jnp.int32, (128, 128), 0) < N) == (lax.broadcasted_iota(jnp.int32, (128, 128), 1) < N)
    zero = jnp.zeros((), BF16)

    items = range(n_chunks * n_pairs)
    block = lambda ref, i: ref[0, (i // n_pairs) * L:(i // n_pairs + 1) * L, (i % n_pairs) * 128:(i % n_pairs + 1) * 128]
    xs_in = [jnp.concatenate([block(kt_ref, i), block(rt_ref, i)], axis=0) for i in items]
    ys_in = [jnp.concatenate([block(kh_ref, i), block(bh_ref, i)], axis=0) for i in items]
    gms = [_dot_nt(jnp.where(first2 if sub == 0 else ~first2, xs_in[i], zero), ys_in[i])
           for i in items for sub in range(2)]
    a_k = jnp.stack([jnp.where(strict2, gm[:L], 0.0) for gm in gms])
    a_r = jnp.stack([jnp.where(incl2, gm[L:], 0.0) for gm in gms])
    a_kb = a_k[:, :, L:]
    bmm = lambda a, b: jnp.einsum("hij,hjk->hik", a.astype(BF16), b.astype(BF16), preferred_element_type=F32)
    tinv = eye[None] - a_kb
    pw = bmm(a_kb, a_kb)
    for _ in range(int(np.log2(L)) - 2):
        both = bmm(jnp.concatenate([tinv, pw], axis=1), pw)
        tinv = tinv + both[:, :L]
        pw = both[:, L:]
    tinv = tinv + bmm(tinv, pw)

    pick = lambda t: jnp.where(first, t[:L], t[L:])
    hsum = lambda t: jnp.where(first, jnp.sum(jnp.where(first, t, 0.0), axis=-1, keepdims=True),
                               jnp.sum(jnp.where(first, 0.0, t), axis=-1, keepdims=True))
    states = [state_ref[p] for p in range(n_pairs)]
    sls = [slice(p * 128, (p + 1) * 128) for p in range(n_pairs)]
    pairs = range(n_pairs)
    stack2 = lambda t, i: jnp.concatenate([t[2 * i], t[2 * i + 1]], axis=0).astype(BF16)
    vs = [block(v_ref, i) for i in items]
    akkv = [pick(_dot(stack2(a_k[:, :, :L], i), vs[i])) for i in items]
    tk = [_dot(stack2(tinv, i), jnp.concatenate([xs_in[i][:L], akkv[i].astype(BF16)], axis=1)) for i in items]
    xm = [jnp.concatenate([pick(tk[i][:, :128]).astype(BF16), xs_in[i][L:]], axis=0) for i in items]
    u0 = [-pick(tk[i][:, 128:]) for i in items]
    for c in range(n_chunks):
        rs = slice(c * L, (c + 1) * L)
        i0 = c * n_pairs
        v = vs[i0:i0 + n_pairs]
        xs = [_dot_nt(xm[i0 + p], states[p].astype(BF16)) for p in pairs]
        vu = [jnp.concatenate([v[p], (u0[i0 + p] - xs[p][:L]).astype(BF16)], axis=0) for p in pairs]
        o = [xs[p][L:] + pick(_dot(stack2(a_r, i0 + p), vu[p])) for p in pairs]
        for p, sl in enumerate(sls):
            kb = jnp.concatenate([kbar_ref[0, rs, sl], bbar_ref[0, rs, sl]], axis=0)
            states[p] = states[p] * etot_ref[0, c * L:c * L + 1, sl] + jnp.where(bd, _dot_tn(vu[p], kb), 0.0)
        for p, sl in enumerate(sls):
            ctr = o[p] - hsum(o[p]) * (1.0 / N)
            on = ctr * lax.rsqrt(hsum(ctr * ctr) * (1.0 / N) + RWKV_GN_EPS)
            y_ref[0, rs, sl] = ((on * gw_ref[:, sl] + gb_ref[:, sl] + bonus_ref[0, rs, sl]) * g_ref[0, rs, sl]).astype(y_ref.dtype)
    for p in pairs:
        state_ref[p] = states[p]


def rwkv7_mixer(p, col0, mu, w0, w2, a0, a2, g2, k_k, k_a, r_k, gn_w, gn_b, *, tm=256):
    b, s, _ = p.shape
    zw = mu.shape[0]
    zh = zw // RWKV_Z_PIECES
    prev_rows = 16 if p.dtype == BF16 else 8
    assert col0 % zh == 0
    tm = min(tm, s)
    c, hd, nh = BRANCH_WIDTH, RWKV_HEAD_DIM, RWKV_HEADS
    L = min(RWKV_CHUNK, s)
    assert tm % L == 0 and L & (L - 1) == 0
    bd = jnp.asarray(np.kron(np.eye(nh, dtype=np.float32), np.ones((hd, hd), np.float32)), dtype=BF16)
    chunk = np.arange(tm) // L
    same = chunk[:, None] == chunk[None, :]
    csum = np.concatenate([same & (np.arange(tm)[None, :] <= np.arange(tm)[:, None]), same], axis=0)
    csum = jnp.asarray(csum.astype(np.float32), dtype=BF16)
    row = lambda w: pl.BlockSpec((1, tm, w), lambda bi, i: (bi, i, 0))
    vec = lambda w: pl.BlockSpec((1, w), lambda bi, i: (0, 0))
    mat = lambda shape: pl.BlockSpec(shape, lambda bi, i: (0,) * len(shape))
    w2, a2, g2 = (jnp.stack(_split2_bf16(w)) for w in (w2, a2, g2))
    outs = pl.pallas_call(
        _rwkv_prep_kernel,
        grid=(b, s // tm),
        in_specs=[pl.BlockSpec((1, tm, zh), functools.partial(lambda bi, i, j: (bi, i, j), j=col0 // zh + j))
                  for j in range(RWKV_Z_PIECES)]
        + [pl.BlockSpec((1, prev_rows, zh),
                        functools.partial(lambda bi, i, j: (bi, jnp.maximum(i * (tm // prev_rows) - 1, 0), j), j=col0 // zh + j))
           for j in range(RWKV_Z_PIECES)]
        + [vec(zw), vec(c), vec(c), vec(c), vec(c), vec(c), mat(w2.shape), mat(a2.shape), mat(g2.shape),
                  mat(bd.shape), mat(csum.shape)],
        out_specs=[row(c)] * 10,
        out_shape=[jax.ShapeDtypeStruct((b, s, c), BF16)] * 7 + [jax.ShapeDtypeStruct((b, s, c), F32)] * 3,
        compiler_params=_params("parallel", "parallel"),
        name="rwkv_prep",
    )(*([p] * (2 * RWKV_Z_PIECES)), mu.reshape(1, zw), w0.reshape(1, c), a0.reshape(1, c), k_k.reshape(1, c), k_a.reshape(1, c),
      r_k.reshape(1, c), w2, a2, g2, bd, csum)
    rows = min(RWKV_CHUNKS_PER_STEP * L, s)
    blk = pl.BlockSpec((1, rows, c), lambda bi, ci: (bi, ci, 0))
    gvec = pl.BlockSpec((1, c), lambda bi, ci: (0, 0))
    return pl.pallas_call(
        functools.partial(_rwkv_scan_kernel, L=L),
        grid=(b, s // rows),
        in_specs=[blk] * 10 + [gvec, gvec],
        out_specs=blk,
        out_shape=jax.ShapeDtypeStruct((b, s, c), BF16),
        scratch_shapes=[pltpu.VMEM((c // 128, 128, 128), F32)],
        compiler_params=_params("parallel", "arbitrary"),
        name="rwkv_scan",
    )(*outs, gn_w.reshape(1, c), gn_b.reshape(1, c))


def _retention_kernel(qk_ref, v_ref, g_ref, cos_ref, sin_ref, inner_ref, qd_ref, kd_ref, cd_ref, o_ref, state_ref):
    @pl.when(pl.program_id(1) == 0)
    def _():
        state_ref[...] = jnp.zeros_like(state_ref)

    C = qk_ref.shape[1]
    kw = RET_HEADS * RET_K_DIM
    cos, sin = cos_ref[...], sin_ref[...]
    rope = lambda t: jnp.concatenate(
        [_rope(t[:, j:j + 128], cos[:, j:j + 128], sin[:, j:j + 128], RET_K_DIM // 2, RET_K_DIM) for j in range(0, kw, 128)], axis=1)
    q = rope(qk_ref[0, :, :kw].astype(F32))
    k = rope(qk_ref[0, :, kw:].astype(F32)) * RET_K_DIM ** -0.5
    lane = lax.broadcasted_iota(jnp.int32, (C, 128), 1)
    heads = range(RET_HEADS)
    vsl = [slice(h * RET_V_DIM, (h + 1) * RET_V_DIM) for h in heads]
    own = lambda t, h: jnp.where((lane // RET_K_DIM) == h % 2, t[:, (h // 2) * 128:(h // 2 + 1) * 128], 0.0)
    qh = [own(q, h) for h in heads]
    kh = [own(k, h) for h in heads]
    vh = [v_ref[0, :, sl] for sl in vsl]
    st = [state_ref[h] for h in heads]
    scores = [_dot_nt(qh[h].astype(BF16), kh[h].astype(BF16)) * inner_ref[h] for h in heads]
    cross = [_dot((qh[h] * qd_ref[h]).astype(BF16), st[h].astype(BF16)) for h in heads]
    o = [_dot(scores[h].astype(BF16), vh[h]) + cross[h] for h in heads]
    for h in heads:
        state_ref[h] = st[h] * cd_ref[h, 0:1, :] + _dot_tn((kh[h] * kd_ref[h]).astype(BF16), vh[h])
    for h, sl in enumerate(vsl):
        gate = g_ref[0, :, sl].astype(F32)
        c = o[h] - jnp.mean(o[h], axis=-1, keepdims=True)
        y = c * lax.rsqrt(jnp.mean(c * c, axis=-1, keepdims=True) + NORM_EPS)
        o_ref[0, :, sl] = (y * (gate * jax.nn.sigmoid(gate))).astype(o_ref.dtype)


def retention_mixer(p, col0, cos, sin):
    b, s, _ = p.shape
    assert col0 % 512 == 0
    H, C = RET_HEADS, min(RET_CHUNK, s)
    log_gamma = jnp.log1p(-jnp.exp2(-5.0 - jnp.arange(H, dtype=F32)))
    n = jnp.arange(C, dtype=F32)
    dist = n[:, None] - n[None, :]
    inner = jnp.where(dist >= 0, jnp.exp(jnp.maximum(dist, 0.0) * log_gamma[:, None, None]), 0.0)
    q_decay = jnp.exp((n + 1.0) * log_gamma[:, None])
    k_decay = jnp.exp((C - 1.0 - n) * log_gamma[:, None])
    chunk_decay = jnp.exp(C * log_gamma)
    lanes = lambda t: jnp.broadcast_to(t[:, :, None], (H, C, 128))
    const = lambda shape: pl.BlockSpec(shape, lambda bi, ci: (0,) * len(shape))
    kw = RET_HEADS * RET_K_DIM
    return pl.pallas_call(
        _retention_kernel,
        grid=(b, s // C),
        in_specs=[pl.BlockSpec((1, C, 512), functools.partial(lambda bi, ci, j: (bi, ci, j), j=col0 // 512 + j))
                  for j in range(3)] + [
                  pl.BlockSpec((C, kw), lambda bi, ci: (ci, 0)), pl.BlockSpec((C, kw), lambda bi, ci: (ci, 0)),
                  const((H, C, C)), const((H, C, 128)), const((H, C, 128)), const((H, 8, 128))],
        out_specs=pl.BlockSpec((1, C, H * RET_V_DIM), lambda bi, ci: (bi, ci, 0)),
        out_shape=jax.ShapeDtypeStruct((b, s, H * RET_V_DIM), BF16),
        scratch_shapes=[pltpu.VMEM((H, 128, 128), F32)],
        compiler_params=_params("parallel", "arbitrary"),
        name="retention",
    )(p, p, p, cos, sin, inner, lanes(q_decay), lanes(k_decay), jnp.broadcast_to(chunk_decay[:, None, None], (H, 8, 128)))


def _pad_cols(w, width):
    return jnp.pad(w, [(0, 0)] * (w.ndim - 1) + [(0, width - w.shape[-1])])


def _rope_tables(s, inv_freq, width, fill_cos):
    inv_freq = np.asarray(inv_freq, np.float32)
    ang = np.arange(s, dtype=np.float32)[:, None] * inv_freq[None, :]
    cos, sin = np.cos(ang).astype(np.float32), np.sin(ang).astype(np.float32)
    rot = 2 * cos.shape[1]
    cos2 = np.concatenate([cos, cos, np.full((s, width - rot), fill_cos, np.float32)], axis=1)
    sin2 = np.concatenate([sin, sin, np.zeros((s, width - rot), np.float32)], axis=1)
    return cos2, sin2


def _inv_freq(rot_dim, theta):
    return np.float32(theta) ** (-np.arange(0, rot_dim, 2, dtype=np.float32) / np.float32(rot_dim))


def _split_offsets():
    sizes = (MLA_Q_RANK, MLA_KV_RANK, MLA_ROPE_DIM, 512, 128, 128, 128, 128, 128, 128, 12,
             3 * BRANCH_WIDTH + RWKV_DECAY_LORA + RWKV_A_LORA + RWKV_GATE_LORA, 256, 256, 512, 512, N_MIXERS * D_MODEL)
    offs = np.concatenate([[0], np.cumsum(sizes)])
    return [(int(offs[i]), int(offs[i + 1])) for i in range(len(sizes))]


def _pack_in_proj(w):
    w = w.astype(BF16)
    sl = [w[..., a:b] for a, b in _split_offsets()]
    (ql, kvl, kr, nq, nkc, nvc, nks, nvs, nkw, nvw, ng, rz, rq, rk, rv, rg, mg) = sl
    c = BRANCH_WIDTH
    w_rwkv = [rz[..., :3 * c], _pad_cols(rz[..., 3 * c:3 * c + 96], 128), _pad_cols(rz[..., 3 * c + 96:3 * c + 192], 128),
              rz[..., 3 * c + 192:]]
    groups = {"nsa": [nq, nkc, nvc, nks, nvs, nkw, nvw, _pad_cols(ng, 128)], "mla_kv": [kvl], "mla_q": [ql],
              "mla_kr": [_pad_cols(kr, 128)], "ret": [rq, rk, rv, rg], "gate": [mg], "rwkv": w_rwkv}
    cols, parts, off = {}, [], 0
    for name, ws in groups.items():
        cols[name] = off
        parts += ws
        off += sum(t.shape[-1] for t in ws)
    return jnp.concatenate(parts, axis=-1), cols


def _pack_mla_weights(w_uq, w_ukv):
    H, dn, dr, dv = MLA_HEADS, MLA_NOPE_DIM, MLA_ROPE_DIM, MLA_V_DIM
    uq = w_uq.reshape(-1, H, dn + dr)
    nope = uq[:, :, :dn].reshape(-1, H * dn)
    rope = uq[:, :, dn:]
    rope_p = _pad_cols(rope, 128).reshape(-1, H * 128)
    ukv = w_ukv.reshape(-1, H, dn + dv)
    wk, wv_t = ukv[:, :, :dn].reshape(-1, H * dn), ukv[:, :, dn:].reshape(-1, H * dv).T
    return jnp.concatenate([nope, rope_p], axis=1).astype(BF16), (wk.astype(BF16), wv_t.astype(BF16))


def _pad_rows(w, rows):
    return jnp.pad(w, ((0, rows - w.shape[0]), (0, 0)))


def kernel(x, w_in, w_branch, w_out, w_up, w_down, norm_gains, mla_g_q, mla_g_kv, mla_w_uq, mla_w_ukv,
           nsa_cmp_pos, nsa_cmp_w1, nsa_cmp_w2, rwkv_mu, rwkv_w0, rwkv_w2, rwkv_a0, rwkv_a2, rwkv_g2,
           rwkv_k_k, rwkv_k_a, rwkv_r_k, rwkv_gn_w, rwkv_gn_b):
    B, S, D = x.shape
    depth = w_in.shape[0]
    T = B * S
    tm, tm_in = min(512, T), min(1024, T)
    mla_cos, mla_sin = _rope_tables(S, _inv_freq(MLA_ROPE_DIM, ROPE_THETA), 128, 0.0)
    nsa_cos, nsa_sin = _rope_tables(S, _inv_freq(NSA_ROT_DIM, ROPE_THETA), 128, 1.0)
    ret_cos, ret_sin = _rope_tables(S, np.float32(RET_THETA) ** (-np.linspace(0.0, 1.0, RET_K_DIM // 2, dtype=np.float32)),
                                    RET_K_DIM, 0.0)
    ret_cos, ret_sin = np.tile(ret_cos, (1, RET_HEADS)), np.tile(ret_sin, (1, RET_HEADS))
    c = BRANCH_WIDTH
    x = x.reshape(T, D)
    w_all, cols = _pack_in_proj(w_in)
    tn_in = w_all.shape[-1] // 9
    assert tn_in % 256 == 0 and tn_in * 9 == w_all.shape[-1]
    w_branch, w_out, w_up, w_down = (w.astype(BF16) for w in (w_branch, w_out, w_up, w_down))
    for l in range(depth):
        g_pre = norm_gains[l, 0]
        p2 = norm_matmul(x, g_pre, w_all, l, BF16, tm_in, tn_in)
        p = p2.reshape(B, S, -1)

        wq, wkv = _pack_mla_weights(mla_w_uq[l], mla_w_ukv[l])
        y_mla = mla_mixer(p, (cols["mla_q"], cols["mla_kv"], cols["mla_kr"]), mla_g_q[l], mla_g_kv[l], wq, wkv, mla_cos, mla_sin)
        y_nsa = nsa_mixer(p, cols["nsa"], nsa_cmp_pos[l], nsa_cmp_w1[l], nsa_cmp_w2[l], nsa_cos, nsa_sin)
        mu = rwkv_mu[l]
        mu_p = jnp.concatenate([mu[:3 * c], _pad_cols(mu[3 * c:3 * c + 96], 128), _pad_cols(mu[3 * c + 96:3 * c + 192], 128),
                                mu[3 * c + 192:]])
        y_rwkv = rwkv7_mixer(p, cols["rwkv"], mu_p, rwkv_w0[l], _pad_rows(rwkv_w2[l], 128), rwkv_a0[l],
                             _pad_rows(rwkv_a2[l], 128), rwkv_g2[l], rwkv_k_k[l], rwkv_k_a[l], rwkv_r_k[l].reshape(-1),
                             rwkv_gn_w[l], rwkv_gn_b[l])
        y_ret = retention_mixer(p, cols["ret"], ret_cos, ret_sin)

        ys = [y.reshape(T, c) for y in (y_mla, y_nsa, y_rwkv, y_ret)]
        x, h = merge_out_residual(ys, p2, cols["gate"], w_branch, w_out, l, norm_gains[l, 1], norm_gains[l, 2], x, min(256, T))
        x = ffn_block(x, h, w_up, w_down, l, norm_gains[l, 3], tm, 1024)
    return x.reshape(B, S, D)
```
